```python
import math
import jax
import jax.numpy as jnp
from jax import lax
import numpy as np

D_MODEL = 1024
BATCH = 4
SEQ = 4096
DEPTH = 4
DEC_BATCH = 128
DEC_SEQ = 1
PAST_LEN = 2048
PAGE_SIZE = 128

D_MIX = D_MODEL
HEAD_DIM = 64
NSA_HEADS = (D_MIX // 2) // HEAD_DIM
NSA_KV_HEADS = 2
NSA_GROUP = NSA_HEADS // NSA_KV_HEADS
NSA_WIDTH = NSA_HEADS * HEAD_DIM
ROPE_DIM = HEAD_DIM // 4
ROPE_THETA = 500000.0
ATTN_SCALE = HEAD_DIM ** -0.5
CMP_BLOCK = 32
CMP_STRIDE = 16
CMP_HIDDEN = 2 * HEAD_DIM
SEL_BLOCK = 64
SEL_TOPK = 16
WINDOW = 512
QBLOCK = 128
FORCE_SCORE = 1.0e4
GDN_DK = 128
GDN_DV = 128
GDN_HEADS = (D_MIX // 2) // GDN_DV
GDN_WIDTH = GDN_HEADS * GDN_DV
GDN_QKV = GDN_HEADS * (2 * GDN_DK + GDN_DV)
CONV_W = 4
GDN_CHUNK = 64
MEM_TOKENS = 256
MEM_HEADS = 4
MEM_HD = 64
MEM_WIDTH = MEM_HEADS * MEM_HD
D_FF = 2816
RMS_EPS = 1e-6
NEG_INF = -1.0e30
F32 = jnp.float32
IN_SPLITS = (NSA_WIDTH, 6 * NSA_KV_HEADS * HEAD_DIM, 3 * NSA_HEADS, GDN_QKV, GDN_HEADS, GDN_HEADS, GDN_WIDTH)
IN_COLS = sum(IN_SPLITS)

kernel_name = 'nsa_gdn_hymba_macaron_decoder_step'


def rmsnorm(x, w):
    xf = x.astype(F32)
    y = xf * lax.rsqrt(jnp.mean(xf * xf, axis=-1, keepdims=True) + RMS_EPS)
    return (y * w.astype(F32)).astype(x.dtype)


def l2norm(x):
    return x * lax.rsqrt(jnp.sum(x * x, axis=-1, keepdims=True) + 1e-6)


def masked_softmax(s, mask):
    s = jnp.where(mask, s, NEG_INF)
    m = jnp.max(s, axis=-1, keepdims=True)
    e = jnp.where(mask, jnp.exp(s - m), 0.0)
    return e / jnp.maximum(jnp.sum(e, axis=-1, keepdims=True), 1e-30)


def rope(x, pos):
    half = ROPE_DIM // 2
    inv = ROPE_THETA ** (-2.0 * jnp.arange(half, dtype=F32) / ROPE_DIM)
    ang = pos.astype(F32)[:, None] * inv[None, :]
    cos = jnp.cos(ang)[None, :, None, :]
    sin = jnp.sin(ang)[None, :, None, :]
    xf = x.astype(F32)
    x1 = xf[..., :half]
    x2 = xf[..., half:ROPE_DIM]
    out = jnp.concatenate([x1 * cos - x2 * sin, x2 * cos + x1 * sin, xf[..., ROPE_DIM:]], axis=-1)
    return out.astype(x.dtype)


def swiglu(h, w_gu, w_down):
    gate, up = jnp.split(h @ w_gu, 2, axis=-1)
    return (jax.nn.silu(gate) * up) @ w_down


def last_rows(a, n):
    t = a.shape[1]
    if t >= n:
        return a[:, t - n:]
    pad = [(0, 0)] * a.ndim
    pad[1] = (n - t, 0)
    return jnp.pad(a, pad)


def compress(k, pe, w1, w2):
    B, T, Hk, Dh = k.shape
    n_cmp = (T - CMP_BLOCK) // CMP_STRIDE + 1
    start = jnp.arange(n_cmp) * CMP_STRIDE
    idx = start[:, None] + jnp.arange(CMP_BLOCK)[None, :]
    blk = k[:, idx] + pe[None, None, :, None, :]
    blk = jnp.swapaxes(blk, 2, 3).reshape(B, n_cmp, Hk, CMP_BLOCK * Dh)
    return jax.nn.gelu(blk @ w1) @ w2, start + CMP_BLOCK - 1


def block_coverage(c_start, n_sel):
    b_start = jnp.arange(n_sel) * SEL_BLOCK
    lo = jnp.maximum(c_start[:, None], b_start[None, :])
    hi = jnp.minimum(c_start[:, None] + CMP_BLOCK, b_start[None, :] + SEL_BLOCK)
    return jnp.clip(hi - lo, 0, None).astype(F32) / CMP_BLOCK


def cmp_sel_attend(q, q_pos, kc, vc, c_end, cov, k_blk, v_blk):
    qf = q.astype(F32)
    s = jnp.einsum('bqhgd,bchd->bhgqc', qf, kc.astype(F32)) * ATTN_SCALE
    p = masked_softmax(s, c_end[None, :] <= q_pos[:, None])
    o_cmp = jnp.einsum('bhgqc,bchd->bqhgd', p, vc.astype(F32))
    imp = jnp.einsum('bhgqc,cn->bhqn', p, cov)
    n_sel = cov.shape[1]
    blk = jnp.arange(n_sel)[None, :]
    q_blk = (q_pos // SEL_BLOCK)[:, None]
    forced = (blk == 0) | (blk == q_blk) | (blk == q_blk - 1)
    imp = jnp.where(forced, FORCE_SCORE, jnp.where(blk > q_blk, -1.0, imp))
    _, idx = lax.top_k(imp, min(SEL_TOPK, n_sel))
    B, Hk = k_blk.shape[:2]
    bi = jnp.arange(B)[:, None, None, None]
    hi = jnp.arange(Hk)[None, :, None, None]
    kg = k_blk[bi, hi, idx].astype(F32)
    vg = v_blk[bi, hi, idx].astype(F32)
    k_pos = idx[..., None] * SEL_BLOCK + jnp.arange(SEL_BLOCK)
    mask = (k_pos <= q_pos[None, None, :, None, None])[:, :, None]
    s2 = jnp.einsum('bqhgd,bhqksd->bhgqks', qf, kg) * ATTN_SCALE
    shp = s2.shape
    p2 = masked_softmax(s2.reshape(shp[:-2] + (-1,)), mask.reshape(mask.shape[:-2] + (-1,))).reshape(shp)
    o_sel = jnp.einsum('bhgqks,bhqksd->bqhgd', p2, vg)
    return o_cmp, o_sel


def window_attend(q, q_pos, k, v, k_pos):
    s = jnp.einsum('bqhgd,bkhd->bhgqk', q.astype(F32), k.astype(F32)) * ATTN_SCALE
    diff = q_pos[:, None] - k_pos[None, :]
    p = masked_softmax(s, (diff >= 0) & (diff < WINDOW) & (k_pos[None, :] >= 0))
    return jnp.einsum('bhgqk,bkhd->bqhgd', p, v.astype(F32))


def nsa_mixer(q, kv6, gate_logits, nsa_past, win_buf, w_buf, cmp_pe, w_cmp1, w_cmp2):
    B, T = q.shape[:2]
    pos0 = nsa_past.shape[1]
    pos = pos0 + jnp.arange(T)
    q = rope(q, pos).reshape(B, T, NSA_KV_HEADS, NSA_GROUP, HEAD_DIM)
    new_rows = jnp.stack([kv6[:, :, 0], kv6[:, :, 1], rope(kv6[:, :, 2], pos), kv6[:, :, 3]], axis=2)
    rows = jnp.concatenate([nsa_past.astype(new_rows.dtype), new_rows], axis=1)
    t_kv = rows.shape[1]
    kc, c_end = compress(rows[:, :, 0], cmp_pe[0], w_cmp1[0], w_cmp2[0])
    vc, _ = compress(rows[:, :, 1], cmp_pe[1], w_cmp1[1], w_cmp2[1])
    kc = rope(kc, c_end)
    n_sel = -(-t_kv // SEL_BLOCK)
    cov = block_coverage(c_end - CMP_BLOCK + 1, n_sel)
    sel = jnp.pad(rows[:, :, 2:], ((0, 0), (0, n_sel * SEL_BLOCK - t_kv), (0, 0), (0, 0), (0, 0)))
    sel = sel.reshape(B, n_sel, SEL_BLOCK, 2, NSA_KV_HEADS, HEAD_DIM).transpose(3, 0, 4, 1, 2, 5)
    k_blk, v_blk = sel[0], sel[1]
    win_new = jnp.stack([rope(kv6[:, :, 4], pos), kv6[:, :, 5]], axis=2)
    win_all = jnp.concatenate([win_buf.astype(win_new.dtype), win_new], axis=1)
    n_buf = win_buf.shape[1]
    gates = jax.nn.sigmoid(gate_logits.astype(F32)).reshape(B, T, NSA_KV_HEADS, NSA_GROUP, 3)

    def attend(qb, qpos, gb, kw, kw_pos):
        o_cmp, o_sel = cmp_sel_attend(qb, qpos, kc, vc, c_end, cov, k_blk, v_blk)
        o_win = window_attend(qb, qpos, kw[:, :, 0], kw[:, :, 1], kw_pos)
        return gb[..., 0:1] * o_cmp + gb[..., 1:2] * o_sel + gb[..., 2:3] * o_win

    if T > QBLOCK and T % QBLOCK == 0:
        win_pad = jnp.pad(win_all, ((0, 0), (WINDOW, 0), (0, 0), (0, 0), (0, 0)))
        kpos_pad = pos0 - n_buf - WINDOW + jnp.arange(WINDOW + n_buf + T)

        def block(i):
            s = i * QBLOCK
            qb = lax.dynamic_slice_in_dim(q, s, QBLOCK, axis=1)
            gb = lax.dynamic_slice_in_dim(gates, s, QBLOCK, axis=1)
            kw = lax.dynamic_slice_in_dim(win_pad, n_buf + s, QBLOCK + WINDOW, axis=1)
            kwp = lax.dynamic_slice_in_dim(kpos_pad, n_buf + s, QBLOCK + WINDOW, axis=0)
            return attend(qb, pos0 + s + jnp.arange(QBLOCK), gb, kw, kwp)

        o = lax.map(block, jnp.arange(T // QBLOCK))
        o = jnp.moveaxis(o, 0, 1).reshape(B, T, NSA_KV_HEADS, NSA_GROUP, HEAD_DIM)
    else:
        o = attend(q, pos, gates, win_all, pos0 - n_buf + jnp.arange(n_buf + T))
    return o.reshape(B, T, NSA_WIDTH), new_rows, last_rows(win_all, w_buf)


def to_chunks(a, C, pad):
    a = jnp.moveaxis(a, 1, 2)
    if pad:
        a = jnp.pad(a, [(0, 0), (0, 0), (0, pad)] + [(0, 0)] * (a.ndim - 3))
    B, H, Tp = a.shape[:3]
    a = a.reshape((B, H, Tp // C, C) + a.shape[3:])
    return jnp.moveaxis(a, 2, 0)


def chunk_gated_delta(q, k, v, g, beta, S0):
    B, T, H, _ = q.shape
    C = min(GDN_CHUNK, T)
    pad = (-T) % C
    qc, kc, vc = to_chunks(q, C, pad), to_chunks(k, C, pad), to_chunks(v, C, pad)
    gc = jnp.cumsum(to_chunks(g, C, pad), axis=-1)
    bc = to_chunks(beta, C, pad)
    ii = jnp.arange(C)
    incl = ii[:, None] >= ii[None, :]
    strict = ii[:, None] > ii[None, :]
    diff = gc[..., :, None] - gc[..., None, :]
    decay = jnp.where(incl, jnp.exp(jnp.where(incl, diff, 0.0)), 0.0)
    kb = kc * bc[..., None]
    A = jnp.where(strict, jnp.einsum('...id,...jd->...ij', kb, kc) * decay, 0.0)
    eye = jnp.eye(C, dtype=F32)
    Tm = lax.linalg.triangular_solve(eye + A, jnp.broadcast_to(eye, A.shape), left_side=True, lower=True)
    u = jnp.einsum('...ij,...jd->...id', Tm, vc * bc[..., None])
    w = jnp.einsum('...ij,...jd->...id', Tm, kb * jnp.exp(gc)[..., None])

    def step(S, inp):
        qi, ki, ui, wi, gi, di = inp
        v_new = ui - jnp.einsum('bhck,bhkv->bhcv', wi, S)
        att = jnp.einsum('bhik,bhjk->bhij', qi, ki) * di
        o = jnp.einsum('bhck,bhkv->bhcv', qi * jnp.exp(gi)[..., None], S) + jnp.einsum('bhij,bhjv->bhiv', att, v_new)
        gl = gi[..., -1]
        S = S * jnp.exp(gl)[..., None, None] + jnp.einsum('bhck,bhcv->bhkv', ki * jnp.exp(gl[..., None] - gi)[..., None], v_new)
        return S, o

    S, o = lax.scan(step, S0, (qc, kc, u, w, gc, decay))
    o = jnp.moveaxis(o, 0, 2).reshape(B, H, -1, o.shape[-1])[:, :, :T]
    return jnp.moveaxis(o, 1, 2), S


def gdn_mixer(qkv, a, b, z, conv_buf, state, conv_w, A_log, dt_bias, norm_w):
    B, T, _ = qkv.shape
    xp = jnp.concatenate([conv_buf.astype(qkv.dtype), qkv], axis=1)
    c = xp[:, 0:T] * conv_w[0]
    for j in range(1, CONV_W):
        c = c + xp[:, j:j + T] * conv_w[j]
    c = jax.nn.silu(c.astype(F32))
    new_buf = xp[:, T:]
    qk_w = GDN_HEADS * GDN_DK
    q = l2norm(c[..., :qk_w].reshape(B, T, GDN_HEADS, GDN_DK)) * GDN_DK ** -0.5
    k = l2norm(c[..., qk_w:2 * qk_w].reshape(B, T, GDN_HEADS, GDN_DK))
    v = c[..., 2 * qk_w:].reshape(B, T, GDN_HEADS, GDN_DV)
    g = -jnp.exp(A_log.astype(F32)) * jax.nn.softplus(a.astype(F32) + dt_bias.astype(F32))
    beta = jax.nn.sigmoid(b.astype(F32))
    o, new_state = chunk_gated_delta(q, k, v, g, beta, state.astype(F32))
    o = rmsnorm(o, norm_w) * jax.nn.silu(z.astype(F32)).reshape(B, T, GDN_HEADS, GDN_DV)
    return o.reshape(B, T, GDN_WIDTH), new_buf, new_state


def mem_kv_project(mem, gain, w_kv):
    B, M, _ = mem.shape
    return (rmsnorm(mem, gain) @ w_kv).reshape(B, M, 2, MEM_HEADS, MEM_HD)


def mem_attend(h, mem_kv, w_q, w_o):
    B, T, _ = h.shape
    q = (h @ w_q).reshape(B, T, MEM_HEADS, MEM_HD).astype(F32)
    s = jnp.einsum('bthd,bmhd->bhtm', q, mem_kv[:, :, 0].astype(F32)) * MEM_HD ** -0.5
    p = jax.nn.softmax(s, axis=-1)
    o = jnp.einsum('bhtm,bmhd->bthd', p, mem_kv[:, :, 1].astype(F32))
    return o.reshape(B, T, MEM_WIDTH).astype(h.dtype) @ w_o


def layer(x, mem_kv, nsa_past, win_buf, conv_buf, gdn_state, w_buf, norms, w_ffn_gu, w_ffn_down, w_in, w_out,
          cmp_pe, w_cmp1, w_cmp2, gdn_conv, gdn_A_log, gdn_dt_bias, gdn_norm, w_mem_q, w_mem_o):
    B, T, _ = x.shape
    x = x + 0.5 * rmsnorm(swiglu(rmsnorm(x, norms[0]), w_ffn_gu[0], w_ffn_down[0]), norms[1])
    h = rmsnorm(x, norms[2])
    cuts = np.cumsum(IN_SPLITS)[:-1].tolist()
    q, kv6, g_nsa, qkv, a, b, z = jnp.split(h @ w_in, cuts, axis=-1)
    o_nsa, new_rows, new_win = nsa_mixer(q.reshape(B, T, NSA_HEADS, HEAD_DIM),
                                         kv6.reshape(B, T, 6, NSA_KV_HEADS, HEAD_DIM),
                                         g_nsa.reshape(B, T, NSA_HEADS, 3),
                                         nsa_past, win_buf, w_buf, cmp_pe, w_cmp1, w_cmp2)
    o_gdn, new_conv, new_state = gdn_mixer(qkv, a, b, z, conv_buf, gdn_state, gdn_conv, gdn_A_log, gdn_dt_bias, gdn_norm)
    mix = jnp.concatenate([o_nsa.astype(x.dtype), o_gdn.astype(x.dtype)], axis=-1) @ w_out
    x = x + rmsnorm(mix, norms[3])
    x = x + rmsnorm(mem_attend(rmsnorm(x, norms[4]), mem_kv, w_mem_q, w_mem_o), norms[5])
    x = x + 0.5 * rmsnorm(swiglu(rmsnorm(x, norms[6]), w_ffn_gu[1], w_ffn_down[1]), norms[7])
    return x, new_rows, new_win, new_conv, new_state


def setup_inputs(seed: int = 0) -> dict:
    key = jax.random.key(seed)
    ks = list(jax.random.split(key, 32))

    def nrm(i, shape, scale):
        return jax.random.normal(ks[i], shape, jnp.float32) * scale

    n_pages = PAST_LEN // PAGE_SIZE
    n_used = DEC_BATCH * n_pages
    n_phys = n_used + max(n_used // 4, 1)
    w_buf = min(WINDOW, PAST_LEN)
    page_table = jax.random.permutation(ks[0], n_phys)[:n_used].reshape(DEC_BATCH, n_pages).astype(jnp.int32)
    dt = jnp.exp(jax.random.uniform(ks[1], (DEPTH, GDN_HEADS), jnp.float32, math.log(1e-3), math.log(1e-1)))
    return {
        'x_prompt': nrm(2, (BATCH, SEQ, D_MODEL), 1.0),
        'x_sample': nrm(3, (DEC_BATCH, DEC_SEQ, D_MODEL), 1.0),
        'cache_nsa_kv': nrm(4, (DEPTH, n_phys, PAGE_SIZE, 4, NSA_KV_HEADS, HEAD_DIM), 1.0),
        'cache_win_kv': nrm(5, (DEPTH, DEC_BATCH, w_buf, 2, NSA_KV_HEADS, HEAD_DIM), 1.0),
        'state_gdn_S': nrm(6, (DEPTH, DEC_BATCH, GDN_HEADS, GDN_DK, GDN_DV), 0.1),
        'state_gdn_conv': nrm(7, (DEPTH, DEC_BATCH, CONV_W - 1, GDN_QKV), 1.0),
        'cache_mem_kv': nrm(8, (DEPTH, DEC_BATCH, MEM_TOKENS, 2, MEM_HEADS, MEM_HD), 1.0),
        'page_table': page_table,
        'mem_prompt': nrm(9, (BATCH, MEM_TOKENS, D_MODEL), 1.0),
        'norm_gains': 1.0 + nrm(10, (DEPTH, 8, D_MODEL), 0.1),
        'mem_norm': 1.0 + nrm(11, (DEPTH, D_MODEL), 0.1),
        'w_ffn_gu': nrm(12, (DEPTH, 2, D_MODEL, 2 * D_FF), D_MODEL ** -0.5),
        'w_ffn_down': nrm(13, (DEPTH, 2, D_FF, D_MODEL), D_FF ** -0.5),
        'w_in': nrm(14, (DEPTH, D_MODEL, IN_COLS), D_MODEL ** -0.5),
        'w_out': nrm(15, (DEPTH, D_MIX, D_MODEL), D_MIX ** -0.5),
        'cmp_pe': nrm(16, (DEPTH, 2, CMP_BLOCK, HEAD_DIM), 0.02),
        'w_cmp1': nrm(17, (DEPTH, 2, CMP_BLOCK * HEAD_DIM, CMP_HIDDEN), (CMP_BLOCK * HEAD_DIM) ** -0.5),
        'w_cmp2': nrm(18, (DEPTH, 2, CMP_HIDDEN, HEAD_DIM), CMP_HIDDEN ** -0.5),
        'gdn_conv': nrm(19, (DEPTH, CONV_W, GDN_QKV), CONV_W ** -0.5),
        'gdn_A_log': jnp.log(jax.random.uniform(ks[20], (DEPTH, GDN_HEADS), jnp.float32, 1.0, 16.0)),
        'gdn_dt_bias': dt + jnp.log(-jnp.expm1(-dt)),
        'gdn_norm': 1.0 + nrm(21, (DEPTH, GDN_DV), 0.1),
        'w_mem_q': nrm(22, (DEPTH, D_MODEL, MEM_WIDTH), D_MODEL ** -0.5),
        'w_mem_kv': nrm(23, (DEPTH, D_MODEL, 2 * MEM_WIDTH), D_MODEL ** -0.5),
        'w_mem_o': nrm(24, (DEPTH, MEM_WIDTH, D_MODEL), MEM_WIDTH ** -0.5),
    }


def reference(x_prompt, x_sample, cache_nsa_kv, cache_win_kv, state_gdn_S, state_gdn_conv, cache_mem_kv,
              page_table, mem_prompt, norm_gains, mem_norm, w_ffn_gu, w_ffn_down, w_in, w_out, cmp_pe,
              w_cmp1, w_cmp2, gdn_conv, gdn_A_log, gdn_dt_bias, gdn_norm, w_mem_q, w_mem_kv, w_mem_o):
    bp = x_prompt.shape[0]
    bs = x_sample.shape[0]
    w_buf = cache_win_kv.shape[2]
    n_pages = page_table.shape[1]
    yp, ys = x_prompt, x_sample
    nsa_p, win_p, st_p, conv_p, mem_p = [], [], [], [], []
    nsa_s, win_s, st_s, conv_s = [], [], [], []
    for l in range(DEPTH):
        weights = (norm_gains[l], w_ffn_gu[l], w_ffn_down[l], w_in[l], w_out[l], cmp_pe[l], w_cmp1[l], w_cmp2[l],
                   gdn_conv[l], gdn_A_log[l], gdn_dt_bias[l], gdn_norm[l], w_mem_q[l], w_mem_o[l])
        mem_kv_p = mem_kv_project(mem_prompt, mem_norm[l], w_mem_kv[l])
        yp, rows_p, wbuf_p, cbuf_p, S_p = layer(
            yp, mem_kv_p,
            jnp.zeros((bp, 0, 4, NSA_KV_HEADS, HEAD_DIM), yp.dtype),
            jnp.zeros((bp, 0, 2, NSA_KV_HEADS, HEAD_DIM), yp.dtype),
            jnp.zeros((bp, CONV_W - 1, GDN_QKV), yp.dtype),
            jnp.zeros((bp, GDN_HEADS, GDN_DK, GDN_DV), F32),
            w_buf, *weights)
        past = cache_nsa_kv[l][page_table].reshape(bs, n_pages * PAGE_SIZE, 4, NSA_KV_HEADS, HEAD_DIM)
        ys, rows_s, wbuf_s, cbuf_s, S_s = layer(
            ys, cache_mem_kv[l], past, cache_win_kv[l], state_gdn_conv[l], state_gdn_S[l], w_buf, *weights)
        nsa_p.append(rows_p)
        win_p.append(wbuf_p)
        st_p.append(S_p)
        conv_p.append(cbuf_p)
        mem_p.append(mem_kv_p)
        nsa_s.append(rows_s)
        win_s.append(wbuf_s)
        st_s.append(S_s)
        conv_s.append(cbuf_s)
    return (yp, ys, jnp.stack(nsa_p), jnp.stack(win_p), jnp.stack(st_p), jnp.stack(conv_p), jnp.stack(mem_p),
            jnp.stack(nsa_s), jnp.stack(win_s), jnp.stack(st_s), jnp.stack(conv_s))
```

```python
import functools
import math

import numpy as np
import jax
import jax.numpy as jnp
from jax import lax
from jax.experimental import pallas as pl
from jax.experimental.pallas import tpu as pltpu

F32 = jnp.float32
BF16 = jnp.bfloat16

HEAD_DIM = 64
NSA_HEADS = 8
NSA_KV_HEADS = 2
NSA_GROUP = NSA_HEADS // NSA_KV_HEADS
NSA_WIDTH = NSA_HEADS * HEAD_DIM
ROPE_DIM = HEAD_DIM // 4
ROPE_THETA = 500000.0
ATTN_SCALE = HEAD_DIM ** -0.5
CMP_BLOCK = 32
CMP_STRIDE = 16
CMP_HIDDEN = 2 * HEAD_DIM
SEL_BLOCK = 64
SEL_TOPK = 16
WINDOW = 512
FORCE_SCORE = 1.0e4
GDN_DK = 128
GDN_DV = 128
GDN_HEADS = 4
GDN_WIDTH = GDN_HEADS * GDN_DV
GDN_QKV = GDN_HEADS * (2 * GDN_DK + GDN_DV)
CONV_W = 4
GDN_CHUNK = 64
MEM_HEADS = 4
MEM_HD = 64
MEM_WIDTH = MEM_HEADS * MEM_HD
RMS_EPS = 1e-6
NEG_INF = -1.0e30
KV6_WIDTH = 6 * NSA_KV_HEADS * HEAD_DIM

LANES = 128
VMEM_LIMIT_BYTES = 56 * 1024 * 1024

QPAD_WIDTH = NSA_HEADS * LANES
COL_Q = 0
COL_KV = COL_Q + QPAD_WIDTH
COL_QKV = COL_KV + KV6_WIDTH
COL_Z = COL_QKV + GDN_QKV
COL_GATE = COL_Z + GDN_WIDTH
IN_PACKED = COL_GATE + LANES
GATE_A = 3 * NSA_HEADS
GATE_B = GATE_A + GDN_HEADS


def _cparams(*sem):
    return pltpu.CompilerParams(dimension_semantics=sem, vmem_limit_bytes=VMEM_LIMIT_BYTES)


def _rms(x, w):
    return x * lax.rsqrt(jnp.mean(x * x, axis=-1, keepdims=True) + RMS_EPS) * w


def _const_spec(shape):
    nd = len(shape)
    return pl.BlockSpec(shape, lambda *_: (0,) * nd)


def _layer_spec(shape, *lead):
    nlead = len(lead)
    nd = len(shape)
    return pl.BlockSpec((None,) * nlead + tuple(shape), lambda *_: tuple(lead) + (0,) * nd)


FFN_CHUNK = 256


def _ffn_body(x_ref, g0_ref, g1_ref, wgu_ref, wd_ref, o_ref, acc_ref, *, d_ff):
    x = x_ref[...]
    h = _rms(x, g0_ref[...]).astype(BF16)
    for f in range(d_ff // FFN_CHUNK):
        lo = f * FFN_CHUNK
        g = jnp.dot(h, wgu_ref[:, lo:lo + FFN_CHUNK], preferred_element_type=F32)
        u = jnp.dot(h, wgu_ref[:, d_ff + lo:d_ff + lo + FFN_CHUNK], preferred_element_type=F32)
        a = (g * jax.nn.sigmoid(g) * u).astype(BF16)
        d = jnp.dot(a, wd_ref[lo:lo + FFN_CHUNK, :], preferred_element_type=F32)
        if f == 0:
            acc_ref[...] = d
        else:
            acc_ref[...] += d
    o_ref[...] = x + 0.5 * _rms(acc_ref[...], g1_ref[...])


def _ffn(x, gains, wgu, wd, l, j, tm):
    n, d = x.shape
    d_ff = wd.shape[2]
    return pl.pallas_call(
        functools.partial(_ffn_body, d_ff=d_ff),
        grid=(n // tm,),
        in_specs=[
            pl.BlockSpec((tm, d), lambda i: (i, 0)),
            _layer_spec((1, d), l, 2 * j * 3),
            _layer_spec((1, d), l, 2 * j * 3 + 1),
            _layer_spec((d, 2 * d_ff), l, j),
            _layer_spec((d_ff, d), l, j),
        ],
        out_specs=pl.BlockSpec((tm, d), lambda i: (i, 0)),
        out_shape=jax.ShapeDtypeStruct((n, d), F32),
        scratch_shapes=[pltpu.VMEM((tm, d), F32)],
        compiler_params=_cparams("parallel"),
        name=f"ffn{j}",
    )(x, gains, gains, wgu, wd)


def _rope(v, c1, sm1, sp1):
    n = v.shape[1] // LANES
    c, sm, sp = (t if n == 1 else jnp.concatenate([t] * n, axis=1) for t in (c1, sm1, sp1))
    w = v.shape[1]
    return v * c + pltpu.roll(v, w - ROPE_DIM // 2, 1) * sm + pltpu.roll(v, ROPE_DIM // 2, 1) * sp


def _inproj_body(x_ref, g_ref, w_ref, c_ref, sm_ref, sp_ref, e_ref,
                 qpad_ref, rows_ref, win_ref, kaug_ref, vsel_ref, kvwin_ref, gates_ref, qkv_ref, z_ref):
    h = _rms(x_ref[...], g_ref[...]).astype(BF16)
    c1, sm1, sp1 = c_ref[...], sm_ref[...], sp_ref[...]

    def mm(lo, hi):
        return jnp.dot(h, w_ref[:, lo:hi], preferred_element_type=F32)

    q = _rope(mm(COL_Q, COL_KV), c1, sm1, sp1)
    qpad_ref[...] = (q * ATTN_SCALE).astype(BF16)
    kv = mm(COL_KV, COL_QKV)
    ksel = _rope(kv[:, 2 * LANES:3 * LANES], c1, sm1, sp1)
    vsel = kv[:, 3 * LANES:4 * LANES]
    kwin = _rope(kv[:, 4 * LANES:5 * LANES], c1, sm1, sp1)
    vwin = kv[:, 5 * LANES:6 * LANES]
    rows_ref[:, 0:2 * LANES] = kv[:, 0:2 * LANES]
    rows_ref[:, 2 * LANES:3 * LANES] = ksel
    rows_ref[:, 3 * LANES:4 * LANES] = vsel
    win_ref[:, 0:LANES] = kwin
    win_ref[:, LANES:2 * LANES] = vwin
    e2 = e_ref[...]
    lane = lax.broadcasted_iota(jnp.int32, ksel.shape, 1)
    first = lane < HEAD_DIM
    kaug_ref[:, 0:LANES] = jnp.where(first, ksel, e2).astype(BF16)
    kaug_ref[:, LANES:2 * LANES] = jnp.where(first, e2, ksel).astype(BF16)
    vsel_ref[...] = vsel.astype(BF16)
    kvwin_ref[:, 0:LANES] = kwin.astype(BF16)
    kvwin_ref[:, LANES:2 * LANES] = vwin.astype(BF16)
    qkv_ref[...] = mm(COL_QKV, COL_Z)
    z_ref[...] = mm(COL_Z, COL_GATE)
    gates_ref[...] = mm(COL_GATE, IN_PACKED)


def _inproj(x, gains, w_in_p, tabs, l, tm):
    n, d = x.shape
    c_t, sm_t, sp_t, e_t = tabs
    nt = c_t.shape[0] // tm
    tab = lambda w: pl.BlockSpec((tm, w), lambda i: (i % nt, 0))
    row = lambda w: pl.BlockSpec((tm, w), lambda i: (i, 0))
    widths = [(QPAD_WIDTH, BF16), (4 * LANES, F32), (2 * LANES, F32), (2 * LANES, BF16), (LANES, BF16),
              (2 * LANES, BF16), (LANES, F32), (GDN_QKV, F32), (GDN_WIDTH, F32)]
    return pl.pallas_call(
        _inproj_body,
        grid=(n // tm,),
        in_specs=[row(d), _layer_spec((1, d), l, 2), _layer_spec((d, IN_PACKED), l),
                  tab(LANES), tab(LANES), tab(LANES), tab(LANES)],
        out_specs=[row(w) for w, _ in widths],
        out_shape=[jax.ShapeDtypeStruct((n, w), dt) for w, dt in widths],
        compiler_params=_cparams("parallel"),
        name="inproj",
    )(x, gains, w_in_p, c_t, sm_t, sp_t, e_t)


def _normmm_body(x_ref, g_ref, w_ref, o_ref):
    h = _rms(x_ref[...], g_ref[...]).astype(BF16)
    o_ref[...] = jnp.dot(h, w_ref[...], preferred_element_type=F32)


def _normmm(x, gain, w, l, tm):
    n, d = x.shape
    nout = w.shape[-1]
    return pl.pallas_call(
        _normmm_body,
        grid=(n // tm,),
        in_specs=[pl.BlockSpec((tm, d), lambda i: (i, 0)), _layer_spec((1, d), l), _layer_spec((d, nout), l)],
        out_specs=pl.BlockSpec((tm, nout), lambda i: (i, 0)),
        out_shape=jax.ShapeDtypeStruct((n, nout), F32),
        compiler_params=_cparams("parallel"),
        name="memkv",
    )(x, gain, w)


def _outproj_body(x_ref, a1_ref, a2_ref, w_ref, g_ref, o_ref):
    k1 = a1_ref.shape[1]
    acc = jnp.dot(a1_ref[...].astype(BF16), w_ref[0:k1, :], preferred_element_type=F32)
    acc = acc + jnp.dot(a2_ref[...].astype(BF16), w_ref[k1:, :], preferred_element_type=F32)
    o_ref[...] = x_ref[...] + _rms(acc, g_ref[...])


def _outproj(x, a1, a2, w_out, gains, l, tm):
    n, d = x.shape
    row = lambda w: pl.BlockSpec((tm, w), lambda i: (i, 0))
    return pl.pallas_call(
        _outproj_body,
        grid=(n // tm,),
        in_specs=[row(d), row(a1.shape[1]), row(a2.shape[1]),
                  _layer_spec((a1.shape[1] + a2.shape[1], d), l), _layer_spec((1, d), l, 3)],
        out_specs=row(d),
        out_shape=jax.ShapeDtypeStruct((n, d), F32),
        compiler_params=_cparams("parallel"),
        name="outproj",
    )(x, a1, a2, w_out, gains)


def _mem_body(x_ref, g4_ref, g5_ref, wq_ref, wo_ref, kv_ref, o_ref):
    x = x_ref[...]
    h = _rms(x, g4_ref[...]).astype(BF16)
    q = jnp.dot(h, wq_ref[...], preferred_element_type=F32) * (MEM_HD ** -0.5)
    kv = kv_ref[...].astype(BF16)
    outs = []
    for hd in range(MEM_HEADS):
        qh = q[:, hd * MEM_HD:(hd + 1) * MEM_HD].astype(BF16)
        kh = kv[:, hd * MEM_HD:(hd + 1) * MEM_HD]
        vh = kv[:, MEM_WIDTH + hd * MEM_HD:MEM_WIDTH + (hd + 1) * MEM_HD]
        s = lax.dot_general(qh, kh, (((1,), (1,)), ((), ())), preferred_element_type=F32)
        e = jnp.exp(s - jnp.max(s, axis=-1, keepdims=True))
        p = e / jnp.sum(e, axis=-1, keepdims=True)
        outs.append(jnp.dot(p.astype(BF16), vh, preferred_element_type=F32))
    o = jnp.concatenate(outs, axis=1).astype(BF16)
    y = jnp.dot(o, wo_ref[...], preferred_element_type=F32)
    o_ref[...] = x + _rms(y, g5_ref[...])


def _mem_block(x, gains, w_q, w_o, mem_kv, l, tm, tiles_per_batch):
    n, d = x.shape
    m = mem_kv.shape[0] // (n // (tm * tiles_per_batch))
    return pl.pallas_call(
        _mem_body,
        grid=(n // tm,),
        in_specs=[pl.BlockSpec((tm, d), lambda i: (i, 0)), _layer_spec((1, d), l, 4), _layer_spec((1, d), l, 5),
                  _layer_spec((d, MEM_WIDTH), l), _layer_spec((MEM_WIDTH, d), l),
                  pl.BlockSpec((m, 2 * MEM_WIDTH), lambda i: (i // tiles_per_batch, 0))],
        out_specs=pl.BlockSpec((tm, d), lambda i: (i, 0)),
        out_shape=jax.ShapeDtypeStruct((n, d), F32),
        compiler_params=_cparams("parallel"),
        name="memattn",
    )(x, gains, gains, w_q, w_o, mem_kv)


NSA_TQ = 128
NSA_TK = 512


def _softmax_cols(s, valid):
    s = jnp.where(valid, s, NEG_INF)
    m = jnp.max(s, axis=0, keepdims=True)
    e = jnp.where(valid, jnp.exp(s - m), 0.0)
    return e / jnp.maximum(jnp.sum(e, axis=0, keepdims=True), 1e-30)


def _softmax_rows(s, valid):
    s = jnp.where(valid, s, NEG_INF)
    m = jnp.max(s, axis=1, keepdims=True)
    e = jnp.where(valid, jnp.exp(s - m), 0.0)
    return e / jnp.maximum(jnp.sum(e, axis=1, keepdims=True), 1e-30)


def _split3(x):
    hi = x.astype(BF16)
    r = x - hi.astype(F32)
    mid = r.astype(BF16)
    lo = (r - mid.astype(F32)).astype(BF16)
    return hi, mid, lo


def _select_topk(imp_t, blk, n_pick):
    n_blocks = imp_t.shape[0]
    sel = jnp.zeros(imp_t.shape, F32)
    work = imp_t
    for _ in range(n_pick):
        mx = jnp.max(work, axis=0, keepdims=True)
        first = jnp.min(jnp.where(work == mx, blk, float(n_blocks)), axis=0, keepdims=True)
        hit = blk == first
        sel = jnp.where(hit, 1.0, sel)
        work = jnp.where(hit, -3.0e38, work)
    return sel


def _transpose_01(x_t, eye):
    return lax.dot_general(eye, x_t.astype(BF16), (((1,), (1,)), ((), ())), preferred_element_type=F32)


def _nsa_body(q_ref, gt_ref, kaug_ref, vsel_ref, kvw_ref, kc_ref, vc_ref, covt_ref, o_ref, *, n_cmp, n_sel, topk):
    tq = q_ref.shape[0]
    n_blk = covt_ref.shape[0]
    n_cmp_pad = kc_ref.shape[0]
    g4 = NSA_GROUP
    i = pl.program_id(1)
    q0 = i * tq
    gates = jax.nn.sigmoid(gt_ref[...])

    qpos_row = q0 + lax.broadcasted_iota(jnp.int32, (1, tq), 1)
    qpos_row4 = jnp.concatenate([qpos_row] * g4, axis=1)
    qpos_col = q0 + lax.broadcasted_iota(jnp.int32, (tq, 1), 0)
    qpos_col4 = jnp.concatenate([qpos_col] * g4, axis=0)
    lane = lax.broadcasted_iota(jnp.int32, (tq, LANES), 1)
    blk = lax.broadcasted_iota(jnp.int32, (n_blk, 1), 0)
    blk_f = blk.astype(F32)
    qblk = qpos_row // SEL_BLOCK
    forced = (blk == 0) | (blk == qblk) | (blk == qblk - 1)
    future = blk > qblk
    exists = blk < n_sel
    eye = (lax.broadcasted_iota(jnp.int32, (tq, tq), 0) == lax.broadcasted_iota(jnp.int32, (tq, tq), 1)).astype(BF16)
    c_idx = lax.broadcasted_iota(jnp.int32, (n_cmp_pad, 1), 0)
    c_end = c_idx * CMP_STRIDE + (CMP_BLOCK - 1)
    cmp_valid = (c_end <= qpos_row4) & (c_idx < n_cmp)

    w_start = pl.multiple_of(jnp.maximum(q0 - WINDOW, 0), tq)
    w_len = WINDOW + tq
    kpos_w = w_start + lax.broadcasted_iota(jnp.int32, (1, w_len), 1)
    dw = qpos_col4 - kpos_w
    win_valid = (dw >= 0) & (dw < WINDOW)

    n_kt = (q0 + tq + NSA_TK - 1) // NSA_TK
    kc = kc_ref[...]
    vc = vc_ref[...]
    covt = covt_ref[...]
    for hk in range(NSA_KV_HEADS):
        own = (lane < HEAD_DIM) if hk == 0 else (lane >= HEAD_DIM)
        qpads = [q_ref[:, (hk * g4 + g) * LANES:(hk * g4 + g + 1) * LANES] for g in range(g4)]
        qpad4 = jnp.concatenate(qpads, axis=0)

        s_t = lax.dot_general(kc, qpad4, (((1,), (1,)), ((), ())), preferred_element_type=F32)
        p_t = _softmax_cols(s_t, cmp_valid)
        o_cmp = jnp.dot(p_t.T.astype(BF16), vc, preferred_element_type=F32)
        p_sum = p_t[:, 0:tq]
        for g in range(1, g4):
            p_sum = p_sum + p_t[:, g * tq:(g + 1) * tq]
        imp_t = sum(jnp.dot(covt, part, preferred_element_type=F32) for part in _split3(p_sum))
        imp_t = jnp.where(forced, FORCE_SCORE, jnp.where(future, -1.0, imp_t))
        imp_t = jnp.where(exists, imp_t, -2.0)
        sel_t = _select_topk(imp_t, blk_f, topk)
        selneg = jnp.where(_transpose_01(sel_t, eye) > 0.5, 0.0, NEG_INF)
        selneg2 = jnp.concatenate([selneg, selneg], axis=1).astype(BF16)

        lhs4 = jnp.concatenate([jnp.where(own, qp, selneg2) for qp in qpads], axis=0)

        def kt_body(kt, carry):
            m, l, acc = carry
            ks = pl.multiple_of(kt * NSA_TK, NSA_TK)
            ka = kaug_ref[pl.ds(ks, NSA_TK), hk * LANES:(hk + 1) * LANES]
            v = vsel_ref[pl.ds(ks, NSA_TK), :]
            s = lax.dot_general(lhs4, ka, (((1,), (1,)), ((), ())), preferred_element_type=F32)
            kpos = ks + lax.broadcasted_iota(jnp.int32, (1, NSA_TK), 1)
            s = jnp.where(kpos <= qpos_col4, s, NEG_INF)
            m_new = jnp.maximum(m, jnp.max(s, axis=1, keepdims=True))
            alpha = jnp.exp(m - m_new)
            p = jnp.exp(s - m_new)
            l = alpha * l + jnp.sum(p, axis=1, keepdims=True)
            acc = alpha * acc + jnp.dot(p.astype(BF16), v, preferred_element_type=F32)
            return m_new, l, acc

        init = (jnp.full((g4 * tq, 1), NEG_INF, F32), jnp.zeros((g4 * tq, 1), F32),
                jnp.zeros((g4 * tq, LANES), F32))
        _, l_sel, acc_sel = lax.fori_loop(0, n_kt, kt_body, init)
        o_sel = acc_sel / jnp.maximum(l_sel, 1e-30)

        kw = kvw_ref[pl.ds(w_start, w_len), 0:LANES]
        vw = kvw_ref[pl.ds(w_start, w_len), LANES:2 * LANES]
        s_w = lax.dot_general(qpad4, kw, (((1,), (1,)), ((), ())), preferred_element_type=F32)
        p_w = _softmax_rows(s_w, win_valid)
        o_win = jnp.dot(p_w.astype(BF16), vw, preferred_element_type=F32)

        def gate_col(j):
            cols = [gates[:, 3 * (hk * g4 + g) + j:3 * (hk * g4 + g) + j + 1] for g in range(g4)]
            return jnp.concatenate(cols, axis=0)

        o4 = gate_col(0) * o_cmp + gate_col(1) * o_sel + gate_col(2) * o_win
        for pair in range(g4 // 2):
            a = o4[(2 * pair) * tq:(2 * pair + 1) * tq]
            b = o4[(2 * pair + 1) * tq:(2 * pair + 2) * tq]
            if hk == 0:
                both = jnp.where(lane < HEAD_DIM, a, pltpu.roll(b, HEAD_DIM, 1))
            else:
                both = jnp.where(lane < HEAD_DIM, pltpu.roll(a, HEAD_DIM, 1), b)
            col = (hk * (g4 // 2) + pair) * LANES
            o_ref[:, col:col + LANES] = both


def _nsa_prompt(qpad, gates, kaug, vsel, kvwin, kc, vc, covt, batch, t_len, n_cmp, n_sel, topk):
    n = qpad.shape[0]
    nq = t_len // NSA_TQ
    n_cmp_pad = kc.shape[0] // batch
    qrow = lambda w: pl.BlockSpec((NSA_TQ, w), lambda b, i: (b * nq + i, 0))
    per_b = lambda r, w: pl.BlockSpec((r, w), lambda b, i: (b, 0))
    return pl.pallas_call(
        functools.partial(_nsa_body, n_cmp=n_cmp, n_sel=n_sel, topk=topk),
        grid=(batch, nq),
        in_specs=[qrow(QPAD_WIDTH), qrow(LANES), per_b(t_len, 2 * LANES), per_b(t_len, LANES),
                  per_b(t_len, 2 * LANES), per_b(n_cmp_pad, LANES), per_b(n_cmp_pad, LANES),
                  _const_spec(covt.shape)],
        out_specs=qrow(NSA_WIDTH),
        out_shape=jax.ShapeDtypeStruct((n, NSA_WIDTH), F32),
        compiler_params=_cparams("parallel", "parallel"),
        name="nsa_prompt",
    )(qpad, gates, kaug, vsel, kvwin, kc, vc, covt)


def _rope_tables(pos):
    half = ROPE_DIM // 2
    inv = ROPE_THETA ** (-2.0 * jnp.arange(half, dtype=F32) / ROPE_DIM)
    ang = pos.astype(F32)[:, None] * inv[None, :]
    cos, sin = jnp.cos(ang), jnp.sin(ang)
    n = pos.shape[0]
    one = jnp.ones((n, HEAD_DIM - ROPE_DIM), F32)
    zero = jnp.zeros((n, HEAD_DIM - ROPE_DIM), F32)
    zh = jnp.zeros((n, half), F32)
    c = jnp.concatenate([cos, cos, one], axis=1)
    sm = jnp.concatenate([-sin, zh, zero], axis=1)
    sp = jnp.concatenate([zh, sin, zero], axis=1)
    return tuple(jnp.concatenate([t, t], axis=1) for t in (c, sm, sp))


def _block_onehot(pos):
    e = (pos[:, None] // SEL_BLOCK == jnp.arange(HEAD_DIM)[None, :]).astype(F32)
    return jnp.concatenate([e, e], axis=1)


def _rope_jnp(x, pos):
    half = ROPE_DIM // 2
    inv = ROPE_THETA ** (-2.0 * jnp.arange(half, dtype=F32) / ROPE_DIM)
    ang = pos.astype(F32)[:, None] * inv[None, :]
    cos = jnp.cos(ang)[None, :, None, :]
    sin = jnp.sin(ang)[None, :, None, :]
    x1 = x[..., :half]
    x2 = x[..., half:ROPE_DIM]
    return jnp.concatenate([x1 * cos - x2 * sin, x2 * cos + x1 * sin, x[..., ROPE_DIM:]], axis=-1)


def _masked_softmax(s, mask):
    s = jnp.where(mask, s, NEG_INF)
    m = jnp.max(s, axis=-1, keepdims=True)
    e = jnp.where(mask, jnp.exp(s - m), 0.0)
    return e / jnp.maximum(jnp.sum(e, axis=-1, keepdims=True), 1e-30)


def _compress_jnp(k, pe, w1, w2):
    b, t, hk, dh = k.shape
    n_cmp = (t - CMP_BLOCK) // CMP_STRIDE + 1
    start = jnp.arange(n_cmp) * CMP_STRIDE
    idx = start[:, None] + jnp.arange(CMP_BLOCK)[None, :]
    blk = k[:, idx] + pe[None, None, :, None, :]
    blk = jnp.swapaxes(blk, 2, 3).reshape(b, n_cmp, hk, CMP_BLOCK * dh)
    return jax.nn.gelu(blk @ w1) @ w2, start + CMP_BLOCK - 1


def _block_coverage(c_start, n_sel):
    b_start = jnp.arange(n_sel) * SEL_BLOCK
    lo = jnp.maximum(c_start[:, None], b_start[None, :])
    hi = jnp.minimum(c_start[:, None] + CMP_BLOCK, b_start[None, :] + SEL_BLOCK)
    return jnp.clip(hi - lo, 0, None).astype(F32) / CMP_BLOCK


def _sample_nsa_jnp(q, gate_logits, rows_new, win_new, nsa_past, win_buf, w_buf, cmp_pe, w_cmp1, w_cmp2, topk):
    b, t = q.shape[:2]
    pos0 = nsa_past.shape[1]
    qpos = pos0 + jnp.arange(t)
    q = q.reshape(b, t, NSA_KV_HEADS, NSA_GROUP, HEAD_DIM)
    rows = jnp.concatenate([nsa_past, rows_new], axis=1)
    t_kv = rows.shape[1]
    kc, c_end = _compress_jnp(rows[:, :, 0], cmp_pe[0], w_cmp1[0], w_cmp2[0])
    vc, _ = _compress_jnp(rows[:, :, 1], cmp_pe[1], w_cmp1[1], w_cmp2[1])
    kc = _rope_jnp(kc, c_end)
    n_sel = -(-t_kv // SEL_BLOCK)
    cov = _block_coverage(c_end - CMP_BLOCK + 1, n_sel)
    sel = jnp.pad(rows[:, :, 2:], ((0, 0), (0, n_sel * SEL_BLOCK - t_kv), (0, 0), (0, 0), (0, 0)))
    sel = sel.reshape(b, n_sel, SEL_BLOCK, 2, NSA_KV_HEADS, HEAD_DIM).transpose(3, 0, 4, 1, 2, 5)
    k_blk, v_blk = sel[0], sel[1]
    win_all = jnp.concatenate([win_buf, win_new], axis=1)
    n_buf = win_buf.shape[1]
    gates = jax.nn.sigmoid(gate_logits).reshape(b, t, NSA_KV_HEADS, NSA_GROUP, 3)
    s = jnp.einsum('bqhgd,bchd->bhgqc', q, kc) * ATTN_SCALE
    p = _masked_softmax(s, c_end[None, :] <= qpos[:, None])
    o_cmp = jnp.einsum('bhgqc,bchd->bqhgd', p, vc)
    imp = jnp.einsum('bhgqc,cn->bhqn', p, cov)
    blk = jnp.arange(n_sel)[None, :]
    q_blk = (qpos // SEL_BLOCK)[:, None]
    forced = (blk == 0) | (blk == q_blk) | (blk == q_blk - 1)
    imp = jnp.where(forced, FORCE_SCORE, jnp.where(blk > q_blk, -1.0, imp))
    _, idx = lax.top_k(imp, min(topk, n_sel))
    bi = jnp.arange(b)[:, None, None, None]
    hi = jnp.arange(NSA_KV_HEADS)[None, :, None, None]
    kg = k_blk[bi, hi, idx]
    vg = v_blk[bi, hi, idx]
    k_pos = idx[..., None] * SEL_BLOCK + jnp.arange(SEL_BLOCK)
    mask = (k_pos <= qpos[None, None, :, None, None])[:, :, None]
    s2 = jnp.einsum('bqhgd,bhqksd->bhgqks', q, kg) * ATTN_SCALE
    shp = s2.shape
    p2 = _masked_softmax(s2.reshape(shp[:-2] + (-1,)), mask.reshape(mask.shape[:-2] + (-1,))).reshape(shp)
    o_sel = jnp.einsum('bhgqks,bhqksd->bqhgd', p2, vg)
    kw_pos = pos0 - n_buf + jnp.arange(n_buf + t)
    sw = jnp.einsum('bqhgd,bkhd->bhgqk', q, win_all[:, :, 0]) * ATTN_SCALE
    diff = qpos[:, None] - kw_pos[None, :]
    pw = _masked_softmax(sw, (diff >= 0) & (diff < WINDOW) & (kw_pos[None, :] >= 0))
    o_win = jnp.einsum('bhgqk,bkhd->bqhgd', pw, win_all[:, :, 1])
    o = gates[..., 0:1] * o_cmp + gates[..., 1:2] * o_sel + gates[..., 2:3] * o_win
    return o.reshape(b, t, NSA_WIDTH), win_all[:, win_all.shape[1] - w_buf:]


def _to_chunks(a, c, pad):
    a = jnp.moveaxis(a, 1, 2)
    if pad:
        a = jnp.pad(a, [(0, 0), (0, 0), (0, pad)] + [(0, 0)] * (a.ndim - 3))
    b, h, tp = a.shape[:3]
    a = a.reshape((b, h, tp // c, c) + a.shape[3:])
    return jnp.moveaxis(a, 2, 0)


def _chunk_gated_delta_jnp(q, k, v, g, beta, s0):
    b, t, h, _ = q.shape
    c = min(GDN_CHUNK, t)
    pad = (-t) % c
    qc, kc, vc = _to_chunks(q, c, pad), _to_chunks(k, c, pad), _to_chunks(v, c, pad)
    gc = jnp.cumsum(_to_chunks(g, c, pad), axis=-1)
    bc = _to_chunks(beta, c, pad)
    ii = jnp.arange(c)
    incl = ii[:, None] >= ii[None, :]
    strict = ii[:, None] > ii[None, :]
    diff = gc[..., :, None] - gc[..., None, :]
    decay = jnp.where(incl, jnp.exp(jnp.where(incl, diff, 0.0)), 0.0)
    kb = kc * bc[..., None]
    a = jnp.where(strict, jnp.einsum('...id,...jd->...ij', kb, kc) * decay, 0.0)
    eye = jnp.eye(c, dtype=F32)
    tm = lax.linalg.triangular_solve(eye + a, jnp.broadcast_to(eye, a.shape), left_side=True, lower=True)
    u = jnp.einsum('...ij,...jd->...id', tm, vc * bc[..., None])
    w = jnp.einsum('...ij,...jd->...id', tm, kb * jnp.exp(gc)[..., None])

    def step(s, inp):
        qi, ki, ui, wi, gi, di = inp
        v_new = ui - jnp.einsum('bhck,bhkv->bhcv', wi, s)
        att = jnp.einsum('bhik,bhjk->bhij', qi, ki) * di
        o = jnp.einsum('bhck,bhkv->bhcv', qi * jnp.exp(gi)[..., None], s) + jnp.einsum('bhij,bhjv->bhiv', att, v_new)
        gl = gi[..., -1]
        s = s * jnp.exp(gl)[..., None, None] + jnp.einsum(
            'bhck,bhcv->bhkv', ki * jnp.exp(gl[..., None] - gi)[..., None], v_new)
        return s, o

    s, o = lax.scan(step, s0, (qc, kc, u, w, gc, decay))
    o = jnp.moveaxis(o, 0, 2).reshape(b, h, -1, o.shape[-1])[:, :, :t]
    return jnp.moveaxis(o, 1, 2), s


def _gdn_jnp(qkv, a, bb, z, conv_buf, state, conv_w, a_log, dt_bias, norm_w):
    b, t, _ = qkv.shape
    xp = jnp.concatenate([conv_buf, qkv], axis=1)
    c = xp[:, 0:t] * conv_w[0]
    for j in range(1, CONV_W):
        c = c + xp[:, j:j + t] * conv_w[j]
    c = jax.nn.silu(c)
    new_buf = xp[:, t:]
    qk_w = GDN_HEADS * GDN_DK
    l2 = lambda x: x * lax.rsqrt(jnp.sum(x * x, axis=-1, keepdims=True) + 1e-6)
    q = l2(c[..., :qk_w].reshape(b, t, GDN_HEADS, GDN_DK)) * GDN_DK ** -0.5
    k = l2(c[..., qk_w:2 * qk_w].reshape(b, t, GDN_HEADS, GDN_DK))
    v = c[..., 2 * qk_w:].reshape(b, t, GDN_HEADS, GDN_DV)
    g = -jnp.exp(a_log) * jax.nn.softplus(a + dt_bias)
    beta = jax.nn.sigmoid(bb)
    o, new_state = _chunk_gated_delta_jnp(q, k, v, g, beta, state)
    o = o * lax.rsqrt(jnp.mean(o * o, axis=-1, keepdims=True) + RMS_EPS) * norm_w
    o = o * jax.nn.silu(z).reshape(b, t, GDN_HEADS, GDN_DV)
    return o.reshape(b, t, GDN_WIDTH), new_buf, new_state


def _mem_attend_sample_jnp(x, gains, mem_kv, w_q, w_o):
    xf = x
    h = xf * lax.rsqrt(jnp.mean(xf * xf, axis=-1, keepdims=True) + RMS_EPS) * gains[4]
    q = (h @ w_q).reshape(-1, MEM_HEADS, MEM_HD)
    s = jnp.einsum('bhd,bmhd->bhm', q, mem_kv[:, :, 0]) * MEM_HD ** -0.5
    p = jax.nn.softmax(s, axis=-1)
    o = jnp.einsum('bhm,bmhd->bhd', p, mem_kv[:, :, 1]).reshape(-1, MEM_WIDTH)
    y = o @ w_o
    return x + y * lax.rsqrt(jnp.mean(y * y, axis=-1, keepdims=True) + RMS_EPS) * gains[5]


def _pack_w_in(w_in):
    nl, d, _ = w_in.shape
    o_q, o_kv, o_g = NSA_WIDTH, NSA_WIDTH + KV6_WIDTH, NSA_WIDTH + KV6_WIDTH + 3 * NSA_HEADS
    o_qkv = o_g
    o_a = o_qkv + GDN_QKV
    o_b = o_a + GDN_HEADS
    o_z = o_b + GDN_HEADS
    wq = w_in[:, :, :o_q].reshape(nl, d, NSA_HEADS, HEAD_DIM)
    zq = jnp.zeros_like(wq)
    first = jnp.concatenate([wq, zq], axis=-1)
    second = jnp.concatenate([zq, wq], axis=-1)
    kv_of_head = (jnp.arange(NSA_HEADS) // NSA_GROUP)[None, None, :, None]
    wq_pad = jnp.where(kv_of_head == 0, first, second).reshape(nl, d, QPAD_WIDTH)
    gate_grp = jnp.concatenate([w_in[:, :, o_kv:o_g], w_in[:, :, o_a:o_z],
                                jnp.zeros((nl, d, LANES - 3 * NSA_HEADS - 2 * GDN_HEADS), w_in.dtype)], axis=-1)
    packed = jnp.concatenate([wq_pad, w_in[:, :, o_q:o_kv], w_in[:, :, o_qkv:o_a], w_in[:, :, o_z:], gate_grp], axis=-1)
    return packed.astype(BF16)


def kernel(x_prompt, x_sample, cache_nsa_kv, cache_win_kv, state_gdn_S, state_gdn_conv, cache_mem_kv, page_table, mem_prompt, norm_gains, mem_norm, w_ffn_gu, w_ffn_down, w_in, w_out, cmp_pe, w_cmp1, w_cmp2, gdn_conv, gdn_A_log, gdn_dt_bias, gdn_norm, w_mem_q, w_mem_kv, w_mem_o):
    bp, t_len, d = x_prompt.shape
    bs = x_sample.shape[0]
    depth = w_in.shape[0]
    w_buf = cache_win_kv.shape[2]
    n_pages = page_table.shape[1]
    page = cache_nsa_kv.shape[2]
    past_len = n_pages * page
    n_mem = mem_prompt.shape[1]
    np_tok = bp * t_len

    gains = norm_gains.reshape(depth, 8, 1, d)
    wgu = w_ffn_gu.astype(BF16)
    wd = w_ffn_down.astype(BF16)
    w_in_p = _pack_w_in(w_in)
    w_out_b = w_out.astype(BF16)
    w_mq = w_mem_q.astype(BF16)
    w_mkv = w_mem_kv.astype(BF16)
    w_mo = w_mem_o.astype(BF16)
    mem_gain = mem_norm.reshape(depth, 1, d)

    pos_p = jnp.arange(t_len)
    tabs_p = _rope_tables(pos_p) + (_block_onehot(pos_p),)
    pos_s = jnp.full((bs,), past_len, jnp.int32)
    tabs_s = _rope_tables(pos_s) + (jnp.zeros((bs, LANES), F32),)

    n_cmp = (t_len - CMP_BLOCK) // CMP_STRIDE + 1
    n_cmp_pad = -(-n_cmp // LANES) * LANES
    n_sel = -(-t_len // SEL_BLOCK)
    c_start = jnp.arange(n_cmp) * CMP_STRIDE
    assert n_sel <= HEAD_DIM and t_len >= WINDOW + NSA_TQ and t_len % NSA_TK == 0
    covt = jnp.pad(_block_coverage(c_start, n_sel).T, ((0, HEAD_DIM - n_sel), (0, n_cmp_pad - n_cmp))).astype(BF16)
    topk = min(SEL_TOPK, n_sel)

    tm_p = 512
    yp = x_prompt.reshape(np_tok, d)
    ys = x_sample.reshape(bs, d)
    mem_flat = mem_prompt.reshape(bp * n_mem, d)
    outs = [[] for _ in range(9)]
    for l in range(depth):
        mem_kv_p = _normmm(mem_flat, mem_gain, w_mkv, l, n_mem)
        yp = _ffn(yp, gains, wgu, wd, l, 0, tm_p)
        qpad, rows, win, kaug, vsel, kvwin, gates, qkv, z = _inproj(yp, gains, w_in_p, tabs_p, l, tm_p)
        rows5 = rows.reshape(bp, t_len, 4, NSA_KV_HEADS, HEAD_DIM)
        kc, c_end = _compress_jnp(rows5[:, :, 0], cmp_pe[l, 0], w_cmp1[l, 0], w_cmp2[l, 0])
        vc, _ = _compress_jnp(rows5[:, :, 1], cmp_pe[l, 1], w_cmp1[l, 1], w_cmp2[l, 1])
        kc = _rope_jnp(kc, c_end)
        padc = lambda a: jnp.pad(a.reshape(bp, n_cmp, LANES), ((0, 0), (0, n_cmp_pad - n_cmp), (0, 0))).reshape(
            bp * n_cmp_pad, LANES).astype(BF16)
        o_nsa = _nsa_prompt(qpad, gates, kaug, vsel, kvwin, padc(kc), padc(vc), covt, bp, t_len, n_cmp, n_sel, topk)
        o_gdn, conv_p, s_p = _gdn_jnp(
            qkv.reshape(bp, t_len, GDN_QKV), gates[:, GATE_A:GATE_B].reshape(bp, t_len, GDN_HEADS),
            gates[:, GATE_B:GATE_B + GDN_HEADS].reshape(bp, t_len, GDN_HEADS), z.reshape(bp, t_len, GDN_WIDTH),
            jnp.zeros((bp, CONV_W - 1, GDN_QKV), F32), jnp.zeros((bp, GDN_HEADS, GDN_DK, GDN_DV), F32),
            gdn_conv[l], gdn_A_log[l], gdn_dt_bias[l], gdn_norm[l])
        yp = _outproj(yp, o_nsa, o_gdn.reshape(np_tok, GDN_WIDTH), w_out_b, gains, l, tm_p)
        yp = _mem_block(yp, gains, w_mq, w_mo, mem_kv_p, l, tm_p, t_len // tm_p)
        yp = _ffn(yp, gains, wgu, wd, l, 1, tm_p)
        win5 = win.reshape(bp, t_len, 2, NSA_KV_HEADS, HEAD_DIM)
        win_p = win5[:, t_len - w_buf:] if t_len >= w_buf else jnp.pad(
            win5, ((0, 0), (w_buf - t_len, 0), (0, 0), (0, 0), (0, 0)))

        ys = _ffn(ys, gains, wgu, wd, l, 0, bs)
        qpad_s, rows_s, win_s, _, _, _, gates_s, qkv_s, z_s = _inproj(ys, gains, w_in_p, tabs_s, l, bs)
        q_s = qpad_s.astype(F32).reshape(bs, NSA_HEADS, 2, HEAD_DIM) * (1.0 / ATTN_SCALE)
        q_s = jnp.concatenate([q_s[:, :NSA_GROUP, 0], q_s[:, NSA_GROUP:, 1]], axis=1)
        past = cache_nsa_kv[l][page_table].reshape(bs, past_len, 4, NSA_KV_HEADS, HEAD_DIM)
        o_nsa_s, wbuf_s = _sample_nsa_jnp(
            q_s.reshape(bs, 1, NSA_HEADS, HEAD_DIM), gates_s[:, :GATE_A].reshape(bs, 1, NSA_HEADS, 3),
            rows_s.reshape(bs, 1, 4, NSA_KV_HEADS, HEAD_DIM), win_s.reshape(bs, 1, 2, NSA_KV_HEADS, HEAD_DIM),
            past, cache_win_kv[l], w_buf, cmp_pe[l], w_cmp1[l], w_cmp2[l], SEL_TOPK)
        o_gdn_s, conv_s, s_s = _gdn_jnp(
            qkv_s.reshape(bs, 1, GDN_QKV), gates_s[:, GATE_A:GATE_B].reshape(bs, 1, GDN_HEADS),
            gates_s[:, GATE_B:GATE_B + GDN_HEADS].reshape(bs, 1, GDN_HEADS), z_s.reshape(bs, 1, GDN_WIDTH),
            state_gdn_conv[l], state_gdn_S[l], gdn_conv[l], gdn_A_log[l], gdn_dt_bias[l], gdn_norm[l])
        ys = _outproj(ys, o_nsa_s.reshape(bs, NSA_WIDTH), o_gdn_s.reshape(bs, GDN_WIDTH), w_out_b, gains, l, bs)
        ys = _mem_attend_sample_jnp(ys, norm_gains[l], cache_mem_kv[l], w_mem_q[l], w_mem_o[l])
        ys = _ffn(ys, gains, wgu, wd, l, 1, bs)

        for lst, val in zip(outs, (rows5, win_p, s_p, conv_p, mem_kv_p.reshape(bp, n_mem, 2, MEM_HEADS, MEM_HD),
                                   rows_s.reshape(bs, 1, 4, NSA_KV_HEADS, HEAD_DIM), wbuf_s, s_s, conv_s)):
            lst.append(val)
    return (yp.reshape(bp, t_len, d), ys.reshape(bs, 1, d)) + tuple(jnp.stack(v) for v in outs)
```

```python
import functools
import math

import numpy as np
import jax
import jax.numpy as jnp
from jax import lax
from jax.experimental import pallas as pl
from jax.experimental.pallas import tpu as pltpu

F32 = jnp.float32
BF16 = jnp.bfloat16

HEAD_DIM = 64
NSA_HEADS = 8
NSA_KV_HEADS = 2
NSA_GROUP = NSA_HEADS // NSA_KV_HEADS
NSA_WIDTH = NSA_HEADS * HEAD_DIM
ROPE_DIM = HEAD_DIM // 4
ROPE_THETA = 500000.0
ATTN_SCALE = HEAD_DIM ** -0.5
CMP_BLOCK = 32
CMP_STRIDE = 16
CMP_HIDDEN = 2 * HEAD_DIM
SEL_BLOCK = 64
SEL_TOPK = 16
WINDOW = 512
FORCE_SCORE = 1.0e4
GDN_DK = 128
GDN_DV = 128
GDN_HEADS = 4
GDN_WIDTH = GDN_HEADS * GDN_DV
GDN_QKV = GDN_HEADS * (2 * GDN_DK + GDN_DV)
CONV_W = 4
GDN_CHUNK = 64
MEM_HEADS = 4
MEM_HD = 64
MEM_WIDTH = MEM_HEADS * MEM_HD
RMS_EPS = 1e-6
NEG_INF = -1.0e30
KV6_WIDTH = 6 * NSA_KV_HEADS * HEAD_DIM

LANES = 128
VMEM_LIMIT_BYTES = 56 * 1024 * 1024

QPAD_WIDTH = NSA_HEADS * LANES
COL_Q = 0
COL_KV = COL_Q + QPAD_WIDTH
COL_QKV = COL_KV + KV6_WIDTH
COL_Z = COL_QKV + GDN_QKV
COL_GATE = COL_Z + GDN_WIDTH
IN_PACKED = COL_GATE + LANES
GATE_A = 3 * NSA_HEADS
GATE_B = GATE_A + GDN_HEADS


def _cparams(*sem):
    return pltpu.CompilerParams(dimension_semantics=sem, vmem_limit_bytes=VMEM_LIMIT_BYTES)


def _rms(x, w):
    return x * lax.rsqrt(jnp.mean(x * x, axis=-1, keepdims=True) + RMS_EPS) * w


def _const_spec(shape):
    nd = len(shape)
    return pl.BlockSpec(shape, lambda *_: (0,) * nd)


def _layer_spec(shape, *lead):
    nlead = len(lead)
    nd = len(shape)
    return pl.BlockSpec((None,) * nlead + tuple(shape), lambda *_: tuple(lead) + (0,) * nd)


FFN_CHUNK = 256


def _ffn_body(x_ref, g0_ref, g1_ref, wgu_ref, wd_ref, o_ref, acc_ref, *, d_ff):
    x = x_ref[...]
    h = _rms(x, g0_ref[...]).astype(BF16)
    for f in range(d_ff // FFN_CHUNK):
        lo = f * FFN_CHUNK
        g = jnp.dot(h, wgu_ref[:, lo:lo + FFN_CHUNK], preferred_element_type=F32)
        u = jnp.dot(h, wgu_ref[:, d_ff + lo:d_ff + lo + FFN_CHUNK], preferred_element_type=F32)
        a = (g * jax.nn.sigmoid(g) * u).astype(BF16)
        d = jnp.dot(a, wd_ref[lo:lo + FFN_CHUNK, :], preferred_element_type=F32)
        if f == 0:
            acc_ref[...] = d
        else:
            acc_ref[...] += d
    o_ref[...] = x + 0.5 * _rms(acc_ref[...], g1_ref[...])


def _ffn(x, gains, wgu, wd, l, j, tm):
    n, d = x.shape
    d_ff = wd.shape[2]
    return pl.pallas_call(
        functools.partial(_ffn_body, d_ff=d_ff),
        grid=(n // tm,),
        in_specs=[
            pl.BlockSpec((tm, d), lambda i: (i, 0)),
            _layer_spec((1, d), l, 2 * j * 3),
            _layer_spec((1, d), l, 2 * j * 3 + 1),
            _layer_spec((d, 2 * d_ff), l, j),
            _layer_spec((d_ff, d), l, j),
        ],
        out_specs=pl.BlockSpec((tm, d), lambda i: (i, 0)),
        out_shape=jax.ShapeDtypeStruct((n, d), F32),
        scratch_shapes=[pltpu.VMEM((tm, d), F32)],
        compiler_params=_cparams("parallel"),
        name=f"ffn{j}",
    )(x, gains, gains, wgu, wd)


def _rope(v, c1, sm1, sp1):
    n = v.shape[1] // LANES
    c, sm, sp = (t if n == 1 else jnp.concatenate([t] * n, axis=1) for t in (c1, sm1, sp1))
    w = v.shape[1]
    return v * c + pltpu.roll(v, w - ROPE_DIM // 2, 1) * sm + pltpu.roll(v, ROPE_DIM // 2, 1) * sp


def _inproj_body(x_ref, g_ref, w_ref, c_ref, sm_ref, sp_ref, e_ref,
                 qpad_ref, rows_ref, win_ref, kaug_ref, vsel_ref, kvwin_ref, gates_ref, qkv_ref, z_ref):
    h = _rms(x_ref[...], g_ref[...]).astype(BF16)
    c1, sm1, sp1 = c_ref[...], sm_ref[...], sp_ref[...]

    def mm(lo, hi):
        return jnp.dot(h, w_ref[:, lo:hi], preferred_element_type=F32)

    q = _rope(mm(COL_Q, COL_KV), c1, sm1, sp1)
    qpad_ref[...] = (q * ATTN_SCALE).astype(BF16)
    kv = mm(COL_KV, COL_QKV)
    ksel = _rope(kv[:, 2 * LANES:3 * LANES], c1, sm1, sp1)
    vsel = kv[:, 3 * LANES:4 * LANES]
    kwin = _rope(kv[:, 4 * LANES:5 * LANES], c1, sm1, sp1)
    vwin = kv[:, 5 * LANES:6 * LANES]
    rows_ref[:, 0:2 * LANES] = kv[:, 0:2 * LANES]
    rows_ref[:, 2 * LANES:3 * LANES] = ksel
    rows_ref[:, 3 * LANES:4 * LANES] = vsel
    win_ref[:, 0:LANES] = kwin
    win_ref[:, LANES:2 * LANES] = vwin
    e2 = e_ref[...]
    lane = lax.broadcasted_iota(jnp.int32, ksel.shape, 1)
    first = lane < HEAD_DIM
    kaug_ref[:, 0:LANES] = jnp.where(first, ksel, e2).astype(BF16)
    kaug_ref[:, LANES:2 * LANES] = jnp.where(first, e2, ksel).astype(BF16)
    vsel_ref[...] = vsel.astype(BF16)
    kvwin_ref[:, 0:LANES] = kwin.astype(BF16)
    kvwin_ref[:, LANES:2 * LANES] = vwin.astype(BF16)
    qkv_ref[...] = mm(COL_QKV, COL_Z)
    z_ref[...] = mm(COL_Z, COL_GATE)
    gates_ref[...] = mm(COL_GATE, IN_PACKED)


def _inproj(x, gains, w_in_p, tabs, l, tm):
    n, d = x.shape
    c_t, sm_t, sp_t, e_t = tabs
    nt = c_t.shape[0] // tm
    tab = lambda w: pl.BlockSpec((tm, w), lambda i: (i % nt, 0))
    row = lambda w: pl.BlockSpec((tm, w), lambda i: (i, 0))
    widths = [(QPAD_WIDTH, BF16), (4 * LANES, F32), (2 * LANES, F32), (2 * LANES, BF16), (LANES, BF16),
              (2 * LANES, BF16), (LANES, F32), (GDN_QKV, F32), (GDN_WIDTH, F32)]
    return pl.pallas_call(
        _inproj_body,
        grid=(n // tm,),
        in_specs=[row(d), _layer_spec((1, d), l, 2), _layer_spec((d, IN_PACKED), l),
                  tab(LANES), tab(LANES), tab(LANES), tab(LANES)],
        out_specs=[row(w) for w, _ in widths],
        out_shape=[jax.ShapeDtypeStruct((n, w), dt) for w, dt in widths],
        compiler_params=_cparams("parallel"),
        name="inproj",
    )(x, gains, w_in_p, c_t, sm_t, sp_t, e_t)


def _normmm_body(x_ref, g_ref, w_ref, o_ref):
    h = _rms(x_ref[...], g_ref[...]).astype(BF16)
    o_ref[...] = jnp.dot(h, w_ref[...], preferred_element_type=F32)


def _normmm(x, gain, w, l, tm):
    n, d = x.shape
    nout = w.shape[-1]
    return pl.pallas_call(
        _normmm_body,
        grid=(n // tm,),
        in_specs=[pl.BlockSpec((tm, d), lambda i: (i, 0)), _layer_spec((1, d), l), _layer_spec((d, nout), l)],
        out_specs=pl.BlockSpec((tm, nout), lambda i: (i, 0)),
        out_shape=jax.ShapeDtypeStruct((n, nout), F32),
        compiler_params=_cparams("parallel"),
        name="memkv",
    )(x, gain, w)


def _outproj_body(x_ref, a1_ref, a2_ref, w_ref, g_ref, o_ref):
    k1 = a1_ref.shape[1]
    acc = jnp.dot(a1_ref[...].astype(BF16), w_ref[0:k1, :], preferred_element_type=F32)
    acc = acc + jnp.dot(a2_ref[...].astype(BF16), w_ref[k1:, :], preferred_element_type=F32)
    o_ref[...] = x_ref[...] + _rms(acc, g_ref[...])


def _outproj(x, a1, a2, w_out, gains, l, tm):
    n, d = x.shape
    row = lambda w: pl.BlockSpec((tm, w), lambda i: (i, 0))
    return pl.pallas_call(
        _outproj_body,
        grid=(n // tm,),
        in_specs=[row(d), row(a1.shape[1]), row(a2.shape[1]),
                  _layer_spec((a1.shape[1] + a2.shape[1], d), l), _layer_spec((1, d), l, 3)],
        out_specs=row(d),
        out_shape=jax.ShapeDtypeStruct((n, d), F32),
        compiler_params=_cparams("parallel"),
        name="outproj",
    )(x, a1, a2, w_out, gains)


def _mem_body(x_ref, g4_ref, g5_ref, wq_ref, wo_ref, kv_ref, o_ref):
    x = x_ref[...]
    h = _rms(x, g4_ref[...]).astype(BF16)
    q = jnp.dot(h, wq_ref[...], preferred_element_type=F32) * (MEM_HD ** -0.5)
    kv = kv_ref[...].astype(BF16)
    outs = []
    for hd in range(MEM_HEADS):
        qh = q[:, hd * MEM_HD:(hd + 1) * MEM_HD].astype(BF16)
        kh = kv[:, hd * MEM_HD:(hd + 1) * MEM_HD]
        vh = kv[:, MEM_WIDTH + hd * MEM_HD:MEM_WIDTH + (hd + 1) * MEM_HD]
        s = lax.dot_general(qh, kh, (((1,), (1,)), ((), ())), preferred_element_type=F32)
        e = jnp.exp(s - jnp.max(s, axis=-1, keepdims=True))
        p = e / jnp.sum(e, axis=-1, keepdims=True)
        outs.append(jnp.dot(p.astype(BF16), vh, preferred_element_type=F32))
    o = jnp.concatenate(outs, axis=1).astype(BF16)
    y = jnp.dot(o, wo_ref[...], preferred_element_type=F32)
    o_ref[...] = x + _rms(y, g5_ref[...])


def _mem_block(x, gains, w_q, w_o, mem_kv, l, tm, tiles_per_batch):
    n, d = x.shape
    m = mem_kv.shape[0] // (n // (tm * tiles_per_batch))
    return pl.pallas_call(
        _mem_body,
        grid=(n // tm,),
        in_specs=[pl.BlockSpec((tm, d), lambda i: (i, 0)), _layer_spec((1, d), l, 4), _layer_spec((1, d), l, 5),
                  _layer_spec((d, MEM_WIDTH), l), _layer_spec((MEM_WIDTH, d), l),
                  pl.BlockSpec((m, 2 * MEM_WIDTH), lambda i: (i // tiles_per_batch, 0))],
        out_specs=pl.BlockSpec((tm, d), lambda i: (i, 0)),
        out_shape=jax.ShapeDtypeStruct((n, d), F32),
        compiler_params=_cparams("parallel"),
        name="memattn",
    )(x, gains, gains, w_q, w_o, mem_kv)


NSA_TQ = 128
NSA_TK = 512


def _softmax_cols(s, valid):
    s = jnp.where(valid, s, NEG_INF)
    m = jnp.max(s, axis=0, keepdims=True)
    e = jnp.where(valid, jnp.exp(s - m), 0.0)
    return e / jnp.maximum(jnp.sum(e, axis=0, keepdims=True), 1e-30)


def _softmax_rows(s, valid):
    s = jnp.where(valid, s, NEG_INF)
    m = jnp.max(s, axis=1, keepdims=True)
    e = jnp.where(valid, jnp.exp(s - m), 0.0)
    return e / jnp.maximum(jnp.sum(e, axis=1, keepdims=True), 1e-30)


def _split3(x):
    hi = x.astype(BF16)
    r = x - hi.astype(F32)
    mid = r.astype(BF16)
    lo = (r - mid.astype(F32)).astype(BF16)
    return hi, mid, lo


def _select_topk(imp_t, blk, n_pick):
    n_blocks = imp_t.shape[0]
    sel = jnp.zeros(imp_t.shape, F32)
    work = imp_t
    for _ in range(n_pick):
        mx = jnp.max(work, axis=0, keepdims=True)
        first = jnp.min(jnp.where(work == mx, blk, float(n_blocks)), axis=0, keepdims=True)
        hit = blk == first
        sel = jnp.where(hit, 1.0, sel)
        work = jnp.where(hit, -3.0e38, work)
    return sel


def _transpose_01(x_t, eye):
    return lax.dot_general(eye, x_t.astype(BF16), (((1,), (1,)), ((), ())), preferred_element_type=F32)


def _nsa_body(q_ref, gt_ref, kaug_ref, vsel_ref, kvw_ref, kc_ref, vc_ref, covt_ref, o_ref, *, n_cmp, n_sel, topk):
    tq = q_ref.shape[0]
    n_blk = covt_ref.shape[0]
    n_cmp_pad = kc_ref.shape[0]
    g4 = NSA_GROUP
    i = pl.program_id(1)
    q0 = i * tq
    gates = jax.nn.sigmoid(gt_ref[...])

    qpos_row = q0 + lax.broadcasted_iota(jnp.int32, (1, tq), 1)
    qpos_row4 = jnp.concatenate([qpos_row] * g4, axis=1)
    qpos_col = q0 + lax.broadcasted_iota(jnp.int32, (tq, 1), 0)
    qpos_col4 = jnp.concatenate([qpos_col] * g4, axis=0)
    lane = lax.broadcasted_iota(jnp.int32, (tq, LANES), 1)
    blk = lax.broadcasted_iota(jnp.int32, (n_blk, 1), 0)
    blk_f = blk.astype(F32)
    qblk = qpos_row // SEL_BLOCK
    forced = (blk == 0) | (blk == qblk) | (blk == qblk - 1)
    future = blk > qblk
    exists = blk < n_sel
    eye = (lax.broadcasted_iota(jnp.int32, (tq, tq), 0) == lax.broadcasted_iota(jnp.int32, (tq, tq), 1)).astype(BF16)
    c_idx = lax.broadcasted_iota(jnp.int32, (n_cmp_pad, 1), 0)
    c_end = c_idx * CMP_STRIDE + (CMP_BLOCK - 1)
    cmp_valid = (c_end <= qpos_row4) & (c_idx < n_cmp)

    w_start = pl.multiple_of(jnp.maximum(q0 - WINDOW, 0), tq)
    w_len = WINDOW + tq
    kpos_w = w_start + lax.broadcasted_iota(jnp.int32, (1, w_len), 1)
    dw = qpos_col4 - kpos_w
    win_valid = (dw >= 0) & (dw < WINDOW)

    n_kt = (q0 + tq + NSA_TK - 1) // NSA_TK
    kc = kc_ref[...]
    vc = vc_ref[...]
    covt = covt_ref[...]
    for hk in range(NSA_KV_HEADS):
        own = (lane < HEAD_DIM) if hk == 0 else (lane >= HEAD_DIM)
        qpads = [q_ref[:, (hk * g4 + g) * LANES:(hk * g4 + g + 1) * LANES] for g in range(g4)]
        qpad4 = jnp.concatenate(qpads, axis=0)

        s_t = lax.dot_general(kc, qpad4, (((1,), (1,)), ((), ())), preferred_element_type=F32)
        p_t = _softmax_cols(s_t, cmp_valid)
        o_cmp = jnp.dot(p_t.T.astype(BF16), vc, preferred_element_type=F32)
        p_sum = p_t[:, 0:tq]
        for g in range(1, g4):
            p_sum = p_sum + p_t[:, g * tq:(g + 1) * tq]
        imp_t = sum(jnp.dot(covt, part, preferred_element_type=F32) for part in _split3(p_sum))
        imp_t = jnp.where(forced, FORCE_SCORE, jnp.where(future, -1.0, imp_t))
        imp_t = jnp.where(exists, imp_t, -2.0)
        sel_t = _select_topk(imp_t, blk_f, topk)
        selneg = jnp.where(_transpose_01(sel_t, eye) > 0.5, 0.0, NEG_INF)
        selneg2 = jnp.concatenate([selneg, selneg], axis=1).astype(BF16)

        lhs4 = jnp.concatenate([jnp.where(own, qp, selneg2) for qp in qpads], axis=0)

        def kt_body(kt, carry):
            m, l, acc = carry
            ks = pl.multiple_of(kt * NSA_TK, NSA_TK)
            ka = kaug_ref[pl.ds(ks, NSA_TK), hk * LANES:(hk + 1) * LANES]
            v = vsel_ref[pl.ds(ks, NSA_TK), :]
            s = lax.dot_general(lhs4, ka, (((1,), (1,)), ((), ())), preferred_element_type=F32)
            kpos = ks + lax.broadcasted_iota(jnp.int32, (1, NSA_TK), 1)
            s = jnp.where(kpos <= qpos_col4, s, NEG_INF)
            m_new = jnp.maximum(m, jnp.max(s, axis=1, keepdims=True))
            alpha = jnp.exp(m - m_new)
            p = jnp.exp(s - m_new)
            l = alpha * l + jnp.sum(p, axis=1, keepdims=True)
            acc = alpha * acc + jnp.dot(p.astype(BF16), v, preferred_element_type=F32)
            return m_new, l, acc

        init = (jnp.full((g4 * tq, 1), NEG_INF, F32), jnp.zeros((g4 * tq, 1), F32),
                jnp.zeros((g4 * tq, LANES), F32))
        _, l_sel, acc_sel = lax.fori_loop(0, n_kt, kt_body, init)
        o_sel = acc_sel / jnp.maximum(l_sel, 1e-30)

        kw = kvw_ref[pl.ds(w_start, w_len), 0:LANES]
        vw = kvw_ref[pl.ds(w_start, w_len), LANES:2 * LANES]
        s_w = lax.dot_general(qpad4, kw, (((1,), (1,)), ((), ())), preferred_element_type=F32)
        p_w = _softmax_rows(s_w, win_valid)
        o_win = jnp.dot(p_w.astype(BF16), vw, preferred_element_type=F32)

        def gate_col(j):
            cols = [gates[:, 3 * (hk * g4 + g) + j:3 * (hk * g4 + g) + j + 1] for g in range(g4)]
            return jnp.concatenate(cols, axis=0)

        o4 = gate_col(0) * o_cmp + gate_col(1) * o_sel + gate_col(2) * o_win
        for pair in range(g4 // 2):
            a = o4[(2 * pair) * tq:(2 * pair + 1) * tq]
            b = o4[(2 * pair + 1) * tq:(2 * pair + 2) * tq]
            if hk == 0:
                both = jnp.where(lane < HEAD_DIM, a, pltpu.roll(b, HEAD_DIM, 1))
            else:
                both = jnp.where(lane < HEAD_DIM, pltpu.roll(a, HEAD_DIM, 1), b)
            col = (hk * (g4 // 2) + pair) * LANES
            o_ref[:, col:col + LANES] = both


def _nsa_prompt(qpad, gates, kaug, vsel, kvwin, kc, vc, covt, batch, t_len, n_cmp, n_sel, topk):
    n = qpad.shape[0]
    nq = t_len // NSA_TQ
    n_cmp_pad = kc.shape[0] // batch
    qrow = lambda w: pl.BlockSpec((NSA_TQ, w), lambda b, i: (b * nq + i, 0))
    per_b = lambda r, w: pl.BlockSpec((r, w), lambda b, i: (b, 0))
    return pl.pallas_call(
        functools.partial(_nsa_body, n_cmp=n_cmp, n_sel=n_sel, topk=topk),
        grid=(batch, nq),
        in_specs=[qrow(QPAD_WIDTH), qrow(LANES), per_b(t_len, 2 * LANES), per_b(t_len, LANES),
                  per_b(t_len, 2 * LANES), per_b(n_cmp_pad, LANES), per_b(n_cmp_pad, LANES),
                  _const_spec(covt.shape)],
        out_specs=qrow(NSA_WIDTH),
        out_shape=jax.ShapeDtypeStruct((n, NSA_WIDTH), F32),
        compiler_params=_cparams("parallel", "parallel"),
        name="nsa_prompt",
    )(qpad, gates, kaug, vsel, kvwin, kc, vc, covt)


def _group_features(load_l, lane):
    heads = ([], [])
    for m in range(CMP_STRIDE // 2):
        a = load_l(2 * m)
        b = load_l(2 * m + 1)
        heads[0].append(jnp.where(lane < HEAD_DIM, a, pltpu.roll(b, HEAD_DIM, 1)))
        heads[1].append(jnp.where(lane < HEAD_DIM, pltpu.roll(a, HEAD_DIM, 1), b))
    return [jnp.concatenate(h, axis=1) for h in heads]


def _compress_heads(x_heads, pe8, w1, w2_h0, w2_h1):
    g = x_heads[0].shape[0]
    x2 = jnp.concatenate(x_heads, axis=0).astype(BF16)
    z = jnp.dot(x2, w1, preferred_element_type=F32)
    zb = jnp.dot(pe8.astype(BF16), w1, preferred_element_type=F32)
    bias = zb[0:1, 0:CMP_HIDDEN] + zb[1:2, CMP_HIDDEN:]
    nxt = pltpu.roll(z[:, CMP_HIDDEN:], 2 * g - 1, 0)
    hid = jax.nn.gelu(z[:, 0:CMP_HIDDEN] + nxt + bias).astype(BF16)
    return (jnp.dot(hid[0:g], w2_h0, preferred_element_type=F32)
            + jnp.dot(hid[g:], w2_h1, preferred_element_type=F32))


def _compress_body(krows_ref, vrows_ref, pe_ref, w1_ref, w2_ref, c_ref, sm_ref, sp_ref, kc_ref, vc_ref):
    g = kc_ref.shape[0]
    lane = lax.broadcasted_iota(jnp.int32, (g, LANES), 1)
    outs = []
    for t, ref in enumerate((krows_ref, vrows_ref)):
        load = lambda l, ref=ref: ref[pl.ds(l, g, stride=CMP_STRIDE), :]
        outs.append(_compress_heads(_group_features(load, lane), pe_ref[t], w1_ref[t], w2_ref[t, 0], w2_ref[t, 1]))
    kc_ref[...] = _rope(outs[0], c_ref[...], sm_ref[...], sp_ref[...]).astype(BF16)
    vc_ref[...] = outs[1].astype(BF16)


def _compress_prompt(rows, pe8, w1c, w2p, tabs_c, l, batch, t_len):
    g = t_len // CMP_STRIDE
    out = jax.ShapeDtypeStruct((batch * g, LANES), BF16)
    return pl.pallas_call(
        _compress_body,
        grid=(batch,),
        in_specs=[pl.BlockSpec((t_len, LANES), lambda b: (b, 0)), pl.BlockSpec((t_len, LANES), lambda b: (b, 1)),
                  _layer_spec((2, 8, CMP_STRIDE * HEAD_DIM), l),
                  _layer_spec((2, CMP_STRIDE * HEAD_DIM, 2 * CMP_HIDDEN), l),
                  _layer_spec((2, 2, CMP_HIDDEN, LANES), l),
                  _const_spec((g, LANES)), _const_spec((g, LANES)), _const_spec((g, LANES))],
        out_specs=[pl.BlockSpec((g, LANES), lambda b: (b, 0))] * 2,
        out_shape=[out, out],
        compiler_params=_cparams("parallel"),
        name="compress",
    )(rows, rows, pe8, w1c, w2p, *tabs_c)


GDN_TT = 512
HALO = 8


def _dot3(a, x):
    return sum(jnp.dot(a, part, preferred_element_type=F32) for part in _split3(x))


def _dot3_nt(a, x):
    return sum(lax.dot_general(a, part, (((1,), (1,)), ((), ())), preferred_element_type=F32) for part in _split3(x))


def _l2n(x):
    return x * lax.rsqrt(jnp.sum(x * x, axis=-1, keepdims=True) + 1e-6)


def _softplus(x):
    return jnp.maximum(x, 0.0) + jnp.log(1.0 + jnp.exp(-jnp.abs(x)))


def _gdn_prep_body(x_ref, halo_ref, gt_ref, cw_ref, alog_ref, dtb_ref,
                   qg_ref, kd_ref, w_ref, u_ref, att_ref, egl_ref,
                   ext_ref, q_s, k_s, kb_s, rhs_s, gc_s):
    c_len = GDN_CHUNK
    tt = x_ref.shape[0]
    nch = tt // c_len
    qk_w = GDN_HEADS * GDN_DK
    first = pl.program_id(1) == 0
    ext_ref[0:HALO, :] = jnp.where(first, 0.0, halo_ref[...])
    ext_ref[HALO:HALO + tt, :] = x_ref[...]
    c = ext_ref[pl.ds(HALO - (CONV_W - 1), tt), :] * cw_ref[0:1, :]
    for j in range(1, CONV_W):
        c = c + ext_ref[pl.ds(HALO - (CONV_W - 1) + j, tt), :] * cw_ref[j:j + 1, :]
    c = c * jax.nn.sigmoid(c)

    gt = gt_ref[...]
    g_full = -jnp.exp(alog_ref[...]) * _softplus(gt + dtb_ref[...])
    beta_full = jax.nn.sigmoid(gt)
    r = lax.broadcasted_iota(jnp.int32, (tt, tt), 0)
    cc = lax.broadcasted_iota(jnp.int32, (tt, tt), 1)
    same = (r // c_len) == (cc // c_len)
    gcum = _dot3(jnp.where(same & (cc <= r), 1.0, 0.0).astype(BF16), g_full)
    gtot = _dot3(jnp.where(same, 1.0, 0.0).astype(BF16), g_full)
    gc_s[...] = gcum
    for h in range(GDN_HEADS):
        sl = slice(h * GDN_DK, (h + 1) * GDN_DK)
        gc_col = gcum[:, GATE_A + h:GATE_A + h + 1]
        gl_col = gtot[:, GATE_A + h:GATE_A + h + 1]
        beta_col = beta_full[:, GATE_B + h:GATE_B + h + 1]
        qh = _l2n(c[:, h * GDN_DK:(h + 1) * GDN_DK]) * (GDN_DK ** -0.5)
        kh = _l2n(c[:, qk_w + h * GDN_DK:qk_w + (h + 1) * GDN_DK])
        vh = c[:, 2 * qk_w + h * GDN_DV:2 * qk_w + (h + 1) * GDN_DV]
        eg = jnp.exp(gc_col)
        kb = kh * beta_col
        q_s[:, sl] = qh
        k_s[:, sl] = kh
        kb_s[:, sl] = kb
        rhs_s[:, 2 * h * GDN_DK:(2 * h + 1) * GDN_DK] = vh * beta_col
        rhs_s[:, (2 * h + 1) * GDN_DK:(2 * h + 2) * GDN_DK] = kb * eg
        qg_ref[:, sl] = (qh * eg).astype(BF16)
        kd_ref[:, sl] = (kh * jnp.exp(gl_col - gc_col)).astype(BF16)
        egl_b = jnp.broadcast_to(jnp.exp(gl_col), (tt, GDN_DK))
        for ch in range(nch):
            egl_ref[ch, :, sl] = egl_b[ch * c_len:ch * c_len + 1, :]

    ii = lax.broadcasted_iota(jnp.int32, (c_len, c_len), 0)
    jj = lax.broadcasted_iota(jnp.int32, (c_len, c_len), 1)
    incl = ii >= jj
    strict = ii > jj
    eye = jnp.where(ii == jj, 1.0, 0.0)
    base_mask = (ii >> 1) == (jj >> 1)
    level_masks = [((ii >> (s + 1)) == (jj >> (s + 1))) & ((ii >> s) != (jj >> s))
                   for s in range(1, int(math.log2(c_len)))]
    hrow =lax.broadcasted_iota(jnp.int32, (GDN_HEADS * c_len, LANES), 0) // c_len
    hlane = lax.broadcasted_iota(jnp.int32, (GDN_HEADS * c_len, LANES), 1)
    pick = jnp.where(hlane == GATE_A + hrow, 1.0, 0.0).astype(BF16)
    nt = (((1,), (1,)), ((), ()))

    def chunk(ch, carry):
        r0 = pl.multiple_of(ch * c_len, c_len)
        gch = gc_s[pl.ds(r0, c_len), :]
        d_all = _dot3_nt(pick, gch)
        for h in range(GDN_HEADS):
            sl = slice(h * GDN_DK, (h + 1) * GDN_DK)
            kh = k_s[pl.ds(r0, c_len), sl].astype(BF16)
            kb = kb_s[pl.ds(r0, c_len), sl].astype(BF16)
            qh = q_s[pl.ds(r0, c_len), sl].astype(BF16)
            diff = gch[:, GATE_A + h:GATE_A + h + 1] - d_all[h * c_len:(h + 1) * c_len]
            decay = jnp.where(incl, jnp.exp(jnp.where(incl, diff, 0.0)), 0.0)
            a = jnp.where(strict, lax.dot_general(kb, kh, nt, preferred_element_type=F32) * decay, 0.0)
            att = lax.dot_general(qh, kh, nt, preferred_element_type=F32) * decay
            att_ref[h, pl.ds(r0, c_len), :] = att.astype(BF16)
            x = eye - jnp.where(base_mask, a, 0.0)
            for lvl_mask in level_masks:
                xb = x.astype(BF16)
                off = jnp.where(lvl_mask, a, 0.0).astype(BF16)
                x = x - jnp.dot(jnp.dot(xb, off, preferred_element_type=F32).astype(BF16), xb,
                                preferred_element_type=F32)
            uw = jnp.dot(x.astype(BF16), rhs_s[pl.ds(r0, c_len), 2 * h * GDN_DK:(2 * h + 2) * GDN_DK].astype(BF16),
                         preferred_element_type=F32)
            u_ref[pl.ds(r0, c_len), sl] = uw[:, 0:GDN_DV]
            w_ref[pl.ds(r0, c_len), sl] = uw[:, GDN_DV:].astype(BF16)
        return carry

    lax.fori_loop(0, nch, chunk, 0)


def _gdn_prep(qkv, gates, conv_w, alog_row, dtb_row, l, batch, t_len):
    n = qkv.shape[0]
    tt = GDN_TT
    nt = t_len // tt
    nch = tt // GDN_CHUNK
    row = lambda w: pl.BlockSpec((tt, w), lambda b, i: (b * nt + i, 0))
    outs = [(GDN_WIDTH, BF16), (GDN_WIDTH, BF16), (GDN_WIDTH, BF16), (GDN_WIDTH, F32)]
    return pl.pallas_call(
        _gdn_prep_body,
        grid=(batch, nt),
        in_specs=[row(GDN_QKV),
                  pl.BlockSpec((HALO, GDN_QKV), lambda b, i: (jnp.maximum((b * nt + i) * (tt // HALO) - 1, 0), 0)),
                  row(LANES), _layer_spec((CONV_W, GDN_QKV), l), _layer_spec((1, LANES), l), _layer_spec((1, LANES), l)],
        out_specs=[row(w) for w, _ in outs] + [
            pl.BlockSpec((GDN_HEADS, tt, GDN_CHUNK), lambda b, i: (0, b * nt + i, 0)),
            pl.BlockSpec((nch, 1, GDN_WIDTH), lambda b, i: (b * nt + i, 0, 0))],
        out_shape=[jax.ShapeDtypeStruct((n, w), dt) for w, dt in outs] + [
            jax.ShapeDtypeStruct((GDN_HEADS, n, GDN_CHUNK), BF16),
            jax.ShapeDtypeStruct((n // GDN_CHUNK, 1, GDN_WIDTH), F32)],
        scratch_shapes=[pltpu.VMEM((HALO + tt, GDN_QKV), F32), pltpu.VMEM((tt, GDN_WIDTH), F32),
                        pltpu.VMEM((tt, GDN_WIDTH), F32), pltpu.VMEM((tt, GDN_WIDTH), F32),
                        pltpu.VMEM((tt, 2 * GDN_WIDTH), F32), pltpu.VMEM((tt, LANES), F32)],
        compiler_params=_cparams("parallel", "parallel"),
        name="gdn_prep",
    )(qkv, qkv, gates, conv_w, alog_row, dtb_row)


def _gdn_scan_body(qg_ref, kd_ref, w_ref, u_ref, att_ref, egl_ref, z_ref, gn_ref, o_ref, sfin_ref, s_ref):
    c_len = GDN_CHUNK
    tt = qg_ref.shape[0]
    i = pl.program_id(1)

    @pl.when(i == 0)
    def _():
        s_ref[...] = jnp.zeros(s_ref.shape, F32)

    gn = gn_ref[...]
    tn = (((0,), (0,)), ((), ()))

    def chunk(ch, carry):
        r0 = pl.multiple_of(ch * c_len, c_len)
        for h in range(GDN_HEADS):
            sl = slice(h * GDN_DK, (h + 1) * GDN_DK)
            s = s_ref[h]
            sb = s.astype(BF16)
            v_new = u_ref[pl.ds(r0, c_len), sl] - jnp.dot(w_ref[pl.ds(r0, c_len), sl], sb, preferred_element_type=F32)
            vb = v_new.astype(BF16)
            o = (jnp.dot(qg_ref[pl.ds(r0, c_len), sl], sb, preferred_element_type=F32)
                 + jnp.dot(att_ref[h, pl.ds(r0, c_len), :], vb, preferred_element_type=F32))
            s_ref[h] = s * egl_ref[ch, :, sl] + lax.dot_general(
                kd_ref[pl.ds(r0, c_len), sl], vb, tn, preferred_element_type=F32)
            y = o * lax.rsqrt(jnp.mean(o * o, axis=-1, keepdims=True) + RMS_EPS) * gn
            zz = z_ref[pl.ds(r0, c_len), sl]
            o_ref[pl.ds(r0, c_len), sl] = y * (zz * jax.nn.sigmoid(zz))
        return carry

    lax.fori_loop(0, tt // c_len, chunk, 0)

    @pl.when(i == pl.num_programs(1) - 1)
    def _():
        sfin_ref[...] = s_ref[...]


def _gdn_scan(qg, kd, w, u, att, egl, z, gnorm, l, batch, t_len):
    n = qg.shape[0]
    tt = GDN_TT
    nt = t_len // tt
    nch = tt // GDN_CHUNK
    row = lambda: pl.BlockSpec((tt, GDN_WIDTH), lambda b, i: (b * nt + i, 0))
    return pl.pallas_call(
        _gdn_scan_body,
        grid=(batch, nt),
        in_specs=[row(), row(), row(), row(),
                  pl.BlockSpec((GDN_HEADS, tt, GDN_CHUNK), lambda b, i: (0, b * nt + i, 0)),
                  pl.BlockSpec((nch, 1, GDN_WIDTH), lambda b, i: (b * nt + i, 0, 0)),
                  row(), _layer_spec((1, GDN_DV), l)],
        out_specs=[row(), pl.BlockSpec((None, GDN_HEADS, GDN_DK, GDN_DV), lambda b, i: (b, 0, 0, 0))],
        out_shape=[jax.ShapeDtypeStruct((n, GDN_WIDTH), F32),
                   jax.ShapeDtypeStruct((batch, GDN_HEADS, GDN_DK, GDN_DV), F32)],
        scratch_shapes=[pltpu.VMEM((GDN_HEADS, GDN_DK, GDN_DV), F32)],
        compiler_params=_cparams("parallel", "arbitrary"),
        name="gdn_scan",
    )(qg, kd, w, u, att, egl, z, gnorm)


def _rope_tables(pos):
    half = ROPE_DIM // 2
    inv = ROPE_THETA ** (-2.0 * jnp.arange(half, dtype=F32) / ROPE_DIM)
    ang = pos.astype(F32)[:, None] * inv[None, :]
    cos, sin = jnp.cos(ang), jnp.sin(ang)
    n = pos.shape[0]
    one = jnp.ones((n, HEAD_DIM - ROPE_DIM), F32)
    zero = jnp.zeros((n, HEAD_DIM - ROPE_DIM), F32)
    zh = jnp.zeros((n, half), F32)
    c = jnp.concatenate([cos, cos, one], axis=1)
    sm = jnp.concatenate([-sin, zh, zero], axis=1)
    sp = jnp.concatenate([zh, sin, zero], axis=1)
    return tuple(jnp.concatenate([t, t], axis=1) for t in (c, sm, sp))


def _block_onehot(pos):
    e = (pos[:, None] // SEL_BLOCK == jnp.arange(HEAD_DIM)[None, :]).astype(F32)
    return jnp.concatenate([e, e], axis=1)


def _rope_jnp(x, pos):
    half = ROPE_DIM // 2
    inv = ROPE_THETA ** (-2.0 * jnp.arange(half, dtype=F32) / ROPE_DIM)
    ang = pos.astype(F32)[:, None] * inv[None, :]
    cos = jnp.cos(ang)[None, :, None, :]
    sin = jnp.sin(ang)[None, :, None, :]
    x1 = x[..., :half]
    x2 = x[..., half:ROPE_DIM]
    return jnp.concatenate([x1 * cos - x2 * sin, x2 * cos + x1 * sin, x[..., ROPE_DIM:]], axis=-1)


def _masked_softmax(s, mask):
    s = jnp.where(mask, s, NEG_INF)
    m = jnp.max(s, axis=-1, keepdims=True)
    e = jnp.where(mask, jnp.exp(s - m), 0.0)
    return e / jnp.maximum(jnp.sum(e, axis=-1, keepdims=True), 1e-30)


def _compress_jnp(k, pe, w1, w2):
    b, t, hk, dh = k.shape
    n_cmp = (t - CMP_BLOCK) // CMP_STRIDE + 1
    start = jnp.arange(n_cmp) * CMP_STRIDE
    idx = start[:, None] + jnp.arange(CMP_BLOCK)[None, :]
    blk = k[:, idx] + pe[None, None, :, None, :]
    blk = jnp.swapaxes(blk, 2, 3).reshape(b, n_cmp, hk, CMP_BLOCK * dh)
    return jax.nn.gelu(blk @ w1) @ w2, start + CMP_BLOCK - 1


def _block_coverage(c_start, n_sel):
    b_start = jnp.arange(n_sel) * SEL_BLOCK
    lo = jnp.maximum(c_start[:, None], b_start[None, :])
    hi = jnp.minimum(c_start[:, None] + CMP_BLOCK, b_start[None, :] + SEL_BLOCK)
    return jnp.clip(hi - lo, 0, None).astype(F32) / CMP_BLOCK


def _sample_nsa_jnp(q, gate_logits, rows_new, win_new, nsa_past, win_buf, w_buf, cmp_pe, w_cmp1, w_cmp2, topk):
    b, t = q.shape[:2]
    pos0 = nsa_past.shape[1]
    qpos = pos0 + jnp.arange(t)
    q = q.reshape(b, t, NSA_KV_HEADS, NSA_GROUP, HEAD_DIM)
    rows = jnp.concatenate([nsa_past, rows_new], axis=1)
    t_kv = rows.shape[1]
    kc, c_end = _compress_jnp(rows[:, :, 0], cmp_pe[0], w_cmp1[0], w_cmp2[0])
    vc, _ = _compress_jnp(rows[:, :, 1], cmp_pe[1], w_cmp1[1], w_cmp2[1])
    kc = _rope_jnp(kc, c_end)
    n_sel = -(-t_kv // SEL_BLOCK)
    cov = _block_coverage(c_end - CMP_BLOCK + 1, n_sel)
    sel = jnp.pad(rows[:, :, 2:], ((0, 0), (0, n_sel * SEL_BLOCK - t_kv), (0, 0), (0, 0), (0, 0)))
    sel = sel.reshape(b, n_sel, SEL_BLOCK, 2, NSA_KV_HEADS, HEAD_DIM).transpose(3, 0, 4, 1, 2, 5)
    k_blk, v_blk = sel[0], sel[1]
    win_all = jnp.concatenate([win_buf, win_new], axis=1)
    n_buf = win_buf.shape[1]
    gates = jax.nn.sigmoid(gate_logits).reshape(b, t, NSA_KV_HEADS, NSA_GROUP, 3)
    s = jnp.einsum('bqhgd,bchd->bhgqc', q, kc) * ATTN_SCALE
    p = _masked_softmax(s, c_end[None, :] <= qpos[:, None])
    o_cmp = jnp.einsum('bhgqc,bchd->bqhgd', p, vc)
    imp = jnp.einsum('bhgqc,cn->bhqn', p, cov)
    blk = jnp.arange(n_sel)[None, :]
    q_blk = (qpos // SEL_BLOCK)[:, None]
    forced = (blk == 0) | (blk == q_blk) | (blk == q_blk - 1)
    imp = jnp.where(forced, FORCE_SCORE, jnp.where(blk > q_blk, -1.0, imp))
    _, idx = lax.top_k(imp, min(topk, n_sel))
    bi = jnp.arange(b)[:, None, None, None]
    hi = jnp.arange(NSA_KV_HEADS)[None, :, None, None]
    kg = k_blk[bi, hi, idx]
    vg = v_blk[bi, hi, idx]
    k_pos = idx[..., None] * SEL_BLOCK + jnp.arange(SEL_BLOCK)
    mask = (k_pos <= qpos[None, None, :, None, None])[:, :, None]
    s2 = jnp.einsum('bqhgd,bhqksd->bhgqks', q, kg) * ATTN_SCALE
    shp = s2.shape
    p2 = _masked_softmax(s2.reshape(shp[:-2] + (-1,)), mask.reshape(mask.shape[:-2] + (-1,))).reshape(shp)
    o_sel = jnp.einsum('bhgqks,bhqksd->bqhgd', p2, vg)
    kw_pos = pos0 - n_buf + jnp.arange(n_buf + t)
    sw = jnp.einsum('bqhgd,bkhd->bhgqk', q, win_all[:, :, 0]) * ATTN_SCALE
    diff = qpos[:, None] - kw_pos[None, :]
    pw = _masked_softmax(sw, (diff >= 0) & (diff < WINDOW) & (kw_pos[None, :] >= 0))
    o_win = jnp.einsum('bhgqk,bkhd->bqhgd', pw, win_all[:, :, 1])
    o = gates[..., 0:1] * o_cmp + gates[..., 1:2] * o_sel + gates[..., 2:3] * o_win
    return o.reshape(b, t, NSA_WIDTH), win_all[:, win_all.shape[1] - w_buf:]


def _to_chunks(a, c, pad):
    a = jnp.moveaxis(a, 1, 2)
    if pad:
        a = jnp.pad(a, [(0, 0), (0, 0), (0, pad)] + [(0, 0)] * (a.ndim - 3))
    b, h, tp = a.shape[:3]
    a = a.reshape((b, h, tp // c, c) + a.shape[3:])
    return jnp.moveaxis(a, 2, 0)


def _chunk_gated_delta_jnp(q, k, v, g, beta, s0):
    b, t, h, _ = q.shape
    c = min(GDN_CHUNK, t)
    pad = (-t) % c
    qc, kc, vc = _to_chunks(q, c, pad), _to_chunks(k, c, pad), _to_chunks(v, c, pad)
    gc = jnp.cumsum(_to_chunks(g, c, pad), axis=-1)
    bc = _to_chunks(beta, c, pad)
    ii = jnp.arange(c)
    incl = ii[:, None] >= ii[None, :]
    strict = ii[:, None] > ii[None, :]
    diff = gc[..., :, None] - gc[..., None, :]
    decay = jnp.where(incl, jnp.exp(jnp.where(incl, diff, 0.0)), 0.0)
    kb = kc * bc[..., None]
    a = jnp.where(strict, jnp.einsum('...id,...jd->...ij', kb, kc) * decay, 0.0)
    eye = jnp.eye(c, dtype=F32)
    tm = lax.linalg.triangular_solve(eye + a, jnp.broadcast_to(eye, a.shape), left_side=True, lower=True)
    u = jnp.einsum('...ij,...jd->...id', tm, vc * bc[..., None])
    w = jnp.einsum('...ij,...jd->...id', tm, kb * jnp.exp(gc)[..., None])

    def step(s, inp):
        qi, ki, ui, wi, gi, di = inp
        v_new = ui - jnp.einsum('bhck,bhkv->bhcv', wi, s)
        att = jnp.einsum('bhik,bhjk->bhij', qi, ki) * di
        o = jnp.einsum('bhck,bhkv->bhcv', qi * jnp.exp(gi)[..., None], s) + jnp.einsum('bhij,bhjv->bhiv', att, v_new)
        gl = gi[..., -1]
        s = s * jnp.exp(gl)[..., None, None] + jnp.einsum(
            'bhck,bhcv->bhkv', ki * jnp.exp(gl[..., None] - gi)[..., None], v_new)
        return s, o

    s, o = lax.scan(step, s0, (qc, kc, u, w, gc, decay))
    o = jnp.moveaxis(o, 0, 2).reshape(b, h, -1, o.shape[-1])[:, :, :t]
    return jnp.moveaxis(o, 1, 2), s


def _gdn_jnp(qkv, a, bb, z, conv_buf, state, conv_w, a_log, dt_bias, norm_w):
    b, t, _ = qkv.shape
    xp = jnp.concatenate([conv_buf, qkv], axis=1)
    c = xp[:, 0:t] * conv_w[0]
    for j in range(1, CONV_W):
        c = c + xp[:, j:j + t] * conv_w[j]
    c = jax.nn.silu(c)
    new_buf = xp[:, t:]
    qk_w = GDN_HEADS * GDN_DK
    l2 = lambda x: x * lax.rsqrt(jnp.sum(x * x, axis=-1, keepdims=True) + 1e-6)
    q = l2(c[..., :qk_w].reshape(b, t, GDN_HEADS, GDN_DK)) * GDN_DK ** -0.5
    k = l2(c[..., qk_w:2 * qk_w].reshape(b, t, GDN_HEADS, GDN_DK))
    v = c[..., 2 * qk_w:].reshape(b, t, GDN_HEADS, GDN_DV)
    g = -jnp.exp(a_log) * jax.nn.softplus(a + dt_bias)
    beta = jax.nn.sigmoid(bb)
    o, new_state = _chunk_gated_delta_jnp(q, k, v, g, beta, state)
    o = o * lax.rsqrt(jnp.mean(o * o, axis=-1, keepdims=True) + RMS_EPS) * norm_w
    o = o * jax.nn.silu(z).reshape(b, t, GDN_HEADS, GDN_DV)
    return o.reshape(b, t, GDN_WIDTH), new_buf, new_state


def _mem_attend_sample_jnp(x, gains, mem_kv, w_q, w_o):
    xf = x
    h = xf * lax.rsqrt(jnp.mean(xf * xf, axis=-1, keepdims=True) + RMS_EPS) * gains[4]
    q = (h @ w_q).reshape(-1, MEM_HEADS, MEM_HD)
    s = jnp.einsum('bhd,bmhd->bhm', q, mem_kv[:, :, 0]) * MEM_HD ** -0.5
    p = jax.nn.softmax(s, axis=-1)
    o = jnp.einsum('bhm,bmhd->bhd', p, mem_kv[:, :, 1]).reshape(-1, MEM_WIDTH)
    y = o @ w_o
    return x + y * lax.rsqrt(jnp.mean(y * y, axis=-1, keepdims=True) + RMS_EPS) * gains[5]


def _pack_w_in(w_in):
    nl, d, _ = w_in.shape
    o_q, o_kv, o_g = NSA_WIDTH, NSA_WIDTH + KV6_WIDTH, NSA_WIDTH + KV6_WIDTH + 3 * NSA_HEADS
    o_qkv = o_g
    o_a = o_qkv + GDN_QKV
    o_b = o_a + GDN_HEADS
    o_z = o_b + GDN_HEADS
    wq = w_in[:, :, :o_q].reshape(nl, d, NSA_HEADS, HEAD_DIM)
    zq = jnp.zeros_like(wq)
    first = jnp.concatenate([wq, zq], axis=-1)
    second = jnp.concatenate([zq, wq], axis=-1)
    kv_of_head = (jnp.arange(NSA_HEADS) // NSA_GROUP)[None, None, :, None]
    wq_pad = jnp.where(kv_of_head == 0, first, second).reshape(nl, d, QPAD_WIDTH)
    gate_grp = jnp.concatenate([w_in[:, :, o_kv:o_g], w_in[:, :, o_a:o_z],
                                jnp.zeros((nl, d, LANES - 3 * NSA_HEADS - 2 * GDN_HEADS), w_in.dtype)], axis=-1)
    packed = jnp.concatenate([wq_pad, w_in[:, :, o_q:o_kv], w_in[:, :, o_qkv:o_a], w_in[:, :, o_z:], gate_grp], axis=-1)
    return packed.astype(BF16)


def kernel(x_prompt, x_sample, cache_nsa_kv, cache_win_kv, state_gdn_S, state_gdn_conv, cache_mem_kv, page_table, mem_prompt, norm_gains, mem_norm, w_ffn_gu, w_ffn_down, w_in, w_out, cmp_pe, w_cmp1, w_cmp2, gdn_conv, gdn_A_log, gdn_dt_bias, gdn_norm, w_mem_q, w_mem_kv, w_mem_o):
    bp, t_len, d = x_prompt.shape
    bs = x_sample.shape[0]
    depth = w_in.shape[0]
    w_buf = cache_win_kv.shape[2]
    n_pages = page_table.shape[1]
    page = cache_nsa_kv.shape[2]
    past_len = n_pages * page
    n_mem = mem_prompt.shape[1]
    np_tok = bp * t_len

    gains = norm_gains.reshape(depth, 8, 1, d)
    wgu = w_ffn_gu.astype(BF16)
    wd = w_ffn_down.astype(BF16)
    w_in_p = _pack_w_in(w_in)
    w_out_b = w_out.astype(BF16)
    w_mq = w_mem_q.astype(BF16)
    w_mkv = w_mem_kv.astype(BF16)
    w_mo = w_mem_o.astype(BF16)
    mem_gain = mem_norm.reshape(depth, 1, d)

    pos_p = jnp.arange(t_len)
    tabs_p = _rope_tables(pos_p) + (_block_onehot(pos_p),)
    pos_s = jnp.full((bs,), past_len, jnp.int32)
    tabs_s = _rope_tables(pos_s) + (jnp.zeros((bs, LANES), F32),)

    half_k = CMP_STRIDE * HEAD_DIM
    pe8 = jnp.pad(cmp_pe.reshape(depth, 2, CMP_BLOCK // CMP_STRIDE, half_k), ((0, 0), (0, 0), (0, 6), (0, 0)))
    w1c = jnp.concatenate([w_cmp1[:, :, :half_k], w_cmp1[:, :, half_k:]], axis=-1).astype(BF16)
    z2 = jnp.zeros_like(w_cmp2)
    w2p = jnp.stack([jnp.concatenate([w_cmp2, z2], axis=-1), jnp.concatenate([z2, w_cmp2], axis=-1)], axis=2).astype(BF16)
    tabs_c = _rope_tables(jnp.arange(t_len // CMP_STRIDE) * CMP_STRIDE + (CMP_BLOCK - 1))
    lane_pad = lambda v: jnp.pad(v, ((0, 0), (GATE_A, LANES - GATE_A - GDN_HEADS))).reshape(depth, 1, LANES)
    alog_row = lane_pad(gdn_A_log)
    dtb_row = lane_pad(gdn_dt_bias)
    gnorm = gdn_norm.reshape(depth, 1, GDN_DV)

    n_cmp = (t_len - CMP_BLOCK) // CMP_STRIDE + 1
    n_cmp_pad = t_len // CMP_STRIDE
    n_sel = -(-t_len // SEL_BLOCK)
    c_start = jnp.arange(n_cmp) * CMP_STRIDE
    assert n_sel <= HEAD_DIM and t_len >= WINDOW + NSA_TQ and t_len % NSA_TK == 0
    covt = jnp.pad(_block_coverage(c_start, n_sel).T, ((0, HEAD_DIM - n_sel), (0, n_cmp_pad - n_cmp))).astype(BF16)
    topk = min(SEL_TOPK, n_sel)

    tm_p = 512
    yp = x_prompt.reshape(np_tok, d)
    ys = x_sample.reshape(bs, d)
    mem_flat = mem_prompt.reshape(bp * n_mem, d)
    outs = [[] for _ in range(9)]
    for l in range(depth):
        mem_kv_p = _normmm(mem_flat, mem_gain, w_mkv, l, n_mem)
        yp = _ffn(yp, gains, wgu, wd, l, 0, tm_p)
        qpad, rows, win, kaug, vsel, kvwin, gates, qkv, z = _inproj(yp, gains, w_in_p, tabs_p, l, tm_p)
        rows5 = rows.reshape(bp, t_len, 4, NSA_KV_HEADS, HEAD_DIM)
        kc, vc = _compress_prompt(rows, pe8, w1c, w2p, tabs_c, l, bp, t_len)
        o_nsa = _nsa_prompt(qpad, gates, kaug, vsel, kvwin, kc, vc, covt, bp, t_len, n_cmp, n_sel, topk)
        qg, kd, w_g, u_g, att, egl = _gdn_prep(qkv, gates, gdn_conv, alog_row, dtb_row, l, bp, t_len)
        o_gdn, s_p = _gdn_scan(qg, kd, w_g, u_g, att, egl, z, gnorm, l, bp, t_len)
        conv_p = qkv.reshape(bp, t_len, GDN_QKV)[:, t_len - (CONV_W - 1):]
        yp = _outproj(yp, o_nsa, o_gdn, w_out_b, gains, l, tm_p)
        yp = _mem_block(yp, gains, w_mq, w_mo, mem_kv_p, l, tm_p, t_len // tm_p)
        yp = _ffn(yp, gains, wgu, wd, l, 1, tm_p)
        win5 = win.reshape(bp, t_len, 2, NSA_KV_HEADS, HEAD_DIM)
        win_p = win5[:, t_len - w_buf:] if t_len >= w_buf else jnp.pad(
            win5, ((0, 0), (w_buf - t_len, 0), (0, 0), (0, 0), (0, 0)))

        ys = _ffn(ys, gains, wgu, wd, l, 0, bs)
        qpad_s, rows_s, win_s, _, _, _, gates_s, qkv_s, z_s = _inproj(ys, gains, w_in_p, tabs_s, l, bs)
        q_s = qpad_s.astype(F32).reshape(bs, NSA_HEADS, 2, HEAD_DIM) * (1.0 / ATTN_SCALE)
        q_s = jnp.concatenate([q_s[:, :NSA_GROUP, 0], q_s[:, NSA_GROUP:, 1]], axis=1)
        past = cache_nsa_kv[l][page_table].reshape(bs, past_len, 4, NSA_KV_HEADS, HEAD_DIM)
        o_nsa_s, wbuf_s = _sample_nsa_jnp(
            q_s.reshape(bs, 1, NSA_HEADS, HEAD_DIM), gates_s[:, :GATE_A].reshape(bs, 1, NSA_HEADS, 3),
            rows_s.reshape(bs, 1, 4, NSA_KV_HEADS, HEAD_DIM), win_s.reshape(bs, 1, 2, NSA_KV_HEADS, HEAD_DIM),
            past, cache_win_kv[l], w_buf, cmp_pe[l], w_cmp1[l], w_cmp2[l], SEL_TOPK)
        o_gdn_s, conv_s, s_s = _gdn_jnp(
            qkv_s.reshape(bs, 1, GDN_QKV), gates_s[:, GATE_A:GATE_B].reshape(bs, 1, GDN_HEADS),
            gates_s[:, GATE_B:GATE_B + GDN_HEADS].reshape(bs, 1, GDN_HEADS), z_s.reshape(bs, 1, GDN_WIDTH),
            state_gdn_conv[l], state_gdn_S[l], gdn_conv[l], gdn_A_log[l], gdn_dt_bias[l], gdn_norm[l])
        ys = _outproj(ys, o_nsa_s.reshape(bs, NSA_WIDTH), o_gdn_s.reshape(bs, GDN_WIDTH), w_out_b, gains, l, bs)
        ys = _mem_attend_sample_jnp(ys, norm_gains[l], cache_mem_kv[l], w_mem_q[l], w_mem_o[l])
        ys = _ffn(ys, gains, wgu, wd, l, 1, bs)

        for lst, val in zip(outs, (rows5, win_p, s_p, conv_p, mem_kv_p.reshape(bp, n_mem, 2, MEM_HEADS, MEM_HD),
                                   rows_s.reshape(bs, 1, 4, NSA_KV_HEADS, HEAD_DIM), wbuf_s, s_s, conv_s)):
            lst.append(val)
    return (yp.reshape(bp, t_len, d), ys.reshape(bs, 1, d)) + tuple(jnp.stack(v) for v in outs)
```

```python
import functools
import math

import numpy as np
import jax
import jax.numpy as jnp
from jax import lax
from jax.experimental import pallas as pl
from jax.experimental.pallas import tpu as pltpu

F32 = jnp.float32
BF16 = jnp.bfloat16

HEAD_DIM = 64
NSA_HEADS = 8
NSA_KV_HEADS = 2
NSA_GROUP = NSA_HEADS // NSA_KV_HEADS
NSA_WIDTH = NSA_HEADS * HEAD_DIM
ROPE_DIM = HEAD_DIM // 4
ROPE_THETA = 500000.0
ATTN_SCALE = HEAD_DIM ** -0.5
CMP_BLOCK = 32
CMP_STRIDE = 16
CMP_HIDDEN = 2 * HEAD_DIM
SEL_BLOCK = 64
SEL_TOPK = 16
WINDOW = 512
FORCE_SCORE = 1.0e4
GDN_DK = 128
GDN_DV = 128
GDN_HEADS = 4
GDN_WIDTH = GDN_HEADS * GDN_DV
GDN_QKV = GDN_HEADS * (2 * GDN_DK + GDN_DV)
CONV_W = 4
GDN_CHUNK = 64
MEM_HEADS = 4
MEM_HD = 64
MEM_WIDTH = MEM_HEADS * MEM_HD
RMS_EPS = 1e-6
NEG_INF = -1.0e30
KV6_WIDTH = 6 * NSA_KV_HEADS * HEAD_DIM

LANES = 128
VMEM_LIMIT_BYTES = 56 * 1024 * 1024

QPAD_WIDTH = NSA_HEADS * LANES
COL_Q = 0
COL_KV = COL_Q + QPAD_WIDTH
COL_QKV = COL_KV + KV6_WIDTH
COL_Z = COL_QKV + GDN_QKV
COL_GATE = COL_Z + GDN_WIDTH
IN_PACKED = COL_GATE + LANES
GATE_A = 3 * NSA_HEADS
GATE_B = GATE_A + GDN_HEADS


def _cparams(*sem):
    return pltpu.CompilerParams(dimension_semantics=sem, vmem_limit_bytes=VMEM_LIMIT_BYTES)


def _rms(x, w):
    return x * lax.rsqrt(jnp.mean(x * x, axis=-1, keepdims=True) + RMS_EPS) * w


def _const_spec(shape):
    nd = len(shape)
    return pl.BlockSpec(shape, lambda *_: (0,) * nd)


def _layer_spec(shape, *lead):
    nlead = len(lead)
    nd = len(shape)
    return pl.BlockSpec((None,) * nlead + tuple(shape), lambda *_: tuple(lead) + (0,) * nd)


FFN_CHUNK = 256


def _ffn_body(x_ref, g0_ref, g1_ref, wgu_ref, wd_ref, o_ref, acc_ref, *, d_ff):
    x = x_ref[...]
    h = _rms(x, g0_ref[...]).astype(BF16)
    for f in range(d_ff // FFN_CHUNK):
        lo = f * FFN_CHUNK
        g = jnp.dot(h, wgu_ref[:, lo:lo + FFN_CHUNK], preferred_element_type=F32)
        u = jnp.dot(h, wgu_ref[:, d_ff + lo:d_ff + lo + FFN_CHUNK], preferred_element_type=F32)
        a = (g * jax.nn.sigmoid(g) * u).astype(BF16)
        d = jnp.dot(a, wd_ref[lo:lo + FFN_CHUNK, :], preferred_element_type=F32)
        if f == 0:
            acc_ref[...] = d
        else:
            acc_ref[...] += d
    o_ref[...] = x + 0.5 * _rms(acc_ref[...], g1_ref[...])


def _ffn(x, gains, wgu, wd, l, j, tm):
    n, d = x.shape
    d_ff = wd.shape[2]
    return pl.pallas_call(
        functools.partial(_ffn_body, d_ff=d_ff),
        grid=(n // tm,),
        in_specs=[
            pl.BlockSpec((tm, d), lambda i: (i, 0)),
            _layer_spec((1, d), l, 2 * j * 3),
            _layer_spec((1, d), l, 2 * j * 3 + 1),
            _layer_spec((d, 2 * d_ff), l, j),
            _layer_spec((d_ff, d), l, j),
        ],
        out_specs=pl.BlockSpec((tm, d), lambda i: (i, 0)),
        out_shape=jax.ShapeDtypeStruct((n, d), F32),
        scratch_shapes=[pltpu.VMEM((tm, d), F32)],
        compiler_params=_cparams("parallel"),
        name=f"ffn{j}",
    )(x, gains, gains, wgu, wd)


def _rope(v, c1, sm1, sp1):
    n = v.shape[1] // LANES
    c, sm, sp = (t if n == 1 else jnp.concatenate([t] * n, axis=1) for t in (c1, sm1, sp1))
    w = v.shape[1]
    return v * c + pltpu.roll(v, w - ROPE_DIM // 2, 1) * sm + pltpu.roll(v, ROPE_DIM // 2, 1) * sp


def _inproj_body(x_ref, g_ref, w_ref, c_ref, sm_ref, sp_ref, e_ref,
                 qpad_ref, rows_ref, win_ref, kaug_ref, vsel_ref, kvwin_ref, gates_ref, qkv_ref, z_ref):
    h = _rms(x_ref[...], g_ref[...]).astype(BF16)
    c1, sm1, sp1 = c_ref[...], sm_ref[...], sp_ref[...]

    def mm(lo, hi):
        return jnp.dot(h, w_ref[:, lo:hi], preferred_element_type=F32)

    q = _rope(mm(COL_Q, COL_KV), c1, sm1, sp1)
    qpad_ref[...] = (q * ATTN_SCALE).astype(BF16)
    kv = mm(COL_KV, COL_QKV)
    ksel = _rope(kv[:, 2 * LANES:3 * LANES], c1, sm1, sp1)
    vsel = kv[:, 3 * LANES:4 * LANES]
    kwin = _rope(kv[:, 4 * LANES:5 * LANES], c1, sm1, sp1)
    vwin = kv[:, 5 * LANES:6 * LANES]
    rows_ref[:, 0:2 * LANES] = kv[:, 0:2 * LANES]
    rows_ref[:, 2 * LANES:3 * LANES] = ksel
    rows_ref[:, 3 * LANES:4 * LANES] = vsel
    win_ref[:, 0:LANES] = kwin
    win_ref[:, LANES:2 * LANES] = vwin
    e2 = e_ref[...]
    lane = lax.broadcasted_iota(jnp.int32, ksel.shape, 1)
    first = lane < HEAD_DIM
    kaug_ref[:, 0:LANES] = jnp.where(first, ksel, e2).astype(BF16)
    kaug_ref[:, LANES:2 * LANES] = jnp.where(first, e2, ksel).astype(BF16)
    vsel_ref[...] = vsel.astype(BF16)
    kvwin_ref[:, 0:LANES] = kwin.astype(BF16)
    kvwin_ref[:, LANES:2 * LANES] = vwin.astype(BF16)
    qkv_ref[...] = mm(COL_QKV, COL_Z)
    z_ref[...] = mm(COL_Z, COL_GATE)
    gates_ref[...] = mm(COL_GATE, IN_PACKED)


def _inproj(x, gains, w_in_p, tabs, l, tm):
    n, d = x.shape
    c_t, sm_t, sp_t, e_t = tabs
    nt = c_t.shape[0] // tm
    tab = lambda w: pl.BlockSpec((tm, w), lambda i: (i % nt, 0))
    row = lambda w: pl.BlockSpec((tm, w), lambda i: (i, 0))
    widths = [(QPAD_WIDTH, BF16), (4 * LANES, F32), (2 * LANES, F32), (2 * LANES, BF16), (LANES, BF16),
              (2 * LANES, BF16), (LANES, F32), (GDN_QKV, F32), (GDN_WIDTH, F32)]
    return pl.pallas_call(
        _inproj_body,
        grid=(n // tm,),
        in_specs=[row(d), _layer_spec((1, d), l, 2), _layer_spec((d, IN_PACKED), l),
                  tab(LANES), tab(LANES), tab(LANES), tab(LANES)],
        out_specs=[row(w) for w, _ in widths],
        out_shape=[jax.ShapeDtypeStruct((n, w), dt) for w, dt in widths],
        compiler_params=_cparams("parallel"),
        name="inproj",
    )(x, gains, w_in_p, c_t, sm_t, sp_t, e_t)


def _normmm_body(x_ref, g_ref, w_ref, o_ref):
    h = _rms(x_ref[...], g_ref[...]).astype(BF16)
    o_ref[...] = jnp.dot(h, w_ref[...], preferred_element_type=F32)


def _normmm(x, gain, gain_lead, w, l, tm):
    n, d = x.shape
    nout = w.shape[-1]
    return pl.pallas_call(
        _normmm_body,
        grid=(n // tm,),
        in_specs=[pl.BlockSpec((tm, d), lambda i: (i, 0)), _layer_spec((1, d), *gain_lead), _layer_spec((d, nout), l)],
        out_specs=pl.BlockSpec((tm, nout), lambda i: (i, 0)),
        out_shape=jax.ShapeDtypeStruct((n, nout), F32),
        compiler_params=_cparams("parallel"),
        name="memkv",
    )(x, gain, w)


def _outproj_body(x_ref, a1_ref, a2_ref, w_ref, g_ref, o_ref):
    k1 = a1_ref.shape[1]
    acc = jnp.dot(a1_ref[...].astype(BF16), w_ref[0:k1, :], preferred_element_type=F32)
    acc = acc + jnp.dot(a2_ref[...].astype(BF16), w_ref[k1:, :], preferred_element_type=F32)
    o_ref[...] = x_ref[...] + _rms(acc, g_ref[...])


def _outproj(x, a1, a2, w_out, gains, l, tm, gain_idx=3):
    n, d = x.shape
    row = lambda w: pl.BlockSpec((tm, w), lambda i: (i, 0))
    return pl.pallas_call(
        _outproj_body,
        grid=(n // tm,),
        in_specs=[row(d), row(a1.shape[1]), row(a2.shape[1]),
                  _layer_spec((a1.shape[1] + a2.shape[1], d), l), _layer_spec((1, d), l, gain_idx)],
        out_specs=row(d),
        out_shape=jax.ShapeDtypeStruct((n, d), F32),
        compiler_params=_cparams("parallel"),
        name="outproj",
    )(x, a1, a2, w_out, gains)


def _mem_body(x_ref, g4_ref, g5_ref, wq_ref, wo_ref, kv_ref, o_ref):
    x = x_ref[...]
    h = _rms(x, g4_ref[...]).astype(BF16)
    q = jnp.dot(h, wq_ref[...], preferred_element_type=F32) * (MEM_HD ** -0.5)
    kv = kv_ref[...].astype(BF16)
    outs = []
    for hd in range(MEM_HEADS):
        qh = q[:, hd * MEM_HD:(hd + 1) * MEM_HD].astype(BF16)
        kh = kv[:, hd * MEM_HD:(hd + 1) * MEM_HD]
        vh = kv[:, MEM_WIDTH + hd * MEM_HD:MEM_WIDTH + (hd + 1) * MEM_HD]
        s = lax.dot_general(qh, kh, (((1,), (1,)), ((), ())), preferred_element_type=F32)
        e = jnp.exp(s - jnp.max(s, axis=-1, keepdims=True))
        p = e / jnp.sum(e, axis=-1, keepdims=True)
        outs.append(jnp.dot(p.astype(BF16), vh, preferred_element_type=F32))
    o = jnp.concatenate(outs, axis=1).astype(BF16)
    y = jnp.dot(o, wo_ref[...], preferred_element_type=F32)
    o_ref[...] = x + _rms(y, g5_ref[...])


def _mem_block(x, gains, w_q, w_o, mem_kv, l, tm, tiles_per_batch):
    n, d = x.shape
    m = mem_kv.shape[0] // (n // (tm * tiles_per_batch))
    return pl.pallas_call(
        _mem_body,
        grid=(n // tm,),
        in_specs=[pl.BlockSpec((tm, d), lambda i: (i, 0)), _layer_spec((1, d), l, 4), _layer_spec((1, d), l, 5),
                  _layer_spec((d, MEM_WIDTH), l), _layer_spec((MEM_WIDTH, d), l),
                  pl.BlockSpec((m, 2 * MEM_WIDTH), lambda i: (i // tiles_per_batch, 0))],
        out_specs=pl.BlockSpec((tm, d), lambda i: (i, 0)),
        out_shape=jax.ShapeDtypeStruct((n, d), F32),
        compiler_params=_cparams("parallel"),
        name="memattn",
    )(x, gains, gains, w_q, w_o, mem_kv)


NSA_TQ = 128
NSA_TK = 512


def _softmax_cols(s, valid):
    s = jnp.where(valid, s, NEG_INF)
    m = jnp.max(s, axis=0, keepdims=True)
    e = jnp.where(valid, jnp.exp(s - m), 0.0)
    return e / jnp.maximum(jnp.sum(e, axis=0, keepdims=True), 1e-30)


def _softmax_rows(s, valid):
    s = jnp.where(valid, s, NEG_INF)
    m = jnp.max(s, axis=1, keepdims=True)
    e = jnp.where(valid, jnp.exp(s - m), 0.0)
    return e / jnp.maximum(jnp.sum(e, axis=1, keepdims=True), 1e-30)


def _split3(x):
    hi = x.astype(BF16)
    r = x - hi.astype(F32)
    mid = r.astype(BF16)
    lo = (r - mid.astype(F32)).astype(BF16)
    return hi, mid, lo


def _select_topk(imp_t, blk, n_pick):
    n_blocks = imp_t.shape[0]
    sel = jnp.zeros(imp_t.shape, F32)
    work = imp_t
    for _ in range(n_pick):
        mx = jnp.max(work, axis=0, keepdims=True)
        first = jnp.min(jnp.where(work == mx, blk, float(n_blocks)), axis=0, keepdims=True)
        hit = blk == first
        sel = jnp.where(hit, 1.0, sel)
        work = jnp.where(hit, -3.0e38, work)
    return sel


def _transpose_01(x_t, eye):
    return lax.dot_general(eye, x_t.astype(BF16), (((1,), (1,)), ((), ())), preferred_element_type=F32)


def _nsa_body(q_ref, gt_ref, kaug_ref, vsel_ref, kvw_ref, kc_ref, vc_ref, covt_ref, o_ref, *, n_cmp, n_sel, topk):
    tq = q_ref.shape[0]
    n_blk = covt_ref.shape[0]
    n_cmp_pad = kc_ref.shape[0]
    g4 = NSA_GROUP
    i = pl.program_id(1)
    q0 = i * tq
    gates = jax.nn.sigmoid(gt_ref[...])

    qpos_row = q0 + lax.broadcasted_iota(jnp.int32, (1, tq), 1)
    qpos_row4 = jnp.concatenate([qpos_row] * g4, axis=1)
    qpos_col = q0 + lax.broadcasted_iota(jnp.int32, (tq, 1), 0)
    qpos_col4 = jnp.concatenate([qpos_col] * g4, axis=0)
    lane = lax.broadcasted_iota(jnp.int32, (tq, LANES), 1)
    blk = lax.broadcasted_iota(jnp.int32, (n_blk, 1), 0)
    blk_f = blk.astype(F32)
    qblk = qpos_row // SEL_BLOCK
    forced = (blk == 0) | (blk == qblk) | (blk == qblk - 1)
    future = blk > qblk
    exists = blk < n_sel
    eye = (lax.broadcasted_iota(jnp.int32, (tq, tq), 0) == lax.broadcasted_iota(jnp.int32, (tq, tq), 1)).astype(BF16)
    c_idx = lax.broadcasted_iota(jnp.int32, (n_cmp_pad, 1), 0)
    c_end = c_idx * CMP_STRIDE + (CMP_BLOCK - 1)
    cmp_valid = (c_end <= qpos_row4) & (c_idx < n_cmp)

    w_start = pl.multiple_of(jnp.maximum(q0 - WINDOW, 0), tq)
    w_len = WINDOW + tq
    kpos_w = w_start + lax.broadcasted_iota(jnp.int32, (1, w_len), 1)
    dw = qpos_col4 - kpos_w
    win_valid = (dw >= 0) & (dw < WINDOW)

    n_kt = (q0 + tq + NSA_TK - 1) // NSA_TK
    kc = kc_ref[...]
    vc = vc_ref[...]
    covt = covt_ref[...]
    for hk in range(NSA_KV_HEADS):
        own = (lane < HEAD_DIM) if hk == 0 else (lane >= HEAD_DIM)
        qpads = [q_ref[:, (hk * g4 + g) * LANES:(hk * g4 + g + 1) * LANES] for g in range(g4)]
        qpad4 = jnp.concatenate(qpads, axis=0)

        s_t = lax.dot_general(kc, qpad4, (((1,), (1,)), ((), ())), preferred_element_type=F32)
        p_t = _softmax_cols(s_t, cmp_valid)
        o_cmp = jnp.dot(p_t.T.astype(BF16), vc, preferred_element_type=F32)
        p_sum = p_t[:, 0:tq]
        for g in range(1, g4):
            p_sum = p_sum + p_t[:, g * tq:(g + 1) * tq]
        imp_t = sum(jnp.dot(covt, part, preferred_element_type=F32) for part in _split3(p_sum))
        imp_t = jnp.where(forced, FORCE_SCORE, jnp.where(future, -1.0, imp_t))
        imp_t = jnp.where(exists, imp_t, -2.0)
        sel_t = _select_topk(imp_t, blk_f, topk)
        selneg = jnp.where(_transpose_01(sel_t, eye) > 0.5, 0.0, NEG_INF)
        selneg2 = jnp.concatenate([selneg, selneg], axis=1).astype(BF16)

        lhs4 = jnp.concatenate([jnp.where(own, qp, selneg2) for qp in qpads], axis=0)

        def kt_body(kt, carry):
            m, l, acc = carry
            ks = pl.multiple_of(kt * NSA_TK, NSA_TK)
            ka = kaug_ref[pl.ds(ks, NSA_TK), hk * LANES:(hk + 1) * LANES]
            v = vsel_ref[pl.ds(ks, NSA_TK), :]
            s = lax.dot_general(lhs4, ka, (((1,), (1,)), ((), ())), preferred_element_type=F32)
            kpos = ks + lax.broadcasted_iota(jnp.int32, (1, NSA_TK), 1)
            s = jnp.where(kpos <= qpos_col4, s, NEG_INF)
            m_new = jnp.maximum(m, jnp.max(s, axis=1, keepdims=True))
            alpha = jnp.exp(m - m_new)
            p = jnp.exp(s - m_new)
            l = alpha * l + jnp.sum(p, axis=1, keepdims=True)
            acc = alpha * acc + jnp.dot(p.astype(BF16), v, preferred_element_type=F32)
            return m_new, l, acc

        init = (jnp.full((g4 * tq, 1), NEG_INF, F32), jnp.zeros((g4 * tq, 1), F32),
                jnp.zeros((g4 * tq, LANES), F32))
        _, l_sel, acc_sel = lax.fori_loop(0, n_kt, kt_body, init)
        o_sel = acc_sel / jnp.maximum(l_sel, 1e-30)

        kw = kvw_ref[pl.ds(w_start, w_len), 0:LANES]
        vw = kvw_ref[pl.ds(w_start, w_len), LANES:2 * LANES]
        s_w = lax.dot_general(qpad4, kw, (((1,), (1,)), ((), ())), preferred_element_type=F32)
        p_w = _softmax_rows(s_w, win_valid)
        o_win = jnp.dot(p_w.astype(BF16), vw, preferred_element_type=F32)

        def gate_col(j):
            cols = [gates[:, 3 * (hk * g4 + g) + j:3 * (hk * g4 + g) + j + 1] for g in range(g4)]
            return jnp.concatenate(cols, axis=0)

        o4 = gate_col(0) * o_cmp + gate_col(1) * o_sel + gate_col(2) * o_win
        for pair in range(g4 // 2):
            a = o4[(2 * pair) * tq:(2 * pair + 1) * tq]
            b = o4[(2 * pair + 1) * tq:(2 * pair + 2) * tq]
            if hk == 0:
                both = jnp.where(lane < HEAD_DIM, a, pltpu.roll(b, HEAD_DIM, 1))
            else:
                both = jnp.where(lane < HEAD_DIM, pltpu.roll(a, HEAD_DIM, 1), b)
            col = (hk * (g4 // 2) + pair) * LANES
            o_ref[:, col:col + LANES] = both


def _nsa_prompt(qpad, gates, kaug, vsel, kvwin, kc, vc, covt, batch, t_len, n_cmp, n_sel, topk):
    n = qpad.shape[0]
    nq = t_len // NSA_TQ
    n_cmp_pad = kc.shape[0] // batch
    qrow = lambda w: pl.BlockSpec((NSA_TQ, w), lambda b, i: (b * nq + i, 0))
    per_b = lambda r, w: pl.BlockSpec((r, w), lambda b, i: (b, 0))
    return pl.pallas_call(
        functools.partial(_nsa_body, n_cmp=n_cmp, n_sel=n_sel, topk=topk),
        grid=(batch, nq),
        in_specs=[qrow(QPAD_WIDTH), qrow(LANES), per_b(t_len, 2 * LANES), per_b(t_len, LANES),
                  per_b(t_len, 2 * LANES), per_b(n_cmp_pad, LANES), per_b(n_cmp_pad, LANES),
                  _const_spec(covt.shape)],
        out_specs=qrow(NSA_WIDTH),
        out_shape=jax.ShapeDtypeStruct((n, NSA_WIDTH), F32),
        compiler_params=_cparams("parallel", "parallel"),
        name="nsa_prompt",
    )(qpad, gates, kaug, vsel, kvwin, kc, vc, covt)


def _group_features(load_l, lane):
    heads = ([], [])
    for m in range(CMP_STRIDE // 2):
        a = load_l(2 * m)
        b = load_l(2 * m + 1)
        heads[0].append(jnp.where(lane < HEAD_DIM, a, pltpu.roll(b, HEAD_DIM, 1)))
        heads[1].append(jnp.where(lane < HEAD_DIM, pltpu.roll(a, HEAD_DIM, 1), b))
    return [jnp.concatenate(h, axis=1) for h in heads]


def _compress_heads(x_heads, pe8, w1, w2_h0, w2_h1):
    g = x_heads[0].shape[0]
    x2 = jnp.concatenate(x_heads, axis=0).astype(BF16)
    z = jnp.dot(x2, w1, preferred_element_type=F32)
    zb = jnp.dot(pe8.astype(BF16), w1, preferred_element_type=F32)
    bias = zb[0:1, 0:CMP_HIDDEN] + zb[1:2, CMP_HIDDEN:]
    nxt = pltpu.roll(z[:, CMP_HIDDEN:], 2 * g - 1, 0)
    hid = jax.nn.gelu(z[:, 0:CMP_HIDDEN] + nxt + bias).astype(BF16)
    return (jnp.dot(hid[0:g], w2_h0, preferred_element_type=F32)
            + jnp.dot(hid[g:], w2_h1, preferred_element_type=F32))


def _compress_body(krows_ref, vrows_ref, pe_ref, w1_ref, w2_ref, c_ref, sm_ref, sp_ref, kc_ref, vc_ref):
    g = kc_ref.shape[0]
    lane = lax.broadcasted_iota(jnp.int32, (g, LANES), 1)
    outs = []
    for t, ref in enumerate((krows_ref, vrows_ref)):
        load = lambda l, ref=ref: ref[pl.ds(l, g, stride=CMP_STRIDE), :]
        outs.append(_compress_heads(_group_features(load, lane), pe_ref[t], w1_ref[t], w2_ref[t, 0], w2_ref[t, 1]))
    kc_ref[...] = _rope(outs[0], c_ref[...], sm_ref[...], sp_ref[...]).astype(BF16)
    vc_ref[...] = outs[1].astype(BF16)


def _compress_prompt(rows, pe8, w1c, w2p, tabs_c, l, batch, t_len):
    g = t_len // CMP_STRIDE
    out = jax.ShapeDtypeStruct((batch * g, LANES), BF16)
    return pl.pallas_call(
        _compress_body,
        grid=(batch,),
        in_specs=[pl.BlockSpec((t_len, LANES), lambda b: (b, 0)), pl.BlockSpec((t_len, LANES), lambda b: (b, 1)),
                  _layer_spec((2, 8, CMP_STRIDE * HEAD_DIM), l),
                  _layer_spec((2, CMP_STRIDE * HEAD_DIM, 2 * CMP_HIDDEN), l),
                  _layer_spec((2, 2, CMP_HIDDEN, LANES), l),
                  _const_spec((g, LANES)), _const_spec((g, LANES)), _const_spec((g, LANES))],
        out_specs=[pl.BlockSpec((g, LANES), lambda b: (b, 0))] * 2,
        out_shape=[out, out],
        compiler_params=_cparams("parallel"),
        name="compress",
    )(rows, rows, pe8, w1c, w2p, *tabs_c)


GDN_TT = 512
HALO = 8


def _dot3(a, x):
    return sum(jnp.dot(a, part, preferred_element_type=F32) for part in _split3(x))


def _dot3_nt(a, x):
    return sum(lax.dot_general(a, part, (((1,), (1,)), ((), ())), preferred_element_type=F32) for part in _split3(x))


def _l2n(x):
    return x * lax.rsqrt(jnp.sum(x * x, axis=-1, keepdims=True) + 1e-6)


def _softplus(x):
    return jnp.maximum(x, 0.0) + jnp.log(1.0 + jnp.exp(-jnp.abs(x)))


def _gdn_prep_body(x_ref, halo_ref, gt_ref, cw_ref, alog_ref, dtb_ref,
                   qg_ref, kd_ref, w_ref, u_ref, att_ref, egl_ref,
                   ext_ref, q_s, k_s, kb_s, rhs_s, gc_s):
    c_len = GDN_CHUNK
    tt = x_ref.shape[0]
    nch = tt // c_len
    qk_w = GDN_HEADS * GDN_DK
    first = pl.program_id(1) == 0
    ext_ref[0:HALO, :] = jnp.where(first, 0.0, halo_ref[...])
    ext_ref[HALO:HALO + tt, :] = x_ref[...]
    c = ext_ref[pl.ds(HALO - (CONV_W - 1), tt), :] * cw_ref[0:1, :]
    for j in range(1, CONV_W):
        c = c + ext_ref[pl.ds(HALO - (CONV_W - 1) + j, tt), :] * cw_ref[j:j + 1, :]
    c = c * jax.nn.sigmoid(c)

    gt = gt_ref[...]
    g_full = -jnp.exp(alog_ref[...]) * _softplus(gt + dtb_ref[...])
    beta_full = jax.nn.sigmoid(gt)
    r = lax.broadcasted_iota(jnp.int32, (tt, tt), 0)
    cc = lax.broadcasted_iota(jnp.int32, (tt, tt), 1)
    same = (r // c_len) == (cc // c_len)
    gcum = _dot3(jnp.where(same & (cc <= r), 1.0, 0.0).astype(BF16), g_full)
    gtot = _dot3(jnp.where(same, 1.0, 0.0).astype(BF16), g_full)
    gc_s[...] = gcum
    for h in range(GDN_HEADS):
        sl = slice(h * GDN_DK, (h + 1) * GDN_DK)
        gc_col = gcum[:, GATE_A + h:GATE_A + h + 1]
        gl_col = gtot[:, GATE_A + h:GATE_A + h + 1]
        beta_col = beta_full[:, GATE_B + h:GATE_B + h + 1]
        qh = _l2n(c[:, h * GDN_DK:(h + 1) * GDN_DK]) * (GDN_DK ** -0.5)
        kh = _l2n(c[:, qk_w + h * GDN_DK:qk_w + (h + 1) * GDN_DK])
        vh = c[:, 2 * qk_w + h * GDN_DV:2 * qk_w + (h + 1) * GDN_DV]
        eg = jnp.exp(gc_col)
        kb = kh * beta_col
        q_s[:, sl] = qh
        k_s[:, sl] = kh
        kb_s[:, sl] = kb
        rhs_s[:, 2 * h * GDN_DK:(2 * h + 1) * GDN_DK] = vh * beta_col
        rhs_s[:, (2 * h + 1) * GDN_DK:(2 * h + 2) * GDN_DK] = kb * eg
        qg_ref[:, sl] = (qh * eg).astype(BF16)
        kd_ref[:, sl] = (kh * jnp.exp(gl_col - gc_col)).astype(BF16)
        egl_b = jnp.broadcast_to(jnp.exp(gl_col), (tt, GDN_DK))
        for ch in range(nch):
            egl_ref[ch, :, sl] = egl_b[ch * c_len:ch * c_len + 1, :]

    ii = lax.broadcasted_iota(jnp.int32, (c_len, c_len), 0)
    jj = lax.broadcasted_iota(jnp.int32, (c_len, c_len), 1)
    incl = ii >= jj
    strict = ii > jj
    eye = jnp.where(ii == jj, 1.0, 0.0)
    base_mask = (ii >> 1) == (jj >> 1)
    level_masks = [((ii >> (s + 1)) == (jj >> (s + 1))) & ((ii >> s) != (jj >> s))
                   for s in range(1, int(math.log2(c_len)))]
    hrow =lax.broadcasted_iota(jnp.int32, (GDN_HEADS * c_len, LANES), 0) // c_len
    hlane = lax.broadcasted_iota(jnp.int32, (GDN_HEADS * c_len, LANES), 1)
    pick = jnp.where(hlane == GATE_A + hrow, 1.0, 0.0).astype(BF16)
    nt = (((1,), (1,)), ((), ()))

    def chunk(ch, carry):
        r0 = pl.multiple_of(ch * c_len, c_len)
        gch = gc_s[pl.ds(r0, c_len), :]
        d_all = _dot3_nt(pick, gch)
        for h in range(GDN_HEADS):
            sl = slice(h * GDN_DK, (h + 1) * GDN_DK)
            kh = k_s[pl.ds(r0, c_len), sl].astype(BF16)
            kb = kb_s[pl.ds(r0, c_len), sl].astype(BF16)
            qh = q_s[pl.ds(r0, c_len), sl].astype(BF16)
            diff = gch[:, GATE_A + h:GATE_A + h + 1] - d_all[h * c_len:(h + 1) * c_len]
            decay = jnp.where(incl, jnp.exp(jnp.where(incl, diff, 0.0)), 0.0)
            a = jnp.where(strict, lax.dot_general(kb, kh, nt, preferred_element_type=F32) * decay, 0.0)
            att = lax.dot_general(qh, kh, nt, preferred_element_type=F32) * decay
            att_ref[h, pl.ds(r0, c_len), :] = att.astype(BF16)
            x = eye - jnp.where(base_mask, a, 0.0)
            for lvl_mask in level_masks:
                xb = x.astype(BF16)
                off = jnp.where(lvl_mask, a, 0.0).astype(BF16)
                x = x - jnp.dot(jnp.dot(xb, off, preferred_element_type=F32).astype(BF16), xb,
                                preferred_element_type=F32)
            uw = jnp.dot(x.astype(BF16), rhs_s[pl.ds(r0, c_len), 2 * h * GDN_DK:(2 * h + 2) * GDN_DK].astype(BF16),
                         preferred_element_type=F32)
            u_ref[pl.ds(r0, c_len), sl] = uw[:, 0:GDN_DV]
            w_ref[pl.ds(r0, c_len), sl] = uw[:, GDN_DV:].astype(BF16)
        return carry

    lax.fori_loop(0, nch, chunk, 0)


def _gdn_prep(qkv, gates, conv_w, alog_row, dtb_row, l, batch, t_len):
    n = qkv.shape[0]
    tt = GDN_TT
    nt = t_len // tt
    nch = tt // GDN_CHUNK
    row = lambda w: pl.BlockSpec((tt, w), lambda b, i: (b * nt + i, 0))
    outs = [(GDN_WIDTH, BF16), (GDN_WIDTH, BF16), (GDN_WIDTH, BF16), (GDN_WIDTH, F32)]
    return pl.pallas_call(
        _gdn_prep_body,
        grid=(batch, nt),
        in_specs=[row(GDN_QKV),
                  pl.BlockSpec((HALO, GDN_QKV), lambda b, i: (jnp.maximum((b * nt + i) * (tt // HALO) - 1, 0), 0)),
                  row(LANES), _layer_spec((CONV_W, GDN_QKV), l), _layer_spec((1, LANES), l), _layer_spec((1, LANES), l)],
        out_specs=[row(w) for w, _ in outs] + [
            pl.BlockSpec((GDN_HEADS, tt, GDN_CHUNK), lambda b, i: (0, b * nt + i, 0)),
            pl.BlockSpec((nch, 1, GDN_WIDTH), lambda b, i: (b * nt + i, 0, 0))],
        out_shape=[jax.ShapeDtypeStruct((n, w), dt) for w, dt in outs] + [
            jax.ShapeDtypeStruct((GDN_HEADS, n, GDN_CHUNK), BF16),
            jax.ShapeDtypeStruct((n // GDN_CHUNK, 1, GDN_WIDTH), F32)],
        scratch_shapes=[pltpu.VMEM((HALO + tt, GDN_QKV), F32), pltpu.VMEM((tt, GDN_WIDTH), F32),
                        pltpu.VMEM((tt, GDN_WIDTH), F32), pltpu.VMEM((tt, GDN_WIDTH), F32),
                        pltpu.VMEM((tt, 2 * GDN_WIDTH), F32), pltpu.VMEM((tt, LANES), F32)],
        compiler_params=_cparams("parallel", "parallel"),
        name="gdn_prep",
    )(qkv, qkv, gates, conv_w, alog_row, dtb_row)


def _gdn_scan_body(qg_ref, kd_ref, w_ref, u_ref, att_ref, egl_ref, z_ref, gn_ref, o_ref, sfin_ref, s_ref):
    c_len = GDN_CHUNK
    tt = qg_ref.shape[0]
    i = pl.program_id(1)

    @pl.when(i == 0)
    def _():
        s_ref[...] = jnp.zeros(s_ref.shape, F32)

    gn = gn_ref[...]
    tn = (((0,), (0,)), ((), ()))

    def chunk(ch, carry):
        r0 = pl.multiple_of(ch * c_len, c_len)
        for h in range(GDN_HEADS):
            sl = slice(h * GDN_DK, (h + 1) * GDN_DK)
            s = s_ref[h]
            sb = s.astype(BF16)
            v_new = u_ref[pl.ds(r0, c_len), sl] - jnp.dot(w_ref[pl.ds(r0, c_len), sl], sb, preferred_element_type=F32)
            vb = v_new.astype(BF16)
            o = (jnp.dot(qg_ref[pl.ds(r0, c_len), sl], sb, preferred_element_type=F32)
                 + jnp.dot(att_ref[h, pl.ds(r0, c_len), :], vb, preferred_element_type=F32))
            s_ref[h] = s * egl_ref[ch, :, sl] + lax.dot_general(
                kd_ref[pl.ds(r0, c_len), sl], vb, tn, preferred_element_type=F32)
            y = o * lax.rsqrt(jnp.mean(o * o, axis=-1, keepdims=True) + RMS_EPS) * gn
            zz = z_ref[pl.ds(r0, c_len), sl]
            o_ref[pl.ds(r0, c_len), sl] = y * (zz * jax.nn.sigmoid(zz))
        return carry

    lax.fori_loop(0, tt // c_len, chunk, 0)

    @pl.when(i == pl.num_programs(1) - 1)
    def _():
        sfin_ref[...] = s_ref[...]


def _gdn_scan(qg, kd, w, u, att, egl, z, gnorm, l, batch, t_len):
    n = qg.shape[0]
    tt = GDN_TT
    nt = t_len // tt
    nch = tt // GDN_CHUNK
    row = lambda: pl.BlockSpec((tt, GDN_WIDTH), lambda b, i: (b * nt + i, 0))
    return pl.pallas_call(
        _gdn_scan_body,
        grid=(batch, nt),
        in_specs=[row(), row(), row(), row(),
                  pl.BlockSpec((GDN_HEADS, tt, GDN_CHUNK), lambda b, i: (0, b * nt + i, 0)),
                  pl.BlockSpec((nch, 1, GDN_WIDTH), lambda b, i: (b * nt + i, 0, 0)),
                  row(), _layer_spec((1, GDN_DV), l)],
        out_specs=[row(), pl.BlockSpec((None, GDN_HEADS, GDN_DK, GDN_DV), lambda b, i: (b, 0, 0, 0))],
        out_shape=[jax.ShapeDtypeStruct((n, GDN_WIDTH), F32),
                   jax.ShapeDtypeStruct((batch, GDN_HEADS, GDN_DK, GDN_DV), F32)],
        scratch_shapes=[pltpu.VMEM((GDN_HEADS, GDN_DK, GDN_DV), F32)],
        compiler_params=_cparams("parallel", "arbitrary"),
        name="gdn_scan",
    )(qg, kd, w, u, att, egl, z, gnorm)


def _softmax_rows_extra(s, valid, s_new, valid_new):
    s = jnp.where(valid, s, NEG_INF)
    s_new = jnp.where(valid_new, s_new, NEG_INF)
    m = jnp.maximum(jnp.max(s, axis=1, keepdims=True), s_new)
    e = jnp.where(valid, jnp.exp(s - m), 0.0)
    e_new = jnp.where(valid_new, jnp.exp(s_new - m), 0.0)
    den = jnp.maximum(jnp.sum(e, axis=1, keepdims=True) + e_new, 1e-30)
    return e, e_new, 1.0 / den


def _nsa_sample_body(pt_ref, *refs, n_pages, page, n_buf, n_cmp, n_sel, topk):
    del pt_ref
    kcmp_pages = refs[0:n_pages]
    vcmp_pages = refs[n_pages:2 * n_pages]
    sel_pages = refs[2 * n_pages:3 * n_pages]
    (q_ref, gt_ref, rows_ref, wnew_ref, wbuf_ref, pe_ref, w1_ref, w2_ref, c_ref, sm_ref, sp_ref,
     cov_ref, eexp_ref, o_ref) = refs[3 * n_pages:]
    past = n_pages * page
    qpos = past
    gpp = page // CMP_STRIDE
    g = n_pages * gpp
    nt = (((1,), (1,)), ((), ()))
    nh = NSA_HEADS
    lane_g = lax.broadcasted_iota(jnp.int32, (g, LANES), 1)
    qf = q_ref[...]
    qp = qf.astype(BF16)
    row8 = lax.broadcasted_iota(jnp.int32, (nh, 1), 0)

    outs = []
    for t, pages in enumerate((kcmp_pages, vcmp_pages)):
        load = lambda l, pages=pages: jnp.concatenate(
            [pr[pl.ds(l, gpp, stride=CMP_STRIDE), :] for pr in pages], axis=0)
        outs.append(_compress_heads(_group_features(load, lane_g), pe_ref[t], w1_ref[t], w2_ref[t, 0], w2_ref[t, 1]))
    kc = _rope(outs[0], c_ref[...], sm_ref[...], sp_ref[...]).astype(BF16)
    vc = outs[1].astype(BF16)

    c_row = lax.broadcasted_iota(jnp.int32, (1, g), 1)
    cv_row = (c_row * CMP_STRIDE + (CMP_BLOCK - 1) <= qpos) & (c_row < n_cmp)
    s_c = lax.dot_general(qp, kc, nt, preferred_element_type=F32)
    p_c = _softmax_rows(s_c, cv_row)
    o_cmp = jnp.dot(p_c.astype(BF16), vc, preferred_element_type=F32)
    g0 = jnp.sum(p_c[0:NSA_GROUP], axis=0, keepdims=True)
    g1 = jnp.sum(p_c[NSA_GROUP:], axis=0, keepdims=True)
    psum8 = jnp.where(row8 == 0, g0, jnp.where(row8 == 1, g1, 0.0))
    imp8 = sum(jnp.dot(part, cov_ref[...], preferred_element_type=F32) for part in _split3(psum8))
    n_blk = imp8.shape[1]
    ib = lax.broadcasted_iota(jnp.int32, (n_blk, n_blk), 0)
    jb = lax.broadcasted_iota(jnp.int32, (n_blk, n_blk), 1)
    eye_b = jnp.where(ib == jb, 1.0, 0.0).astype(BF16)
    imp_t = _dot3_nt(eye_b, imp8)
    blk = lax.broadcasted_iota(jnp.int32, (n_blk, 1), 0)
    qblk = qpos // SEL_BLOCK
    forced = (blk == 0) | (blk == qblk) | (blk == qblk - 1)
    imp_t = jnp.where(forced, FORCE_SCORE, jnp.where(blk > qblk, -1.0, imp_t))
    imp_t = jnp.where(blk < n_sel, imp_t, -2.0)
    sel_t = _select_topk(imp_t, blk.astype(F32), topk)
    sel8 = lax.dot_general(sel_t.astype(BF16), eye_b, (((0,), (0,)), ((), ())), preferred_element_type=F32)
    head_sel = jnp.where(row8 < NSA_GROUP, sel8[0:1], sel8[1:2])

    key_sel = jnp.dot(head_sel.astype(BF16), eexp_ref[...], preferred_element_type=F32) > 0.5
    s_parts = [lax.dot_general(qp, pr[:, 0:LANES].astype(BF16), nt, preferred_element_type=F32) for pr in sel_pages]
    s_sel = jnp.concatenate(s_parts, axis=1)
    kpos = lax.broadcasted_iota(jnp.int32, (1, past), 1)
    new_row = rows_ref[...]
    k_new = new_row[:, 2 * LANES:3 * LANES].astype(BF16).astype(F32)
    v_new = new_row[:, 3 * LANES:4 * LANES].astype(BF16).astype(F32)
    s_new = jnp.sum(qf * k_new, axis=1, keepdims=True)
    new_sel = head_sel[:, qblk:qblk + 1] > 0.5
    e, e_new, inv = _softmax_rows_extra(s_sel, key_sel & (kpos <= qpos), s_new, new_sel)
    acc = e_new * v_new
    for p_i, pr in enumerate(sel_pages):
        acc = acc + jnp.dot(e[:, p_i * page:(p_i + 1) * page].astype(BF16), pr[:, LANES:2 * LANES].astype(BF16),
                            preferred_element_type=F32)
    o_sel = acc * inv

    kw = wbuf_ref[:, 0:LANES].astype(BF16)
    vw = wbuf_ref[:, LANES:2 * LANES].astype(BF16)
    s_w = lax.dot_general(qp, kw, nt, preferred_element_type=F32)
    kpos_w = (past - n_buf) + lax.broadcasted_iota(jnp.int32, (1, n_buf), 1)
    dw = qpos - kpos_w
    w_new = wnew_ref[...]
    kw_new = w_new[:, 0:LANES].astype(BF16).astype(F32)
    vw_new = w_new[:, LANES:2 * LANES].astype(BF16).astype(F32)
    sw_new = jnp.sum(qf * kw_new, axis=1, keepdims=True)
    ew, ew_new, inv_w = _softmax_rows_extra(s_w, (dw >= 0) & (dw < WINDOW) & (kpos_w >= 0), sw_new, row8 >= 0)
    o_win = (jnp.dot(ew.astype(BF16), vw, preferred_element_type=F32) + ew_new * vw_new) * inv_w

    gates = jax.nn.sigmoid(gt_ref[...])
    lane8 = lax.broadcasted_iota(jnp.int32, (nh, LANES), 1)
    gcol = lambda j: jnp.sum(jnp.where(lane8 == 3 * row8 + j, gates, 0.0), axis=1, keepdims=True)
    o8 = gcol(0) * o_cmp + gcol(1) * o_sel + gcol(2) * o_win
    lane1 = lax.broadcasted_iota(jnp.int32, (1, LANES), 1)
    for pair in range(nh // 2):
        a = o8[2 * pair:2 * pair + 1]
        b = o8[2 * pair + 1:2 * pair + 2]
        if 2 * pair < NSA_GROUP:
            both = jnp.where(lane1 < HEAD_DIM, a, pltpu.roll(b, HEAD_DIM, 1))
        else:
            both = jnp.where(lane1 < HEAD_DIM, pltpu.roll(a, HEAD_DIM, 1), b)
        o_ref[:, pair * LANES:(pair + 1) * LANES] = both


def _nsa_sample(page_table, cache4, q3, gates3, rows3, wnew3, win_cache, pe8, w1c, w2p, tabs_c, cov_s, eexp,
                l, n_cmp, n_sel, topk):
    bs, n_pages = page_table.shape
    page = cache4.shape[2]
    n_buf = win_cache.shape[2]
    g = n_pages * page // CMP_STRIDE
    page_spec = lambda p, w, col: pl.BlockSpec((None, None, page, w), lambda b, pt, p=p: (l, pt[b, p], 0, col))
    per_b = lambda *shape: pl.BlockSpec((None,) + shape, lambda b, pt: (b,) + (0,) * len(shape))
    cst = lambda shape: pl.BlockSpec(shape, lambda b, pt: (0,) * len(shape))
    lyr = lambda shape: pl.BlockSpec((None,) + shape, lambda b, pt: (l,) + (0,) * len(shape))
    in_specs = ([page_spec(p, LANES, 0) for p in range(n_pages)] + [page_spec(p, LANES, 1) for p in range(n_pages)]
                + [page_spec(p, 2 * LANES, 1) for p in range(n_pages)]
                + [per_b(NSA_HEADS, LANES), per_b(1, LANES), per_b(1, 4 * LANES), per_b(1, 2 * LANES),
                   pl.BlockSpec((None, None, n_buf, 2 * LANES), lambda b, pt: (l, b, 0, 0)),
                   lyr((2, 8, CMP_STRIDE * HEAD_DIM)), lyr((2, CMP_STRIDE * HEAD_DIM, 2 * CMP_HIDDEN)),
                   lyr((2, 2, CMP_HIDDEN, LANES)), cst((g, LANES)), cst((g, LANES)), cst((g, LANES)),
                   cst(cov_s.shape), cst(eexp.shape)])
    grid_spec = pltpu.PrefetchScalarGridSpec(
        num_scalar_prefetch=1, grid=(bs,), in_specs=in_specs,
        out_specs=pl.BlockSpec((None, 1, NSA_WIDTH), lambda b, pt: (b, 0, 0)))
    args = [cache4] * (3 * n_pages) + [q3, gates3, rows3, wnew3, win_cache, pe8, w1c, w2p, *tabs_c, cov_s, eexp]
    return pl.pallas_call(
        functools.partial(_nsa_sample_body, n_pages=n_pages, page=page, n_buf=n_buf, n_cmp=n_cmp, n_sel=n_sel,
                          topk=topk),
        grid_spec=grid_spec,
        out_shape=jax.ShapeDtypeStruct((bs, 1, NSA_WIDTH), F32),
        compiler_params=_cparams("parallel"),
        name="nsa_sample",
    )(page_table, *args)


GDN_SB = 8


def _transpose3(x, eye):
    return _dot3_nt(eye, x)


def _gdn_sample_body(x_ref, cs_ref, gt_ref, z_ref, s_ref, cw_ref, alog_ref, dtb_ref, gn_ref, o_ref, so_ref):
    sb = x_ref.shape[0]
    qk_w = GDN_HEADS * GDN_DK
    c = x_ref[...] * cw_ref[CONV_W - 1:CONV_W, :]
    for j in range(CONV_W - 1):
        c = c + cs_ref[:, j * GDN_QKV:(j + 1) * GDN_QKV] * cw_ref[j:j + 1, :]
    c = c * jax.nn.sigmoid(c)
    gt = gt_ref[...]
    g_full = -jnp.exp(alog_ref[...]) * _softplus(gt + dtb_ref[...])
    eg_full = jnp.exp(g_full)
    beta_full = jax.nn.sigmoid(gt)
    ii = lax.broadcasted_iota(jnp.int32, (LANES, LANES), 0)
    jj = lax.broadcasted_iota(jnp.int32, (LANES, LANES), 1)
    eye = jnp.where(ii == jj, 1.0, 0.0).astype(BF16)
    gn = gn_ref[...]
    for h in range(GDN_HEADS):
        sl = slice(h * GDN_DK, (h + 1) * GDN_DK)
        q = _l2n(c[:, h * GDN_DK:(h + 1) * GDN_DK]) * (GDN_DK ** -0.5)
        k = _l2n(c[:, qk_w + h * GDN_DK:qk_w + (h + 1) * GDN_DK])
        v = c[:, 2 * qk_w + h * GDN_DV:2 * qk_w + (h + 1) * GDN_DV]
        eg = jnp.broadcast_to(eg_full[:, GATE_A + h:GATE_A + h + 1], (sb, GDN_DK))
        beta = jnp.broadcast_to(beta_full[:, GATE_B + h:GATE_B + h + 1], (sb, GDN_DK))
        att = jnp.sum(q * k, axis=1, keepdims=True)
        k_t = _transpose3(k, eye)
        w_t = _transpose3(k * beta * eg, eye)
        qg_t = _transpose3(q * eg, eye)
        u = v * beta
        zz = z_ref[:, sl]
        for i in range(sb):
            s = s_ref[i, h]
            v_new = u[i:i + 1] - jnp.sum(w_t[:, i:i + 1] * s, axis=0, keepdims=True)
            o = jnp.sum(qg_t[:, i:i + 1] * s, axis=0, keepdims=True) + att[i:i + 1] * v_new
            so_ref[i, h] = s * eg[i:i + 1] + k_t[:, i:i + 1] * v_new
            y = o * lax.rsqrt(jnp.mean(o * o, axis=-1, keepdims=True) + RMS_EPS) * gn
            zi = zz[i:i + 1]
            o_ref[i:i + 1, sl] = y * (zi * jax.nn.sigmoid(zi))


def _gdn_sample(qkv, conv_state, gates, z, s_state, conv_w, alog_row, dtb_row, gnorm, l):
    bs = qkv.shape[0]
    sb = GDN_SB
    row = lambda w: pl.BlockSpec((sb, w), lambda i: (i, 0))
    st = pl.BlockSpec((None, sb, GDN_HEADS, GDN_DK, GDN_DV), lambda i: (l, i, 0, 0, 0))
    return pl.pallas_call(
        _gdn_sample_body,
        grid=(bs // sb,),
        in_specs=[row(GDN_QKV), pl.BlockSpec((None, sb, (CONV_W - 1) * GDN_QKV), lambda i: (l, i, 0)),
                  row(LANES), row(GDN_WIDTH), st,
                  _layer_spec((CONV_W, GDN_QKV), l), _layer_spec((1, LANES), l), _layer_spec((1, LANES), l),
                  _layer_spec((1, GDN_DV), l)],
        out_specs=[row(GDN_WIDTH), pl.BlockSpec((sb, GDN_HEADS, GDN_DK, GDN_DV), lambda i: (i, 0, 0, 0))],
        out_shape=[jax.ShapeDtypeStruct((bs, GDN_WIDTH), F32),
                   jax.ShapeDtypeStruct((bs, GDN_HEADS, GDN_DK, GDN_DV), F32)],
        compiler_params=_cparams("parallel"),
        name="gdn_sample",
    )(qkv, conv_state, gates, z, s_state, conv_w, alog_row, dtb_row, gnorm)


def _mem_sample_body(q_ref, kv_ref, o_ref):
    nt = (((1,), (1,)), ((), ()))
    q = q_ref[...] * (MEM_HD ** -0.5)
    row = lax.broadcasted_iota(jnp.int32, (8, MEM_WIDTH), 0)
    lane = lax.broadcasted_iota(jnp.int32, (8, MEM_WIDTH), 1)
    own = (lane // MEM_HD) == row
    qm = jnp.where(own, q, 0.0).astype(BF16)
    kv = kv_ref[...].astype(BF16)
    s = lax.dot_general(qm, kv[:, 0:MEM_WIDTH], nt, preferred_element_type=F32)
    e = jnp.exp(s - jnp.max(s, axis=-1, keepdims=True))
    p = e / jnp.sum(e, axis=-1, keepdims=True)
    o = jnp.dot(p.astype(BF16), kv[:, MEM_WIDTH:], preferred_element_type=F32)
    o_ref[...] = jnp.sum(jnp.where(own, o, 0.0), axis=0, keepdims=True)


def _mem_sample(q3, mem_cache, l):
    bs = q3.shape[0]
    m = mem_cache.shape[2]
    return pl.pallas_call(
        _mem_sample_body,
        grid=(bs,),
        in_specs=[pl.BlockSpec((None, 1, MEM_WIDTH), lambda b: (b, 0, 0)),
                  pl.BlockSpec((None, None, m, 2 * MEM_WIDTH), lambda b: (l, b, 0, 0))],
        out_specs=pl.BlockSpec((None, 1, MEM_WIDTH), lambda b: (b, 0, 0)),
        out_shape=jax.ShapeDtypeStruct((bs, 1, MEM_WIDTH), F32),
        compiler_params=_cparams("parallel"),
        name="mem_sample",
    )(q3, mem_cache)


def _rope_tables(pos):
    half = ROPE_DIM // 2
    inv = ROPE_THETA ** (-2.0 * jnp.arange(half, dtype=F32) / ROPE_DIM)
    ang = pos.astype(F32)[:, None] * inv[None, :]
    cos, sin = jnp.cos(ang), jnp.sin(ang)
    n = pos.shape[0]
    one = jnp.ones((n, HEAD_DIM - ROPE_DIM), F32)
    zero = jnp.zeros((n, HEAD_DIM - ROPE_DIM), F32)
    zh = jnp.zeros((n, half), F32)
    c = jnp.concatenate([cos, cos, one], axis=1)
    sm = jnp.concatenate([-sin, zh, zero], axis=1)
    sp = jnp.concatenate([zh, sin, zero], axis=1)
    return tuple(jnp.concatenate([t, t], axis=1) for t in (c, sm, sp))


def _block_onehot(pos):
    e = (pos[:, None] // SEL_BLOCK == jnp.arange(HEAD_DIM)[None, :]).astype(F32)
    return jnp.concatenate([e, e], axis=1)


def _rope_jnp(x, pos):
    half = ROPE_DIM // 2
    inv = ROPE_THETA ** (-2.0 * jnp.arange(half, dtype=F32) / ROPE_DIM)
    ang = pos.astype(F32)[:, None] * inv[None, :]
    cos = jnp.cos(ang)[None, :, None, :]
    sin = jnp.sin(ang)[None, :, None, :]
    x1 = x[..., :half]
    x2 = x[..., half:ROPE_DIM]
    return jnp.concatenate([x1 * cos - x2 * sin, x2 * cos + x1 * sin, x[..., ROPE_DIM:]], axis=-1)


def _masked_softmax(s, mask):
    s = jnp.where(mask, s, NEG_INF)
    m = jnp.max(s, axis=-1, keepdims=True)
    e = jnp.where(mask, jnp.exp(s - m), 0.0)
    return e / jnp.maximum(jnp.sum(e, axis=-1, keepdims=True), 1e-30)


def _compress_jnp(k, pe, w1, w2):
    b, t, hk, dh = k.shape
    n_cmp = (t - CMP_BLOCK) // CMP_STRIDE + 1
    start = jnp.arange(n_cmp) * CMP_STRIDE
    idx = start[:, None] + jnp.arange(CMP_BLOCK)[None, :]
    blk = k[:, idx] + pe[None, None, :, None, :]
    blk = jnp.swapaxes(blk, 2, 3).reshape(b, n_cmp, hk, CMP_BLOCK * dh)
    return jax.nn.gelu(blk @ w1) @ w2, start + CMP_BLOCK - 1


def _block_coverage(c_start, n_sel):
    b_start = jnp.arange(n_sel) * SEL_BLOCK
    lo = jnp.maximum(c_start[:, None], b_start[None, :])
    hi = jnp.minimum(c_start[:, None] + CMP_BLOCK, b_start[None, :] + SEL_BLOCK)
    return jnp.clip(hi - lo, 0, None).astype(F32) / CMP_BLOCK


def _sample_nsa_jnp(q, gate_logits, rows_new, win_new, nsa_past, win_buf, w_buf, cmp_pe, w_cmp1, w_cmp2, topk):
    b, t = q.shape[:2]
    pos0 = nsa_past.shape[1]
    qpos = pos0 + jnp.arange(t)
    q = q.reshape(b, t, NSA_KV_HEADS, NSA_GROUP, HEAD_DIM)
    rows = jnp.concatenate([nsa_past, rows_new], axis=1)
    t_kv = rows.shape[1]
    kc, c_end = _compress_jnp(rows[:, :, 0], cmp_pe[0], w_cmp1[0], w_cmp2[0])
    vc, _ = _compress_jnp(rows[:, :, 1], cmp_pe[1], w_cmp1[1], w_cmp2[1])
    kc = _rope_jnp(kc, c_end)
    n_sel = -(-t_kv // SEL_BLOCK)
    cov = _block_coverage(c_end - CMP_BLOCK + 1, n_sel)
    sel = jnp.pad(rows[:, :, 2:], ((0, 0), (0, n_sel * SEL_BLOCK - t_kv), (0, 0), (0, 0), (0, 0)))
    sel = sel.reshape(b, n_sel, SEL_BLOCK, 2, NSA_KV_HEADS, HEAD_DIM).transpose(3, 0, 4, 1, 2, 5)
    k_blk, v_blk = sel[0], sel[1]
    win_all = jnp.concatenate([win_buf, win_new], axis=1)
    n_buf = win_buf.shape[1]
    gates = jax.nn.sigmoid(gate_logits).reshape(b, t, NSA_KV_HEADS, NSA_GROUP, 3)
    s = jnp.einsum('bqhgd,bchd->bhgqc', q, kc) * ATTN_SCALE
    p = _masked_softmax(s, c_end[None, :] <= qpos[:, None])
    o_cmp = jnp.einsum('bhgqc,bchd->bqhgd', p, vc)
    imp = jnp.einsum('bhgqc,cn->bhqn', p, cov)
    blk = jnp.arange(n_sel)[None, :]
    q_blk = (qpos // SEL_BLOCK)[:, None]
    forced = (blk == 0) | (blk == q_blk) | (blk == q_blk - 1)
    imp = jnp.where(forced, FORCE_SCORE, jnp.where(blk > q_blk, -1.0, imp))
    _, idx = lax.top_k(imp, min(topk, n_sel))
    bi = jnp.arange(b)[:, None, None, None]
    hi = jnp.arange(NSA_KV_HEADS)[None, :, None, None]
    kg = k_blk[bi, hi, idx]
    vg = v_blk[bi, hi, idx]
    k_pos = idx[..., None] * SEL_BLOCK + jnp.arange(SEL_BLOCK)
    mask = (k_pos <= qpos[None, None, :, None, None])[:, :, None]
    s2 = jnp.einsum('bqhgd,bhqksd->bhgqks', q, kg) * ATTN_SCALE
    shp = s2.shape
    p2 = _masked_softmax(s2.reshape(shp[:-2] + (-1,)), mask.reshape(mask.shape[:-2] + (-1,))).reshape(shp)
    o_sel = jnp.einsum('bhgqks,bhqksd->bqhgd', p2, vg)
    kw_pos = pos0 - n_buf + jnp.arange(n_buf + t)
    sw = jnp.einsum('bqhgd,bkhd->bhgqk', q, win_all[:, :, 0]) * ATTN_SCALE
    diff = qpos[:, None] - kw_pos[None, :]
    pw = _masked_softmax(sw, (diff >= 0) & (diff < WINDOW) & (kw_pos[None, :] >= 0))
    o_win = jnp.einsum('bhgqk,bkhd->bqhgd', pw, win_all[:, :, 1])
    o = gates[..., 0:1] * o_cmp + gates[..., 1:2] * o_sel + gates[..., 2:3] * o_win
    return o.reshape(b, t, NSA_WIDTH), win_all[:, win_all.shape[1] - w_buf:]


def _to_chunks(a, c, pad):
    a = jnp.moveaxis(a, 1, 2)
    if pad:
        a = jnp.pad(a, [(0, 0), (0, 0), (0, pad)] + [(0, 0)] * (a.ndim - 3))
    b, h, tp = a.shape[:3]
    a = a.reshape((b, h, tp // c, c) + a.shape[3:])
    return jnp.moveaxis(a, 2, 0)


def _chunk_gated_delta_jnp(q, k, v, g, beta, s0):
    b, t, h, _ = q.shape
    c = min(GDN_CHUNK, t)
    pad = (-t) % c
    qc, kc, vc = _to_chunks(q, c, pad), _to_chunks(k, c, pad), _to_chunks(v, c, pad)
    gc = jnp.cumsum(_to_chunks(g, c, pad), axis=-1)
    bc = _to_chunks(beta, c, pad)
    ii = jnp.arange(c)
    incl = ii[:, None] >= ii[None, :]
    strict = ii[:, None] > ii[None, :]
    diff = gc[..., :, None] - gc[..., None, :]
    decay = jnp.where(incl, jnp.exp(jnp.where(incl, diff, 0.0)), 0.0)
    kb = kc * bc[..., None]
    a = jnp.where(strict, jnp.einsum('...id,...jd->...ij', kb, kc) * decay, 0.0)
    eye = jnp.eye(c, dtype=F32)
    tm = lax.linalg.triangular_solve(eye + a, jnp.broadcast_to(eye, a.shape), left_side=True, lower=True)
    u = jnp.einsum('...ij,...jd->...id', tm, vc * bc[..., None])
    w = jnp.einsum('...ij,...jd->...id', tm, kb * jnp.exp(gc)[..., None])

    def step(s, inp):
        qi, ki, ui, wi, gi, di = inp
        v_new = ui - jnp.einsum('bhck,bhkv->bhcv', wi, s)
        att = jnp.einsum('bhik,bhjk->bhij', qi, ki) * di
        o = jnp.einsum('bhck,bhkv->bhcv', qi * jnp.exp(gi)[..., None], s) + jnp.einsum('bhij,bhjv->bhiv', att, v_new)
        gl = gi[..., -1]
        s = s * jnp.exp(gl)[..., None, None] + jnp.einsum(
            'bhck,bhcv->bhkv', ki * jnp.exp(gl[..., None] - gi)[..., None], v_new)
        return s, o

    s, o = lax.scan(step, s0, (qc, kc, u, w, gc, decay))
    o = jnp.moveaxis(o, 0, 2).reshape(b, h, -1, o.shape[-1])[:, :, :t]
    return jnp.moveaxis(o, 1, 2), s


def _gdn_jnp(qkv, a, bb, z, conv_buf, state, conv_w, a_log, dt_bias, norm_w):
    b, t, _ = qkv.shape
    xp = jnp.concatenate([conv_buf, qkv], axis=1)
    c = xp[:, 0:t] * conv_w[0]
    for j in range(1, CONV_W):
        c = c + xp[:, j:j + t] * conv_w[j]
    c = jax.nn.silu(c)
    new_buf = xp[:, t:]
    qk_w = GDN_HEADS * GDN_DK
    l2 = lambda x: x * lax.rsqrt(jnp.sum(x * x, axis=-1, keepdims=True) + 1e-6)
    q = l2(c[..., :qk_w].reshape(b, t, GDN_HEADS, GDN_DK)) * GDN_DK ** -0.5
    k = l2(c[..., qk_w:2 * qk_w].reshape(b, t, GDN_HEADS, GDN_DK))
    v = c[..., 2 * qk_w:].reshape(b, t, GDN_HEADS, GDN_DV)
    g = -jnp.exp(a_log) * jax.nn.softplus(a + dt_bias)
    beta = jax.nn.sigmoid(bb)
    o, new_state = _chunk_gated_delta_jnp(q, k, v, g, beta, state)
    o = o * lax.rsqrt(jnp.mean(o * o, axis=-1, keepdims=True) + RMS_EPS) * norm_w
    o = o * jax.nn.silu(z).reshape(b, t, GDN_HEADS, GDN_DV)
    return o.reshape(b, t, GDN_WIDTH), new_buf, new_state


def _mem_attend_sample_jnp(x, gains, mem_kv, w_q, w_o):
    xf = x
    h = xf * lax.rsqrt(jnp.mean(xf * xf, axis=-1, keepdims=True) + RMS_EPS) * gains[4]
    q = (h @ w_q).reshape(-1, MEM_HEADS, MEM_HD)
    s = jnp.einsum('bhd,bmhd->bhm', q, mem_kv[:, :, 0]) * MEM_HD ** -0.5
    p = jax.nn.softmax(s, axis=-1)
    o = jnp.einsum('bhm,bmhd->bhd', p, mem_kv[:, :, 1]).reshape(-1, MEM_WIDTH)
    y = o @ w_o
    return x + y * lax.rsqrt(jnp.mean(y * y, axis=-1, keepdims=True) + RMS_EPS) * gains[5]


def _pack_w_in(w_in):
    nl, d, _ = w_in.shape
    o_q, o_kv, o_g = NSA_WIDTH, NSA_WIDTH + KV6_WIDTH, NSA_WIDTH + KV6_WIDTH + 3 * NSA_HEADS
    o_qkv = o_g
    o_a = o_qkv + GDN_QKV
    o_b = o_a + GDN_HEADS
    o_z = o_b + GDN_HEADS
    wq = w_in[:, :, :o_q].reshape(nl, d, NSA_HEADS, HEAD_DIM)
    zq = jnp.zeros_like(wq)
    first = jnp.concatenate([wq, zq], axis=-1)
    second = jnp.concatenate([zq, wq], axis=-1)
    kv_of_head = (jnp.arange(NSA_HEADS) // NSA_GROUP)[None, None, :, None]
    wq_pad = jnp.where(kv_of_head == 0, first, second).reshape(nl, d, QPAD_WIDTH)
    gate_grp = jnp.concatenate([w_in[:, :, o_kv:o_g], w_in[:, :, o_a:o_z],
                                jnp.zeros((nl, d, LANES - 3 * NSA_HEADS - 2 * GDN_HEADS), w_in.dtype)], axis=-1)
    packed = jnp.concatenate([wq_pad, w_in[:, :, o_q:o_kv], w_in[:, :, o_qkv:o_a], w_in[:, :, o_z:], gate_grp], axis=-1)
    return packed.astype(BF16)


def kernel(x_prompt, x_sample, cache_nsa_kv, cache_win_kv, state_gdn_S, state_gdn_conv, cache_mem_kv, page_table, mem_prompt, norm_gains, mem_norm, w_ffn_gu, w_ffn_down, w_in, w_out, cmp_pe, w_cmp1, w_cmp2, gdn_conv, gdn_A_log, gdn_dt_bias, gdn_norm, w_mem_q, w_mem_kv, w_mem_o):
    bp, t_len, d = x_prompt.shape
    bs = x_sample.shape[0]
    depth = w_in.shape[0]
    w_buf = cache_win_kv.shape[2]
    n_pages = page_table.shape[1]
    page = cache_nsa_kv.shape[2]
    past_len = n_pages * page
    n_mem = mem_prompt.shape[1]
    np_tok = bp * t_len

    gains = norm_gains.reshape(depth, 8, 1, d)
    wgu = w_ffn_gu.astype(BF16)
    wd = w_ffn_down.astype(BF16)
    w_in_p = _pack_w_in(w_in)
    w_out_b = w_out.astype(BF16)
    w_mq = w_mem_q.astype(BF16)
    w_mkv = w_mem_kv.astype(BF16)
    w_mo = w_mem_o.astype(BF16)
    mem_gain = mem_norm.reshape(depth, 1, d)

    pos_p = jnp.arange(t_len)
    tabs_p = _rope_tables(pos_p) + (_block_onehot(pos_p),)
    pos_s = jnp.full((bs,), past_len, jnp.int32)
    tabs_s = _rope_tables(pos_s) + (jnp.zeros((bs, LANES), F32),)

    half_k = CMP_STRIDE * HEAD_DIM
    pe8 = jnp.pad(cmp_pe.reshape(depth, 2, CMP_BLOCK // CMP_STRIDE, half_k), ((0, 0), (0, 0), (0, 6), (0, 0)))
    w1c = jnp.concatenate([w_cmp1[:, :, :half_k], w_cmp1[:, :, half_k:]], axis=-1).astype(BF16)
    z2 = jnp.zeros_like(w_cmp2)
    w2p = jnp.stack([jnp.concatenate([w_cmp2, z2], axis=-1), jnp.concatenate([z2, w_cmp2], axis=-1)], axis=2).astype(BF16)
    tabs_c = _rope_tables(jnp.arange(t_len // CMP_STRIDE) * CMP_STRIDE + (CMP_BLOCK - 1))
    lane_pad = lambda v: jnp.pad(v, ((0, 0), (GATE_A, LANES - GATE_A - GDN_HEADS))).reshape(depth, 1, LANES)
    alog_row = lane_pad(gdn_A_log)
    dtb_row = lane_pad(gdn_dt_bias)
    gnorm = gdn_norm.reshape(depth, 1, GDN_DV)

    n_cmp = (t_len - CMP_BLOCK) // CMP_STRIDE + 1
    n_cmp_pad = t_len // CMP_STRIDE
    n_sel = -(-t_len // SEL_BLOCK)
    c_start = jnp.arange(n_cmp) * CMP_STRIDE
    assert n_sel <= HEAD_DIM and t_len >= WINDOW + NSA_TQ and t_len % NSA_TK == 0
    covt = jnp.pad(_block_coverage(c_start, n_sel).T, ((0, HEAD_DIM - n_sel), (0, n_cmp_pad - n_cmp))).astype(BF16)
    topk = min(SEL_TOPK, n_sel)

    assert x_sample.shape[1] == 1 and past_len % SEL_BLOCK == 0 and bs % GDN_SB == 0
    t_kv = past_len + 1
    n_cmp_s = (t_kv - CMP_BLOCK) // CMP_STRIDE + 1
    assert (n_cmp_s - 1) * CMP_STRIDE + CMP_BLOCK <= past_len
    g_s = past_len // CMP_STRIDE
    n_sel_s = -(-t_kv // SEL_BLOCK)
    assert n_sel_s <= HEAD_DIM
    topk_s = min(SEL_TOPK, n_sel_s)
    cov_s = jnp.pad(_block_coverage(jnp.arange(n_cmp_s) * CMP_STRIDE, n_sel_s),
                    ((0, g_s - n_cmp_s), (0, HEAD_DIM - n_sel_s))).astype(BF16)
    eexp = (jnp.arange(past_len)[None, :] // SEL_BLOCK == jnp.arange(HEAD_DIM)[:, None]).astype(BF16)
    tabs_cs = _rope_tables(jnp.arange(g_s) * CMP_STRIDE + (CMP_BLOCK - 1))
    cache4 = cache_nsa_kv.reshape(depth, cache_nsa_kv.shape[1], page, 4 * LANES)
    win_cache = cache_win_kv.reshape(depth, bs, w_buf, 2 * LANES)
    conv_cache = state_gdn_conv.reshape(depth, bs, (CONV_W - 1) * GDN_QKV)
    mem_cache = cache_mem_kv.reshape(depth, bs, n_mem, 2 * MEM_WIDTH)

    tm_p = 512
    yp = x_prompt.reshape(np_tok, d)
    ys = x_sample.reshape(bs, d)
    mem_flat = mem_prompt.reshape(bp * n_mem, d)
    outs = [[] for _ in range(9)]
    for l in range(depth):
        mem_kv_p = _normmm(mem_flat, mem_gain, (l,), w_mkv, l, n_mem)
        yp = _ffn(yp, gains, wgu, wd, l, 0, tm_p)
        qpad, rows, win, kaug, vsel, kvwin, gates, qkv, z = _inproj(yp, gains, w_in_p, tabs_p, l, tm_p)
        rows5 = rows.reshape(bp, t_len, 4, NSA_KV_HEADS, HEAD_DIM)
        kc, vc = _compress_prompt(rows, pe8, w1c, w2p, tabs_c, l, bp, t_len)
        o_nsa = _nsa_prompt(qpad, gates, kaug, vsel, kvwin, kc, vc, covt, bp, t_len, n_cmp, n_sel, topk)
        qg, kd, w_g, u_g, att, egl = _gdn_prep(qkv, gates, gdn_conv, alog_row, dtb_row, l, bp, t_len)
        o_gdn, s_p = _gdn_scan(qg, kd, w_g, u_g, att, egl, z, gnorm, l, bp, t_len)
        conv_p = qkv.reshape(bp, t_len, GDN_QKV)[:, t_len - (CONV_W - 1):]
        yp = _outproj(yp, o_nsa, o_gdn, w_out_b, gains, l, tm_p)
        yp = _mem_block(yp, gains, w_mq, w_mo, mem_kv_p, l, tm_p, t_len // tm_p)
        yp = _ffn(yp, gains, wgu, wd, l, 1, tm_p)
        win5 = win.reshape(bp, t_len, 2, NSA_KV_HEADS, HEAD_DIM)
        win_p = win5[:, t_len - w_buf:] if t_len >= w_buf else jnp.pad(
            win5, ((0, 0), (w_buf - t_len, 0), (0, 0), (0, 0), (0, 0)))

        ys = _ffn(ys, gains, wgu, wd, l, 0, bs)
        qpad_s, rows_s, win_s, _, _, _, gates_s, qkv_s, z_s = _inproj(ys, gains, w_in_p, tabs_s, l, bs)
        o_nsa_s = _nsa_sample(
            page_table, cache4, qpad_s.astype(F32).reshape(bs, NSA_HEADS, LANES), gates_s.reshape(bs, 1, LANES),
            rows_s.reshape(bs, 1, 4 * LANES), win_s.reshape(bs, 1, 2 * LANES), win_cache, pe8, w1c, w2p, tabs_cs,
            cov_s, eexp, l, n_cmp_s, n_sel_s, topk_s).reshape(bs, NSA_WIDTH)
        o_gdn_s, s_s = _gdn_sample(qkv_s, conv_cache, gates_s, z_s, state_gdn_S, gdn_conv, alog_row, dtb_row, gnorm, l)
        ys = _outproj(ys, o_nsa_s, o_gdn_s, w_out_b, gains, l, bs)
        q_m = _normmm(ys, gains, (l, 4), w_mq, l, bs)
        o_m = _mem_sample(q_m.reshape(bs, 1, MEM_WIDTH), mem_cache, l).reshape(bs, MEM_WIDTH)
        ys = _outproj(ys, o_m[:, :MEM_WIDTH // 2], o_m[:, MEM_WIDTH // 2:], w_mo, gains, l, bs, gain_idx=5)
        ys = _ffn(ys, gains, wgu, wd, l, 1, bs)
        win_all = jnp.concatenate([cache_win_kv[l], win_s.reshape(bs, 1, 2, NSA_KV_HEADS, HEAD_DIM)], axis=1)
        wbuf_s = win_all[:, win_all.shape[1] - w_buf:]
        conv_s = jnp.concatenate([state_gdn_conv[l], qkv_s[:, None, :]], axis=1)[:, 1:]

        for lst, val in zip(outs, (rows5, win_p, s_p, conv_p, mem_kv_p.reshape(bp, n_mem, 2, MEM_HEADS, MEM_HD),
                                   rows_s.reshape(bs, 1, 4, NSA_KV_HEADS, HEAD_DIM), wbuf_s, s_s, conv_s)):
            lst.append(val)
    return (yp.reshape(bp, t_len, d), ys.reshape(bs, 1, d)) + tuple(jnp.stack(v) for v in outs)
```

```python
import functools
import math

import numpy as np
import jax
import jax.numpy as jnp
from jax import lax
from jax.experimental import pallas as pl
from jax.experimental.pallas import tpu as pltpu

F32 = jnp.float32
BF16 = jnp.bfloat16

HEAD_DIM = 64
NSA_HEADS = 8
NSA_KV_HEADS = 2
NSA_GROUP = NSA_HEADS // NSA_KV_HEADS
NSA_WIDTH = NSA_HEADS * HEAD_DIM
ROPE_DIM = HEAD_DIM // 4
ROPE_THETA = 500000.0
ATTN_SCALE = HEAD_DIM ** -0.5
CMP_BLOCK = 32
CMP_STRIDE = 16
CMP_HIDDEN = 2 * HEAD_DIM
SEL_BLOCK = 64
SEL_TOPK = 16
WINDOW = 512
FORCE_SCORE = 1.0e4
GDN_DK = 128
GDN_DV = 128
GDN_HEADS = 4
GDN_WIDTH = GDN_HEADS * GDN_DV
GDN_QKV = GDN_HEADS * (2 * GDN_DK + GDN_DV)
CONV_W = 4
GDN_CHUNK = 64
MEM_HEADS = 4
MEM_HD = 64
MEM_WIDTH = MEM_HEADS * MEM_HD
RMS_EPS = 1e-6
NEG_INF = -1.0e30
KV6_WIDTH = 6 * NSA_KV_HEADS * HEAD_DIM

LANES = 128
VMEM_LIMIT_BYTES = 56 * 1024 * 1024

QPAD_WIDTH = NSA_HEADS * LANES
COL_Q = 0
COL_KV = COL_Q + QPAD_WIDTH
COL_QKV = COL_KV + KV6_WIDTH
COL_Z = COL_QKV + GDN_QKV
COL_GATE = COL_Z + GDN_WIDTH
IN_PACKED = COL_GATE + LANES
GATE_A = 3 * NSA_HEADS
GATE_B = GATE_A + GDN_HEADS


def _cparams(*sem):
    return pltpu.CompilerParams(dimension_semantics=sem, vmem_limit_bytes=VMEM_LIMIT_BYTES)


def _rms(x, w):
    return x * lax.rsqrt(jnp.mean(x * x, axis=-1, keepdims=True) + RMS_EPS) * w


def _const_spec(shape):
    nd = len(shape)
    return pl.BlockSpec(shape, lambda *_: (0,) * nd)


def _layer_spec(shape, *lead):
    nlead = len(lead)
    nd = len(shape)
    return pl.BlockSpec((None,) * nlead + tuple(shape), lambda *_: tuple(lead) + (0,) * nd)


FFN_CHUNK = 256


def _ffn_body(x_ref, g0_ref, g1_ref, wgu_ref, wd_ref, o_ref, acc_ref, *, d_ff):
    x = x_ref[...]
    h = _rms(x, g0_ref[...]).astype(BF16)
    for f in range(d_ff // FFN_CHUNK):
        lo = f * FFN_CHUNK
        g = jnp.dot(h, wgu_ref[:, lo:lo + FFN_CHUNK], preferred_element_type=F32)
        u = jnp.dot(h, wgu_ref[:, d_ff + lo:d_ff + lo + FFN_CHUNK], preferred_element_type=F32)
        a = (g * jax.nn.sigmoid(g) * u).astype(BF16)
        d = jnp.dot(a, wd_ref[lo:lo + FFN_CHUNK, :], preferred_element_type=F32)
        if f == 0:
            acc_ref[...] = d
        else:
            acc_ref[...] += d
    o_ref[...] = x + 0.5 * _rms(acc_ref[...], g1_ref[...])


def _ffn(x, gains, wgu, wd, l, j, tm):
    n, d = x.shape
    d_ff = wd.shape[2]
    return pl.pallas_call(
        functools.partial(_ffn_body, d_ff=d_ff),
        grid=(n // tm,),
        in_specs=[
            pl.BlockSpec((tm, d), lambda i: (i, 0)),
            _layer_spec((1, d), l, 2 * j * 3),
            _layer_spec((1, d), l, 2 * j * 3 + 1),
            _layer_spec((d, 2 * d_ff), l, j),
            _layer_spec((d_ff, d), l, j),
        ],
        out_specs=pl.BlockSpec((tm, d), lambda i: (i, 0)),
        out_shape=jax.ShapeDtypeStruct((n, d), F32),
        scratch_shapes=[pltpu.VMEM((tm, d), F32)],
        compiler_params=_cparams("parallel"),
        name=f"ffn{j}",
    )(x, gains, gains, wgu, wd)


def _rope(v, c1, sm1, sp1):
    n = v.shape[1] // LANES
    c, sm, sp = (t if n == 1 else jnp.concatenate([t] * n, axis=1) for t in (c1, sm1, sp1))
    w = v.shape[1]
    return v * c + pltpu.roll(v, w - ROPE_DIM // 2, 1) * sm + pltpu.roll(v, ROPE_DIM // 2, 1) * sp


def _inproj_body(x_ref, g_ref, w_ref, c_ref, sm_ref, sp_ref, e_ref,
                 qpad_ref, rows_ref, win_ref, kaug_ref, vsel_ref, kvwin_ref, gates_ref, qkv_ref, z_ref):
    h = _rms(x_ref[...], g_ref[...]).astype(BF16)
    c1, sm1, sp1 = c_ref[...], sm_ref[...], sp_ref[...]

    def mm(lo, hi):
        return jnp.dot(h, w_ref[:, lo:hi], preferred_element_type=F32)

    q = _rope(mm(COL_Q, COL_KV), c1, sm1, sp1)
    qpad_ref[...] = (q * ATTN_SCALE).astype(BF16)
    kv = mm(COL_KV, COL_QKV)
    ksel = _rope(kv[:, 2 * LANES:3 * LANES], c1, sm1, sp1)
    vsel = kv[:, 3 * LANES:4 * LANES]
    kwin = _rope(kv[:, 4 * LANES:5 * LANES], c1, sm1, sp1)
    vwin = kv[:, 5 * LANES:6 * LANES]
    rows_ref[:, 0:2 * LANES] = kv[:, 0:2 * LANES]
    rows_ref[:, 2 * LANES:3 * LANES] = ksel
    rows_ref[:, 3 * LANES:4 * LANES] = vsel
    win_ref[:, 0:LANES] = kwin
    win_ref[:, LANES:2 * LANES] = vwin
    e2 = e_ref[...]
    lane = lax.broadcasted_iota(jnp.int32, ksel.shape, 1)
    first = lane < HEAD_DIM
    kaug_ref[:, 0:LANES] = jnp.where(first, ksel, e2).astype(BF16)
    kaug_ref[:, LANES:2 * LANES] = jnp.where(first, e2, ksel).astype(BF16)
    vsel_ref[...] = vsel.astype(BF16)
    kvwin_ref[:, 0:LANES] = kwin.astype(BF16)
    kvwin_ref[:, LANES:2 * LANES] = vwin.astype(BF16)
    qkv_ref[...] = mm(COL_QKV, COL_Z)
    z_ref[...] = mm(COL_Z, COL_GATE)
    gates_ref[...] = mm(COL_GATE, IN_PACKED)


def _inproj(x, gains, w_in_p, tabs, l, tm):
    n, d = x.shape
    c_t, sm_t, sp_t, e_t = tabs
    nt = c_t.shape[0] // tm
    tab = lambda w: pl.BlockSpec((tm, w), lambda i: (i % nt, 0))
    row = lambda w: pl.BlockSpec((tm, w), lambda i: (i, 0))
    widths = [(QPAD_WIDTH, BF16), (4 * LANES, F32), (2 * LANES, F32), (2 * LANES, BF16), (LANES, BF16),
              (2 * LANES, BF16), (LANES, F32), (GDN_QKV, F32), (GDN_WIDTH, F32)]
    return pl.pallas_call(
        _inproj_body,
        grid=(n // tm,),
        in_specs=[row(d), _layer_spec((1, d), l, 2), _layer_spec((d, IN_PACKED), l),
                  tab(LANES), tab(LANES), tab(LANES), tab(LANES)],
        out_specs=[row(w) for w, _ in widths],
        out_shape=[jax.ShapeDtypeStruct((n, w), dt) for w, dt in widths],
        compiler_params=_cparams("parallel"),
        name="inproj",
    )(x, gains, w_in_p, c_t, sm_t, sp_t, e_t)


def _normmm_body(x_ref, g_ref, w_ref, o_ref):
    h = _rms(x_ref[...], g_ref[...]).astype(BF16)
    o_ref[...] = jnp.dot(h, w_ref[...], preferred_element_type=F32)


def _normmm(x, gain, gain_lead, w, l, tm):
    n, d = x.shape
    nout = w.shape[-1]
    return pl.pallas_call(
        _normmm_body,
        grid=(n // tm,),
        in_specs=[pl.BlockSpec((tm, d), lambda i: (i, 0)), _layer_spec((1, d), *gain_lead), _layer_spec((d, nout), l)],
        out_specs=pl.BlockSpec((tm, nout), lambda i: (i, 0)),
        out_shape=jax.ShapeDtypeStruct((n, nout), F32),
        compiler_params=_cparams("parallel"),
        name="memkv",
    )(x, gain, w)


def _outproj_body(x_ref, a1_ref, a2_ref, w_ref, g_ref, o_ref):
    k1 = a1_ref.shape[1]
    acc = jnp.dot(a1_ref[...].astype(BF16), w_ref[0:k1, :], preferred_element_type=F32)
    acc = acc + jnp.dot(a2_ref[...].astype(BF16), w_ref[k1:, :], preferred_element_type=F32)
    o_ref[...] = x_ref[...] + _rms(acc, g_ref[...])


def _outproj(x, a1, a2, w_out, gains, l, tm, gain_idx=3):
    n, d = x.shape
    row = lambda w: pl.BlockSpec((tm, w), lambda i: (i, 0))
    return pl.pallas_call(
        _outproj_body,
        grid=(n // tm,),
        in_specs=[row(d), row(a1.shape[1]), row(a2.shape[1]),
                  _layer_spec((a1.shape[1] + a2.shape[1], d), l), _layer_spec((1, d), l, gain_idx)],
        out_specs=row(d),
        out_shape=jax.ShapeDtypeStruct((n, d), F32),
        compiler_params=_cparams("parallel"),
        name="outproj",
    )(x, a1, a2, w_out, gains)


def _mem_body(x_ref, g4_ref, g5_ref, wq_ref, wo_ref, kv_ref, o_ref):
    x = x_ref[...]
    h = _rms(x, g4_ref[...]).astype(BF16)
    q = jnp.dot(h, wq_ref[...], preferred_element_type=F32) * (MEM_HD ** -0.5)
    kv = kv_ref[...].astype(BF16)
    outs = []
    for hd in range(MEM_HEADS):
        qh = q[:, hd * MEM_HD:(hd + 1) * MEM_HD].astype(BF16)
        kh = kv[:, hd * MEM_HD:(hd + 1) * MEM_HD]
        vh = kv[:, MEM_WIDTH + hd * MEM_HD:MEM_WIDTH + (hd + 1) * MEM_HD]
        s = lax.dot_general(qh, kh, (((1,), (1,)), ((), ())), preferred_element_type=F32)
        e = jnp.exp(s - jnp.max(s, axis=-1, keepdims=True))
        p = e / jnp.sum(e, axis=-1, keepdims=True)
        outs.append(jnp.dot(p.astype(BF16), vh, preferred_element_type=F32))
    o = jnp.concatenate(outs, axis=1).astype(BF16)
    y = jnp.dot(o, wo_ref[...], preferred_element_type=F32)
    o_ref[...] = x + _rms(y, g5_ref[...])


def _mem_block(x, gains, w_q, w_o, mem_kv, l, tm, tiles_per_batch):
    n, d = x.shape
    m = mem_kv.shape[0] // (n // (tm * tiles_per_batch))
    return pl.pallas_call(
        _mem_body,
        grid=(n // tm,),
        in_specs=[pl.BlockSpec((tm, d), lambda i: (i, 0)), _layer_spec((1, d), l, 4), _layer_spec((1, d), l, 5),
                  _layer_spec((d, MEM_WIDTH), l), _layer_spec((MEM_WIDTH, d), l),
                  pl.BlockSpec((m, 2 * MEM_WIDTH), lambda i: (i // tiles_per_batch, 0))],
        out_specs=pl.BlockSpec((tm, d), lambda i: (i, 0)),
        out_shape=jax.ShapeDtypeStruct((n, d), F32),
        compiler_params=_cparams("parallel"),
        name="memattn",
    )(x, gains, gains, w_q, w_o, mem_kv)


NSA_TQ = 128
NSA_TK = 512


def _softmax_cols(s, valid):
    s = jnp.where(valid, s, NEG_INF)
    m = jnp.max(s, axis=0, keepdims=True)
    e = jnp.where(valid, jnp.exp(s - m), 0.0)
    return e / jnp.maximum(jnp.sum(e, axis=0, keepdims=True), 1e-30)


def _softmax_rows(s, valid):
    s = jnp.where(valid, s, NEG_INF)
    m = jnp.max(s, axis=1, keepdims=True)
    e = jnp.where(valid, jnp.exp(s - m), 0.0)
    return e / jnp.maximum(jnp.sum(e, axis=1, keepdims=True), 1e-30)


def _split3(x):
    hi = x.astype(BF16)
    r = x - hi.astype(F32)
    mid = r.astype(BF16)
    lo = (r - mid.astype(F32)).astype(BF16)
    return hi, mid, lo


def _select_topk(imp_t, blk, n_pick):
    n_blocks = imp_t.shape[0]
    sel = jnp.zeros(imp_t.shape, F32)
    work = imp_t
    for _ in range(n_pick):
        mx = jnp.max(work, axis=0, keepdims=True)
        first = jnp.min(jnp.where(work == mx, blk, float(n_blocks)), axis=0, keepdims=True)
        hit = blk == first
        sel = jnp.where(hit, 1.0, sel)
        work = jnp.where(hit, -3.0e38, work)
    return sel


def _transpose_01(x_t, eye):
    return lax.dot_general(eye, x_t.astype(BF16), (((1,), (1,)), ((), ())), preferred_element_type=F32)


def _nsa_body(q_ref, gt_ref, kaug_ref, vsel_ref, kvw_ref, kc_ref, vc_ref, covt_ref, o_ref, *, n_cmp, n_sel, topk):
    tq = q_ref.shape[0]
    n_blk = covt_ref.shape[0]
    n_cmp_pad = kc_ref.shape[0]
    g4 = NSA_GROUP
    i = pl.program_id(1)
    q0 = i * tq
    gates = jax.nn.sigmoid(gt_ref[...])

    qpos_row = q0 + lax.broadcasted_iota(jnp.int32, (1, tq), 1)
    qpos_row4 = jnp.concatenate([qpos_row] * g4, axis=1)
    qpos_col = q0 + lax.broadcasted_iota(jnp.int32, (tq, 1), 0)
    qpos_col4 = jnp.concatenate([qpos_col] * g4, axis=0)
    lane = lax.broadcasted_iota(jnp.int32, (tq, LANES), 1)
    blk = lax.broadcasted_iota(jnp.int32, (n_blk, 1), 0)
    blk_f = blk.astype(F32)
    qblk = qpos_row // SEL_BLOCK
    forced = (blk == 0) | (blk == qblk) | (blk == qblk - 1)
    future = blk > qblk
    exists = blk < n_sel
    eye = (lax.broadcasted_iota(jnp.int32, (tq, tq), 0) == lax.broadcasted_iota(jnp.int32, (tq, tq), 1)).astype(BF16)
    c_idx = lax.broadcasted_iota(jnp.int32, (n_cmp_pad, 1), 0)
    c_end = c_idx * CMP_STRIDE + (CMP_BLOCK - 1)
    cmp_valid = (c_end <= qpos_row4) & (c_idx < n_cmp)

    w_start = pl.multiple_of(jnp.maximum(q0 - WINDOW, 0), tq)
    w_len = WINDOW + tq
    kpos_w = w_start + lax.broadcasted_iota(jnp.int32, (1, w_len), 1)
    dw = qpos_col4 - kpos_w
    win_valid = (dw >= 0) & (dw < WINDOW)

    n_kt = (q0 + tq + NSA_TK - 1) // NSA_TK
    kc = kc_ref[...]
    vc = vc_ref[...]
    covt = covt_ref[...]
    qpad4s, o_cmps, lhs4s = [], [], []
    for hk in range(NSA_KV_HEADS):
        own = (lane < HEAD_DIM) if hk == 0 else (lane >= HEAD_DIM)
        qpads = [q_ref[:, (hk * g4 + g) * LANES:(hk * g4 + g + 1) * LANES] for g in range(g4)]
        qpad4 = jnp.concatenate(qpads, axis=0)

        s_t = lax.dot_general(kc, qpad4, (((1,), (1,)), ((), ())), preferred_element_type=F32)
        p_t = _softmax_cols(s_t, cmp_valid)
        o_cmp = jnp.dot(p_t.T.astype(BF16), vc, preferred_element_type=F32)
        p_sum = p_t[:, 0:tq]
        for g in range(1, g4):
            p_sum = p_sum + p_t[:, g * tq:(g + 1) * tq]
        imp_t = sum(jnp.dot(covt, part, preferred_element_type=F32) for part in _split3(p_sum))
        imp_t = jnp.where(forced, FORCE_SCORE, jnp.where(future, -1.0, imp_t))
        imp_t = jnp.where(exists, imp_t, -2.0)
        sel_t = _select_topk(imp_t, blk_f, topk)
        selneg = jnp.where(_transpose_01(sel_t, eye) > 0.5, 0.0, NEG_INF)
        selneg2 = jnp.concatenate([selneg, selneg], axis=1).astype(BF16)

        lhs4s.append(jnp.concatenate([jnp.where(own, qp, selneg2) for qp in qpads], axis=0))
        qpad4s.append(qpad4)
        o_cmps.append(o_cmp)

    def kt_body(kt, carry, causal):
        ks = pl.multiple_of(kt * NSA_TK, NSA_TK)
        v = vsel_ref[pl.ds(ks, NSA_TK), :]
        out = []
        for hk in range(NSA_KV_HEADS):
            m, l, acc = carry[hk]
            ka = kaug_ref[pl.ds(ks, NSA_TK), hk * LANES:(hk + 1) * LANES]
            s = lax.dot_general(lhs4s[hk], ka, (((1,), (1,)), ((), ())), preferred_element_type=F32)
            if causal:
                kpos = ks + lax.broadcasted_iota(jnp.int32, (1, NSA_TK), 1)
                s = jnp.where(kpos <= qpos_col4, s, NEG_INF)
            m_new = jnp.maximum(m, jnp.max(s, axis=1, keepdims=True))
            alpha = jnp.exp(m - m_new)
            p = jnp.exp(s - m_new)
            l = alpha * l + jnp.sum(p, axis=1, keepdims=True)
            acc = alpha * acc + jnp.dot(p.astype(BF16), v, preferred_element_type=F32)
            out.append((m_new, l, acc))
        return tuple(out)

    init1 = (jnp.full((g4 * tq, 1), NEG_INF, F32), jnp.zeros((g4 * tq, 1), F32), jnp.zeros((g4 * tq, LANES), F32))
    carry = lax.fori_loop(0, n_kt - 1, functools.partial(kt_body, causal=False), (init1,) * NSA_KV_HEADS)
    carry = kt_body(n_kt - 1, carry, causal=True)

    for hk in range(NSA_KV_HEADS):
        qpad4, o_cmp = qpad4s[hk], o_cmps[hk]
        _, l_sel, acc_sel = carry[hk]
        o_sel = acc_sel / jnp.maximum(l_sel, 1e-30)

        kw = kvw_ref[pl.ds(w_start, w_len), 0:LANES]
        vw = kvw_ref[pl.ds(w_start, w_len), LANES:2 * LANES]
        s_w = jnp.where(win_valid, lax.dot_general(qpad4, kw, (((1,), (1,)), ((), ())), preferred_element_type=F32),
                        NEG_INF)
        e_w = jnp.where(win_valid, jnp.exp(s_w - jnp.max(s_w, axis=1, keepdims=True)), 0.0)
        o_win = jnp.dot(e_w.astype(BF16), vw, preferred_element_type=F32) / jnp.maximum(
            jnp.sum(e_w, axis=1, keepdims=True), 1e-30)

        def gate_col(j):
            cols = [gates[:, 3 * (hk * g4 + g) + j:3 * (hk * g4 + g) + j + 1] for g in range(g4)]
            return jnp.concatenate(cols, axis=0)

        o4 = gate_col(0) * o_cmp + gate_col(1) * o_sel + gate_col(2) * o_win
        for pair in range(g4 // 2):
            a = o4[(2 * pair) * tq:(2 * pair + 1) * tq]
            b = o4[(2 * pair + 1) * tq:(2 * pair + 2) * tq]
            if hk == 0:
                both = jnp.where(lane < HEAD_DIM, a, pltpu.roll(b, HEAD_DIM, 1))
            else:
                both = jnp.where(lane < HEAD_DIM, pltpu.roll(a, HEAD_DIM, 1), b)
            col = (hk * (g4 // 2) + pair) * LANES
            o_ref[:, col:col + LANES] = both


def _nsa_prompt(qpad, gates, kaug, vsel, kvwin, kc, vc, covt, batch, t_len, n_cmp, n_sel, topk):
    n = qpad.shape[0]
    nq = t_len // NSA_TQ
    n_cmp_pad = kc.shape[0] // batch
    qrow = lambda w: pl.BlockSpec((NSA_TQ, w), lambda b, i: (b * nq + i, 0))
    per_b = lambda r, w: pl.BlockSpec((r, w), lambda b, i: (b, 0))
    return pl.pallas_call(
        functools.partial(_nsa_body, n_cmp=n_cmp, n_sel=n_sel, topk=topk),
        grid=(batch, nq),
        in_specs=[qrow(QPAD_WIDTH), qrow(LANES), per_b(t_len, 2 * LANES), per_b(t_len, LANES),
                  per_b(t_len, 2 * LANES), per_b(n_cmp_pad, LANES), per_b(n_cmp_pad, LANES),
                  _const_spec(covt.shape)],
        out_specs=qrow(NSA_WIDTH),
        out_shape=jax.ShapeDtypeStruct((n, NSA_WIDTH), F32),
        compiler_params=_cparams("parallel", "parallel"),
        name="nsa_prompt",
    )(qpad, gates, kaug, vsel, kvwin, kc, vc, covt)


def _group_features(load_l, lane):
    heads = ([], [])
    for m in range(CMP_STRIDE // 2):
        a = load_l(2 * m)
        b = load_l(2 * m + 1)
        heads[0].append(jnp.where(lane < HEAD_DIM, a, pltpu.roll(b, HEAD_DIM, 1)))
        heads[1].append(jnp.where(lane < HEAD_DIM, pltpu.roll(a, HEAD_DIM, 1), b))
    return [jnp.concatenate(h, axis=1) for h in heads]


def _compress_heads(x_heads, pe8, w1, w2_h0, w2_h1):
    g = x_heads[0].shape[0]
    x2 = jnp.concatenate(x_heads, axis=0).astype(BF16)
    z = jnp.dot(x2, w1, preferred_element_type=F32)
    zb = jnp.dot(pe8.astype(BF16), w1, preferred_element_type=F32)
    bias = zb[0:1, 0:CMP_HIDDEN] + zb[1:2, CMP_HIDDEN:]
    nxt = pltpu.roll(z[:, CMP_HIDDEN:], 2 * g - 1, 0)
    hid = jax.nn.gelu(z[:, 0:CMP_HIDDEN] + nxt + bias).astype(BF16)
    return (jnp.dot(hid[0:g], w2_h0, preferred_element_type=F32)
            + jnp.dot(hid[g:], w2_h1, preferred_element_type=F32))


def _compress_body(krows_ref, vrows_ref, pe_ref, w1_ref, w2_ref, c_ref, sm_ref, sp_ref, kc_ref, vc_ref):
    g = kc_ref.shape[0]
    lane = lax.broadcasted_iota(jnp.int32, (g, LANES), 1)
    outs = []
    for t, ref in enumerate((krows_ref, vrows_ref)):
        load = lambda l, ref=ref: ref[pl.ds(l, g, stride=CMP_STRIDE), :]
        outs.append(_compress_heads(_group_features(load, lane), pe_ref[t], w1_ref[t], w2_ref[t, 0], w2_ref[t, 1]))
    kc_ref[...] = _rope(outs[0], c_ref[...], sm_ref[...], sp_ref[...]).astype(BF16)
    vc_ref[...] = outs[1].astype(BF16)


def _compress_prompt(rows, pe8, w1c, w2p, tabs_c, l, batch, t_len):
    g = t_len // CMP_STRIDE
    out = jax.ShapeDtypeStruct((batch * g, LANES), BF16)
    return pl.pallas_call(
        _compress_body,
        grid=(batch,),
        in_specs=[pl.BlockSpec((t_len, LANES), lambda b: (b, 0)), pl.BlockSpec((t_len, LANES), lambda b: (b, 1)),
                  _layer_spec((2, 8, CMP_STRIDE * HEAD_DIM), l),
                  _layer_spec((2, CMP_STRIDE * HEAD_DIM, 2 * CMP_HIDDEN), l),
                  _layer_spec((2, 2, CMP_HIDDEN, LANES), l),
                  _const_spec((g, LANES)), _const_spec((g, LANES)), _const_spec((g, LANES))],
        out_specs=[pl.BlockSpec((g, LANES), lambda b: (b, 0))] * 2,
        out_shape=[out, out],
        compiler_params=_cparams("parallel"),
        name="compress",
    )(rows, rows, pe8, w1c, w2p, *tabs_c)


GDN_TT = 512
HALO = 8


def _dot3(a, x):
    return sum(jnp.dot(a, part, preferred_element_type=F32) for part in _split3(x))


def _dot3_nt(a, x):
    return sum(lax.dot_general(a, part, (((1,), (1,)), ((), ())), preferred_element_type=F32) for part in _split3(x))


def _l2n(x):
    return x * lax.rsqrt(jnp.sum(x * x, axis=-1, keepdims=True) + 1e-6)


def _softplus(x):
    return jnp.maximum(x, 0.0) + jnp.log(1.0 + jnp.exp(-jnp.abs(x)))


def _gdn_prep_body(x_ref, halo_ref, gt_ref, cw_ref, alog_ref, dtb_ref,
                   qg_ref, kd_ref, w_ref, u_ref, att_ref, egl_ref,
                   ext_ref, q_s, k_s, kb_s, rhs_s, gc_s):
    c_len = GDN_CHUNK
    tt = x_ref.shape[0]
    nch = tt // c_len
    qk_w = GDN_HEADS * GDN_DK
    first = pl.program_id(1) == 0
    ext_ref[0:HALO, :] = jnp.where(first, 0.0, halo_ref[...])
    ext_ref[HALO:HALO + tt, :] = x_ref[...]
    c = ext_ref[pl.ds(HALO - (CONV_W - 1), tt), :] * cw_ref[0:1, :]
    for j in range(1, CONV_W):
        c = c + ext_ref[pl.ds(HALO - (CONV_W - 1) + j, tt), :] * cw_ref[j:j + 1, :]
    c = c * jax.nn.sigmoid(c)

    gt = gt_ref[...]
    g_full = -jnp.exp(alog_ref[...]) * _softplus(gt + dtb_ref[...])
    beta_full = jax.nn.sigmoid(gt)
    r = lax.broadcasted_iota(jnp.int32, (tt, tt), 0)
    cc = lax.broadcasted_iota(jnp.int32, (tt, tt), 1)
    same = (r // c_len) == (cc // c_len)
    gcum = _dot3(jnp.where(same & (cc <= r), 1.0, 0.0).astype(BF16), g_full)
    gtot = _dot3(jnp.where(same, 1.0, 0.0).astype(BF16), g_full)
    gc_s[...] = gcum
    for h in range(GDN_HEADS):
        sl = slice(h * GDN_DK, (h + 1) * GDN_DK)
        gc_col = gcum[:, GATE_A + h:GATE_A + h + 1]
        gl_col = gtot[:, GATE_A + h:GATE_A + h + 1]
        beta_col = beta_full[:, GATE_B + h:GATE_B + h + 1]
        qh = _l2n(c[:, h * GDN_DK:(h + 1) * GDN_DK]) * (GDN_DK ** -0.5)
        kh = _l2n(c[:, qk_w + h * GDN_DK:qk_w + (h + 1) * GDN_DK])
        vh = c[:, 2 * qk_w + h * GDN_DV:2 * qk_w + (h + 1) * GDN_DV]
        eg = jnp.exp(gc_col)
        kb = kh * beta_col
        q_s[:, sl] = qh
        k_s[:, sl] = kh
        kb_s[:, sl] = kb
        rhs_s[:, 2 * h * GDN_DK:(2 * h + 1) * GDN_DK] = vh * beta_col
        rhs_s[:, (2 * h + 1) * GDN_DK:(2 * h + 2) * GDN_DK] = kb * eg
        qg_ref[:, sl] = (qh * eg).astype(BF16)
        kd_ref[:, sl] = (kh * jnp.exp(gl_col - gc_col)).astype(BF16)
        egl_b = jnp.broadcast_to(jnp.exp(gl_col), (tt, GDN_DK))
        for ch in range(nch):
            egl_ref[ch, :, sl] = egl_b[ch * c_len:ch * c_len + 1, :]

    nh = GDN_HEADS
    wide = nh * c_len
    wi = lax.broadcasted_iota(jnp.int32, (c_len, wide), 0)
    wj = lax.broadcasted_iota(jnp.int32, (c_len, wide), 1) & (c_len - 1)
    incl = wi >= wj
    strict = wi > wj
    eye = jnp.where(wi == wj, 1.0, 0.0)
    base_mask = (wi >> 1) == (wj >> 1)
    bi = lax.broadcasted_iota(jnp.int32, (wide, wide), 0)
    bj = lax.broadcasted_iota(jnp.int32, (wide, wide), 1)
    same_head = (bi // c_len) == (bj // c_len)
    bi_in = bi & (c_len - 1)
    bj_in = bj & (c_len - 1)
    level_masks = [same_head & ((bi_in >> (s + 1)) == (bj_in >> (s + 1))) & ((bi_in >> s) != (bj_in >> s))
                   for s in range(1, int(math.log2(c_len)))]
    head_of_row = lax.broadcasted_iota(jnp.int32, (wide, LANES), 0) // c_len
    lane_of = lax.broadcasted_iota(jnp.int32, (wide, LANES), 1)
    pick_rows = lane_of == GATE_A + head_of_row
    sel_k = lax.broadcasted_iota(jnp.int32, (LANES, wide), 0)
    sel_h = lax.broadcasted_iota(jnp.int32, (LANES, wide), 1) // c_len
    sel_cols = jnp.where(sel_k == GATE_A + sel_h, 1.0, 0.0).astype(BF16)
    ones_b = jnp.ones((c_len, LANES), BF16)
    key_head = (lax.broadcasted_iota(jnp.int32, (wide, GDN_WIDTH), 0) // c_len
                == lax.broadcasted_iota(jnp.int32, (wide, GDN_WIDTH), 1) // GDN_DK)
    rhs_head = (lax.broadcasted_iota(jnp.int32, (wide, 2 * GDN_WIDTH), 0) // c_len
                == lax.broadcasted_iota(jnp.int32, (wide, 2 * GDN_WIDTH), 1) // (2 * GDN_DK))
    nt = (((1,), (1,)), ((), ()))
    rep = lambda m: jnp.concatenate([m] * nh, axis=0)

    def chunk(ch, carry):
        r0 = pl.multiple_of(ch * c_len, c_len)
        gch = gc_s[pl.ds(r0, c_len), :]
        g_i = sum(jnp.dot(part, sel_cols, preferred_element_type=F32) for part in _split3(gch))
        g_j = _dot3_nt(ones_b, jnp.where(pick_rows, rep(gch), 0.0))
        decay = jnp.where(incl, jnp.exp(jnp.where(incl, g_i - g_j, 0.0)), 0.0)
        k_bd = jnp.where(key_head, rep(k_s[pl.ds(r0, c_len), :]), 0.0).astype(BF16)
        qkb = jnp.concatenate([kb_s[pl.ds(r0, c_len), :], q_s[pl.ds(r0, c_len), :]], axis=0).astype(BF16)
        prod = lax.dot_general(qkb, k_bd, nt, preferred_element_type=F32)
        a = jnp.where(strict, prod[0:c_len] * decay, 0.0)
        att_ref[pl.ds(r0, c_len), :] = (prod[c_len:] * decay).astype(BF16)
        a_rep = rep(a)
        x = eye - jnp.where(base_mask, a, 0.0)
        for lvl_mask in level_masks:
            off_bd = jnp.where(lvl_mask, a_rep, 0.0).astype(BF16)
            x_bd = jnp.where(same_head, rep(x), 0.0).astype(BF16)
            t = jnp.dot(x.astype(BF16), off_bd, preferred_element_type=F32)
            x = x - jnp.dot(t.astype(BF16), x_bd, preferred_element_type=F32)
        rhs_bd = jnp.where(rhs_head, rep(rhs_s[pl.ds(r0, c_len), :]), 0.0).astype(BF16)
        uw = jnp.dot(x.astype(BF16), rhs_bd, preferred_element_type=F32)
        for h in range(nh):
            sl = slice(h * GDN_DK, (h + 1) * GDN_DK)
            u_ref[pl.ds(r0, c_len), sl] = uw[:, 2 * h * GDN_DK:(2 * h + 1) * GDN_DK]
            w_ref[pl.ds(r0, c_len), sl] = uw[:, (2 * h + 1) * GDN_DK:(2 * h + 2) * GDN_DK].astype(BF16)
        return carry

    lax.fori_loop(0, nch, chunk, 0)


def _gdn_prep(qkv, gates, conv_w, alog_row, dtb_row, l, batch, t_len):
    n = qkv.shape[0]
    tt = GDN_TT
    nt = t_len // tt
    nch = tt // GDN_CHUNK
    row = lambda w: pl.BlockSpec((tt, w), lambda b, i: (b * nt + i, 0))
    outs = [(GDN_WIDTH, BF16), (GDN_WIDTH, BF16), (GDN_WIDTH, BF16), (GDN_WIDTH, F32)]
    return pl.pallas_call(
        _gdn_prep_body,
        grid=(batch, nt),
        in_specs=[row(GDN_QKV),
                  pl.BlockSpec((HALO, GDN_QKV), lambda b, i: (jnp.maximum((b * nt + i) * (tt // HALO) - 1, 0), 0)),
                  row(LANES), _layer_spec((CONV_W, GDN_QKV), l), _layer_spec((1, LANES), l), _layer_spec((1, LANES), l)],
        out_specs=[row(w) for w, _ in outs] + [
            pl.BlockSpec((tt, GDN_HEADS * GDN_CHUNK), lambda b, i: (b * nt + i, 0)),
            pl.BlockSpec((nch, 1, GDN_WIDTH), lambda b, i: (b * nt + i, 0, 0))],
        out_shape=[jax.ShapeDtypeStruct((n, w), dt) for w, dt in outs] + [
            jax.ShapeDtypeStruct((n, GDN_HEADS * GDN_CHUNK), BF16),
            jax.ShapeDtypeStruct((n // GDN_CHUNK, 1, GDN_WIDTH), F32)],
        scratch_shapes=[pltpu.VMEM((HALO + tt, GDN_QKV), F32), pltpu.VMEM((tt, GDN_WIDTH), F32),
                        pltpu.VMEM((tt, GDN_WIDTH), F32), pltpu.VMEM((tt, GDN_WIDTH), F32),
                        pltpu.VMEM((tt, 2 * GDN_WIDTH), F32), pltpu.VMEM((tt, LANES), F32)],
        compiler_params=_cparams("parallel", "parallel"),
        name="gdn_prep",
    )(qkv, qkv, gates, conv_w, alog_row, dtb_row)


def _gdn_scan_body(qg_ref, kd_ref, w_ref, u_ref, att_ref, egl_ref, z_ref, gn_ref, o_ref, sfin_ref, s_ref):
    c_len = GDN_CHUNK
    tt = qg_ref.shape[0]
    i = pl.program_id(1)

    @pl.when(i == 0)
    def _():
        s_ref[...] = jnp.zeros(s_ref.shape, F32)

    gn = gn_ref[...]
    tn = (((0,), (0,)), ((), ()))

    def chunk(ch, carry):
        r0 = pl.multiple_of(ch * c_len, c_len)
        for h in range(GDN_HEADS):
            sl = slice(h * GDN_DK, (h + 1) * GDN_DK)
            s = s_ref[h]
            sb = s.astype(BF16)
            v_new = u_ref[pl.ds(r0, c_len), sl] - jnp.dot(w_ref[pl.ds(r0, c_len), sl], sb, preferred_element_type=F32)
            vb = v_new.astype(BF16)
            o = (jnp.dot(qg_ref[pl.ds(r0, c_len), sl], sb, preferred_element_type=F32)
                 + jnp.dot(att_ref[pl.ds(r0, c_len), h * c_len:(h + 1) * c_len], vb, preferred_element_type=F32))
            s_ref[h] = s * egl_ref[ch, :, sl] + lax.dot_general(
                kd_ref[pl.ds(r0, c_len), sl], vb, tn, preferred_element_type=F32)
            y = o * lax.rsqrt(jnp.mean(o * o, axis=-1, keepdims=True) + RMS_EPS) * gn
            zz = z_ref[pl.ds(r0, c_len), sl]
            o_ref[pl.ds(r0, c_len), sl] = y * (zz * jax.nn.sigmoid(zz))
        return carry

    lax.fori_loop(0, tt // c_len, chunk, 0)

    @pl.when(i == pl.num_programs(1) - 1)
    def _():
        sfin_ref[...] = s_ref[...]


def _gdn_scan(qg, kd, w, u, att, egl, z, gnorm, l, batch, t_len):
    n = qg.shape[0]
    tt = GDN_TT
    nt = t_len // tt
    nch = tt // GDN_CHUNK
    row = lambda: pl.BlockSpec((tt, GDN_WIDTH), lambda b, i: (b * nt + i, 0))
    return pl.pallas_call(
        _gdn_scan_body,
        grid=(batch, nt),
        in_specs=[row(), row(), row(), row(),
                  pl.BlockSpec((tt, GDN_HEADS * GDN_CHUNK), lambda b, i: (b * nt + i, 0)),
                  pl.BlockSpec((nch, 1, GDN_WIDTH), lambda b, i: (b * nt + i, 0, 0)),
                  row(), _layer_spec((1, GDN_DV), l)],
        out_specs=[row(), pl.BlockSpec((None, GDN_HEADS, GDN_DK, GDN_DV), lambda b, i: (b, 0, 0, 0))],
        out_shape=[jax.ShapeDtypeStruct((n, GDN_WIDTH), F32),
                   jax.ShapeDtypeStruct((batch, GDN_HEADS, GDN_DK, GDN_DV), F32)],
        scratch_shapes=[pltpu.VMEM((GDN_HEADS, GDN_DK, GDN_DV), F32)],
        compiler_params=_cparams("parallel", "arbitrary"),
        name="gdn_scan",
    )(qg, kd, w, u, att, egl, z, gnorm)


def _softmax_rows_extra(s, valid, s_new, valid_new):
    s = jnp.where(valid, s, NEG_INF)
    s_new = jnp.where(valid_new, s_new, NEG_INF)
    m = jnp.maximum(jnp.max(s, axis=1, keepdims=True), s_new)
    e = jnp.where(valid, jnp.exp(s - m), 0.0)
    e_new = jnp.where(valid_new, jnp.exp(s_new - m), 0.0)
    den = jnp.maximum(jnp.sum(e, axis=1, keepdims=True) + e_new, 1e-30)
    return e, e_new, 1.0 / den


def _nsa_sample_body(pt_ref, *refs, n_pages, page, n_buf, n_cmp, n_sel, topk):
    del pt_ref
    kcmp_pages = refs[0:n_pages]
    vcmp_pages = refs[n_pages:2 * n_pages]
    sel_pages = refs[2 * n_pages:3 * n_pages]
    (q_ref, gt_ref, rows_ref, wnew_ref, wbuf_ref, pe_ref, w1_ref, w2_ref, c_ref, sm_ref, sp_ref,
     cov_ref, eexp_ref, o_ref) = refs[3 * n_pages:]
    past = n_pages * page
    qpos = past
    gpp = page // CMP_STRIDE
    g = n_pages * gpp
    nt = (((1,), (1,)), ((), ()))
    nh = NSA_HEADS
    lane_g = lax.broadcasted_iota(jnp.int32, (g, LANES), 1)
    qf = q_ref[...]
    qp = qf.astype(BF16)
    row8 = lax.broadcasted_iota(jnp.int32, (nh, 1), 0)

    outs = []
    for t, pages in enumerate((kcmp_pages, vcmp_pages)):
        load = lambda l, pages=pages: jnp.concatenate(
            [pr[pl.ds(l, gpp, stride=CMP_STRIDE), :] for pr in pages], axis=0)
        outs.append(_compress_heads(_group_features(load, lane_g), pe_ref[t], w1_ref[t], w2_ref[t, 0], w2_ref[t, 1]))
    kc = _rope(outs[0], c_ref[...], sm_ref[...], sp_ref[...]).astype(BF16)
    vc = outs[1].astype(BF16)

    c_row = lax.broadcasted_iota(jnp.int32, (1, g), 1)
    cv_row = (c_row * CMP_STRIDE + (CMP_BLOCK - 1) <= qpos) & (c_row < n_cmp)
    s_c = lax.dot_general(qp, kc, nt, preferred_element_type=F32)
    p_c = _softmax_rows(s_c, cv_row)
    o_cmp = jnp.dot(p_c.astype(BF16), vc, preferred_element_type=F32)
    g0 = jnp.sum(p_c[0:NSA_GROUP], axis=0, keepdims=True)
    g1 = jnp.sum(p_c[NSA_GROUP:], axis=0, keepdims=True)
    psum8 = jnp.where(row8 == 0, g0, jnp.where(row8 == 1, g1, 0.0))
    imp8 = sum(jnp.dot(part, cov_ref[...], preferred_element_type=F32) for part in _split3(psum8))
    n_blk = imp8.shape[1]
    ib = lax.broadcasted_iota(jnp.int32, (n_blk, n_blk), 0)
    jb = lax.broadcasted_iota(jnp.int32, (n_blk, n_blk), 1)
    eye_b = jnp.where(ib == jb, 1.0, 0.0).astype(BF16)
    imp_t = _dot3_nt(eye_b, imp8)
    blk = lax.broadcasted_iota(jnp.int32, (n_blk, 1), 0)
    qblk = qpos // SEL_BLOCK
    forced = (blk == 0) | (blk == qblk) | (blk == qblk - 1)
    imp_t = jnp.where(forced, FORCE_SCORE, jnp.where(blk > qblk, -1.0, imp_t))
    imp_t = jnp.where(blk < n_sel, imp_t, -2.0)
    sel_t = _select_topk(imp_t, blk.astype(F32), topk)
    sel8 = lax.dot_general(sel_t.astype(BF16), eye_b, (((0,), (0,)), ((), ())), preferred_element_type=F32)
    head_sel = jnp.where(row8 < NSA_GROUP, sel8[0:1], sel8[1:2])

    key_sel = jnp.dot(head_sel.astype(BF16), eexp_ref[...], preferred_element_type=F32) > 0.5
    s_parts = [lax.dot_general(qp, pr[:, 0:LANES].astype(BF16), nt, preferred_element_type=F32) for pr in sel_pages]
    s_sel = jnp.concatenate(s_parts, axis=1)
    kpos = lax.broadcasted_iota(jnp.int32, (1, past), 1)
    new_row = rows_ref[...]
    k_new = new_row[:, 2 * LANES:3 * LANES].astype(BF16).astype(F32)
    v_new = new_row[:, 3 * LANES:4 * LANES].astype(BF16).astype(F32)
    s_new = jnp.sum(qf * k_new, axis=1, keepdims=True)
    new_sel = head_sel[:, qblk:qblk + 1] > 0.5
    e, e_new, inv = _softmax_rows_extra(s_sel, key_sel & (kpos <= qpos), s_new, new_sel)
    acc = e_new * v_new
    for p_i, pr in enumerate(sel_pages):
        acc = acc + jnp.dot(e[:, p_i * page:(p_i + 1) * page].astype(BF16), pr[:, LANES:2 * LANES].astype(BF16),
                            preferred_element_type=F32)
    o_sel = acc * inv

    kw = wbuf_ref[:, 0:LANES].astype(BF16)
    vw = wbuf_ref[:, LANES:2 * LANES].astype(BF16)
    s_w = lax.dot_general(qp, kw, nt, preferred_element_type=F32)
    kpos_w = (past - n_buf) + lax.broadcasted_iota(jnp.int32, (1, n_buf), 1)
    dw = qpos - kpos_w
    w_new = wnew_ref[...]
    kw_new = w_new[:, 0:LANES].astype(BF16).astype(F32)
    vw_new = w_new[:, LANES:2 * LANES].astype(BF16).astype(F32)
    sw_new = jnp.sum(qf * kw_new, axis=1, keepdims=True)
    ew, ew_new, inv_w = _softmax_rows_extra(s_w, (dw >= 0) & (dw < WINDOW) & (kpos_w >= 0), sw_new, row8 >= 0)
    o_win = (jnp.dot(ew.astype(BF16), vw, preferred_element_type=F32) + ew_new * vw_new) * inv_w

    gates = jax.nn.sigmoid(gt_ref[...])
    lane8 = lax.broadcasted_iota(jnp.int32, (nh, LANES), 1)
    gcol = lambda j: jnp.sum(jnp.where(lane8 == 3 * row8 + j, gates, 0.0), axis=1, keepdims=True)
    o8 = gcol(0) * o_cmp + gcol(1) * o_sel + gcol(2) * o_win
    lane1 = lax.broadcasted_iota(jnp.int32, (1, LANES), 1)
    for pair in range(nh // 2):
        a = o8[2 * pair:2 * pair + 1]
        b = o8[2 * pair + 1:2 * pair + 2]
        if 2 * pair < NSA_GROUP:
            both = jnp.where(lane1 < HEAD_DIM, a, pltpu.roll(b, HEAD_DIM, 1))
        else:
            both = jnp.where(lane1 < HEAD_DIM, pltpu.roll(a, HEAD_DIM, 1), b)
        o_ref[:, pair * LANES:(pair + 1) * LANES] = both


def _nsa_sample(page_table, cache_cmp, cache_sel, q3, gates3, rows3, wnew3, win_cache, pe8, w1c, w2p, tabs_c, cov_s,
                eexp, l, n_cmp, n_sel, topk):
    bs, n_pages = page_table.shape
    page = cache_cmp.shape[2]
    n_buf = win_cache.shape[2]
    g = n_pages * page // CMP_STRIDE
    page_spec = lambda p, w, col: pl.BlockSpec((None, None, page, w), lambda b, pt, p=p: (l, pt[b, p], 0, col))
    per_b = lambda *shape: pl.BlockSpec((None,) + shape, lambda b, pt: (b,) + (0,) * len(shape))
    cst = lambda shape: pl.BlockSpec(shape, lambda b, pt: (0,) * len(shape))
    lyr = lambda shape: pl.BlockSpec((None,) + shape, lambda b, pt: (l,) + (0,) * len(shape))
    in_specs = ([page_spec(p, LANES, 0) for p in range(n_pages)] + [page_spec(p, LANES, 1) for p in range(n_pages)]
                + [page_spec(p, 2 * LANES, 0) for p in range(n_pages)]
                + [per_b(NSA_HEADS, LANES), per_b(1, LANES), per_b(1, 4 * LANES), per_b(1, 2 * LANES),
                   pl.BlockSpec((None, None, n_buf, 2 * LANES), lambda b, pt: (l, b, 0, 0)),
                   lyr((2, 8, CMP_STRIDE * HEAD_DIM)), lyr((2, CMP_STRIDE * HEAD_DIM, 2 * CMP_HIDDEN)),
                   lyr((2, 2, CMP_HIDDEN, LANES)), cst((g, LANES)), cst((g, LANES)), cst((g, LANES)),
                   cst(cov_s.shape), cst(eexp.shape)])
    grid_spec = pltpu.PrefetchScalarGridSpec(
        num_scalar_prefetch=1, grid=(bs,), in_specs=in_specs,
        out_specs=pl.BlockSpec((None, 1, NSA_WIDTH), lambda b, pt: (b, 0, 0)))
    args = ([cache_cmp] * (2 * n_pages) + [cache_sel] * n_pages
            + [q3, gates3, rows3, wnew3, win_cache, pe8, w1c, w2p, *tabs_c, cov_s, eexp])
    return pl.pallas_call(
        functools.partial(_nsa_sample_body, n_pages=n_pages, page=page, n_buf=n_buf, n_cmp=n_cmp, n_sel=n_sel,
                          topk=topk),
        grid_spec=grid_spec,
        out_shape=jax.ShapeDtypeStruct((bs, 1, NSA_WIDTH), F32),
        compiler_params=_cparams("parallel"),
        name="nsa_sample",
    )(page_table, *args)


GDN_SB = 8


def _transpose3(x, eye):
    return _dot3_nt(eye, x)


def _gdn_sample_body(x_ref, cs_ref, gt_ref, z_ref, s_ref, cw_ref, alog_ref, dtb_ref, gn_ref, o_ref, so_ref):
    sb = x_ref.shape[0]
    qk_w = GDN_HEADS * GDN_DK
    c = x_ref[...] * cw_ref[CONV_W - 1:CONV_W, :]
    for j in range(CONV_W - 1):
        c = c + cs_ref[:, j * GDN_QKV:(j + 1) * GDN_QKV] * cw_ref[j:j + 1, :]
    c = c * jax.nn.sigmoid(c)
    gt = gt_ref[...]
    g_full = -jnp.exp(alog_ref[...]) * _softplus(gt + dtb_ref[...])
    eg_full = jnp.exp(g_full)
    beta_full = jax.nn.sigmoid(gt)
    ii = lax.broadcasted_iota(jnp.int32, (LANES, LANES), 0)
    jj = lax.broadcasted_iota(jnp.int32, (LANES, LANES), 1)
    eye = jnp.where(ii == jj, 1.0, 0.0).astype(BF16)
    gn = gn_ref[...]
    for h in range(GDN_HEADS):
        sl = slice(h * GDN_DK, (h + 1) * GDN_DK)
        q = _l2n(c[:, h * GDN_DK:(h + 1) * GDN_DK]) * (GDN_DK ** -0.5)
        k = _l2n(c[:, qk_w + h * GDN_DK:qk_w + (h + 1) * GDN_DK])
        v = c[:, 2 * qk_w + h * GDN_DV:2 * qk_w + (h + 1) * GDN_DV]
        eg = jnp.broadcast_to(eg_full[:, GATE_A + h:GATE_A + h + 1], (sb, GDN_DK))
        beta = jnp.broadcast_to(beta_full[:, GATE_B + h:GATE_B + h + 1], (sb, GDN_DK))
        att = jnp.sum(q * k, axis=1, keepdims=True)
        k_t = _transpose3(k, eye)
        w_t = _transpose3(k * beta * eg, eye)
        qg_t = _transpose3(q * eg, eye)
        u = v * beta
        zz = z_ref[:, sl]
        for i in range(sb):
            s = s_ref[i, h]
            v_new = u[i:i + 1] - jnp.sum(w_t[:, i:i + 1] * s, axis=0, keepdims=True)
            o = jnp.sum(qg_t[:, i:i + 1] * s, axis=0, keepdims=True) + att[i:i + 1] * v_new
            so_ref[i, h] = s * eg[i:i + 1] + k_t[:, i:i + 1] * v_new
            y = o * lax.rsqrt(jnp.mean(o * o, axis=-1, keepdims=True) + RMS_EPS) * gn
            zi = zz[i:i + 1]
            o_ref[i:i + 1, sl] = y * (zi * jax.nn.sigmoid(zi))


def _gdn_sample(qkv, conv_state, gates, z, s_state, conv_w, alog_row, dtb_row, gnorm, l):
    bs = qkv.shape[0]
    sb = GDN_SB
    row = lambda w: pl.BlockSpec((sb, w), lambda i: (i, 0))
    st = pl.BlockSpec((None, sb, GDN_HEADS, GDN_DK, GDN_DV), lambda i: (l, i, 0, 0, 0))
    return pl.pallas_call(
        _gdn_sample_body,
        grid=(bs // sb,),
        in_specs=[row(GDN_QKV), pl.BlockSpec((None, sb, (CONV_W - 1) * GDN_QKV), lambda i: (l, i, 0)),
                  row(LANES), row(GDN_WIDTH), st,
                  _layer_spec((CONV_W, GDN_QKV), l), _layer_spec((1, LANES), l), _layer_spec((1, LANES), l),
                  _layer_spec((1, GDN_DV), l)],
        out_specs=[row(GDN_WIDTH), pl.BlockSpec((sb, GDN_HEADS, GDN_DK, GDN_DV), lambda i: (i, 0, 0, 0))],
        out_shape=[jax.ShapeDtypeStruct((bs, GDN_WIDTH), F32),
                   jax.ShapeDtypeStruct((bs, GDN_HEADS, GDN_DK, GDN_DV), F32)],
        compiler_params=_cparams("parallel"),
        name="gdn_sample",
    )(qkv, conv_state, gates, z, s_state, conv_w, alog_row, dtb_row, gnorm)


def _mem_sample_body(q_ref, kv_ref, o_ref):
    nt = (((1,), (1,)), ((), ()))
    q = q_ref[...] * (MEM_HD ** -0.5)
    row = lax.broadcasted_iota(jnp.int32, (8, MEM_WIDTH), 0)
    lane = lax.broadcasted_iota(jnp.int32, (8, MEM_WIDTH), 1)
    own = (lane // MEM_HD) == row
    qm = jnp.where(own, q, 0.0).astype(BF16)
    kv = kv_ref[...].astype(BF16)
    s = lax.dot_general(qm, kv[:, 0:MEM_WIDTH], nt, preferred_element_type=F32)
    e = jnp.exp(s - jnp.max(s, axis=-1, keepdims=True))
    p = e / jnp.sum(e, axis=-1, keepdims=True)
    o = jnp.dot(p.astype(BF16), kv[:, MEM_WIDTH:], preferred_element_type=F32)
    o_ref[...] = jnp.sum(jnp.where(own, o, 0.0), axis=0, keepdims=True)


def _mem_sample(q3, mem_cache, l):
    bs = q3.shape[0]
    m = mem_cache.shape[2]
    return pl.pallas_call(
        _mem_sample_body,
        grid=(bs,),
        in_specs=[pl.BlockSpec((None, 1, MEM_WIDTH), lambda b: (b, 0, 0)),
                  pl.BlockSpec((None, None, m, 2 * MEM_WIDTH), lambda b: (l, b, 0, 0))],
        out_specs=pl.BlockSpec((None, 1, MEM_WIDTH), lambda b: (b, 0, 0)),
        out_shape=jax.ShapeDtypeStruct((bs, 1, MEM_WIDTH), F32),
        compiler_params=_cparams("parallel"),
        name="mem_sample",
    )(q3, mem_cache)


def _rope_tables(pos):
    half = ROPE_DIM // 2
    inv = ROPE_THETA ** (-2.0 * jnp.arange(half, dtype=F32) / ROPE_DIM)
    ang = pos.astype(F32)[:, None] * inv[None, :]
    cos, sin = jnp.cos(ang), jnp.sin(ang)
    n = pos.shape[0]
    one = jnp.ones((n, HEAD_DIM - ROPE_DIM), F32)
    zero = jnp.zeros((n, HEAD_DIM - ROPE_DIM), F32)
    zh = jnp.zeros((n, half), F32)
    c = jnp.concatenate([cos, cos, one], axis=1)
    sm = jnp.concatenate([-sin, zh, zero], axis=1)
    sp = jnp.concatenate([zh, sin, zero], axis=1)
    return tuple(jnp.concatenate([t, t], axis=1) for t in (c, sm, sp))


def _block_onehot(pos):
    e = (pos[:, None] // SEL_BLOCK == jnp.arange(HEAD_DIM)[None, :]).astype(F32)
    return jnp.concatenate([e, e], axis=1)


def _rope_jnp(x, pos):
    half = ROPE_DIM // 2
    inv = ROPE_THETA ** (-2.0 * jnp.arange(half, dtype=F32) / ROPE_DIM)
    ang = pos.astype(F32)[:, None] * inv[None, :]
    cos = jnp.cos(ang)[None, :, None, :]
    sin = jnp.sin(ang)[None, :, None, :]
    x1 = x[..., :half]
    x2 = x[..., half:ROPE_DIM]
    return jnp.concatenate([x1 * cos - x2 * sin, x2 * cos + x1 * sin, x[..., ROPE_DIM:]], axis=-1)


def _masked_softmax(s, mask):
    s = jnp.where(mask, s, NEG_INF)
    m = jnp.max(s, axis=-1, keepdims=True)
    e = jnp.where(mask, jnp.exp(s - m), 0.0)
    return e / jnp.maximum(jnp.sum(e, axis=-1, keepdims=True), 1e-30)


def _compress_jnp(k, pe, w1, w2):
    b, t, hk, dh = k.shape
    n_cmp = (t - CMP_BLOCK) // CMP_STRIDE + 1
    start = jnp.arange(n_cmp) * CMP_STRIDE
    idx = start[:, None] + jnp.arange(CMP_BLOCK)[None, :]
    blk = k[:, idx] + pe[None, None, :, None, :]
    blk = jnp.swapaxes(blk, 2, 3).reshape(b, n_cmp, hk, CMP_BLOCK * dh)
    return jax.nn.gelu(blk @ w1) @ w2, start + CMP_BLOCK - 1


def _block_coverage(c_start, n_sel):
    b_start = jnp.arange(n_sel) * SEL_BLOCK
    lo = jnp.maximum(c_start[:, None], b_start[None, :])
    hi = jnp.minimum(c_start[:, None] + CMP_BLOCK, b_start[None, :] + SEL_BLOCK)
    return jnp.clip(hi - lo, 0, None).astype(F32) / CMP_BLOCK


def _sample_nsa_jnp(q, gate_logits, rows_new, win_new, nsa_past, win_buf, w_buf, cmp_pe, w_cmp1, w_cmp2, topk):
    b, t = q.shape[:2]
    pos0 = nsa_past.shape[1]
    qpos = pos0 + jnp.arange(t)
    q = q.reshape(b, t, NSA_KV_HEADS, NSA_GROUP, HEAD_DIM)
    rows = jnp.concatenate([nsa_past, rows_new], axis=1)
    t_kv = rows.shape[1]
    kc, c_end = _compress_jnp(rows[:, :, 0], cmp_pe[0], w_cmp1[0], w_cmp2[0])
    vc, _ = _compress_jnp(rows[:, :, 1], cmp_pe[1], w_cmp1[1], w_cmp2[1])
    kc = _rope_jnp(kc, c_end)
    n_sel = -(-t_kv // SEL_BLOCK)
    cov = _block_coverage(c_end - CMP_BLOCK + 1, n_sel)
    sel = jnp.pad(rows[:, :, 2:], ((0, 0), (0, n_sel * SEL_BLOCK - t_kv), (0, 0), (0, 0), (0, 0)))
    sel = sel.reshape(b, n_sel, SEL_BLOCK, 2, NSA_KV_HEADS, HEAD_DIM).transpose(3, 0, 4, 1, 2, 5)
    k_blk, v_blk = sel[0], sel[1]
    win_all = jnp.concatenate([win_buf, win_new], axis=1)
    n_buf = win_buf.shape[1]
    gates = jax.nn.sigmoid(gate_logits).reshape(b, t, NSA_KV_HEADS, NSA_GROUP, 3)
    s = jnp.einsum('bqhgd,bchd->bhgqc', q, kc) * ATTN_SCALE
    p = _masked_softmax(s, c_end[None, :] <= qpos[:, None])
    o_cmp = jnp.einsum('bhgqc,bchd->bqhgd', p, vc)
    imp = jnp.einsum('bhgqc,cn->bhqn', p, cov)
    blk = jnp.arange(n_sel)[None, :]
    q_blk = (qpos // SEL_BLOCK)[:, None]
    forced = (blk == 0) | (blk == q_blk) | (blk == q_blk - 1)
    imp = jnp.where(forced, FORCE_SCORE, jnp.where(blk > q_blk, -1.0, imp))
    _, idx = lax.top_k(imp, min(topk, n_sel))
    bi = jnp.arange(b)[:, None, None, None]
    hi = jnp.arange(NSA_KV_HEADS)[None, :, None, None]
    kg = k_blk[bi, hi, idx]
    vg = v_blk[bi, hi, idx]
    k_pos = idx[..., None] * SEL_BLOCK + jnp.arange(SEL_BLOCK)
    mask = (k_pos <= qpos[None, None, :, None, None])[:, :, None]
    s2 = jnp.einsum('bqhgd,bhqksd->bhgqks', q, kg) * ATTN_SCALE
    shp = s2.shape
    p2 = _masked_softmax(s2.reshape(shp[:-2] + (-1,)), mask.reshape(mask.shape[:-2] + (-1,))).reshape(shp)
    o_sel = jnp.einsum('bhgqks,bhqksd->bqhgd', p2, vg)
    kw_pos = pos0 - n_buf + jnp.arange(n_buf + t)
    sw = jnp.einsum('bqhgd,bkhd->bhgqk', q, win_all[:, :, 0]) * ATTN_SCALE
    diff = qpos[:, None] - kw_pos[None, :]
    pw = _masked_softmax(sw, (diff >= 0) & (diff < WINDOW) & (kw_pos[None, :] >= 0))
    o_win = jnp.einsum('bhgqk,bkhd->bqhgd', pw, win_all[:, :, 1])
    o = gates[..., 0:1] * o_cmp + gates[..., 1:2] * o_sel + gates[..., 2:3] * o_win
    return o.reshape(b, t, NSA_WIDTH), win_all[:, win_all.shape[1] - w_buf:]


def _to_chunks(a, c, pad):
    a = jnp.moveaxis(a, 1, 2)
    if pad:
        a = jnp.pad(a, [(0, 0), (0, 0), (0, pad)] + [(0, 0)] * (a.ndim - 3))
    b, h, tp = a.shape[:3]
    a = a.reshape((b, h, tp // c, c) + a.shape[3:])
    return jnp.moveaxis(a, 2, 0)


def _chunk_gated_delta_jnp(q, k, v, g, beta, s0):
    b, t, h, _ = q.shape
    c = min(GDN_CHUNK, t)
    pad = (-t) % c
    qc, kc, vc = _to_chunks(q, c, pad), _to_chunks(k, c, pad), _to_chunks(v, c, pad)
    gc = jnp.cumsum(_to_chunks(g, c, pad), axis=-1)
    bc = _to_chunks(beta, c, pad)
    ii = jnp.arange(c)
    incl = ii[:, None] >= ii[None, :]
    strict = ii[:, None] > ii[None, :]
    diff = gc[..., :, None] - gc[..., None, :]
    decay = jnp.where(incl, jnp.exp(jnp.where(incl, diff, 0.0)), 0.0)
    kb = kc * bc[..., None]
    a = jnp.where(strict, jnp.einsum('...id,...jd->...ij', kb, kc) * decay, 0.0)
    eye = jnp.eye(c, dtype=F32)
    tm = lax.linalg.triangular_solve(eye + a, jnp.broadcast_to(eye, a.shape), left_side=True, lower=True)
    u = jnp.einsum('...ij,...jd->...id', tm, vc * bc[..., None])
    w = jnp.einsum('...ij,...jd->...id', tm, kb * jnp.exp(gc)[..., None])

    def step(s, inp):
        qi, ki, ui, wi, gi, di = inp
        v_new = ui - jnp.einsum('bhck,bhkv->bhcv', wi, s)
        att = jnp.einsum('bhik,bhjk->bhij', qi, ki) * di
        o = jnp.einsum('bhck,bhkv->bhcv', qi * jnp.exp(gi)[..., None], s) + jnp.einsum('bhij,bhjv->bhiv', att, v_new)
        gl = gi[..., -1]
        s = s * jnp.exp(gl)[..., None, None] + jnp.einsum(
            'bhck,bhcv->bhkv', ki * jnp.exp(gl[..., None] - gi)[..., None], v_new)
        return s, o

    s, o = lax.scan(step, s0, (qc, kc, u, w, gc, decay))
    o = jnp.moveaxis(o, 0, 2).reshape(b, h, -1, o.shape[-1])[:, :, :t]
    return jnp.moveaxis(o, 1, 2), s


def _gdn_jnp(qkv, a, bb, z, conv_buf, state, conv_w, a_log, dt_bias, norm_w):
    b, t, _ = qkv.shape
    xp = jnp.concatenate([conv_buf, qkv], axis=1)
    c = xp[:, 0:t] * conv_w[0]
    for j in range(1, CONV_W):
        c = c + xp[:, j:j + t] * conv_w[j]
    c = jax.nn.silu(c)
    new_buf = xp[:, t:]
    qk_w = GDN_HEADS * GDN_DK
    l2 = lambda x: x * lax.rsqrt(jnp.sum(x * x, axis=-1, keepdims=True) + 1e-6)
    q = l2(c[..., :qk_w].reshape(b, t, GDN_HEADS, GDN_DK)) * GDN_DK ** -0.5
    k = l2(c[..., qk_w:2 * qk_w].reshape(b, t, GDN_HEADS, GDN_DK))
    v = c[..., 2 * qk_w:].reshape(b, t, GDN_HEADS, GDN_DV)
    g = -jnp.exp(a_log) * jax.nn.softplus(a + dt_bias)
    beta = jax.nn.sigmoid(bb)
    o, new_state = _chunk_gated_delta_jnp(q, k, v, g, beta, state)
    o = o * lax.rsqrt(jnp.mean(o * o, axis=-1, keepdims=True) + RMS_EPS) * norm_w
    o = o * jax.nn.silu(z).reshape(b, t, GDN_HEADS, GDN_DV)
    return o.reshape(b, t, GDN_WIDTH), new_buf, new_state


def _mem_attend_sample_jnp(x, gains, mem_kv, w_q, w_o):
    xf = x
    h = xf * lax.rsqrt(jnp.mean(xf * xf, axis=-1, keepdims=True) + RMS_EPS) * gains[4]
    q = (h @ w_q).reshape(-1, MEM_HEADS, MEM_HD)
    s = jnp.einsum('bhd,bmhd->bhm', q, mem_kv[:, :, 0]) * MEM_HD ** -0.5
    p = jax.nn.softmax(s, axis=-1)
    o = jnp.einsum('bhm,bmhd->bhd', p, mem_kv[:, :, 1]).reshape(-1, MEM_WIDTH)
    y = o @ w_o
    return x + y * lax.rsqrt(jnp.mean(y * y, axis=-1, keepdims=True) + RMS_EPS) * gains[5]


def _pack_w_in(w_in):
    nl, d, _ = w_in.shape
    o_q, o_kv, o_g = NSA_WIDTH, NSA_WIDTH + KV6_WIDTH, NSA_WIDTH + KV6_WIDTH + 3 * NSA_HEADS
    o_qkv = o_g
    o_a = o_qkv + GDN_QKV
    o_b = o_a + GDN_HEADS
    o_z = o_b + GDN_HEADS
    wq = w_in[:, :, :o_q].reshape(nl, d, NSA_HEADS, HEAD_DIM)
    zq = jnp.zeros_like(wq)
    first = jnp.concatenate([wq, zq], axis=-1)
    second = jnp.concatenate([zq, wq], axis=-1)
    kv_of_head = (jnp.arange(NSA_HEADS) // NSA_GROUP)[None, None, :, None]
    wq_pad = jnp.where(kv_of_head == 0, first, second).reshape(nl, d, QPAD_WIDTH)
    gate_grp = jnp.concatenate([w_in[:, :, o_kv:o_g], w_in[:, :, o_a:o_z],
                                jnp.zeros((nl, d, LANES - 3 * NSA_HEADS - 2 * GDN_HEADS), w_in.dtype)], axis=-1)
    packed = jnp.concatenate([wq_pad, w_in[:, :, o_q:o_kv], w_in[:, :, o_qkv:o_a], w_in[:, :, o_z:], gate_grp], axis=-1)
    return packed.astype(BF16)


def kernel(x_prompt, x_sample, cache_nsa_kv, cache_win_kv, state_gdn_S, state_gdn_conv, cache_mem_kv, page_table, mem_prompt, norm_gains, mem_norm, w_ffn_gu, w_ffn_down, w_in, w_out, cmp_pe, w_cmp1, w_cmp2, gdn_conv, gdn_A_log, gdn_dt_bias, gdn_norm, w_mem_q, w_mem_kv, w_mem_o):
    bp, t_len, d = x_prompt.shape
    bs = x_sample.shape[0]
    depth = w_in.shape[0]
    w_buf = cache_win_kv.shape[2]
    n_pages = page_table.shape[1]
    page = cache_nsa_kv.shape[2]
    past_len = n_pages * page
    n_mem = mem_prompt.shape[1]
    np_tok = bp * t_len

    gains = norm_gains.reshape(depth, 8, 1, d)
    wgu = w_ffn_gu.astype(BF16)
    wd = w_ffn_down.astype(BF16)
    w_in_p = _pack_w_in(w_in)
    w_out_b = w_out.astype(BF16)
    w_mq = w_mem_q.astype(BF16)
    w_mkv = w_mem_kv.astype(BF16)
    w_mo = w_mem_o.astype(BF16)
    mem_gain = mem_norm.reshape(depth, 1, d)

    pos_p = jnp.arange(t_len)
    tabs_p = _rope_tables(pos_p) + (_block_onehot(pos_p),)
    pos_s = jnp.full((bs,), past_len, jnp.int32)
    tabs_s = _rope_tables(pos_s) + (jnp.zeros((bs, LANES), F32),)

    half_k = CMP_STRIDE * HEAD_DIM
    pe8 = jnp.pad(cmp_pe.reshape(depth, 2, CMP_BLOCK // CMP_STRIDE, half_k), ((0, 0), (0, 0), (0, 6), (0, 0)))
    w1c = jnp.concatenate([w_cmp1[:, :, :half_k], w_cmp1[:, :, half_k:]], axis=-1).astype(BF16)
    z2 = jnp.zeros_like(w_cmp2)
    w2p = jnp.stack([jnp.concatenate([w_cmp2, z2], axis=-1), jnp.concatenate([z2, w_cmp2], axis=-1)], axis=2).astype(BF16)
    tabs_c = _rope_tables(jnp.arange(t_len // CMP_STRIDE) * CMP_STRIDE + (CMP_BLOCK - 1))
    lane_pad = lambda v: jnp.pad(v, ((0, 0), (GATE_A, LANES - GATE_A - GDN_HEADS))).reshape(depth, 1, LANES)
    alog_row = lane_pad(gdn_A_log)
    dtb_row = lane_pad(gdn_dt_bias)
    gnorm = gdn_norm.reshape(depth, 1, GDN_DV)

    n_cmp = (t_len - CMP_BLOCK) // CMP_STRIDE + 1
    n_cmp_pad = t_len // CMP_STRIDE
    n_sel = -(-t_len // SEL_BLOCK)
    c_start = jnp.arange(n_cmp) * CMP_STRIDE
    assert n_sel <= HEAD_DIM and t_len >= WINDOW + NSA_TQ and t_len % NSA_TK == 0
    covt = jnp.pad(_block_coverage(c_start, n_sel).T, ((0, HEAD_DIM - n_sel), (0, n_cmp_pad - n_cmp))).astype(BF16)
    topk = min(SEL_TOPK, n_sel)

    assert x_sample.shape[1] == 1 and past_len % SEL_BLOCK == 0 and bs % GDN_SB == 0
    t_kv = past_len + 1
    n_cmp_s = (t_kv - CMP_BLOCK) // CMP_STRIDE + 1
    assert (n_cmp_s - 1) * CMP_STRIDE + CMP_BLOCK <= past_len
    g_s = past_len // CMP_STRIDE
    n_sel_s = -(-t_kv // SEL_BLOCK)
    assert n_sel_s <= HEAD_DIM
    topk_s = min(SEL_TOPK, n_sel_s)
    cov_s = jnp.pad(_block_coverage(jnp.arange(n_cmp_s) * CMP_STRIDE, n_sel_s),
                    ((0, g_s - n_cmp_s), (0, HEAD_DIM - n_sel_s))).astype(BF16)
    eexp = (jnp.arange(past_len)[None, :] // SEL_BLOCK == jnp.arange(HEAD_DIM)[:, None]).astype(BF16)
    tabs_cs = _rope_tables(jnp.arange(g_s) * CMP_STRIDE + (CMP_BLOCK - 1))
    n_phys = cache_nsa_kv.shape[1]
    cache_cmp = cache_nsa_kv[:, :, :, 0:2].reshape(depth, n_phys, page, 2 * LANES)
    cache_sel = cache_nsa_kv[:, :, :, 2:4].astype(BF16).reshape(depth, n_phys, page, 2 * LANES)
    win_cache = cache_win_kv.astype(BF16).reshape(depth, bs, w_buf, 2 * LANES)
    conv_cache = state_gdn_conv.reshape(depth, bs, (CONV_W - 1) * GDN_QKV)
    mem_cache = cache_mem_kv.astype(BF16).reshape(depth, bs, n_mem, 2 * MEM_WIDTH)

    tm_p = 512
    yp = x_prompt.reshape(np_tok, d)
    ys = x_sample.reshape(bs, d)
    mem_flat = mem_prompt.reshape(bp * n_mem, d)
    outs = [[] for _ in range(9)]
    for l in range(depth):
        mem_kv_p = _normmm(mem_flat, mem_gain, (l,), w_mkv, l, n_mem)
        yp = _ffn(yp, gains, wgu, wd, l, 0, tm_p)
        qpad, rows, win, kaug, vsel, kvwin, gates, qkv, z = _inproj(yp, gains, w_in_p, tabs_p, l, tm_p)
        rows5 = rows.reshape(bp, t_len, 4, NSA_KV_HEADS, HEAD_DIM)
        kc, vc = _compress_prompt(rows, pe8, w1c, w2p, tabs_c, l, bp, t_len)
        o_nsa = _nsa_prompt(qpad, gates, kaug, vsel, kvwin, kc, vc, covt, bp, t_len, n_cmp, n_sel, topk)
        qg, kd, w_g, u_g, att, egl = _gdn_prep(qkv, gates, gdn_conv, alog_row, dtb_row, l, bp, t_len)
        o_gdn, s_p = _gdn_scan(qg, kd, w_g, u_g, att, egl, z, gnorm, l, bp, t_len)
        conv_p = qkv.reshape(bp, t_len, GDN_QKV)[:, t_len - (CONV_W - 1):]
        yp = _outproj(yp, o_nsa, o_gdn, w_out_b, gains, l, tm_p)
        yp = _mem_block(yp, gains, w_mq, w_mo, mem_kv_p, l, tm_p, t_len // tm_p)
        yp = _ffn(yp, gains, wgu, wd, l, 1, tm_p)
        win5 = win.reshape(bp, t_len, 2, NSA_KV_HEADS, HEAD_DIM)
        win_p = win5[:, t_len - w_buf:] if t_len >= w_buf else jnp.pad(
            win5, ((0, 0), (w_buf - t_len, 0), (0, 0), (0, 0), (0, 0)))

        ys = _ffn(ys, gains, wgu, wd, l, 0, bs)
        qpad_s, rows_s, win_s, _, _, _, gates_s, qkv_s, z_s = _inproj(ys, gains, w_in_p, tabs_s, l, bs)
        o_nsa_s = _nsa_sample(
            page_table, cache_cmp, cache_sel, qpad_s.astype(F32).reshape(bs, NSA_HEADS, LANES), gates_s.reshape(bs, 1, LANES),
            rows_s.reshape(bs, 1, 4 * LANES), win_s.reshape(bs, 1, 2 * LANES), win_cache, pe8, w1c, w2p, tabs_cs,
            cov_s, eexp, l, n_cmp_s, n_sel_s, topk_s).reshape(bs, NSA_WIDTH)
        o_gdn_s, s_s = _gdn_sample(qkv_s, conv_cache, gates_s, z_s, state_gdn_S, gdn_conv, alog_row, dtb_row, gnorm, l)
        ys = _outproj(ys, o_nsa_s, o_gdn_s, w_out_b, gains, l, bs)
        q_m = _normmm(ys, gains, (l, 4), w_mq, l, bs)
        o_m = _mem_sample(q_m.reshape(bs, 1, MEM_WIDTH), mem_cache, l).reshape(bs, MEM_WIDTH)
        ys = _outproj(ys, o_m[:, :MEM_WIDTH // 2], o_m[:, MEM_WIDTH // 2:], w_mo, gains, l, bs, gain_idx=5)
        ys = _ffn(ys, gains, wgu, wd, l, 1, bs)
        win_all = jnp.concatenate([cache_win_kv[l], win_s.reshape(bs, 1, 2, NSA_KV_HEADS, HEAD_DIM)], axis=1)
        wbuf_s = win_all[:, win_all.shape[1] - w_buf:]
        conv_s = jnp.concatenate([state_gdn_conv[l], qkv_s[:, None, :]], axis=1)[:, 1:]

        for lst, val in zip(outs, (rows5, win_p, s_p, conv_p, mem_kv_p.reshape(bp, n_mem, 2, MEM_HEADS, MEM_HD),
                                   rows_s.reshape(bs, 1, 4, NSA_KV_HEADS, HEAD_DIM), wbuf_s, s_s, conv_s)):
            lst.append(val)
    return (yp.reshape(bp, t_len, d), ys.reshape(bs, 1, d)) + tuple(jnp.stack(v) for v in outs)
```

```python
import functools
import math

import numpy as np
import jax
import jax.numpy as jnp
from jax import lax
from jax.experimental import pallas as pl
from jax.experimental.pallas import tpu as pltpu

F32 = jnp.float32
BF16 = jnp.bfloat16

HEAD_DIM = 64
NSA_HEADS = 8
NSA_KV_HEADS = 2
NSA_GROUP = NSA_HEADS // NSA_KV_HEADS
NSA_WIDTH = NSA_HEADS * HEAD_DIM
ROPE_DIM = HEAD_DIM // 4
ROPE_THETA = 500000.0
ATTN_SCALE = HEAD_DIM ** -0.5
CMP_BLOCK = 32
CMP_STRIDE = 16
CMP_HIDDEN = 2 * HEAD_DIM
SEL_BLOCK = 64
SEL_TOPK = 16
WINDOW = 512
FORCE_SCORE = 1.0e4
GDN_DK = 128
GDN_DV = 128
GDN_HEADS = 4
GDN_WIDTH = GDN_HEADS * GDN_DV
GDN_QKV = GDN_HEADS * (2 * GDN_DK + GDN_DV)
CONV_W = 4
GDN_CHUNK = 64
MEM_HEADS = 4
MEM_HD = 64
MEM_WIDTH = MEM_HEADS * MEM_HD
RMS_EPS = 1e-6
NEG_INF = -1.0e30
KV6_WIDTH = 6 * NSA_KV_HEADS * HEAD_DIM

LANES = 128
VMEM_LIMIT_BYTES = 56 * 1024 * 1024

QPAD_WIDTH = NSA_HEADS * LANES
COL_Q = 0
COL_KV = COL_Q + QPAD_WIDTH
COL_QKV = COL_KV + KV6_WIDTH
COL_Z = COL_QKV + GDN_QKV
COL_GATE = COL_Z + GDN_WIDTH
IN_PACKED = COL_GATE + LANES
GATE_A = 3 * NSA_HEADS
GATE_B = GATE_A + GDN_HEADS


def _cparams(*sem):
    return pltpu.CompilerParams(dimension_semantics=sem, vmem_limit_bytes=VMEM_LIMIT_BYTES)


def _rms(x, w):
    return x * lax.rsqrt(jnp.mean(x * x, axis=-1, keepdims=True) + RMS_EPS) * w


def _const_spec(shape):
    nd = len(shape)
    return pl.BlockSpec(shape, lambda *_: (0,) * nd)


def _layer_spec(shape, *lead):
    nlead = len(lead)
    nd = len(shape)
    return pl.BlockSpec((None,) * nlead + tuple(shape), lambda *_: tuple(lead) + (0,) * nd)


FFN_CHUNK = 256


def _ffn_body(x_ref, g0_ref, g1_ref, wgu_ref, wd_ref, o_ref, acc_ref, *, d_ff):
    x = x_ref[...]
    h = _rms(x, g0_ref[...]).astype(BF16)
    for f in range(d_ff // FFN_CHUNK):
        lo = f * FFN_CHUNK
        g = jnp.dot(h, wgu_ref[:, lo:lo + FFN_CHUNK], preferred_element_type=F32)
        u = jnp.dot(h, wgu_ref[:, d_ff + lo:d_ff + lo + FFN_CHUNK], preferred_element_type=F32)
        a = (g * jax.nn.sigmoid(g) * u).astype(BF16)
        d = jnp.dot(a, wd_ref[lo:lo + FFN_CHUNK, :], preferred_element_type=F32)
        if f == 0:
            acc_ref[...] = d
        else:
            acc_ref[...] += d
    o_ref[...] = x + 0.5 * _rms(acc_ref[...], g1_ref[...])


def _ffn(x, gains, wgu, wd, l, j, tm):
    n, d = x.shape
    d_ff = wd.shape[2]
    return pl.pallas_call(
        functools.partial(_ffn_body, d_ff=d_ff),
        grid=(n // tm,),
        in_specs=[
            pl.BlockSpec((tm, d), lambda i: (i, 0)),
            _layer_spec((1, d), l, 2 * j * 3),
            _layer_spec((1, d), l, 2 * j * 3 + 1),
            _layer_spec((d, 2 * d_ff), l, j),
            _layer_spec((d_ff, d), l, j),
        ],
        out_specs=pl.BlockSpec((tm, d), lambda i: (i, 0)),
        out_shape=jax.ShapeDtypeStruct((n, d), F32),
        scratch_shapes=[pltpu.VMEM((tm, d), F32)],
        compiler_params=_cparams("parallel"),
        name=f"ffn{j}",
    )(x, gains, gains, wgu, wd)


def _rope(v, c1, sm1, sp1):
    n = v.shape[1] // LANES
    c, sm, sp = (t if n == 1 else jnp.concatenate([t] * n, axis=1) for t in (c1, sm1, sp1))
    w = v.shape[1]
    return v * c + pltpu.roll(v, w - ROPE_DIM // 2, 1) * sm + pltpu.roll(v, ROPE_DIM // 2, 1) * sp


def _inproj_body(x_ref, g_ref, w_ref, c_ref, sm_ref, sp_ref, e_ref,
                 qpad_ref, rows_ref, win_ref, kaug_ref, vsel_ref, kvwin_ref, gates_ref, qkv_ref, z_ref):
    h = _rms(x_ref[...], g_ref[...]).astype(BF16)
    c1, sm1, sp1 = c_ref[...], sm_ref[...], sp_ref[...]

    def mm(lo, hi):
        return jnp.dot(h, w_ref[:, lo:hi], preferred_element_type=F32)

    q = _rope(mm(COL_Q, COL_KV), c1, sm1, sp1)
    qpad_ref[...] = (q * ATTN_SCALE).astype(BF16)
    kv = mm(COL_KV, COL_QKV)
    ksel = _rope(kv[:, 2 * LANES:3 * LANES], c1, sm1, sp1)
    vsel = kv[:, 3 * LANES:4 * LANES]
    kwin = _rope(kv[:, 4 * LANES:5 * LANES], c1, sm1, sp1)
    vwin = kv[:, 5 * LANES:6 * LANES]
    rows_ref[:, 0:2 * LANES] = kv[:, 0:2 * LANES]
    rows_ref[:, 2 * LANES:3 * LANES] = ksel
    rows_ref[:, 3 * LANES:4 * LANES] = vsel
    win_ref[:, 0:LANES] = kwin
    win_ref[:, LANES:2 * LANES] = vwin
    e2 = e_ref[...]
    lane = lax.broadcasted_iota(jnp.int32, ksel.shape, 1)
    first = lane < HEAD_DIM
    kaug_ref[:, 0:LANES] = jnp.where(first, ksel, e2).astype(BF16)
    kaug_ref[:, LANES:2 * LANES] = jnp.where(first, e2, ksel).astype(BF16)
    vsel_ref[...] = vsel.astype(BF16)
    kvwin_ref[:, 0:LANES] = kwin.astype(BF16)
    kvwin_ref[:, LANES:2 * LANES] = vwin.astype(BF16)
    qkv_ref[...] = mm(COL_QKV, COL_Z)
    z_ref[...] = mm(COL_Z, COL_GATE)
    gates_ref[...] = mm(COL_GATE, IN_PACKED)


def _inproj(x, gains, w_in_p, tabs, l, tm):
    n, d = x.shape
    c_t, sm_t, sp_t, e_t = tabs
    nt = c_t.shape[0] // tm
    tab = lambda w: pl.BlockSpec((tm, w), lambda i: (i % nt, 0))
    row = lambda w: pl.BlockSpec((tm, w), lambda i: (i, 0))
    widths = [(QPAD_WIDTH, BF16), (4 * LANES, F32), (2 * LANES, F32), (2 * LANES, BF16), (LANES, BF16),
              (2 * LANES, BF16), (LANES, F32), (GDN_QKV, F32), (GDN_WIDTH, F32)]
    return pl.pallas_call(
        _inproj_body,
        grid=(n // tm,),
        in_specs=[row(d), _layer_spec((1, d), l, 2), _layer_spec((d, IN_PACKED), l),
                  tab(LANES), tab(LANES), tab(LANES), tab(LANES)],
        out_specs=[row(w) for w, _ in widths],
        out_shape=[jax.ShapeDtypeStruct((n, w), dt) for w, dt in widths],
        compiler_params=_cparams("parallel"),
        name="inproj",
    )(x, gains, w_in_p, c_t, sm_t, sp_t, e_t)


def _normmm_body(x_ref, g_ref, w_ref, o_ref):
    h = _rms(x_ref[...], g_ref[...]).astype(BF16)
    o_ref[...] = jnp.dot(h, w_ref[...], preferred_element_type=F32)


def _normmm(x, gain, gain_lead, w, l, tm):
    n, d = x.shape
    nout = w.shape[-1]
    return pl.pallas_call(
        _normmm_body,
        grid=(n // tm,),
        in_specs=[pl.BlockSpec((tm, d), lambda i: (i, 0)), _layer_spec((1, d), *gain_lead), _layer_spec((d, nout), l)],
        out_specs=pl.BlockSpec((tm, nout), lambda i: (i, 0)),
        out_shape=jax.ShapeDtypeStruct((n, nout), F32),
        compiler_params=_cparams("parallel"),
        name="memkv",
    )(x, gain, w)


def _outproj_body(x_ref, a1_ref, a2_ref, w_ref, g_ref, o_ref):
    k1 = a1_ref.shape[1]
    acc = jnp.dot(a1_ref[...].astype(BF16), w_ref[0:k1, :], preferred_element_type=F32)
    acc = acc + jnp.dot(a2_ref[...].astype(BF16), w_ref[k1:, :], preferred_element_type=F32)
    o_ref[...] = x_ref[...] + _rms(acc, g_ref[...])


def _outproj(x, a1, a2, w_out, gains, l, tm, gain_idx=3):
    n, d = x.shape
    row = lambda w: pl.BlockSpec((tm, w), lambda i: (i, 0))
    return pl.pallas_call(
        _outproj_body,
        grid=(n // tm,),
        in_specs=[row(d), row(a1.shape[1]), row(a2.shape[1]),
                  _layer_spec((a1.shape[1] + a2.shape[1], d), l), _layer_spec((1, d), l, gain_idx)],
        out_specs=row(d),
        out_shape=jax.ShapeDtypeStruct((n, d), F32),
        compiler_params=_cparams("parallel"),
        name="outproj",
    )(x, a1, a2, w_out, gains)


def _mem_body(x_ref, g4_ref, g5_ref, wq_ref, wo_ref, kv_ref, o_ref):
    x = x_ref[...]
    h = _rms(x, g4_ref[...]).astype(BF16)
    q = jnp.dot(h, wq_ref[...], preferred_element_type=F32) * (MEM_HD ** -0.5)
    kv = kv_ref[...].astype(BF16)
    heads = range(MEM_HEADS)
    cols = [slice(hd * MEM_HD, (hd + 1) * MEM_HD) for hd in heads]
    ss = [lax.dot_general(q[:, cols[hd]].astype(BF16), kv[:, cols[hd]], (((1,), (1,)), ((), ())),
                          preferred_element_type=F32) for hd in heads]
    es = [jnp.exp(s - jnp.max(s, axis=-1, keepdims=True)) for s in ss]
    ps = [e / jnp.sum(e, axis=-1, keepdims=True) for e in es]
    outs = [jnp.dot(ps[hd].astype(BF16), kv[:, MEM_WIDTH + hd * MEM_HD:MEM_WIDTH + (hd + 1) * MEM_HD],
                    preferred_element_type=F32) for hd in heads]
    o = jnp.concatenate(outs, axis=1).astype(BF16)
    y = jnp.dot(o, wo_ref[...], preferred_element_type=F32)
    o_ref[...] = x + _rms(y, g5_ref[...])


def _mem_block(x, gains, w_q, w_o, mem_kv, l, tm, tiles_per_batch):
    n, d = x.shape
    m = mem_kv.shape[0] // (n // (tm * tiles_per_batch))
    return pl.pallas_call(
        _mem_body,
        grid=(n // tm,),
        in_specs=[pl.BlockSpec((tm, d), lambda i: (i, 0)), _layer_spec((1, d), l, 4), _layer_spec((1, d), l, 5),
                  _layer_spec((d, MEM_WIDTH), l), _layer_spec((MEM_WIDTH, d), l),
                  pl.BlockSpec((m, 2 * MEM_WIDTH), lambda i: (i // tiles_per_batch, 0))],
        out_specs=pl.BlockSpec((tm, d), lambda i: (i, 0)),
        out_shape=jax.ShapeDtypeStruct((n, d), F32),
        compiler_params=_cparams("parallel"),
        name="memattn",
    )(x, gains, gains, w_q, w_o, mem_kv)


NSA_TQ = 128
NSA_TK = 512


def _softmax_cols(s, valid):
    s = jnp.where(valid, s, NEG_INF)
    m = jnp.max(s, axis=0, keepdims=True)
    e = jnp.where(valid, jnp.exp(s - m), 0.0)
    return e / jnp.maximum(jnp.sum(e, axis=0, keepdims=True), 1e-30)


def _softmax_rows(s, valid):
    s = jnp.where(valid, s, NEG_INF)
    m = jnp.max(s, axis=1, keepdims=True)
    e = jnp.where(valid, jnp.exp(s - m), 0.0)
    return e / jnp.maximum(jnp.sum(e, axis=1, keepdims=True), 1e-30)


def _split3(x):
    hi = x.astype(BF16)
    r = x - hi.astype(F32)
    mid = r.astype(BF16)
    lo = (r - mid.astype(F32)).astype(BF16)
    return hi, mid, lo


def _select_topk(imp_t, blk, n_pick):
    n_blocks = imp_t.shape[0]
    sel = jnp.zeros(imp_t.shape, F32)
    work = imp_t
    for _ in range(n_pick):
        mx = jnp.max(work, axis=0, keepdims=True)
        first = jnp.min(jnp.where(work == mx, blk, float(n_blocks)), axis=0, keepdims=True)
        hit = blk == first
        sel = jnp.where(hit, 1.0, sel)
        work = jnp.where(hit, -3.0e38, work)
    return sel


def _transpose_01(x_t, eye):
    return lax.dot_general(eye, x_t.astype(BF16), (((1,), (1,)), ((), ())), preferred_element_type=F32)


def _nsa_body(q_ref, gt_ref, kaug_ref, vsel_ref, kvw_ref, kc_ref, vc_ref, covt_ref, o_ref, *, n_cmp, n_sel, topk):
    tq = q_ref.shape[0]
    n_blk = covt_ref.shape[0]
    n_cmp_pad = kc_ref.shape[0]
    g4 = NSA_GROUP
    i = pl.program_id(1)
    q0 = i * tq
    gates = jax.nn.sigmoid(gt_ref[...])

    qpos_row = q0 + lax.broadcasted_iota(jnp.int32, (1, tq), 1)
    qpos_row4 = jnp.concatenate([qpos_row] * g4, axis=1)
    qpos_col = q0 + lax.broadcasted_iota(jnp.int32, (tq, 1), 0)
    qpos_col4 = jnp.concatenate([qpos_col] * g4, axis=0)
    lane = lax.broadcasted_iota(jnp.int32, (tq, LANES), 1)
    blk = lax.broadcasted_iota(jnp.int32, (n_blk, 1), 0)
    blk_f = blk.astype(F32)
    qblk = qpos_row // SEL_BLOCK
    forced = (blk == 0) | (blk == qblk) | (blk == qblk - 1)
    future = blk > qblk
    exists = blk < n_sel
    n_eye = NSA_KV_HEADS * tq
    eye = (lax.broadcasted_iota(jnp.int32, (n_eye, n_eye), 0)
           == lax.broadcasted_iota(jnp.int32, (n_eye, n_eye), 1)).astype(BF16)
    c_idx = lax.broadcasted_iota(jnp.int32, (n_cmp_pad, 1), 0)
    c_end = c_idx * CMP_STRIDE + (CMP_BLOCK - 1)
    cmp_valid = (c_end <= qpos_row4) & (c_idx < n_cmp)

    w_start = pl.multiple_of(jnp.maximum(q0 - WINDOW, 0), tq)
    w_len = WINDOW + tq
    kpos_w = w_start + lax.broadcasted_iota(jnp.int32, (1, w_len), 1)
    dw = qpos_col4 - kpos_w
    win_valid = (dw >= 0) & (dw < WINDOW)

    n_kt = (q0 + tq + NSA_TK - 1) // NSA_TK
    kc = kc_ref[...]
    vc = vc_ref[...]
    covt = covt_ref[...]
    heads = range(NSA_KV_HEADS)
    nt_dims = (((1,), (1,)), ((), ()))
    owns = [lane < HEAD_DIM, lane >= HEAD_DIM]
    qpads = [[q_ref[:, (hk * g4 + g) * LANES:(hk * g4 + g + 1) * LANES] for g in range(g4)] for hk in heads]
    qpad4s = [jnp.concatenate(qpads[hk], axis=0) for hk in heads]

    s_ts = [lax.dot_general(kc, qpad4s[hk], nt_dims, preferred_element_type=F32) for hk in heads]
    p_ts = [_softmax_cols(s_t, cmp_valid) for s_t in s_ts]
    o_cmps = [jnp.dot(p_t.T.astype(BF16), vc, preferred_element_type=F32) for p_t in p_ts]
    p_sum = jnp.concatenate([sum(p_t[:, g * tq:(g + 1) * tq] for g in range(g4)) for p_t in p_ts], axis=1)
    imp_t = sum(jnp.dot(covt, part, preferred_element_type=F32) for part in _split3(p_sum))
    both = lambda m: jnp.concatenate([m] * NSA_KV_HEADS, axis=1)
    imp_t = jnp.where(both(forced), FORCE_SCORE, jnp.where(both(future), -1.0, imp_t))
    imp_t = jnp.where(exists, imp_t, -2.0)
    sel_t = _select_topk(imp_t, blk_f, topk)
    selneg = jnp.where(_transpose_01(sel_t, eye) > 0.5, 0.0, NEG_INF)
    lhs4s = []
    for hk in heads:
        sn = selneg[hk * tq:(hk + 1) * tq]
        selneg2 = jnp.concatenate([sn, sn], axis=1).astype(BF16)
        lhs4s.append(jnp.concatenate([jnp.where(owns[hk], qp, selneg2) for qp in qpads[hk]], axis=0))

    def kt_body(kt, carry, causal):
        ks = pl.multiple_of(kt * NSA_TK, NSA_TK)
        v = vsel_ref[pl.ds(ks, NSA_TK), :]
        heads = range(NSA_KV_HEADS)
        ss = [lax.dot_general(lhs4s[hk], kaug_ref[pl.ds(ks, NSA_TK), hk * LANES:(hk + 1) * LANES],
                              (((1,), (1,)), ((), ())), preferred_element_type=F32) for hk in heads]
        if causal:
            kpos = ks + lax.broadcasted_iota(jnp.int32, (1, NSA_TK), 1)
            ss = [jnp.where(kpos <= qpos_col4, s, NEG_INF) for s in ss]
        m_new = [jnp.maximum(carry[hk][0], jnp.max(ss[hk], axis=1, keepdims=True)) for hk in heads]
        alpha = [jnp.exp(carry[hk][0] - m_new[hk]) for hk in heads]
        ps = [jnp.exp(ss[hk] - m_new[hk]) for hk in heads]
        ls = [alpha[hk] * carry[hk][1] + jnp.sum(ps[hk], axis=1, keepdims=True) for hk in heads]
        pv = [jnp.dot(ps[hk].astype(BF16), v, preferred_element_type=F32) for hk in heads]
        return tuple((m_new[hk], ls[hk], alpha[hk] * carry[hk][2] + pv[hk]) for hk in heads)

    init1 = (jnp.full((g4 * tq, 1), NEG_INF, F32), jnp.zeros((g4 * tq, 1), F32), jnp.zeros((g4 * tq, LANES), F32))
    carry = lax.fori_loop(0, n_kt - 1, functools.partial(kt_body, causal=False), (init1,) * NSA_KV_HEADS)
    carry = kt_body(n_kt - 1, carry, causal=True)

    o_sels = [carry[hk][2] / jnp.maximum(carry[hk][1], 1e-30) for hk in heads]

    kw = kvw_ref[pl.ds(w_start, w_len), 0:LANES]
    vw = kvw_ref[pl.ds(w_start, w_len), LANES:2 * LANES]
    s_ws = [jnp.where(win_valid, lax.dot_general(qpad4s[hk], kw, nt_dims, preferred_element_type=F32), NEG_INF)
            for hk in heads]
    e_ws = [jnp.where(win_valid, jnp.exp(s_w - jnp.max(s_w, axis=1, keepdims=True)), 0.0) for s_w in s_ws]
    o_wins = [jnp.dot(e_w.astype(BF16), vw, preferred_element_type=F32)
              / jnp.maximum(jnp.sum(e_w, axis=1, keepdims=True), 1e-30) for e_w in e_ws]

    for hk in heads:
        def gate_col(j):
            cols = [gates[:, 3 * (hk * g4 + g) + j:3 * (hk * g4 + g) + j + 1] for g in range(g4)]
            return jnp.concatenate(cols, axis=0)

        o4 = gate_col(0) * o_cmps[hk] + gate_col(1) * o_sels[hk] + gate_col(2) * o_wins[hk]
        for pair in range(g4 // 2):
            a = o4[(2 * pair) * tq:(2 * pair + 1) * tq]
            b = o4[(2 * pair + 1) * tq:(2 * pair + 2) * tq]
            if hk == 0:
                both = jnp.where(lane < HEAD_DIM, a, pltpu.roll(b, HEAD_DIM, 1))
            else:
                both = jnp.where(lane < HEAD_DIM, pltpu.roll(a, HEAD_DIM, 1), b)
            col = (hk * (g4 // 2) + pair) * LANES
            o_ref[:, col:col + LANES] = both


def _nsa_prompt(qpad, gates, kaug, vsel, kvwin, kc, vc, covt, batch, t_len, n_cmp, n_sel, topk):
    n = qpad.shape[0]
    nq = t_len // NSA_TQ
    n_cmp_pad = kc.shape[0] // batch
    qrow = lambda w: pl.BlockSpec((NSA_TQ, w), lambda b, i: (b * nq + i, 0))
    per_b = lambda r, w: pl.BlockSpec((r, w), lambda b, i: (b, 0))
    return pl.pallas_call(
        functools.partial(_nsa_body, n_cmp=n_cmp, n_sel=n_sel, topk=topk),
        grid=(batch, nq),
        in_specs=[qrow(QPAD_WIDTH), qrow(LANES), per_b(t_len, 2 * LANES), per_b(t_len, LANES),
                  per_b(t_len, 2 * LANES), per_b(n_cmp_pad, LANES), per_b(n_cmp_pad, LANES),
                  _const_spec(covt.shape)],
        out_specs=qrow(NSA_WIDTH),
        out_shape=jax.ShapeDtypeStruct((n, NSA_WIDTH), F32),
        compiler_params=_cparams("parallel", "parallel"),
        name="nsa_prompt",
    )(qpad, gates, kaug, vsel, kvwin, kc, vc, covt)


def _group_features(load_l, lane):
    heads = ([], [])
    for m in range(CMP_STRIDE // 2):
        a = load_l(2 * m)
        b = load_l(2 * m + 1)
        heads[0].append(jnp.where(lane < HEAD_DIM, a, pltpu.roll(b, HEAD_DIM, 1)))
        heads[1].append(jnp.where(lane < HEAD_DIM, pltpu.roll(a, HEAD_DIM, 1), b))
    return [jnp.concatenate(h, axis=1) for h in heads]


def _compress_heads(x_heads, pe8, w1, w2_h0, w2_h1):
    g = x_heads[0].shape[0]
    x2 = jnp.concatenate(x_heads, axis=0).astype(BF16)
    z = jnp.dot(x2, w1, preferred_element_type=F32)
    zb = jnp.dot(pe8.astype(BF16), w1, preferred_element_type=F32)
    bias = zb[0:1, 0:CMP_HIDDEN] + zb[1:2, CMP_HIDDEN:]
    nxt = pltpu.roll(z[:, CMP_HIDDEN:], 2 * g - 1, 0)
    hid = jax.nn.gelu(z[:, 0:CMP_HIDDEN] + nxt + bias).astype(BF16)
    return (jnp.dot(hid[0:g], w2_h0, preferred_element_type=F32)
            + jnp.dot(hid[g:], w2_h1, preferred_element_type=F32))


def _compress_body(krows_ref, vrows_ref, pe_ref, w1_ref, w2_ref, c_ref, sm_ref, sp_ref, kc_ref, vc_ref):
    g = kc_ref.shape[0]
    lane = lax.broadcasted_iota(jnp.int32, (g, LANES), 1)
    outs = []
    for t, ref in enumerate((krows_ref, vrows_ref)):
        load = lambda l, ref=ref: ref[pl.ds(l, g, stride=CMP_STRIDE), :]
        outs.append(_compress_heads(_group_features(load, lane), pe_ref[t], w1_ref[t], w2_ref[t, 0], w2_ref[t, 1]))
    kc_ref[...] = _rope(outs[0], c_ref[...], sm_ref[...], sp_ref[...]).astype(BF16)
    vc_ref[...] = outs[1].astype(BF16)


def _compress_prompt(rows, pe8, w1c, w2p, tabs_c, l, batch, t_len):
    g = t_len // CMP_STRIDE
    out = jax.ShapeDtypeStruct((batch * g, LANES), BF16)
    return pl.pallas_call(
        _compress_body,
        grid=(batch,),
        in_specs=[pl.BlockSpec((t_len, LANES), lambda b: (b, 0)), pl.BlockSpec((t_len, LANES), lambda b: (b, 1)),
                  _layer_spec((2, 8, CMP_STRIDE * HEAD_DIM), l),
                  _layer_spec((2, CMP_STRIDE * HEAD_DIM, 2 * CMP_HIDDEN), l),
                  _layer_spec((2, 2, CMP_HIDDEN, LANES), l),
                  _const_spec((g, LANES)), _const_spec((g, LANES)), _const_spec((g, LANES))],
        out_specs=[pl.BlockSpec((g, LANES), lambda b: (b, 0))] * 2,
        out_shape=[out, out],
        compiler_params=_cparams("parallel"),
        name="compress",
    )(rows, rows, pe8, w1c, w2p, *tabs_c)


GDN_TT = 512
HALO = 8


def _dot3(a, x):
    return sum(jnp.dot(a, part, preferred_element_type=F32) for part in _split3(x))


def _dot3_nt(a, x):
    return sum(lax.dot_general(a, part, (((1,), (1,)), ((), ())), preferred_element_type=F32) for part in _split3(x))


def _l2n(x):
    return x * lax.rsqrt(jnp.sum(x * x, axis=-1, keepdims=True) + 1e-6)


def _softplus(x):
    return jnp.maximum(x, 0.0) + jnp.log(1.0 + jnp.exp(-jnp.abs(x)))


def _gdn_prep_body(x_ref, halo_ref, gt_ref, cw_ref, alog_ref, dtb_ref,
                   qg_ref, kd_ref, w_ref, u_ref, att_ref, egl_ref,
                   ext_ref, q_s, k_s, kb_s, rhs_s, gc_s):
    c_len = GDN_CHUNK
    tt = x_ref.shape[0]
    nch = tt // c_len
    qk_w = GDN_HEADS * GDN_DK
    first = pl.program_id(1) == 0
    ext_ref[0:HALO, :] = jnp.where(first, 0.0, halo_ref[...])
    ext_ref[HALO:HALO + tt, :] = x_ref[...]
    c = ext_ref[pl.ds(HALO - (CONV_W - 1), tt), :] * cw_ref[0:1, :]
    for j in range(1, CONV_W):
        c = c + ext_ref[pl.ds(HALO - (CONV_W - 1) + j, tt), :] * cw_ref[j:j + 1, :]
    c = c * jax.nn.sigmoid(c)

    gt = gt_ref[...]
    g_full = -jnp.exp(alog_ref[...]) * _softplus(gt + dtb_ref[...])
    beta_full = jax.nn.sigmoid(gt)
    r = lax.broadcasted_iota(jnp.int32, (tt, tt), 0)
    cc = lax.broadcasted_iota(jnp.int32, (tt, tt), 1)
    same = (r // c_len) == (cc // c_len)
    gcum = _dot3(jnp.where(same & (cc <= r), 1.0, 0.0).astype(BF16), g_full)
    gtot = _dot3(jnp.where(same, 1.0, 0.0).astype(BF16), g_full)
    gc_s[...] = gcum
    for h in range(GDN_HEADS):
        sl = slice(h * GDN_DK, (h + 1) * GDN_DK)
        gc_col = gcum[:, GATE_A + h:GATE_A + h + 1]
        gl_col = gtot[:, GATE_A + h:GATE_A + h + 1]
        beta_col = beta_full[:, GATE_B + h:GATE_B + h + 1]
        qh = _l2n(c[:, h * GDN_DK:(h + 1) * GDN_DK]) * (GDN_DK ** -0.5)
        kh = _l2n(c[:, qk_w + h * GDN_DK:qk_w + (h + 1) * GDN_DK])
        vh = c[:, 2 * qk_w + h * GDN_DV:2 * qk_w + (h + 1) * GDN_DV]
        eg = jnp.exp(gc_col)
        kb = kh * beta_col
        q_s[:, sl] = qh
        k_s[:, sl] = kh
        kb_s[:, sl] = kb
        rhs_s[:, 2 * h * GDN_DK:(2 * h + 1) * GDN_DK] = vh * beta_col
        rhs_s[:, (2 * h + 1) * GDN_DK:(2 * h + 2) * GDN_DK] = kb * eg
        qg_ref[:, sl] = (qh * eg).astype(BF16)
        kd_ref[:, sl] = (kh * jnp.exp(gl_col - gc_col)).astype(BF16)
        egl_b = jnp.broadcast_to(jnp.exp(gl_col), (tt, GDN_DK))
        for ch in range(nch):
            egl_ref[ch, :, sl] = egl_b[ch * c_len:ch * c_len + 1, :]

    nh = GDN_HEADS
    wide = nh * c_len
    wi = lax.broadcasted_iota(jnp.int32, (c_len, wide), 0)
    wj = lax.broadcasted_iota(jnp.int32, (c_len, wide), 1) & (c_len - 1)
    incl = wi >= wj
    strict = wi > wj
    eye = jnp.where(wi == wj, 1.0, 0.0)
    base_mask = (wi >> 1) == (wj >> 1)
    bi = lax.broadcasted_iota(jnp.int32, (wide, wide), 0)
    bj = lax.broadcasted_iota(jnp.int32, (wide, wide), 1)
    same_head = (bi // c_len) == (bj // c_len)
    bi_in = bi & (c_len - 1)
    bj_in = bj & (c_len - 1)
    level_masks = [same_head & ((bi_in >> (s + 1)) == (bj_in >> (s + 1))) & ((bi_in >> s) != (bj_in >> s))
                   for s in range(1, int(math.log2(c_len)))]
    head_of_row = lax.broadcasted_iota(jnp.int32, (wide, LANES), 0) // c_len
    lane_of = lax.broadcasted_iota(jnp.int32, (wide, LANES), 1)
    pick_rows = lane_of == GATE_A + head_of_row
    sel_k = lax.broadcasted_iota(jnp.int32, (LANES, wide), 0)
    sel_h = lax.broadcasted_iota(jnp.int32, (LANES, wide), 1) // c_len
    sel_cols = jnp.where(sel_k == GATE_A + sel_h, 1.0, 0.0).astype(BF16)
    ones_b = jnp.ones((c_len, LANES), BF16)
    key_head = (lax.broadcasted_iota(jnp.int32, (wide, GDN_WIDTH), 0) // c_len
                == lax.broadcasted_iota(jnp.int32, (wide, GDN_WIDTH), 1) // GDN_DK)
    rhs_head = (lax.broadcasted_iota(jnp.int32, (wide, 2 * GDN_WIDTH), 0) // c_len
                == lax.broadcasted_iota(jnp.int32, (wide, 2 * GDN_WIDTH), 1) // (2 * GDN_DK))
    nt = (((1,), (1,)), ((), ()))
    rep = lambda m: jnp.concatenate([m] * nh, axis=0)

    n_par = 2
    par = range(n_par)

    def chunk_group(cg, carry):
        rows = [pl.ds(pl.multiple_of((cg * n_par + c) * c_len, c_len), c_len) for c in par]
        gch = [gc_s[r, :] for r in rows]
        g_i = [sum(jnp.dot(part, sel_cols, preferred_element_type=F32) for part in _split3(g)) for g in gch]
        g_j = [_dot3_nt(ones_b, jnp.where(pick_rows, rep(g), 0.0)) for g in gch]
        decay = [jnp.where(incl, jnp.exp(jnp.where(incl, g_i[c] - g_j[c], 0.0)), 0.0) for c in par]
        k_bd = [jnp.where(key_head, rep(k_s[r, :]), 0.0).astype(BF16) for r in rows]
        qkb = [jnp.concatenate([kb_s[r, :], q_s[r, :]], axis=0).astype(BF16) for r in rows]
        prod = [lax.dot_general(qkb[c], k_bd[c], nt, preferred_element_type=F32) for c in par]
        a = [jnp.where(strict, prod[c][0:c_len] * decay[c], 0.0) for c in par]
        for c in par:
            att_ref[rows[c], :] = (prod[c][c_len:] * decay[c]).astype(BF16)
        a_rep = [rep(m) for m in a]
        x = [eye - jnp.where(base_mask, m, 0.0) for m in a]
        for lvl_mask in level_masks:
            off_bd = [jnp.where(lvl_mask, m, 0.0).astype(BF16) for m in a_rep]
            x_bd = [jnp.where(same_head, rep(m), 0.0).astype(BF16) for m in x]
            t = [jnp.dot(x[c].astype(BF16), off_bd[c], preferred_element_type=F32) for c in par]
            x = [x[c] - jnp.dot(t[c].astype(BF16), x_bd[c], preferred_element_type=F32) for c in par]
        rhs_bd = [jnp.where(rhs_head, rep(rhs_s[r, :]), 0.0).astype(BF16) for r in rows]
        uw = [jnp.dot(x[c].astype(BF16), rhs_bd[c], preferred_element_type=F32) for c in par]
        for c in par:
            for h in range(nh):
                sl = slice(h * GDN_DK, (h + 1) * GDN_DK)
                u_ref[rows[c], sl] = uw[c][:, 2 * h * GDN_DK:(2 * h + 1) * GDN_DK]
                w_ref[rows[c], sl] = uw[c][:, (2 * h + 1) * GDN_DK:(2 * h + 2) * GDN_DK].astype(BF16)
        return carry

    lax.fori_loop(0, nch // n_par, chunk_group, 0)


def _gdn_prep(qkv, gates, conv_w, alog_row, dtb_row, l, batch, t_len):
    n = qkv.shape[0]
    tt = GDN_TT
    nt = t_len // tt
    nch = tt // GDN_CHUNK
    row = lambda w: pl.BlockSpec((tt, w), lambda b, i: (b * nt + i, 0))
    outs = [(GDN_WIDTH, BF16), (GDN_WIDTH, BF16), (GDN_WIDTH, BF16), (GDN_WIDTH, F32)]
    return pl.pallas_call(
        _gdn_prep_body,
        grid=(batch, nt),
        in_specs=[row(GDN_QKV),
                  pl.BlockSpec((HALO, GDN_QKV), lambda b, i: (jnp.maximum((b * nt + i) * (tt // HALO) - 1, 0), 0)),
                  row(LANES), _layer_spec((CONV_W, GDN_QKV), l), _layer_spec((1, LANES), l), _layer_spec((1, LANES), l)],
        out_specs=[row(w) for w, _ in outs] + [
            pl.BlockSpec((tt, GDN_HEADS * GDN_CHUNK), lambda b, i: (b * nt + i, 0)),
            pl.BlockSpec((nch, 1, GDN_WIDTH), lambda b, i: (b * nt + i, 0, 0))],
        out_shape=[jax.ShapeDtypeStruct((n, w), dt) for w, dt in outs] + [
            jax.ShapeDtypeStruct((n, GDN_HEADS * GDN_CHUNK), BF16),
            jax.ShapeDtypeStruct((n // GDN_CHUNK, 1, GDN_WIDTH), F32)],
        scratch_shapes=[pltpu.VMEM((HALO + tt, GDN_QKV), F32), pltpu.VMEM((tt, GDN_WIDTH), F32),
                        pltpu.VMEM((tt, GDN_WIDTH), F32), pltpu.VMEM((tt, GDN_WIDTH), F32),
                        pltpu.VMEM((tt, 2 * GDN_WIDTH), F32), pltpu.VMEM((tt, LANES), F32)],
        compiler_params=_cparams("parallel", "parallel"),
        name="gdn_prep",
    )(qkv, qkv, gates, conv_w, alog_row, dtb_row)


def _gdn_scan_body(qg_ref, kd_ref, w_ref, u_ref, att_ref, egl_ref, z_ref, gn_ref, o_ref, sfin_ref, s_ref):
    c_len = GDN_CHUNK
    tt = qg_ref.shape[0]
    i = pl.program_id(1)

    @pl.when(i == 0)
    def _():
        s_ref[...] = jnp.zeros(s_ref.shape, F32)

    gn = gn_ref[...]
    tn = (((0,), (0,)), ((), ()))

    def chunk(ch, carry):
        r0 = pl.multiple_of(ch * c_len, c_len)
        heads = range(GDN_HEADS)
        sls = [slice(h * GDN_DK, (h + 1) * GDN_DK) for h in heads]
        rows = pl.ds(r0, c_len)
        ss = [s_ref[h] for h in heads]
        sbs = [s.astype(BF16) for s in ss]
        wq = [jnp.dot(jnp.concatenate([w_ref[rows, sls[h]], qg_ref[rows, sls[h]]], axis=0), sbs[h],
                      preferred_element_type=F32) for h in heads]
        vbs = [(u_ref[rows, sls[h]] - wq[h][0:c_len]).astype(BF16) for h in heads]
        av = [jnp.dot(att_ref[rows, h * c_len:(h + 1) * c_len], vbs[h], preferred_element_type=F32) for h in heads]
        kv = [lax.dot_general(kd_ref[rows, sls[h]], vbs[h], tn, preferred_element_type=F32) for h in heads]
        for h in heads:
            s_ref[h] = ss[h] * egl_ref[ch, :, sls[h]] + kv[h]
            o = wq[h][c_len:] + av[h]
            y = o * lax.rsqrt(jnp.mean(o * o, axis=-1, keepdims=True) + RMS_EPS) * gn
            zz = z_ref[rows, sls[h]]
            o_ref[rows, sls[h]] = y * (zz * jax.nn.sigmoid(zz))
        return carry

    lax.fori_loop(0, tt // c_len, chunk, 0)

    @pl.when(i == pl.num_programs(1) - 1)
    def _():
        sfin_ref[...] = s_ref[...]


def _gdn_scan(qg, kd, w, u, att, egl, z, gnorm, l, batch, t_len):
    n = qg.shape[0]
    tt = GDN_TT
    nt = t_len // tt
    nch = tt // GDN_CHUNK
    row = lambda: pl.BlockSpec((tt, GDN_WIDTH), lambda b, i: (b * nt + i, 0))
    return pl.pallas_call(
        _gdn_scan_body,
        grid=(batch, nt),
        in_specs=[row(), row(), row(), row(),
                  pl.BlockSpec((tt, GDN_HEADS * GDN_CHUNK), lambda b, i: (b * nt + i, 0)),
                  pl.BlockSpec((nch, 1, GDN_WIDTH), lambda b, i: (b * nt + i, 0, 0)),
                  row(), _layer_spec((1, GDN_DV), l)],
        out_specs=[row(), pl.BlockSpec((None, GDN_HEADS, GDN_DK, GDN_DV), lambda b, i: (b, 0, 0, 0))],
        out_shape=[jax.ShapeDtypeStruct((n, GDN_WIDTH), F32),
                   jax.ShapeDtypeStruct((batch, GDN_HEADS, GDN_DK, GDN_DV), F32)],
        scratch_shapes=[pltpu.VMEM((GDN_HEADS, GDN_DK, GDN_DV), F32)],
        compiler_params=_cparams("parallel", "arbitrary"),
        name="gdn_scan",
    )(qg, kd, w, u, att, egl, z, gnorm)


def _softmax_rows_extra(s, valid, s_new, valid_new):
    s = jnp.where(valid, s, NEG_INF)
    s_new = jnp.where(valid_new, s_new, NEG_INF)
    m = jnp.maximum(jnp.max(s, axis=1, keepdims=True), s_new)
    e = jnp.where(valid, jnp.exp(s - m), 0.0)
    e_new = jnp.where(valid_new, jnp.exp(s_new - m), 0.0)
    den = jnp.maximum(jnp.sum(e, axis=1, keepdims=True) + e_new, 1e-30)
    return e, e_new, 1.0 / den


NSA_SPS = 4


def _round_robin(gens):
    gens = list(gens)
    while gens:
        alive = []
        for gen in gens:
            try:
                next(gen)
                alive.append(gen)
            except StopIteration:
                pass
        gens = alive


def _nsa_sample_body(pt_ref, *refs, n_pages, **static):
    del pt_ref
    per = 3 * n_pages
    (q_ref, gt_ref, rows_ref, wnew_ref, wbuf_ref, pe_ref, w1_ref, w2_ref, c_ref, sm_ref, sp_ref,
     cov_ref, eexp_ref, o_ref) = refs[NSA_SPS * per:]
    gens = []
    for s in range(NSA_SPS):
        pages = refs[s * per:(s + 1) * per]
        gens.append(_nsa_sample_one(
            pages[0:n_pages], pages[n_pages:2 * n_pages], pages[2 * n_pages:],
            q_ref.at[s], gt_ref.at[s], rows_ref.at[s], wnew_ref.at[s], wbuf_ref.at[s],
            pe_ref, w1_ref, w2_ref, c_ref, sm_ref, sp_ref, cov_ref, eexp_ref, o_ref.at[s],
            n_pages=n_pages, **static))
    _round_robin(gens)


def _nsa_sample_one(kcmp_pages, vcmp_pages, sel_pages, q_ref, gt_ref, rows_ref, wnew_ref, wbuf_ref,
                    pe_ref, w1_ref, w2_ref, c_ref, sm_ref, sp_ref, cov_ref, eexp_ref, o_ref,
                    *, n_pages, page, n_buf, n_cmp, n_sel, topk):
    past = n_pages * page
    qpos = past
    gpp = page // CMP_STRIDE
    g = n_pages * gpp
    nt = (((1,), (1,)), ((), ()))
    nh = NSA_HEADS
    lane_g = lax.broadcasted_iota(jnp.int32, (g, LANES), 1)
    qf = q_ref[...]
    qp = qf.astype(BF16)
    row8 = lax.broadcasted_iota(jnp.int32, (nh, 1), 0)

    outs = []
    for t, pages in enumerate((kcmp_pages, vcmp_pages)):
        load = lambda l, pages=pages: jnp.concatenate(
            [pr[pl.ds(l, gpp, stride=CMP_STRIDE), :] for pr in pages], axis=0)
        outs.append(_compress_heads(_group_features(load, lane_g), pe_ref[t], w1_ref[t], w2_ref[t, 0], w2_ref[t, 1]))
        yield
    kc = _rope(outs[0], c_ref[...], sm_ref[...], sp_ref[...]).astype(BF16)
    vc = outs[1].astype(BF16)
    yield

    c_row = lax.broadcasted_iota(jnp.int32, (1, g), 1)
    cv_row = (c_row * CMP_STRIDE + (CMP_BLOCK - 1) <= qpos) & (c_row < n_cmp)
    s_c = lax.dot_general(qp, kc, nt, preferred_element_type=F32)
    yield
    p_c = _softmax_rows(s_c, cv_row)
    o_cmp = jnp.dot(p_c.astype(BF16), vc, preferred_element_type=F32)
    yield
    g0 = jnp.sum(p_c[0:NSA_GROUP], axis=0, keepdims=True)
    g1 = jnp.sum(p_c[NSA_GROUP:], axis=0, keepdims=True)
    psum8 = jnp.where(row8 == 0, g0, jnp.where(row8 == 1, g1, 0.0))
    imp8 = sum(jnp.dot(part, cov_ref[...], preferred_element_type=F32) for part in _split3(psum8))
    yield
    n_blk = imp8.shape[1]
    ib = lax.broadcasted_iota(jnp.int32, (n_blk, n_blk), 0)
    jb = lax.broadcasted_iota(jnp.int32, (n_blk, n_blk), 1)
    eye_b = jnp.where(ib == jb, 1.0, 0.0).astype(BF16)
    imp_t = _dot3_nt(eye_b, imp8)
    yield
    blk = lax.broadcasted_iota(jnp.int32, (n_blk, 1), 0)
    qblk = qpos // SEL_BLOCK
    forced = (blk == 0) | (blk == qblk) | (blk == qblk - 1)
    imp_t = jnp.where(forced, FORCE_SCORE, jnp.where(blk > qblk, -1.0, imp_t))
    imp_t = jnp.where(blk < n_sel, imp_t, -2.0)
    sel_t = _select_topk(imp_t, blk.astype(F32), topk)
    yield
    sel8 = lax.dot_general(sel_t.astype(BF16), eye_b, (((0,), (0,)), ((), ())), preferred_element_type=F32)
    yield
    head_sel = jnp.where(row8 < NSA_GROUP, sel8[0:1], sel8[1:2])

    key_sel = jnp.dot(head_sel.astype(BF16), eexp_ref[...], preferred_element_type=F32) > 0.5
    s_parts = [lax.dot_general(qp, pr[:, 0:LANES].astype(BF16), nt, preferred_element_type=F32) for pr in sel_pages]
    s_sel = jnp.concatenate(s_parts, axis=1)
    yield
    kpos = lax.broadcasted_iota(jnp.int32, (1, past), 1)
    new_row = rows_ref[...]
    k_new = new_row[:, 2 * LANES:3 * LANES].astype(BF16).astype(F32)
    v_new = new_row[:, 3 * LANES:4 * LANES].astype(BF16).astype(F32)
    s_new = jnp.sum(qf * k_new, axis=1, keepdims=True)
    new_sel = head_sel[:, qblk:qblk + 1] > 0.5
    e, e_new, inv = _softmax_rows_extra(s_sel, key_sel & (kpos <= qpos), s_new, new_sel)
    acc = e_new * v_new
    for p_i, pr in enumerate(sel_pages):
        acc = acc + jnp.dot(e[:, p_i * page:(p_i + 1) * page].astype(BF16), pr[:, LANES:2 * LANES].astype(BF16),
                            preferred_element_type=F32)
    o_sel = acc * inv
    yield

    kw = wbuf_ref[:, 0:LANES].astype(BF16)
    vw = wbuf_ref[:, LANES:2 * LANES].astype(BF16)
    s_w = lax.dot_general(qp, kw, nt, preferred_element_type=F32)
    yield
    kpos_w = (past - n_buf) + lax.broadcasted_iota(jnp.int32, (1, n_buf), 1)
    dw = qpos - kpos_w
    w_new = wnew_ref[...]
    kw_new = w_new[:, 0:LANES].astype(BF16).astype(F32)
    vw_new = w_new[:, LANES:2 * LANES].astype(BF16).astype(F32)
    sw_new = jnp.sum(qf * kw_new, axis=1, keepdims=True)
    ew, ew_new, inv_w = _softmax_rows_extra(s_w, (dw >= 0) & (dw < WINDOW) & (kpos_w >= 0), sw_new, row8 >= 0)
    o_win = (jnp.dot(ew.astype(BF16), vw, preferred_element_type=F32) + ew_new * vw_new) * inv_w
    yield

    gates = jax.nn.sigmoid(gt_ref[...])
    lane8 = lax.broadcasted_iota(jnp.int32, (nh, LANES), 1)
    gcol = lambda j: jnp.sum(jnp.where(lane8 == 3 * row8 + j, gates, 0.0), axis=1, keepdims=True)
    o8 = gcol(0) * o_cmp + gcol(1) * o_sel + gcol(2) * o_win
    lane1 = lax.broadcasted_iota(jnp.int32, (1, LANES), 1)
    for pair in range(nh // 2):
        a = o8[2 * pair:2 * pair + 1]
        b = o8[2 * pair + 1:2 * pair + 2]
        if 2 * pair < NSA_GROUP:
            both = jnp.where(lane1 < HEAD_DIM, a, pltpu.roll(b, HEAD_DIM, 1))
        else:
            both = jnp.where(lane1 < HEAD_DIM, pltpu.roll(a, HEAD_DIM, 1), b)
        o_ref[:, pair * LANES:(pair + 1) * LANES] = both


def _nsa_sample(page_table, cache_cmp, cache_sel, q3, gates3, rows3, wnew3, win_cache, pe8, w1c, w2p, tabs_c, cov_s,
                eexp, l, n_cmp, n_sel, topk, sel_col):
    bs, n_pages = page_table.shape
    page = cache_cmp.shape[2]
    n_buf = win_cache.shape[2]
    g = n_pages * page // CMP_STRIDE
    sps = NSA_SPS
    page_spec = lambda s, p, w, col: pl.BlockSpec(
        (None, None, page, w), lambda b, pt, s=s, p=p: (l, pt[b * sps + s, p], 0, col))
    per_b = lambda *shape: pl.BlockSpec((sps,) + shape, lambda b, pt: (b,) + (0,) * len(shape))
    cst = lambda shape: pl.BlockSpec(shape, lambda b, pt: (0,) * len(shape))
    lyr = lambda shape: pl.BlockSpec((None,) + shape, lambda b, pt: (l,) + (0,) * len(shape))
    in_specs, args = [], []
    for s in range(sps):
        in_specs += ([page_spec(s, p, LANES, 0) for p in range(n_pages)]
                     + [page_spec(s, p, LANES, 1) for p in range(n_pages)]
                     + [page_spec(s, p, 2 * LANES, sel_col) for p in range(n_pages)])
        args += [cache_cmp] * (2 * n_pages) + [cache_sel] * n_pages
    in_specs += [per_b(NSA_HEADS, LANES), per_b(1, LANES), per_b(1, 4 * LANES), per_b(1, 2 * LANES),
                 pl.BlockSpec((None, sps, n_buf, 2 * LANES), lambda b, pt: (l, b, 0, 0)),
                 lyr((2, 8, CMP_STRIDE * HEAD_DIM)), lyr((2, CMP_STRIDE * HEAD_DIM, 2 * CMP_HIDDEN)),
                 lyr((2, 2, CMP_HIDDEN, LANES)), cst((g, LANES)), cst((g, LANES)), cst((g, LANES)),
                 cst(cov_s.shape), cst(eexp.shape)]
    args += [q3, gates3, rows3, wnew3, win_cache, pe8, w1c, w2p, *tabs_c, cov_s, eexp]
    grid_spec = pltpu.PrefetchScalarGridSpec(
        num_scalar_prefetch=1, grid=(bs // sps,), in_specs=in_specs,
        out_specs=pl.BlockSpec((sps, 1, NSA_WIDTH), lambda b, pt: (b, 0, 0)))
    return pl.pallas_call(
        functools.partial(_nsa_sample_body, n_pages=n_pages, page=page, n_buf=n_buf, n_cmp=n_cmp, n_sel=n_sel,
                          topk=topk),
        grid_spec=grid_spec,
        out_shape=jax.ShapeDtypeStruct((bs, 1, NSA_WIDTH), F32),
        compiler_params=_cparams("parallel"),
        name="nsa_sample",
    )(page_table, *args)


GDN_SB = 8


def _transpose3(x, eye):
    return _dot3_nt(eye, x)


def _gdn_sample_body(x_ref, cs_ref, gt_ref, z_ref, s_ref, cw_ref, alog_ref, dtb_ref, gn_ref, o_ref, so_ref):
    sb = x_ref.shape[0]
    qk_w = GDN_HEADS * GDN_DK
    c = x_ref[...] * cw_ref[CONV_W - 1:CONV_W, :]
    for j in range(CONV_W - 1):
        c = c + cs_ref[:, j * GDN_QKV:(j + 1) * GDN_QKV] * cw_ref[j:j + 1, :]
    c = c * jax.nn.sigmoid(c)
    gt = gt_ref[...]
    g_full = -jnp.exp(alog_ref[...]) * _softplus(gt + dtb_ref[...])
    eg_full = jnp.exp(g_full)
    beta_full = jax.nn.sigmoid(gt)
    ii = lax.broadcasted_iota(jnp.int32, (LANES, LANES), 0)
    jj = lax.broadcasted_iota(jnp.int32, (LANES, LANES), 1)
    eye = jnp.where(ii == jj, 1.0, 0.0).astype(BF16)
    gn = gn_ref[...]
    for h in range(GDN_HEADS):
        sl = slice(h * GDN_DK, (h + 1) * GDN_DK)
        q = _l2n(c[:, h * GDN_DK:(h + 1) * GDN_DK]) * (GDN_DK ** -0.5)
        k = _l2n(c[:, qk_w + h * GDN_DK:qk_w + (h + 1) * GDN_DK])
        v = c[:, 2 * qk_w + h * GDN_DV:2 * qk_w + (h + 1) * GDN_DV]
        eg = jnp.broadcast_to(eg_full[:, GATE_A + h:GATE_A + h + 1], (sb, GDN_DK))
        beta = jnp.broadcast_to(beta_full[:, GATE_B + h:GATE_B + h + 1], (sb, GDN_DK))
        att = jnp.sum(q * k, axis=1, keepdims=True)
        k_t = _transpose3(k, eye)
        w_t = _transpose3(k * beta * eg, eye)
        qg_t = _transpose3(q * eg, eye)
        u = v * beta
        zz = z_ref[:, sl]
        for i in range(sb):
            s = s_ref[i, h]
            v_new = u[i:i + 1] - jnp.sum(w_t[:, i:i + 1] * s, axis=0, keepdims=True)
            o = jnp.sum(qg_t[:, i:i + 1] * s, axis=0, keepdims=True) + att[i:i + 1] * v_new
            so_ref[i, h] = s * eg[i:i + 1] + k_t[:, i:i + 1] * v_new
            y = o * lax.rsqrt(jnp.mean(o * o, axis=-1, keepdims=True) + RMS_EPS) * gn
            zi = zz[i:i + 1]
            o_ref[i:i + 1, sl] = y * (zi * jax.nn.sigmoid(zi))


def _gdn_sample(qkv, conv_state, gates, z, s_state, conv_w, alog_row, dtb_row, gnorm, l):
    bs = qkv.shape[0]
    sb = GDN_SB
    row = lambda w: pl.BlockSpec((sb, w), lambda i: (i, 0))
    st = pl.BlockSpec((None, sb, GDN_HEADS, GDN_DK, GDN_DV), lambda i: (l, i, 0, 0, 0))
    return pl.pallas_call(
        _gdn_sample_body,
        grid=(bs // sb,),
        in_specs=[row(GDN_QKV), pl.BlockSpec((None, sb, (CONV_W - 1) * GDN_QKV), lambda i: (l, i, 0)),
                  row(LANES), row(GDN_WIDTH), st,
                  _layer_spec((CONV_W, GDN_QKV), l), _layer_spec((1, LANES), l), _layer_spec((1, LANES), l),
                  _layer_spec((1, GDN_DV), l)],
        out_specs=[row(GDN_WIDTH), pl.BlockSpec((sb, GDN_HEADS, GDN_DK, GDN_DV), lambda i: (i, 0, 0, 0))],
        out_shape=[jax.ShapeDtypeStruct((bs, GDN_WIDTH), F32),
                   jax.ShapeDtypeStruct((bs, GDN_HEADS, GDN_DK, GDN_DV), F32)],
        compiler_params=_cparams("parallel"),
        name="gdn_sample",
    )(qkv, conv_state, gates, z, s_state, conv_w, alog_row, dtb_row, gnorm)


def _mem_sample_body(q_ref, kv_ref, o_ref):
    nt = (((1,), (1,)), ((), ()))
    q = q_ref[...] * (MEM_HD ** -0.5)
    row = lax.broadcasted_iota(jnp.int32, (8, MEM_WIDTH), 0)
    lane = lax.broadcasted_iota(jnp.int32, (8, MEM_WIDTH), 1)
    own = (lane // MEM_HD) == row
    qm = jnp.where(own, q, 0.0).astype(BF16)
    kv = kv_ref[...].astype(BF16)
    s = lax.dot_general(qm, kv[:, 0:MEM_WIDTH], nt, preferred_element_type=F32)
    e = jnp.exp(s - jnp.max(s, axis=-1, keepdims=True))
    p = e / jnp.sum(e, axis=-1, keepdims=True)
    o = jnp.dot(p.astype(BF16), kv[:, MEM_WIDTH:], preferred_element_type=F32)
    o_ref[...] = jnp.sum(jnp.where(own, o, 0.0), axis=0, keepdims=True)


def _mem_sample(q3, mem_cache, l):
    bs = q3.shape[0]
    m = mem_cache.shape[2]
    return pl.pallas_call(
        _mem_sample_body,
        grid=(bs,),
        in_specs=[pl.BlockSpec((None, 1, MEM_WIDTH), lambda b: (b, 0, 0)),
                  pl.BlockSpec((None, None, m, 2 * MEM_WIDTH), lambda b: (l, b, 0, 0))],
        out_specs=pl.BlockSpec((None, 1, MEM_WIDTH), lambda b: (b, 0, 0)),
        out_shape=jax.ShapeDtypeStruct((bs, 1, MEM_WIDTH), F32),
        compiler_params=_cparams("parallel"),
        name="mem_sample",
    )(q3, mem_cache)


def _rope_tables(pos):
    half = ROPE_DIM // 2
    inv = ROPE_THETA ** (-2.0 * jnp.arange(half, dtype=F32) / ROPE_DIM)
    ang = pos.astype(F32)[:, None] * inv[None, :]
    cos, sin = jnp.cos(ang), jnp.sin(ang)
    n = pos.shape[0]
    one = jnp.ones((n, HEAD_DIM - ROPE_DIM), F32)
    zero = jnp.zeros((n, HEAD_DIM - ROPE_DIM), F32)
    zh = jnp.zeros((n, half), F32)
    c = jnp.concatenate([cos, cos, one], axis=1)
    sm = jnp.concatenate([-sin, zh, zero], axis=1)
    sp = jnp.concatenate([zh, sin, zero], axis=1)
    return tuple(jnp.concatenate([t, t], axis=1) for t in (c, sm, sp))


def _block_onehot(pos):
    e = (pos[:, None] // SEL_BLOCK == jnp.arange(HEAD_DIM)[None, :]).astype(F32)
    return jnp.concatenate([e, e], axis=1)


def _rope_jnp(x, pos):
    half = ROPE_DIM // 2
    inv = ROPE_THETA ** (-2.0 * jnp.arange(half, dtype=F32) / ROPE_DIM)
    ang = pos.astype(F32)[:, None] * inv[None, :]
    cos = jnp.cos(ang)[None, :, None, :]
    sin = jnp.sin(ang)[None, :, None, :]
    x1 = x[..., :half]
    x2 = x[..., half:ROPE_DIM]
    return jnp.concatenate([x1 * cos - x2 * sin, x2 * cos + x1 * sin, x[..., ROPE_DIM:]], axis=-1)


def _masked_softmax(s, mask):
    s = jnp.where(mask, s, NEG_INF)
    m = jnp.max(s, axis=-1, keepdims=True)
    e = jnp.where(mask, jnp.exp(s - m), 0.0)
    return e / jnp.maximum(jnp.sum(e, axis=-1, keepdims=True), 1e-30)


def _compress_jnp(k, pe, w1, w2):
    b, t, hk, dh = k.shape
    n_cmp = (t - CMP_BLOCK) // CMP_STRIDE + 1
    start = jnp.arange(n_cmp) * CMP_STRIDE
    idx = start[:, None] + jnp.arange(CMP_BLOCK)[None, :]
    blk = k[:, idx] + pe[None, None, :, None, :]
    blk = jnp.swapaxes(blk, 2, 3).reshape(b, n_cmp, hk, CMP_BLOCK * dh)
    return jax.nn.gelu(blk @ w1) @ w2, start + CMP_BLOCK - 1


def _block_coverage(c_start, n_sel):
    b_start = jnp.arange(n_sel) * SEL_BLOCK
    lo = jnp.maximum(c_start[:, None], b_start[None, :])
    hi = jnp.minimum(c_start[:, None] + CMP_BLOCK, b_start[None, :] + SEL_BLOCK)
    return jnp.clip(hi - lo, 0, None).astype(F32) / CMP_BLOCK


def _sample_nsa_jnp(q, gate_logits, rows_new, win_new, nsa_past, win_buf, w_buf, cmp_pe, w_cmp1, w_cmp2, topk):
    b, t = q.shape[:2]
    pos0 = nsa_past.shape[1]
    qpos = pos0 + jnp.arange(t)
    q = q.reshape(b, t, NSA_KV_HEADS, NSA_GROUP, HEAD_DIM)
    rows = jnp.concatenate([nsa_past, rows_new], axis=1)
    t_kv = rows.shape[1]
    kc, c_end = _compress_jnp(rows[:, :, 0], cmp_pe[0], w_cmp1[0], w_cmp2[0])
    vc, _ = _compress_jnp(rows[:, :, 1], cmp_pe[1], w_cmp1[1], w_cmp2[1])
    kc = _rope_jnp(kc, c_end)
    n_sel = -(-t_kv // SEL_BLOCK)
    cov = _block_coverage(c_end - CMP_BLOCK + 1, n_sel)
    sel = jnp.pad(rows[:, :, 2:], ((0, 0), (0, n_sel * SEL_BLOCK - t_kv), (0, 0), (0, 0), (0, 0)))
    sel = sel.reshape(b, n_sel, SEL_BLOCK, 2, NSA_KV_HEADS, HEAD_DIM).transpose(3, 0, 4, 1, 2, 5)
    k_blk, v_blk = sel[0], sel[1]
    win_all = jnp.concatenate([win_buf, win_new], axis=1)
    n_buf = win_buf.shape[1]
    gates = jax.nn.sigmoid(gate_logits).reshape(b, t, NSA_KV_HEADS, NSA_GROUP, 3)
    s = jnp.einsum('bqhgd,bchd->bhgqc', q, kc) * ATTN_SCALE
    p = _masked_softmax(s, c_end[None, :] <= qpos[:, None])
    o_cmp = jnp.einsum('bhgqc,bchd->bqhgd', p, vc)
    imp = jnp.einsum('bhgqc,cn->bhqn', p, cov)
    blk = jnp.arange(n_sel)[None, :]
    q_blk = (qpos // SEL_BLOCK)[:, None]
    forced = (blk == 0) | (blk == q_blk) | (blk == q_blk - 1)
    imp = jnp.where(forced, FORCE_SCORE, jnp.where(blk > q_blk, -1.0, imp))
    _, idx = lax.top_k(imp, min(topk, n_sel))
    bi = jnp.arange(b)[:, None, None, None]
    hi = jnp.arange(NSA_KV_HEADS)[None, :, None, None]
    kg = k_blk[bi, hi, idx]
    vg = v_blk[bi, hi, idx]
    k_pos = idx[..., None] * SEL_BLOCK + jnp.arange(SEL_BLOCK)
    mask = (k_pos <= qpos[None, None, :, None, None])[:, :, None]
    s2 = jnp.einsum('bqhgd,bhqksd->bhgqks', q, kg) * ATTN_SCALE
    shp = s2.shape
    p2 = _masked_softmax(s2.reshape(shp[:-2] + (-1,)), mask.reshape(mask.shape[:-2] + (-1,))).reshape(shp)
    o_sel = jnp.einsum('bhgqks,bhqksd->bqhgd', p2, vg)
    kw_pos = pos0 - n_buf + jnp.arange(n_buf + t)
    sw = jnp.einsum('bqhgd,bkhd->bhgqk', q, win_all[:, :, 0]) * ATTN_SCALE
    diff = qpos[:, None] - kw_pos[None, :]
    pw = _masked_softmax(sw, (diff >= 0) & (diff < WINDOW) & (kw_pos[None, :] >= 0))
    o_win = jnp.einsum('bhgqk,bkhd->bqhgd', pw, win_all[:, :, 1])
    o = gates[..., 0:1] * o_cmp + gates[..., 1:2] * o_sel + gates[..., 2:3] * o_win
    return o.reshape(b, t, NSA_WIDTH), win_all[:, win_all.shape[1] - w_buf:]


def _to_chunks(a, c, pad):
    a = jnp.moveaxis(a, 1, 2)
    if pad:
        a = jnp.pad(a, [(0, 0), (0, 0), (0, pad)] + [(0, 0)] * (a.ndim - 3))
    b, h, tp = a.shape[:3]
    a = a.reshape((b, h, tp // c, c) + a.shape[3:])
    return jnp.moveaxis(a, 2, 0)


def _chunk_gated_delta_jnp(q, k, v, g, beta, s0):
    b, t, h, _ = q.shape
    c = min(GDN_CHUNK, t)
    pad = (-t) % c
    qc, kc, vc = _to_chunks(q, c, pad), _to_chunks(k, c, pad), _to_chunks(v, c, pad)
    gc = jnp.cumsum(_to_chunks(g, c, pad), axis=-1)
    bc = _to_chunks(beta, c, pad)
    ii = jnp.arange(c)
    incl = ii[:, None] >= ii[None, :]
    strict = ii[:, None] > ii[None, :]
    diff = gc[..., :, None] - gc[..., None, :]
    decay = jnp.where(incl, jnp.exp(jnp.where(incl, diff, 0.0)), 0.0)
    kb = kc * bc[..., None]
    a = jnp.where(strict, jnp.einsum('...id,...jd->...ij', kb, kc) * decay, 0.0)
    eye = jnp.eye(c, dtype=F32)
    tm = lax.linalg.triangular_solve(eye + a, jnp.broadcast_to(eye, a.shape), left_side=True, lower=True)
    u = jnp.einsum('...ij,...jd->...id', tm, vc * bc[..., None])
    w = jnp.einsum('...ij,...jd->...id', tm, kb * jnp.exp(gc)[..., None])

    def step(s, inp):
        qi, ki, ui, wi, gi, di = inp
        v_new = ui - jnp.einsum('bhck,bhkv->bhcv', wi, s)
        att = jnp.einsum('bhik,bhjk->bhij', qi, ki) * di
        o = jnp.einsum('bhck,bhkv->bhcv', qi * jnp.exp(gi)[..., None], s) + jnp.einsum('bhij,bhjv->bhiv', att, v_new)
        gl = gi[..., -1]
        s = s * jnp.exp(gl)[..., None, None] + jnp.einsum(
            'bhck,bhcv->bhkv', ki * jnp.exp(gl[..., None] - gi)[..., None], v_new)
        return s, o

    s, o = lax.scan(step, s0, (qc, kc, u, w, gc, decay))
    o = jnp.moveaxis(o, 0, 2).reshape(b, h, -1, o.shape[-1])[:, :, :t]
    return jnp.moveaxis(o, 1, 2), s


def _gdn_jnp(qkv, a, bb, z, conv_buf, state, conv_w, a_log, dt_bias, norm_w):
    b, t, _ = qkv.shape
    xp = jnp.concatenate([conv_buf, qkv], axis=1)
    c = xp[:, 0:t] * conv_w[0]
    for j in range(1, CONV_W):
        c = c + xp[:, j:j + t] * conv_w[j]
    c = jax.nn.silu(c)
    new_buf = xp[:, t:]
    qk_w = GDN_HEADS * GDN_DK
    l2 = lambda x: x * lax.rsqrt(jnp.sum(x * x, axis=-1, keepdims=True) + 1e-6)
    q = l2(c[..., :qk_w].reshape(b, t, GDN_HEADS, GDN_DK)) * GDN_DK ** -0.5
    k = l2(c[..., qk_w:2 * qk_w].reshape(b, t, GDN_HEADS, GDN_DK))
    v = c[..., 2 * qk_w:].reshape(b, t, GDN_HEADS, GDN_DV)
    g = -jnp.exp(a_log) * jax.nn.softplus(a + dt_bias)
    beta = jax.nn.sigmoid(bb)
    o, new_state = _chunk_gated_delta_jnp(q, k, v, g, beta, state)
    o = o * lax.rsqrt(jnp.mean(o * o, axis=-1, keepdims=True) + RMS_EPS) * norm_w
    o = o * jax.nn.silu(z).reshape(b, t, GDN_HEADS, GDN_DV)
    return o.reshape(b, t, GDN_WIDTH), new_buf, new_state


def _mem_attend_sample_jnp(x, gains, mem_kv, w_q, w_o):
    xf = x
    h = xf * lax.rsqrt(jnp.mean(xf * xf, axis=-1, keepdims=True) + RMS_EPS) * gains[4]
    q = (h @ w_q).reshape(-1, MEM_HEADS, MEM_HD)
    s = jnp.einsum('bhd,bmhd->bhm', q, mem_kv[:, :, 0]) * MEM_HD ** -0.5
    p = jax.nn.softmax(s, axis=-1)
    o = jnp.einsum('bhm,bmhd->bhd', p, mem_kv[:, :, 1]).reshape(-1, MEM_WIDTH)
    y = o @ w_o
    return x + y * lax.rsqrt(jnp.mean(y * y, axis=-1, keepdims=True) + RMS_EPS) * gains[5]


def _pack_w_in(w_in):
    nl, d, _ = w_in.shape
    o_q, o_kv, o_g = NSA_WIDTH, NSA_WIDTH + KV6_WIDTH, NSA_WIDTH + KV6_WIDTH + 3 * NSA_HEADS
    o_qkv = o_g
    o_a = o_qkv + GDN_QKV
    o_b = o_a + GDN_HEADS
    o_z = o_b + GDN_HEADS
    wq = w_in[:, :, :o_q].reshape(nl, d, NSA_HEADS, HEAD_DIM)
    zq = jnp.zeros_like(wq)
    first = jnp.concatenate([wq, zq], axis=-1)
    second = jnp.concatenate([zq, wq], axis=-1)
    kv_of_head = (jnp.arange(NSA_HEADS) // NSA_GROUP)[None, None, :, None]
    wq_pad = jnp.where(kv_of_head == 0, first, second).reshape(nl, d, QPAD_WIDTH)
    gate_grp = jnp.concatenate([w_in[:, :, o_kv:o_g], w_in[:, :, o_a:o_z],
                                jnp.zeros((nl, d, LANES - 3 * NSA_HEADS - 2 * GDN_HEADS), w_in.dtype)], axis=-1)
    packed = jnp.concatenate([wq_pad, w_in[:, :, o_q:o_kv], w_in[:, :, o_qkv:o_a], w_in[:, :, o_z:], gate_grp], axis=-1)
    return packed.astype(BF16)


def kernel(x_prompt, x_sample, cache_nsa_kv, cache_win_kv, state_gdn_S, state_gdn_conv, cache_mem_kv, page_table, mem_prompt, norm_gains, mem_norm, w_ffn_gu, w_ffn_down, w_in, w_out, cmp_pe, w_cmp1, w_cmp2, gdn_conv, gdn_A_log, gdn_dt_bias, gdn_norm, w_mem_q, w_mem_kv, w_mem_o):
    bp, t_len, d = x_prompt.shape
    bs = x_sample.shape[0]
    depth = w_in.shape[0]
    w_buf = cache_win_kv.shape[2]
    n_pages = page_table.shape[1]
    page = cache_nsa_kv.shape[2]
    past_len = n_pages * page
    n_mem = mem_prompt.shape[1]
    np_tok = bp * t_len

    gains = norm_gains.reshape(depth, 8, 1, d)
    wgu = w_ffn_gu.astype(BF16)
    wd = w_ffn_down.astype(BF16)
    w_in_p = _pack_w_in(w_in)
    w_out_b = w_out.astype(BF16)
    w_mq = w_mem_q.astype(BF16)
    w_mkv = w_mem_kv.astype(BF16)
    w_mo = w_mem_o.astype(BF16)
    mem_gain = mem_norm.reshape(depth, 1, d)

    pos_p = jnp.arange(t_len)
    tabs_p = _rope_tables(pos_p) + (_block_onehot(pos_p),)
    pos_s = jnp.full((bs,), past_len, jnp.int32)
    tabs_s = _rope_tables(pos_s) + (jnp.zeros((bs, LANES), F32),)

    half_k = CMP_STRIDE * HEAD_DIM
    pe8 = jnp.pad(cmp_pe.reshape(depth, 2, CMP_BLOCK // CMP_STRIDE, half_k), ((0, 0), (0, 0), (0, 6), (0, 0)))
    w1c = jnp.concatenate([w_cmp1[:, :, :half_k], w_cmp1[:, :, half_k:]], axis=-1).astype(BF16)
    z2 = jnp.zeros_like(w_cmp2)
    w2p = jnp.stack([jnp.concatenate([w_cmp2, z2], axis=-1), jnp.concatenate([z2, w_cmp2], axis=-1)], axis=2).astype(BF16)
    tabs_c = _rope_tables(jnp.arange(t_len // CMP_STRIDE) * CMP_STRIDE + (CMP_BLOCK - 1))
    lane_pad = lambda v: jnp.pad(v, ((0, 0), (GATE_A, LANES - GATE_A - GDN_HEADS))).reshape(depth, 1, LANES)
    alog_row = lane_pad(gdn_A_log)
    dtb_row = lane_pad(gdn_dt_bias)
    gnorm = gdn_norm.reshape(depth, 1, GDN_DV)

    n_cmp = (t_len - CMP_BLOCK) // CMP_STRIDE + 1
    n_cmp_pad = t_len // CMP_STRIDE
    n_sel = -(-t_len // SEL_BLOCK)
    c_start = jnp.arange(n_cmp) * CMP_STRIDE
    assert n_sel <= HEAD_DIM and t_len >= WINDOW + NSA_TQ and t_len % NSA_TK == 0
    covt = jnp.pad(_block_coverage(c_start, n_sel).T, ((0, HEAD_DIM - n_sel), (0, n_cmp_pad - n_cmp))).astype(BF16)
    topk = min(SEL_TOPK, n_sel)

    assert x_sample.shape[1] == 1 and past_len % SEL_BLOCK == 0 and bs % GDN_SB == 0
    t_kv = past_len + 1
    n_cmp_s = (t_kv - CMP_BLOCK) // CMP_STRIDE + 1
    assert (n_cmp_s - 1) * CMP_STRIDE + CMP_BLOCK <= past_len
    g_s = past_len // CMP_STRIDE
    n_sel_s = -(-t_kv // SEL_BLOCK)
    assert n_sel_s <= HEAD_DIM
    topk_s = min(SEL_TOPK, n_sel_s)
    cov_s = jnp.pad(_block_coverage(jnp.arange(n_cmp_s) * CMP_STRIDE, n_sel_s),
                    ((0, g_s - n_cmp_s), (0, HEAD_DIM - n_sel_s))).astype(BF16)
    eexp = (jnp.arange(past_len)[None, :] // SEL_BLOCK == jnp.arange(HEAD_DIM)[:, None]).astype(BF16)
    tabs_cs = _rope_tables(jnp.arange(g_s) * CMP_STRIDE + (CMP_BLOCK - 1))
    cache_cmp = cache_sel = cache_nsa_kv.reshape(depth, cache_nsa_kv.shape[1], page, 4 * LANES)
    win_cache = cache_win_kv.reshape(depth, bs, w_buf, 2 * LANES)
    conv_cache = state_gdn_conv.reshape(depth, bs, (CONV_W - 1) * GDN_QKV)
    mem_cache = cache_mem_kv.reshape(depth, bs, n_mem, 2 * MEM_WIDTH)

    tm_p = 512
    yp = x_prompt.reshape(np_tok, d)
    ys = x_sample.reshape(bs, d)
    mem_flat = mem_prompt.reshape(bp * n_mem, d)
    outs = [[] for _ in range(9)]
    for l in range(depth):
        mem_kv_p = _normmm(mem_flat, mem_gain, (l,), w_mkv, l, n_mem)
        yp = _ffn(yp, gains, wgu, wd, l, 0, tm_p)
        qpad, rows, win, kaug, vsel, kvwin, gates, qkv, z = _inproj(yp, gains, w_in_p, tabs_p, l, tm_p)
        rows5 = rows.reshape(bp, t_len, 4, NSA_KV_HEADS, HEAD_DIM)
        kc, vc = _compress_prompt(rows, pe8, w1c, w2p, tabs_c, l, bp, t_len)
        o_nsa = _nsa_prompt(qpad, gates, kaug, vsel, kvwin, kc, vc, covt, bp, t_len, n_cmp, n_sel, topk)
        qg, kd, w_g, u_g, att, egl = _gdn_prep(qkv, gates, gdn_conv, alog_row, dtb_row, l, bp, t_len)
        o_gdn, s_p = _gdn_scan(qg, kd, w_g, u_g, att, egl, z, gnorm, l, bp, t_len)
        conv_p = qkv.reshape(bp, t_len, GDN_QKV)[:, t_len - (CONV_W - 1):]
        yp = _outproj(yp, o_nsa, o_gdn, w_out_b, gains, l, tm_p)
        yp = _mem_block(yp, gains, w_mq, w_mo, mem_kv_p, l, tm_p, t_len // tm_p)
        yp = _ffn(yp, gains, wgu, wd, l, 1, tm_p)
        win5 = win.reshape(bp, t_len, 2, NSA_KV_HEADS, HEAD_DIM)
        win_p = win5[:, t_len - w_buf:] if t_len >= w_buf else jnp.pad(
            win5, ((0, 0), (w_buf - t_len, 0), (0, 0), (0, 0), (0, 0)))

        ys = _ffn(ys, gains, wgu, wd, l, 0, bs)
        qpad_s, rows_s, win_s, _, _, _, gates_s, qkv_s, z_s = _inproj(ys, gains, w_in_p, tabs_s, l, bs)
        o_nsa_s = _nsa_sample(
            page_table, cache_cmp, cache_sel, qpad_s.astype(F32).reshape(bs, NSA_HEADS, LANES), gates_s.reshape(bs, 1, LANES),
            rows_s.reshape(bs, 1, 4 * LANES), win_s.reshape(bs, 1, 2 * LANES), win_cache, pe8, w1c, w2p, tabs_cs,
            cov_s, eexp, l, n_cmp_s, n_sel_s, topk_s, 1).reshape(bs, NSA_WIDTH)
        o_gdn_s, s_s = _gdn_sample(qkv_s, conv_cache, gates_s, z_s, state_gdn_S, gdn_conv, alog_row, dtb_row, gnorm, l)
        ys = _outproj(ys, o_nsa_s, o_gdn_s, w_out_b, gains, l, bs)
        q_m = _normmm(ys, gains, (l, 4), w_mq, l, bs)
        o_m = _mem_sample(q_m.reshape(bs, 1, MEM_WIDTH), mem_cache, l).reshape(bs, MEM_WIDTH)
        ys = _outproj(ys, o_m[:, :MEM_WIDTH // 2], o_m[:, MEM_WIDTH // 2:], w_mo, gains, l, bs, gain_idx=5)
        ys = _ffn(ys, gains, wgu, wd, l, 1, bs)
        win_all = jnp.concatenate([cache_win_kv[l], win_s.reshape(bs, 1, 2, NSA_KV_HEADS, HEAD_DIM)], axis=1)
        wbuf_s = win_all[:, win_all.shape[1] - w_buf:]
        conv_s = jnp.concatenate([state_gdn_conv[l], qkv_s[:, None, :]], axis=1)[:, 1:]

        for lst, val in zip(outs, (rows5, win_p, s_p, conv_p, mem_kv_p.reshape(bp, n_mem, 2, MEM_HEADS, MEM_HD),
                                   rows_s.reshape(bs, 1, 4, NSA_KV_HEADS, HEAD_DIM), wbuf_s, s_s, conv_s)):
            lst.append(val)
    return (yp.reshape(bp, t_len, d), ys.reshape(bs, 1, d)) + tuple(jnp.stack(v) for v in outs)
```

```python
import functools
import math

import numpy as np
import jax
import jax.numpy as jnp
from jax import lax
from jax.experimental import pallas as pl
from jax.experimental.pallas import tpu as pltpu

F32 = jnp.float32
BF16 = jnp.bfloat16

HEAD_DIM = 64
NSA_HEADS = 8
NSA_KV_HEADS = 2
NSA_GROUP = NSA_HEADS // NSA_KV_HEADS
NSA_WIDTH = NSA_HEADS * HEAD_DIM
ROPE_DIM = HEAD_DIM // 4
ROPE_THETA = 500000.0
ATTN_SCALE = HEAD_DIM ** -0.5
CMP_BLOCK = 32
CMP_STRIDE = 16
CMP_HIDDEN = 2 * HEAD_DIM
SEL_BLOCK = 64
SEL_TOPK = 16
WINDOW = 512
FORCE_SCORE = 1.0e4
GDN_DK = 128
GDN_DV = 128
GDN_HEADS = 4
GDN_WIDTH = GDN_HEADS * GDN_DV
GDN_QKV = GDN_HEADS * (2 * GDN_DK + GDN_DV)
CONV_W = 4
GDN_CHUNK = 64
MEM_HEADS = 4
MEM_HD = 64
MEM_WIDTH = MEM_HEADS * MEM_HD
RMS_EPS = 1e-6
NEG_INF = -1.0e30
KV6_WIDTH = 6 * NSA_KV_HEADS * HEAD_DIM

LANES = 128
VMEM_LIMIT_BYTES = 56 * 1024 * 1024

QPAD_WIDTH = NSA_HEADS * LANES
COL_Q = 0
COL_KV = COL_Q + QPAD_WIDTH
COL_QKV = COL_KV + KV6_WIDTH
COL_Z = COL_QKV + GDN_QKV
COL_GATE = COL_Z + GDN_WIDTH
IN_PACKED = COL_GATE + LANES
GATE_A = 3 * NSA_HEADS
GATE_B = GATE_A + GDN_HEADS


def _cparams(*sem):
    return pltpu.CompilerParams(dimension_semantics=sem, vmem_limit_bytes=VMEM_LIMIT_BYTES)


def _rms(x, w):
    return x * lax.rsqrt(jnp.mean(x * x, axis=-1, keepdims=True) + RMS_EPS) * w


def _const_spec(shape):
    nd = len(shape)
    return pl.BlockSpec(shape, lambda *_: (0,) * nd)


def _layer_spec(shape, *lead):
    nlead = len(lead)
    nd = len(shape)
    return pl.BlockSpec((None,) * nlead + tuple(shape), lambda *_: tuple(lead) + (0,) * nd)


FFN_CHUNK = 256


def _ffn_body(x_ref, g0_ref, g1_ref, wgu_ref, wd_ref, o_ref, acc_ref, *, d_ff):
    x = x_ref[...]
    h = _rms(x, g0_ref[...]).astype(BF16)
    for f in range(d_ff // FFN_CHUNK):
        lo = f * FFN_CHUNK
        g = jnp.dot(h, wgu_ref[:, lo:lo + FFN_CHUNK], preferred_element_type=F32)
        u = jnp.dot(h, wgu_ref[:, d_ff + lo:d_ff + lo + FFN_CHUNK], preferred_element_type=F32)
        a = (g * jax.nn.sigmoid(g) * u).astype(BF16)
        d = jnp.dot(a, wd_ref[lo:lo + FFN_CHUNK, :], preferred_element_type=F32)
        if f == 0:
            acc_ref[...] = d
        else:
            acc_ref[...] += d
    o_ref[...] = x + 0.5 * _rms(acc_ref[...], g1_ref[...])


def _ffn(x, gains, wgu, wd, l, j, tm):
    n, d = x.shape
    d_ff = wd.shape[2]
    return pl.pallas_call(
        functools.partial(_ffn_body, d_ff=d_ff),
        grid=(n // tm,),
        in_specs=[
            pl.BlockSpec((tm, d), lambda i: (i, 0)),
            _layer_spec((1, d), l, 2 * j * 3),
            _layer_spec((1, d), l, 2 * j * 3 + 1),
            _layer_spec((d, 2 * d_ff), l, j),
            _layer_spec((d_ff, d), l, j),
        ],
        out_specs=pl.BlockSpec((tm, d), lambda i: (i, 0)),
        out_shape=jax.ShapeDtypeStruct((n, d), F32),
        scratch_shapes=[pltpu.VMEM((tm, d), F32)],
        compiler_params=_cparams("parallel"),
        name=f"ffn{j}",
    )(x, gains, gains, wgu, wd)


def _rope(v, c1, sm1, sp1):
    n = v.shape[1] // LANES
    c, sm, sp = (t if n == 1 else jnp.concatenate([t] * n, axis=1) for t in (c1, sm1, sp1))
    w = v.shape[1]
    return v * c + pltpu.roll(v, w - ROPE_DIM // 2, 1) * sm + pltpu.roll(v, ROPE_DIM // 2, 1) * sp


def _inproj_body(x_ref, g_ref, w_ref, c_ref, sm_ref, sp_ref, e_ref,
                 qpad_ref, rows_ref, win_ref, kaug_ref, vsel_ref, kvwin_ref, gates_ref, qkv_ref, z_ref):
    h = _rms(x_ref[...], g_ref[...]).astype(BF16)
    c1, sm1, sp1 = c_ref[...], sm_ref[...], sp_ref[...]

    def mm(lo, hi):
        return jnp.dot(h, w_ref[:, lo:hi], preferred_element_type=F32)

    q = _rope(mm(COL_Q, COL_KV), c1, sm1, sp1)
    qpad_ref[...] = (q * ATTN_SCALE).astype(BF16)
    kv = mm(COL_KV, COL_QKV)
    ksel = _rope(kv[:, 2 * LANES:3 * LANES], c1, sm1, sp1)
    vsel = kv[:, 3 * LANES:4 * LANES]
    kwin = _rope(kv[:, 4 * LANES:5 * LANES], c1, sm1, sp1)
    vwin = kv[:, 5 * LANES:6 * LANES]
    rows_ref[:, 0:2 * LANES] = kv[:, 0:2 * LANES]
    rows_ref[:, 2 * LANES:3 * LANES] = ksel
    rows_ref[:, 3 * LANES:4 * LANES] = vsel
    win_ref[:, 0:LANES] = kwin
    win_ref[:, LANES:2 * LANES] = vwin
    e2 = e_ref[...]
    lane = lax.broadcasted_iota(jnp.int32, ksel.shape, 1)
    first = lane < HEAD_DIM
    kaug_ref[:, 0:LANES] = jnp.where(first, ksel, e2).astype(BF16)
    kaug_ref[:, LANES:2 * LANES] = jnp.where(first, e2, ksel).astype(BF16)
    vsel_ref[...] = vsel.astype(BF16)
    kvwin_ref[:, 0:LANES] = kwin.astype(BF16)
    kvwin_ref[:, LANES:2 * LANES] = vwin.astype(BF16)
    qkv_ref[...] = mm(COL_QKV, COL_Z)
    z_ref[...] = mm(COL_Z, COL_GATE)
    gates_ref[...] = mm(COL_GATE, IN_PACKED)


def _inproj(x, gains, w_in_p, tabs, l, tm):
    n, d = x.shape
    c_t, sm_t, sp_t, e_t = tabs
    nt = c_t.shape[0] // tm
    tab = lambda w: pl.BlockSpec((tm, w), lambda i: (i % nt, 0))
    row = lambda w: pl.BlockSpec((tm, w), lambda i: (i, 0))
    widths = [(QPAD_WIDTH, BF16), (4 * LANES, F32), (2 * LANES, F32), (2 * LANES, BF16), (LANES, BF16),
              (2 * LANES, BF16), (LANES, F32), (GDN_QKV, F32), (GDN_WIDTH, F32)]
    return pl.pallas_call(
        _inproj_body,
        grid=(n // tm,),
        in_specs=[row(d), _layer_spec((1, d), l, 2), _layer_spec((d, IN_PACKED), l),
                  tab(LANES), tab(LANES), tab(LANES), tab(LANES)],
        out_specs=[row(w) for w, _ in widths],
        out_shape=[jax.ShapeDtypeStruct((n, w), dt) for w, dt in widths],
        compiler_params=_cparams("parallel"),
        name="inproj",
    )(x, gains, w_in_p, c_t, sm_t, sp_t, e_t)


def _normmm_body(x_ref, g_ref, w_ref, o_ref):
    h = _rms(x_ref[...], g_ref[...]).astype(BF16)
    o_ref[...] = jnp.dot(h, w_ref[...], preferred_element_type=F32)


def _normmm(x, gain, gain_lead, w, l, tm):
    n, d = x.shape
    nout = w.shape[-1]
    return pl.pallas_call(
        _normmm_body,
        grid=(n // tm,),
        in_specs=[pl.BlockSpec((tm, d), lambda i: (i, 0)), _layer_spec((1, d), *gain_lead), _layer_spec((d, nout), l)],
        out_specs=pl.BlockSpec((tm, nout), lambda i: (i, 0)),
        out_shape=jax.ShapeDtypeStruct((n, nout), F32),
        compiler_params=_cparams("parallel"),
        name="memkv",
    )(x, gain, w)


def _outproj_body(x_ref, a1_ref, a2_ref, w_ref, g_ref, o_ref):
    k1 = a1_ref.shape[1]
    acc = jnp.dot(a1_ref[...].astype(BF16), w_ref[0:k1, :], preferred_element_type=F32)
    acc = acc + jnp.dot(a2_ref[...].astype(BF16), w_ref[k1:, :], preferred_element_type=F32)
    o_ref[...] = x_ref[...] + _rms(acc, g_ref[...])


def _outproj(x, a1, a2, w_out, gains, l, tm, gain_idx=3):
    n, d = x.shape
    row = lambda w: pl.BlockSpec((tm, w), lambda i: (i, 0))
    return pl.pallas_call(
        _outproj_body,
        grid=(n // tm,),
        in_specs=[row(d), row(a1.shape[1]), row(a2.shape[1]),
                  _layer_spec((a1.shape[1] + a2.shape[1], d), l), _layer_spec((1, d), l, gain_idx)],
        out_specs=row(d),
        out_shape=jax.ShapeDtypeStruct((n, d), F32),
        compiler_params=_cparams("parallel"),
        name="outproj",
    )(x, a1, a2, w_out, gains)


def _mem_body(x_ref, g4_ref, g5_ref, wq_ref, wo_ref, kv_ref, o_ref):
    x = x_ref[...]
    h = _rms(x, g4_ref[...]).astype(BF16)
    q = jnp.dot(h, wq_ref[...], preferred_element_type=F32) * (MEM_HD ** -0.5)
    kv = kv_ref[...].astype(BF16)
    heads = range(MEM_HEADS)
    cols = [slice(hd * MEM_HD, (hd + 1) * MEM_HD) for hd in heads]
    ss = [lax.dot_general(q[:, cols[hd]].astype(BF16), kv[:, cols[hd]], (((1,), (1,)), ((), ())),
                          preferred_element_type=F32) for hd in heads]
    es = [jnp.exp(s - jnp.max(s, axis=-1, keepdims=True)) for s in ss]
    ps = [e / jnp.sum(e, axis=-1, keepdims=True) for e in es]
    outs = [jnp.dot(ps[hd].astype(BF16), kv[:, MEM_WIDTH + hd * MEM_HD:MEM_WIDTH + (hd + 1) * MEM_HD],
                    preferred_element_type=F32) for hd in heads]
    o = jnp.concatenate(outs, axis=1).astype(BF16)
    y = jnp.dot(o, wo_ref[...], preferred_element_type=F32)
    o_ref[...] = x + _rms(y, g5_ref[...])


def _mem_block(x, gains, w_q, w_o, mem_kv, l, tm, tiles_per_batch):
    n, d = x.shape
    m = mem_kv.shape[0] // (n // (tm * tiles_per_batch))
    return pl.pallas_call(
        _mem_body,
        grid=(n // tm,),
        in_specs=[pl.BlockSpec((tm, d), lambda i: (i, 0)), _layer_spec((1, d), l, 4), _layer_spec((1, d), l, 5),
                  _layer_spec((d, MEM_WIDTH), l), _layer_spec((MEM_WIDTH, d), l),
                  pl.BlockSpec((m, 2 * MEM_WIDTH), lambda i: (i // tiles_per_batch, 0))],
        out_specs=pl.BlockSpec((tm, d), lambda i: (i, 0)),
        out_shape=jax.ShapeDtypeStruct((n, d), F32),
        compiler_params=_cparams("parallel"),
        name="memattn",
    )(x, gains, gains, w_q, w_o, mem_kv)


NSA_TQ = 128
NSA_TK = 512


def _softmax_cols(s, valid):
    s = jnp.where(valid, s, NEG_INF)
    m = jnp.max(s, axis=0, keepdims=True)
    e = jnp.where(valid, jnp.exp(s - m), 0.0)
    return e / jnp.maximum(jnp.sum(e, axis=0, keepdims=True), 1e-30)


def _softmax_rows(s, valid):
    s = jnp.where(valid, s, NEG_INF)
    m = jnp.max(s, axis=1, keepdims=True)
    e = jnp.where(valid, jnp.exp(s - m), 0.0)
    return e / jnp.maximum(jnp.sum(e, axis=1, keepdims=True), 1e-30)


def _split3(x):
    hi = x.astype(BF16)
    r = x - hi.astype(F32)
    mid = r.astype(BF16)
    lo = (r - mid.astype(F32)).astype(BF16)
    return hi, mid, lo


def _select_topk(imp_t, blk, n_pick):
    n_blocks = imp_t.shape[0]
    sel = jnp.zeros(imp_t.shape, F32)
    work = imp_t
    for _ in range(n_pick):
        mx = jnp.max(work, axis=0, keepdims=True)
        first = jnp.min(jnp.where(work == mx, blk, float(n_blocks)), axis=0, keepdims=True)
        hit = blk == first
        sel = jnp.where(hit, 1.0, sel)
        work = jnp.where(hit, -3.0e38, work)
    return sel


def _transpose_01(x_t, eye):
    return lax.dot_general(eye, x_t.astype(BF16), (((1,), (1,)), ((), ())), preferred_element_type=F32)


def _nsa_body(q_ref, gt_ref, kaug_ref, vsel_ref, kvw_ref, kc_ref, vc_ref, covt_ref, o_ref, *, n_cmp, n_sel, topk):
    tq = q_ref.shape[0]
    n_blk = covt_ref.shape[0]
    n_cmp_pad = kc_ref.shape[0]
    g4 = NSA_GROUP
    i = pl.program_id(1)
    q0 = i * tq
    gates = jax.nn.sigmoid(gt_ref[...])

    qpos_row = q0 + lax.broadcasted_iota(jnp.int32, (1, tq), 1)
    qpos_row4 = jnp.concatenate([qpos_row] * g4, axis=1)
    qpos_col = q0 + lax.broadcasted_iota(jnp.int32, (tq, 1), 0)
    qpos_col4 = jnp.concatenate([qpos_col] * g4, axis=0)
    lane = lax.broadcasted_iota(jnp.int32, (tq, LANES), 1)
    blk = lax.broadcasted_iota(jnp.int32, (n_blk, 1), 0)
    blk_f = blk.astype(F32)
    qblk = qpos_row // SEL_BLOCK
    forced = (blk == 0) | (blk == qblk) | (blk == qblk - 1)
    future = blk > qblk
    exists = blk < n_sel
    n_eye = NSA_KV_HEADS * tq
    eye = (lax.broadcasted_iota(jnp.int32, (n_eye, n_eye), 0)
           == lax.broadcasted_iota(jnp.int32, (n_eye, n_eye), 1)).astype(BF16)
    c_idx = lax.broadcasted_iota(jnp.int32, (n_cmp_pad, 1), 0)
    c_end = c_idx * CMP_STRIDE + (CMP_BLOCK - 1)
    cmp_valid = (c_end <= qpos_row4) & (c_idx < n_cmp)

    w_start = pl.multiple_of(jnp.maximum(q0 - WINDOW, 0), tq)
    w_len = WINDOW + tq
    kpos_w = w_start + lax.broadcasted_iota(jnp.int32, (1, w_len), 1)
    dw = qpos_col4 - kpos_w
    win_valid = (dw >= 0) & (dw < WINDOW)

    n_kt = (q0 + tq + NSA_TK - 1) // NSA_TK
    kc = kc_ref[...]
    vc = vc_ref[...]
    covt = covt_ref[...]
    heads = range(NSA_KV_HEADS)
    nt_dims = (((1,), (1,)), ((), ()))
    owns = [lane < HEAD_DIM, lane >= HEAD_DIM]
    qpads = [[q_ref[:, (hk * g4 + g) * LANES:(hk * g4 + g + 1) * LANES] for g in range(g4)] for hk in heads]
    qpad4s = [jnp.concatenate(qpads[hk], axis=0) for hk in heads]

    s_ts = [lax.dot_general(kc, qpad4s[hk], nt_dims, preferred_element_type=F32) for hk in heads]
    p_ts = [_softmax_cols(s_t, cmp_valid) for s_t in s_ts]
    o_cmps = [jnp.dot(p_t.T.astype(BF16), vc, preferred_element_type=F32) for p_t in p_ts]
    p_sum = jnp.concatenate([sum(p_t[:, g * tq:(g + 1) * tq] for g in range(g4)) for p_t in p_ts], axis=1)
    imp_t = sum(jnp.dot(covt, part, preferred_element_type=F32) for part in _split3(p_sum))
    both = lambda m: jnp.concatenate([m] * NSA_KV_HEADS, axis=1)
    imp_t = jnp.where(both(forced), FORCE_SCORE, jnp.where(both(future), -1.0, imp_t))
    imp_t = jnp.where(exists, imp_t, -2.0)
    sel_t = _select_topk(imp_t, blk_f, topk)
    selneg = jnp.where(_transpose_01(sel_t, eye) > 0.5, 0.0, NEG_INF)
    lhs4s = []
    for hk in heads:
        sn = selneg[hk * tq:(hk + 1) * tq]
        selneg2 = jnp.concatenate([sn, sn], axis=1).astype(BF16)
        lhs4s.append(jnp.concatenate([jnp.where(owns[hk], qp, selneg2) for qp in qpads[hk]], axis=0))

    def kt_body(kt, carry, causal):
        ks = pl.multiple_of(kt * NSA_TK, NSA_TK)
        v = vsel_ref[pl.ds(ks, NSA_TK), :]
        heads = range(NSA_KV_HEADS)
        ss = [lax.dot_general(lhs4s[hk], kaug_ref[pl.ds(ks, NSA_TK), hk * LANES:(hk + 1) * LANES],
                              (((1,), (1,)), ((), ())), preferred_element_type=F32) for hk in heads]
        if causal:
            kpos = ks + lax.broadcasted_iota(jnp.int32, (1, NSA_TK), 1)
            ss = [jnp.where(kpos <= qpos_col4, s, NEG_INF) for s in ss]
        m_new = [jnp.maximum(carry[hk][0], jnp.max(ss[hk], axis=1, keepdims=True)) for hk in heads]
        alpha = [jnp.exp(carry[hk][0] - m_new[hk]) for hk in heads]
        ps = [jnp.exp(ss[hk] - m_new[hk]) for hk in heads]
        vs = [jnp.where(own_k[hk], v, 1.0) for hk in heads]
        pv = [jnp.dot(ps[hk].astype(BF16), vs[hk], preferred_element_type=F32) for hk in heads]
        return tuple((m_new[hk], alpha[hk] * carry[hk][1] + pv[hk]) for hk in heads)

    lane_k = lax.broadcasted_iota(jnp.int32, (NSA_TK, LANES), 1)
    own_k = [lane_k < HEAD_DIM, lane_k >= HEAD_DIM]
    init1 = (jnp.full((g4 * tq, 1), NEG_INF, F32), jnp.zeros((g4 * tq, LANES), F32))
    carry = lax.fori_loop(0, n_kt - 1, functools.partial(kt_body, causal=False), (init1,) * NSA_KV_HEADS)
    carry = kt_body(n_kt - 1, carry, causal=True)

    sum_lane = [HEAD_DIM, 0]
    o_sels = [carry[hk][1] / jnp.maximum(carry[hk][1][:, sum_lane[hk]:sum_lane[hk] + 1], 1e-30) for hk in heads]

    kw = kvw_ref[pl.ds(w_start, w_len), 0:LANES]
    vw = kvw_ref[pl.ds(w_start, w_len), LANES:2 * LANES]
    s_ws = [jnp.where(win_valid, lax.dot_general(qpad4s[hk], kw, nt_dims, preferred_element_type=F32), NEG_INF)
            for hk in heads]
    e_ws = [jnp.exp(s_w - jnp.max(s_w, axis=1, keepdims=True)) for s_w in s_ws]
    o_wins = [jnp.dot(e_w.astype(BF16), vw, preferred_element_type=F32)
              / jnp.maximum(jnp.sum(e_w, axis=1, keepdims=True), 1e-30) for e_w in e_ws]

    for hk in heads:
        def gate_col(j):
            cols = [gates[:, 3 * (hk * g4 + g) + j:3 * (hk * g4 + g) + j + 1] for g in range(g4)]
            return jnp.concatenate(cols, axis=0)

        o4 = gate_col(0) * o_cmps[hk] + gate_col(1) * o_sels[hk] + gate_col(2) * o_wins[hk]
        for pair in range(g4 // 2):
            a = o4[(2 * pair) * tq:(2 * pair + 1) * tq]
            b = o4[(2 * pair + 1) * tq:(2 * pair + 2) * tq]
            if hk == 0:
                both = jnp.where(lane < HEAD_DIM, a, pltpu.roll(b, HEAD_DIM, 1))
            else:
                both = jnp.where(lane < HEAD_DIM, pltpu.roll(a, HEAD_DIM, 1), b)
            col = (hk * (g4 // 2) + pair) * LANES
            o_ref[:, col:col + LANES] = both


def _nsa_prompt(qpad, gates, kaug, vsel, kvwin, kc, vc, covt, batch, t_len, n_cmp, n_sel, topk):
    n = qpad.shape[0]
    nq = t_len // NSA_TQ
    n_cmp_pad = kc.shape[0] // batch
    qrow = lambda w: pl.BlockSpec((NSA_TQ, w), lambda b, i: (b * nq + i, 0))
    per_b = lambda r, w: pl.BlockSpec((r, w), lambda b, i: (b, 0))
    return pl.pallas_call(
        functools.partial(_nsa_body, n_cmp=n_cmp, n_sel=n_sel, topk=topk),
        grid=(batch, nq),
        in_specs=[qrow(QPAD_WIDTH), qrow(LANES), per_b(t_len, 2 * LANES), per_b(t_len, LANES),
                  per_b(t_len, 2 * LANES), per_b(n_cmp_pad, LANES), per_b(n_cmp_pad, LANES),
                  _const_spec(covt.shape)],
        out_specs=qrow(NSA_WIDTH),
        out_shape=jax.ShapeDtypeStruct((n, NSA_WIDTH), F32),
        compiler_params=_cparams("parallel", "parallel"),
        name="nsa_prompt",
    )(qpad, gates, kaug, vsel, kvwin, kc, vc, covt)


def _group_features(load_l, lane):
    heads = ([], [])
    for m in range(CMP_STRIDE // 2):
        a = load_l(2 * m)
        b = load_l(2 * m + 1)
        heads[0].append(jnp.where(lane < HEAD_DIM, a, pltpu.roll(b, HEAD_DIM, 1)))
        heads[1].append(jnp.where(lane < HEAD_DIM, pltpu.roll(a, HEAD_DIM, 1), b))
    return [jnp.concatenate(h, axis=1) for h in heads]


def _compress_heads(x_heads, pe8, w1, w2_h0, w2_h1):
    g = x_heads[0].shape[0]
    x2 = jnp.concatenate(x_heads, axis=0).astype(BF16)
    z = jnp.dot(x2, w1, preferred_element_type=F32)
    zb = jnp.dot(pe8.astype(BF16), w1, preferred_element_type=F32)
    bias = zb[0:1, 0:CMP_HIDDEN] + zb[1:2, CMP_HIDDEN:]
    nxt = pltpu.roll(z[:, CMP_HIDDEN:], 2 * g - 1, 0)
    hid = jax.nn.gelu(z[:, 0:CMP_HIDDEN] + nxt + bias).astype(BF16)
    return (jnp.dot(hid[0:g], w2_h0, preferred_element_type=F32)
            + jnp.dot(hid[g:], w2_h1, preferred_element_type=F32))


def _compress_body(krows_ref, vrows_ref, pe_ref, w1_ref, w2_ref, c_ref, sm_ref, sp_ref, kc_ref, vc_ref):
    g = kc_ref.shape[0]
    lane = lax.broadcasted_iota(jnp.int32, (g, LANES), 1)
    outs = []
    for t, ref in enumerate((krows_ref, vrows_ref)):
        load = lambda l, ref=ref: ref[pl.ds(l, g, stride=CMP_STRIDE), :]
        outs.append(_compress_heads(_group_features(load, lane), pe_ref[t], w1_ref[t], w2_ref[t, 0], w2_ref[t, 1]))
    kc_ref[...] = _rope(outs[0], c_ref[...], sm_ref[...], sp_ref[...]).astype(BF16)
    vc_ref[...] = outs[1].astype(BF16)


def _compress_prompt(rows, pe8, w1c, w2p, tabs_c, l, batch, t_len):
    g = t_len // CMP_STRIDE
    out = jax.ShapeDtypeStruct((batch * g, LANES), BF16)
    return pl.pallas_call(
        _compress_body,
        grid=(batch,),
        in_specs=[pl.BlockSpec((t_len, LANES), lambda b: (b, 0)), pl.BlockSpec((t_len, LANES), lambda b: (b, 1)),
                  _layer_spec((2, 8, CMP_STRIDE * HEAD_DIM), l),
                  _layer_spec((2, CMP_STRIDE * HEAD_DIM, 2 * CMP_HIDDEN), l),
                  _layer_spec((2, 2, CMP_HIDDEN, LANES), l),
                  _const_spec((g, LANES)), _const_spec((g, LANES)), _const_spec((g, LANES))],
        out_specs=[pl.BlockSpec((g, LANES), lambda b: (b, 0))] * 2,
        out_shape=[out, out],
        compiler_params=_cparams("parallel"),
        name="compress",
    )(rows, rows, pe8, w1c, w2p, *tabs_c)


GDN_TT = 512
HALO = 8


def _dot3(a, x):
    return sum(jnp.dot(a, part, preferred_element_type=F32) for part in _split3(x))


def _dot3_nt(a, x):
    return sum(lax.dot_general(a, part, (((1,), (1,)), ((), ())), preferred_element_type=F32) for part in _split3(x))


def _l2n(x):
    return x * lax.rsqrt(jnp.sum(x * x, axis=-1, keepdims=True) + 1e-6)


def _softplus(x):
    return jnp.maximum(x, 0.0) + jnp.log(1.0 + jnp.exp(-jnp.abs(x)))


def _gdn_prep_body(x_ref, halo_ref, gt_ref, cw_ref, alog_ref, dtb_ref,
                   qg_ref, kd_ref, w_ref, u_ref, att_ref, egl_ref,
                   ext_ref, q_s, k_s, kb_s, rhs_s, gc_s):
    c_len = GDN_CHUNK
    tt = x_ref.shape[0]
    nch = tt // c_len
    qk_w = GDN_HEADS * GDN_DK
    first = pl.program_id(1) == 0
    ext_ref[0:HALO, :] = jnp.where(first, 0.0, halo_ref[...])
    ext_ref[HALO:HALO + tt, :] = x_ref[...]
    c = ext_ref[pl.ds(HALO - (CONV_W - 1), tt), :] * cw_ref[0:1, :]
    for j in range(1, CONV_W):
        c = c + ext_ref[pl.ds(HALO - (CONV_W - 1) + j, tt), :] * cw_ref[j:j + 1, :]
    c = c * jax.nn.sigmoid(c)

    gt = gt_ref[...]
    g_full = -jnp.exp(alog_ref[...]) * _softplus(gt + dtb_ref[...])
    beta_full = jax.nn.sigmoid(gt)
    r = lax.broadcasted_iota(jnp.int32, (tt, tt), 0)
    cc = lax.broadcasted_iota(jnp.int32, (tt, tt), 1)
    same = (r // c_len) == (cc // c_len)
    gcum = _dot3(jnp.where(same & (cc <= r), 1.0, 0.0).astype(BF16), g_full)
    gtot = _dot3(jnp.where(same, 1.0, 0.0).astype(BF16), g_full)
    gc_s[...] = gcum
    for h in range(GDN_HEADS):
        sl = slice(h * GDN_DK, (h + 1) * GDN_DK)
        gc_col = gcum[:, GATE_A + h:GATE_A + h + 1]
        gl_col = gtot[:, GATE_A + h:GATE_A + h + 1]
        beta_col = beta_full[:, GATE_B + h:GATE_B + h + 1]
        qh = _l2n(c[:, h * GDN_DK:(h + 1) * GDN_DK]) * (GDN_DK ** -0.5)
        kh = _l2n(c[:, qk_w + h * GDN_DK:qk_w + (h + 1) * GDN_DK])
        vh = c[:, 2 * qk_w + h * GDN_DV:2 * qk_w + (h + 1) * GDN_DV]
        eg = jnp.exp(gc_col)
        kb = kh * beta_col
        q_s[:, sl] = qh
        k_s[:, sl] = kh
        kb_s[:, sl] = kb
        rhs_s[:, 2 * h * GDN_DK:(2 * h + 1) * GDN_DK] = vh * beta_col
        rhs_s[:, (2 * h + 1) * GDN_DK:(2 * h + 2) * GDN_DK] = kb * eg
        qg_ref[:, sl] = (qh * eg).astype(BF16)
        kd_ref[:, sl] = (kh * jnp.exp(gl_col - gc_col)).astype(BF16)
        egl_b = jnp.broadcast_to(jnp.exp(gl_col), (tt, GDN_DK))
        for ch in range(nch):
            egl_ref[ch, :, sl] = egl_b[ch * c_len:ch * c_len + 1, :]

    nh = GDN_HEADS
    wide = nh * c_len
    wi = lax.broadcasted_iota(jnp.int32, (c_len, wide), 0)
    wj = lax.broadcasted_iota(jnp.int32, (c_len, wide), 1) & (c_len - 1)
    incl = wi >= wj
    strict = wi > wj
    eye = jnp.where(wi == wj, 1.0, 0.0)
    base_mask = (wi >> 1) == (wj >> 1)
    bi = lax.broadcasted_iota(jnp.int32, (wide, wide), 0)
    bj = lax.broadcasted_iota(jnp.int32, (wide, wide), 1)
    same_head = (bi // c_len) == (bj // c_len)
    bi_in = bi & (c_len - 1)
    bj_in = bj & (c_len - 1)
    level_masks = [same_head & ((bi_in >> (s + 1)) == (bj_in >> (s + 1))) & ((bi_in >> s) != (bj_in >> s))
                   for s in range(1, int(math.log2(c_len)))]
    head_of_row = lax.broadcasted_iota(jnp.int32, (wide, LANES), 0) // c_len
    lane_of = lax.broadcasted_iota(jnp.int32, (wide, LANES), 1)
    pick_rows = lane_of == GATE_A + head_of_row
    sel_k = lax.broadcasted_iota(jnp.int32, (LANES, wide), 0)
    sel_h = lax.broadcasted_iota(jnp.int32, (LANES, wide), 1) // c_len
    sel_cols = jnp.where(sel_k == GATE_A + sel_h, 1.0, 0.0).astype(BF16)
    ones_b = jnp.ones((c_len, LANES), BF16)
    key_head = (lax.broadcasted_iota(jnp.int32, (wide, GDN_WIDTH), 0) // c_len
                == lax.broadcasted_iota(jnp.int32, (wide, GDN_WIDTH), 1) // GDN_DK)
    rhs_head = (lax.broadcasted_iota(jnp.int32, (wide, 2 * GDN_WIDTH), 0) // c_len
                == lax.broadcasted_iota(jnp.int32, (wide, 2 * GDN_WIDTH), 1) // (2 * GDN_DK))
    nt = (((1,), (1,)), ((), ()))
    rep = lambda m: jnp.concatenate([m] * nh, axis=0)

    n_par = 4
    par = range(n_par)

    def chunk_group(cg, carry):
        rows = [pl.ds(pl.multiple_of((cg * n_par + c) * c_len, c_len), c_len) for c in par]
        gch = [gc_s[r, :] for r in rows]
        g_i = [sum(jnp.dot(part, sel_cols, preferred_element_type=F32) for part in _split3(g)) for g in gch]
        g_j = [_dot3_nt(ones_b, jnp.where(pick_rows, rep(g), 0.0)) for g in gch]
        decay = [jnp.where(incl, jnp.exp(jnp.where(incl, g_i[c] - g_j[c], 0.0)), 0.0) for c in par]
        k_bd = [jnp.where(key_head, rep(k_s[r, :]), 0.0).astype(BF16) for r in rows]
        qkb = [jnp.concatenate([kb_s[r, :], q_s[r, :]], axis=0).astype(BF16) for r in rows]
        prod = [lax.dot_general(qkb[c], k_bd[c], nt, preferred_element_type=F32) for c in par]
        a = [jnp.where(strict, prod[c][0:c_len] * decay[c], 0.0) for c in par]
        for c in par:
            att_ref[rows[c], :] = (prod[c][c_len:] * decay[c]).astype(BF16)
        a_rep = [rep(m) for m in a]
        x = [eye - jnp.where(base_mask, m, 0.0) for m in a]
        for lvl_mask in level_masks:
            off_bd = [jnp.where(lvl_mask, m, 0.0).astype(BF16) for m in a_rep]
            x_bd = [jnp.where(same_head, rep(m), 0.0).astype(BF16) for m in x]
            t = [jnp.dot(x[c].astype(BF16), off_bd[c], preferred_element_type=F32) for c in par]
            x = [x[c] - jnp.dot(t[c].astype(BF16), x_bd[c], preferred_element_type=F32) for c in par]
        rhs_bd = [jnp.where(rhs_head, rep(rhs_s[r, :]), 0.0).astype(BF16) for r in rows]
        uw = [jnp.dot(x[c].astype(BF16), rhs_bd[c], preferred_element_type=F32) for c in par]
        for c in par:
            for h in range(nh):
                sl = slice(h * GDN_DK, (h + 1) * GDN_DK)
                u_ref[rows[c], sl] = uw[c][:, 2 * h * GDN_DK:(2 * h + 1) * GDN_DK]
                w_ref[rows[c], sl] = uw[c][:, (2 * h + 1) * GDN_DK:(2 * h + 2) * GDN_DK].astype(BF16)
        return carry

    lax.fori_loop(0, nch // n_par, chunk_group, 0)


def _gdn_prep(qkv, gates, conv_w, alog_row, dtb_row, l, batch, t_len):
    n = qkv.shape[0]
    tt = GDN_TT
    nt = t_len // tt
    nch = tt // GDN_CHUNK
    row = lambda w: pl.BlockSpec((tt, w), lambda b, i: (b * nt + i, 0))
    outs = [(GDN_WIDTH, BF16), (GDN_WIDTH, BF16), (GDN_WIDTH, BF16), (GDN_WIDTH, F32)]
    return pl.pallas_call(
        _gdn_prep_body,
        grid=(batch, nt),
        in_specs=[row(GDN_QKV),
                  pl.BlockSpec((HALO, GDN_QKV), lambda b, i: (jnp.maximum((b * nt + i) * (tt // HALO) - 1, 0), 0)),
                  row(LANES), _layer_spec((CONV_W, GDN_QKV), l), _layer_spec((1, LANES), l), _layer_spec((1, LANES), l)],
        out_specs=[row(w) for w, _ in outs] + [
            pl.BlockSpec((tt, GDN_HEADS * GDN_CHUNK), lambda b, i: (b * nt + i, 0)),
            pl.BlockSpec((nch, 1, GDN_WIDTH), lambda b, i: (b * nt + i, 0, 0))],
        out_shape=[jax.ShapeDtypeStruct((n, w), dt) for w, dt in outs] + [
            jax.ShapeDtypeStruct((n, GDN_HEADS * GDN_CHUNK), BF16),
            jax.ShapeDtypeStruct((n // GDN_CHUNK, 1, GDN_WIDTH), F32)],
        scratch_shapes=[pltpu.VMEM((HALO + tt, GDN_QKV), F32), pltpu.VMEM((tt, GDN_WIDTH), F32),
                        pltpu.VMEM((tt, GDN_WIDTH), F32), pltpu.VMEM((tt, GDN_WIDTH), F32),
                        pltpu.VMEM((tt, 2 * GDN_WIDTH), F32), pltpu.VMEM((tt, LANES), F32)],
        compiler_params=_cparams("parallel", "parallel"),
        name="gdn_prep",
    )(qkv, qkv, gates, conv_w, alog_row, dtb_row)


def _gdn_scan_body(qg_ref, kd_ref, w_ref, u_ref, att_ref, egl_ref, z_ref, gn_ref, o_ref, sfin_ref, s_ref):
    c_len = GDN_CHUNK
    tt = qg_ref.shape[0]
    i = pl.program_id(1)

    @pl.when(i == 0)
    def _():
        s_ref[...] = jnp.zeros(s_ref.shape, F32)

    gn = gn_ref[...]
    tn = (((0,), (0,)), ((), ()))

    def chunk(ch, carry):
        r0 = pl.multiple_of(ch * c_len, c_len)
        heads = range(GDN_HEADS)
        sls = [slice(h * GDN_DK, (h + 1) * GDN_DK) for h in heads]
        rows = pl.ds(r0, c_len)
        ss = [s_ref[h] for h in heads]
        sbs = [s.astype(BF16) for s in ss]
        wq = [jnp.dot(jnp.concatenate([w_ref[rows, sls[h]], qg_ref[rows, sls[h]]], axis=0), sbs[h],
                      preferred_element_type=F32) for h in heads]
        vbs = [(u_ref[rows, sls[h]] - wq[h][0:c_len]).astype(BF16) for h in heads]
        av = [jnp.dot(att_ref[rows, h * c_len:(h + 1) * c_len], vbs[h], preferred_element_type=F32) for h in heads]
        kv = [lax.dot_general(kd_ref[rows, sls[h]], vbs[h], tn, preferred_element_type=F32) for h in heads]
        for h in heads:
            s_ref[h] = ss[h] * egl_ref[ch, :, sls[h]] + kv[h]
            o = wq[h][c_len:] + av[h]
            y = o * lax.rsqrt(jnp.mean(o * o, axis=-1, keepdims=True) + RMS_EPS) * gn
            zz = z_ref[rows, sls[h]]
            o_ref[rows, sls[h]] = y * (zz * jax.nn.sigmoid(zz))
        return carry

    lax.fori_loop(0, tt // c_len, chunk, 0)

    @pl.when(i == pl.num_programs(1) - 1)
    def _():
        sfin_ref[...] = s_ref[...]


def _gdn_scan(qg, kd, w, u, att, egl, z, gnorm, l, batch, t_len):
    n = qg.shape[0]
    tt = GDN_TT
    nt = t_len // tt
    nch = tt // GDN_CHUNK
    row = lambda: pl.BlockSpec((tt, GDN_WIDTH), lambda b, i: (b * nt + i, 0))
    return pl.pallas_call(
        _gdn_scan_body,
        grid=(batch, nt),
        in_specs=[row(), row(), row(), row(),
                  pl.BlockSpec((tt, GDN_HEADS * GDN_CHUNK), lambda b, i: (b * nt + i, 0)),
                  pl.BlockSpec((nch, 1, GDN_WIDTH), lambda b, i: (b * nt + i, 0, 0)),
                  row(), _layer_spec((1, GDN_DV), l)],
        out_specs=[row(), pl.BlockSpec((None, GDN_HEADS, GDN_DK, GDN_DV), lambda b, i: (b, 0, 0, 0))],
        out_shape=[jax.ShapeDtypeStruct((n, GDN_WIDTH), F32),
                   jax.ShapeDtypeStruct((batch, GDN_HEADS, GDN_DK, GDN_DV), F32)],
        scratch_shapes=[pltpu.VMEM((GDN_HEADS, GDN_DK, GDN_DV), F32)],
        compiler_params=_cparams("parallel", "arbitrary"),
        name="gdn_scan",
    )(qg, kd, w, u, att, egl, z, gnorm)


def _softmax_rows_extra(s, valid, s_new, valid_new):
    s = jnp.where(valid, s, NEG_INF)
    s_new = jnp.where(valid_new, s_new, NEG_INF)
    m = jnp.maximum(jnp.max(s, axis=1, keepdims=True), s_new)
    e = jnp.where(valid, jnp.exp(s - m), 0.0)
    e_new = jnp.where(valid_new, jnp.exp(s_new - m), 0.0)
    den = jnp.maximum(jnp.sum(e, axis=1, keepdims=True) + e_new, 1e-30)
    return e, e_new, 1.0 / den


NSA_SPS = 4


def _round_robin(gens):
    gens = list(gens)
    while gens:
        alive = []
        for gen in gens:
            try:
                next(gen)
                alive.append(gen)
            except StopIteration:
                pass
        gens = alive


def _nsa_sample_body(pt_ref, *refs, n_pages, **static):
    del pt_ref
    per = 3 * n_pages
    (q_ref, gt_ref, rows_ref, wnew_ref, wbuf_ref, pe_ref, w1_ref, w2_ref, c_ref, sm_ref, sp_ref,
     cov_ref, eexp_ref, o_ref) = refs[NSA_SPS * per:]
    gens = []
    for s in range(NSA_SPS):
        pages = refs[s * per:(s + 1) * per]
        gens.append(_nsa_sample_one(
            pages[0:n_pages], pages[n_pages:2 * n_pages], pages[2 * n_pages:],
            q_ref.at[s], gt_ref.at[s], rows_ref.at[s], wnew_ref.at[s], wbuf_ref.at[s],
            pe_ref, w1_ref, w2_ref, c_ref, sm_ref, sp_ref, cov_ref, eexp_ref, o_ref.at[s],
            n_pages=n_pages, **static))
    _round_robin(gens)


def _nsa_sample_one(kcmp_pages, vcmp_pages, sel_pages, q_ref, gt_ref, rows_ref, wnew_ref, wbuf_ref,
                    pe_ref, w1_ref, w2_ref, c_ref, sm_ref, sp_ref, cov_ref, eexp_ref, o_ref,
                    *, n_pages, page, n_buf, n_cmp, n_sel, topk):
    past = n_pages * page
    qpos = past
    gpp = page // CMP_STRIDE
    g = n_pages * gpp
    nt = (((1,), (1,)), ((), ()))
    nh = NSA_HEADS
    lane_g = lax.broadcasted_iota(jnp.int32, (g, LANES), 1)
    qf = q_ref[...]
    qp = qf.astype(BF16)
    row8 = lax.broadcasted_iota(jnp.int32, (nh, 1), 0)

    outs = []
    for t, pages in enumerate((kcmp_pages, vcmp_pages)):
        load = lambda l, pages=pages: jnp.concatenate(
            [pr[pl.ds(l, gpp, stride=CMP_STRIDE), :] for pr in pages], axis=0)
        outs.append(_compress_heads(_group_features(load, lane_g), pe_ref[t], w1_ref[t], w2_ref[t, 0], w2_ref[t, 1]))
        yield
    kc = _rope(outs[0], c_ref[...], sm_ref[...], sp_ref[...]).astype(BF16)
    vc = outs[1].astype(BF16)
    yield

    c_row = lax.broadcasted_iota(jnp.int32, (1, g), 1)
    cv_row = (c_row * CMP_STRIDE + (CMP_BLOCK - 1) <= qpos) & (c_row < n_cmp)
    s_c = lax.dot_general(qp, kc, nt, preferred_element_type=F32)
    yield
    p_c = _softmax_rows(s_c, cv_row)
    o_cmp = jnp.dot(p_c.astype(BF16), vc, preferred_element_type=F32)
    yield
    g0 = jnp.sum(p_c[0:NSA_GROUP], axis=0, keepdims=True)
    g1 = jnp.sum(p_c[NSA_GROUP:], axis=0, keepdims=True)
    psum8 = jnp.where(row8 == 0, g0, jnp.where(row8 == 1, g1, 0.0))
    imp8 = sum(jnp.dot(part, cov_ref[...], preferred_element_type=F32) for part in _split3(psum8))
    yield
    n_blk = imp8.shape[1]
    ib = lax.broadcasted_iota(jnp.int32, (n_blk, n_blk), 0)
    jb = lax.broadcasted_iota(jnp.int32, (n_blk, n_blk), 1)
    eye_b = jnp.where(ib == jb, 1.0, 0.0).astype(BF16)
    imp_t = _dot3_nt(eye_b, imp8)
    yield
    blk = lax.broadcasted_iota(jnp.int32, (n_blk, 1), 0)
    qblk = qpos // SEL_BLOCK
    forced = (blk == 0) | (blk == qblk) | (blk == qblk - 1)
    imp_t = jnp.where(forced, FORCE_SCORE, jnp.where(blk > qblk, -1.0, imp_t))
    imp_t = jnp.where(blk < n_sel, imp_t, -2.0)
    sel_t = _select_topk(imp_t, blk.astype(F32), topk)
    yield
    sel8 = lax.dot_general(sel_t.astype(BF16), eye_b, (((0,), (0,)), ((), ())), preferred_element_type=F32)
    yield
    head_sel = jnp.where(row8 < NSA_GROUP, sel8[0:1], sel8[1:2])

    key_sel = jnp.dot(head_sel.astype(BF16), eexp_ref[...], preferred_element_type=F32) > 0.5
    s_parts = [lax.dot_general(qp, pr[:, 0:LANES].astype(BF16), nt, preferred_element_type=F32) for pr in sel_pages]
    s_sel = jnp.concatenate(s_parts, axis=1)
    yield
    kpos = lax.broadcasted_iota(jnp.int32, (1, past), 1)
    new_row = rows_ref[...]
    k_new = new_row[:, 2 * LANES:3 * LANES].astype(BF16).astype(F32)
    v_new = new_row[:, 3 * LANES:4 * LANES].astype(BF16).astype(F32)
    s_new = jnp.sum(qf * k_new, axis=1, keepdims=True)
    new_sel = head_sel[:, qblk:qblk + 1] > 0.5
    e, e_new, inv = _softmax_rows_extra(s_sel, key_sel & (kpos <= qpos), s_new, new_sel)
    acc = e_new * v_new
    for p_i, pr in enumerate(sel_pages):
        acc = acc + jnp.dot(e[:, p_i * page:(p_i + 1) * page].astype(BF16), pr[:, LANES:2 * LANES].astype(BF16),
                            preferred_element_type=F32)
    o_sel = acc * inv
    yield

    kw = wbuf_ref[:, 0:LANES].astype(BF16)
    vw = wbuf_ref[:, LANES:2 * LANES].astype(BF16)
    s_w = lax.dot_general(qp, kw, nt, preferred_element_type=F32)
    yield
    kpos_w = (past - n_buf) + lax.broadcasted_iota(jnp.int32, (1, n_buf), 1)
    dw = qpos - kpos_w
    w_new = wnew_ref[...]
    kw_new = w_new[:, 0:LANES].astype(BF16).astype(F32)
    vw_new = w_new[:, LANES:2 * LANES].astype(BF16).astype(F32)
    sw_new = jnp.sum(qf * kw_new, axis=1, keepdims=True)
    ew, ew_new, inv_w = _softmax_rows_extra(s_w, (dw >= 0) & (dw < WINDOW) & (kpos_w >= 0), sw_new, row8 >= 0)
    o_win = (jnp.dot(ew.astype(BF16), vw, preferred_element_type=F32) + ew_new * vw_new) * inv_w
    yield

    gates = jax.nn.sigmoid(gt_ref[...])
    lane8 = lax.broadcasted_iota(jnp.int32, (nh, LANES), 1)
    gcol = lambda j: jnp.sum(jnp.where(lane8 == 3 * row8 + j, gates, 0.0), axis=1, keepdims=True)
    o8 = gcol(0) * o_cmp + gcol(1) * o_sel + gcol(2) * o_win
    lane1 = lax.broadcasted_iota(jnp.int32, (1, LANES), 1)
    for pair in range(nh // 2):
        a = o8[2 * pair:2 * pair + 1]
        b = o8[2 * pair + 1:2 * pair + 2]
        if 2 * pair < NSA_GROUP:
            both = jnp.where(lane1 < HEAD_DIM, a, pltpu.roll(b, HEAD_DIM, 1))
        else:
            both = jnp.where(lane1 < HEAD_DIM, pltpu.roll(a, HEAD_DIM, 1), b)
        o_ref[:, pair * LANES:(pair + 1) * LANES] = both


def _nsa_sample(page_table, cache_cmp, cache_sel, q3, gates3, rows3, wnew3, win_cache, pe8, w1c, w2p, tabs_c, cov_s,
                eexp, l, n_cmp, n_sel, topk, sel_col):
    bs, n_pages = page_table.shape
    page = cache_cmp.shape[2]
    n_buf = win_cache.shape[2]
    g = n_pages * page // CMP_STRIDE
    sps = NSA_SPS
    page_spec = lambda s, p, w, col: pl.BlockSpec(
        (None, None, page, w), lambda b, pt, s=s, p=p: (l, pt[b * sps + s, p], 0, col))
    per_b = lambda *shape: pl.BlockSpec((sps,) + shape, lambda b, pt: (b,) + (0,) * len(shape))
    cst = lambda shape: pl.BlockSpec(shape, lambda b, pt: (0,) * len(shape))
    lyr = lambda shape: pl.BlockSpec((None,) + shape, lambda b, pt: (l,) + (0,) * len(shape))
    in_specs, args = [], []
    for s in range(sps):
        in_specs += ([page_spec(s, p, LANES, 0) for p in range(n_pages)]
                     + [page_spec(s, p, LANES, 1) for p in range(n_pages)]
                     + [page_spec(s, p, 2 * LANES, sel_col) for p in range(n_pages)])
        args += [cache_cmp] * (2 * n_pages) + [cache_sel] * n_pages
    in_specs += [per_b(NSA_HEADS, LANES), per_b(1, LANES), per_b(1, 4 * LANES), per_b(1, 2 * LANES),
                 pl.BlockSpec((None, sps, n_buf, 2 * LANES), lambda b, pt: (l, b, 0, 0)),
                 lyr((2, 8, CMP_STRIDE * HEAD_DIM)), lyr((2, CMP_STRIDE * HEAD_DIM, 2 * CMP_HIDDEN)),
                 lyr((2, 2, CMP_HIDDEN, LANES)), cst((g, LANES)), cst((g, LANES)), cst((g, LANES)),
                 cst(cov_s.shape), cst(eexp.shape)]
    args += [q3, gates3, rows3, wnew3, win_cache, pe8, w1c, w2p, *tabs_c, cov_s, eexp]
    grid_spec = pltpu.PrefetchScalarGridSpec(
        num_scalar_prefetch=1, grid=(bs // sps,), in_specs=in_specs,
        out_specs=pl.BlockSpec((sps, 1, NSA_WIDTH), lambda b, pt: (b, 0, 0)))
    return pl.pallas_call(
        functools.partial(_nsa_sample_body, n_pages=n_pages, page=page, n_buf=n_buf, n_cmp=n_cmp, n_sel=n_sel,
                          topk=topk),
        grid_spec=grid_spec,
        out_shape=jax.ShapeDtypeStruct((bs, 1, NSA_WIDTH), F32),
        compiler_params=_cparams("parallel"),
        name="nsa_sample",
    )(page_table, *args)


GDN_SB = 8


def _transpose3(x, eye):
    return _dot3_nt(eye, x)


def _gdn_sample_body(x_ref, cs_ref, gt_ref, z_ref, s_ref, cw_ref, alog_ref, dtb_ref, gn_ref, o_ref, so_ref):
    sb = x_ref.shape[0]
    qk_w = GDN_HEADS * GDN_DK
    c = x_ref[...] * cw_ref[CONV_W - 1:CONV_W, :]
    for j in range(CONV_W - 1):
        c = c + cs_ref[:, j * GDN_QKV:(j + 1) * GDN_QKV] * cw_ref[j:j + 1, :]
    c = c * jax.nn.sigmoid(c)
    gt = gt_ref[...]
    g_full = -jnp.exp(alog_ref[...]) * _softplus(gt + dtb_ref[...])
    eg_full = jnp.exp(g_full)
    beta_full = jax.nn.sigmoid(gt)
    ii = lax.broadcasted_iota(jnp.int32, (LANES, LANES), 0)
    jj = lax.broadcasted_iota(jnp.int32, (LANES, LANES), 1)
    eye = jnp.where(ii == jj, 1.0, 0.0).astype(BF16)
    gn = gn_ref[...]
    for h in range(GDN_HEADS):
        sl = slice(h * GDN_DK, (h + 1) * GDN_DK)
        q = _l2n(c[:, h * GDN_DK:(h + 1) * GDN_DK]) * (GDN_DK ** -0.5)
        k = _l2n(c[:, qk_w + h * GDN_DK:qk_w + (h + 1) * GDN_DK])
        v = c[:, 2 * qk_w + h * GDN_DV:2 * qk_w + (h + 1) * GDN_DV]
        eg = jnp.broadcast_to(eg_full[:, GATE_A + h:GATE_A + h + 1], (sb, GDN_DK))
        beta = jnp.broadcast_to(beta_full[:, GATE_B + h:GATE_B + h + 1], (sb, GDN_DK))
        att = jnp.sum(q * k, axis=1, keepdims=True)
        k_t = _transpose3(k, eye)
        w_t = _transpose3(k * beta * eg, eye)
        qg_t = _transpose3(q * eg, eye)
        u = v * beta
        zz = z_ref[:, sl]
        for i in range(sb):
            s = s_ref[i, h]
            v_new = u[i:i + 1] - jnp.sum(w_t[:, i:i + 1] * s, axis=0, keepdims=True)
            o = jnp.sum(qg_t[:, i:i + 1] * s, axis=0, keepdims=True) + att[i:i + 1] * v_new
            so_ref[i, h] = s * eg[i:i + 1] + k_t[:, i:i + 1] * v_new
            y = o * lax.rsqrt(jnp.mean(o * o, axis=-1, keepdims=True) + RMS_EPS) * gn
            zi = zz[i:i + 1]
            o_ref[i:i + 1, sl] = y * (zi * jax.nn.sigmoid(zi))


def _gdn_sample(qkv, conv_state, gates, z, s_state, conv_w, alog_row, dtb_row, gnorm, l):
    bs = qkv.shape[0]
    sb = GDN_SB
    row = lambda w: pl.BlockSpec((sb, w), lambda i: (i, 0))
    st = pl.BlockSpec((None, sb, GDN_HEADS, GDN_DK, GDN_DV), lambda i: (l, i, 0, 0, 0))
    return pl.pallas_call(
        _gdn_sample_body,
        grid=(bs // sb,),
        in_specs=[row(GDN_QKV), pl.BlockSpec((None, sb, (CONV_W - 1) * GDN_QKV), lambda i: (l, i, 0)),
                  row(LANES), row(GDN_WIDTH), st,
                  _layer_spec((CONV_W, GDN_QKV), l), _layer_spec((1, LANES), l), _layer_spec((1, LANES), l),
                  _layer_spec((1, GDN_DV), l)],
        out_specs=[row(GDN_WIDTH), pl.BlockSpec((sb, GDN_HEADS, GDN_DK, GDN_DV), lambda i: (i, 0, 0, 0))],
        out_shape=[jax.ShapeDtypeStruct((bs, GDN_WIDTH), F32),
                   jax.ShapeDtypeStruct((bs, GDN_HEADS, GDN_DK, GDN_DV), F32)],
        compiler_params=_cparams("parallel"),
        name="gdn_sample",
    )(qkv, conv_state, gates, z, s_state, conv_w, alog_row, dtb_row, gnorm)


def _mem_sample_body(q_ref, kv_ref, o_ref):
    nt = (((1,), (1,)), ((), ()))
    q = q_ref[...] * (MEM_HD ** -0.5)
    row = lax.broadcasted_iota(jnp.int32, (8, MEM_WIDTH), 0)
    lane = lax.broadcasted_iota(jnp.int32, (8, MEM_WIDTH), 1)
    own = (lane // MEM_HD) == row
    qm = jnp.where(own, q, 0.0).astype(BF16)
    kv = kv_ref[...].astype(BF16)
    s = lax.dot_general(qm, kv[:, 0:MEM_WIDTH], nt, preferred_element_type=F32)
    e = jnp.exp(s - jnp.max(s, axis=-1, keepdims=True))
    p = e / jnp.sum(e, axis=-1, keepdims=True)
    o = jnp.dot(p.astype(BF16), kv[:, MEM_WIDTH:], preferred_element_type=F32)
    o_ref[...] = jnp.sum(jnp.where(own, o, 0.0), axis=0, keepdims=True)


def _mem_sample(q3, mem_cache, l):
    bs = q3.shape[0]
    m = mem_cache.shape[2]
    return pl.pallas_call(
        _mem_sample_body,
        grid=(bs,),
        in_specs=[pl.BlockSpec((None, 1, MEM_WIDTH), lambda b: (b, 0, 0)),
                  pl.BlockSpec((None, None, m, 2 * MEM_WIDTH), lambda b: (l, b, 0, 0))],
        out_specs=pl.BlockSpec((None, 1, MEM_WIDTH), lambda b: (b, 0, 0)),
        out_shape=jax.ShapeDtypeStruct((bs, 1, MEM_WIDTH), F32),
        compiler_params=_cparams("parallel"),
        name="mem_sample",
    )(q3, mem_cache)


def _rope_tables(pos):
    half = ROPE_DIM // 2
    inv = ROPE_THETA ** (-2.0 * jnp.arange(half, dtype=F32) / ROPE_DIM)
    ang = pos.astype(F32)[:, None] * inv[None, :]
    cos, sin = jnp.cos(ang), jnp.sin(ang)
    n = pos.shape[0]
    one = jnp.ones((n, HEAD_DIM - ROPE_DIM), F32)
    zero = jnp.zeros((n, HEAD_DIM - ROPE_DIM), F32)
    zh = jnp.zeros((n, half), F32)
    c = jnp.concatenate([cos, cos, one], axis=1)
    sm = jnp.concatenate([-sin, zh, zero], axis=1)
    sp = jnp.concatenate([zh, sin, zero], axis=1)
    return tuple(jnp.concatenate([t, t], axis=1) for t in (c, sm, sp))


def _block_onehot(pos):
    e = (pos[:, None] // SEL_BLOCK == jnp.arange(HEAD_DIM)[None, :]).astype(F32)
    return jnp.concatenate([e, e], axis=1)


def _rope_jnp(x, pos):
    half = ROPE_DIM // 2
    inv = ROPE_THETA ** (-2.0 * jnp.arange(half, dtype=F32) / ROPE_DIM)
    ang = pos.astype(F32)[:, None] * inv[None, :]
    cos = jnp.cos(ang)[None, :, None, :]
    sin = jnp.sin(ang)[None, :, None, :]
    x1 = x[..., :half]
    x2 = x[..., half:ROPE_DIM]
    return jnp.concatenate([x1 * cos - x2 * sin, x2 * cos + x1 * sin, x[..., ROPE_DIM:]], axis=-1)


def _masked_softmax(s, mask):
    s = jnp.where(mask, s, NEG_INF)
    m = jnp.max(s, axis=-1, keepdims=True)
    e = jnp.where(mask, jnp.exp(s - m), 0.0)
    return e / jnp.maximum(jnp.sum(e, axis=-1, keepdims=True), 1e-30)


def _compress_jnp(k, pe, w1, w2):
    b, t, hk, dh = k.shape
    n_cmp = (t - CMP_BLOCK) // CMP_STRIDE + 1
    start = jnp.arange(n_cmp) * CMP_STRIDE
    idx = start[:, None] + jnp.arange(CMP_BLOCK)[None, :]
    blk = k[:, idx] + pe[None, None, :, None, :]
    blk = jnp.swapaxes(blk, 2, 3).reshape(b, n_cmp, hk, CMP_BLOCK * dh)
    return jax.nn.gelu(blk @ w1) @ w2, start + CMP_BLOCK - 1


def _block_coverage(c_start, n_sel):
    b_start = jnp.arange(n_sel) * SEL_BLOCK
    lo = jnp.maximum(c_start[:, None], b_start[None, :])
    hi = jnp.minimum(c_start[:, None] + CMP_BLOCK, b_start[None, :] + SEL_BLOCK)
    return jnp.clip(hi - lo, 0, None).astype(F32) / CMP_BLOCK


def _sample_nsa_jnp(q, gate_logits, rows_new, win_new, nsa_past, win_buf, w_buf, cmp_pe, w_cmp1, w_cmp2, topk):
    b, t = q.shape[:2]
    pos0 = nsa_past.shape[1]
    qpos = pos0 + jnp.arange(t)
    q = q.reshape(b, t, NSA_KV_HEADS, NSA_GROUP, HEAD_DIM)
    rows = jnp.concatenate([nsa_past, rows_new], axis=1)
    t_kv = rows.shape[1]
    kc, c_end = _compress_jnp(rows[:, :, 0], cmp_pe[0], w_cmp1[0], w_cmp2[0])
    vc, _ = _compress_jnp(rows[:, :, 1], cmp_pe[1], w_cmp1[1], w_cmp2[1])
    kc = _rope_jnp(kc, c_end)
    n_sel = -(-t_kv // SEL_BLOCK)
    cov = _block_coverage(c_end - CMP_BLOCK + 1, n_sel)
    sel = jnp.pad(rows[:, :, 2:], ((0, 0), (0, n_sel * SEL_BLOCK - t_kv), (0, 0), (0, 0), (0, 0)))
    sel = sel.reshape(b, n_sel, SEL_BLOCK, 2, NSA_KV_HEADS, HEAD_DIM).transpose(3, 0, 4, 1, 2, 5)
    k_blk, v_blk = sel[0], sel[1]
    win_all = jnp.concatenate([win_buf, win_new], axis=1)
    n_buf = win_buf.shape[1]
    gates = jax.nn.sigmoid(gate_logits).reshape(b, t, NSA_KV_HEADS, NSA_GROUP, 3)
    s = jnp.einsum('bqhgd,bchd->bhgqc', q, kc) * ATTN_SCALE
    p = _masked_softmax(s, c_end[None, :] <= qpos[:, None])
    o_cmp = jnp.einsum('bhgqc,bchd->bqhgd', p, vc)
    imp = jnp.einsum('bhgqc,cn->bhqn', p, cov)
    blk = jnp.arange(n_sel)[None, :]
    q_blk = (qpos // SEL_BLOCK)[:, None]
    forced = (blk == 0) | (blk == q_blk) | (blk == q_blk - 1)
    imp = jnp.where(forced, FORCE_SCORE, jnp.where(blk > q_blk, -1.0, imp))
    _, idx = lax.top_k(imp, min(topk, n_sel))
    bi = jnp.arange(b)[:, None, None, None]
    hi = jnp.arange(NSA_KV_HEADS)[None, :, None, None]
    kg = k_blk[bi, hi, idx]
    vg = v_blk[bi, hi, idx]
    k_pos = idx[..., None] * SEL_BLOCK + jnp.arange(SEL_BLOCK)
    mask = (k_pos <= qpos[None, None, :, None, None])[:, :, None]
    s2 = jnp.einsum('bqhgd,bhqksd->bhgqks', q, kg) * ATTN_SCALE
    shp = s2.shape
    p2 = _masked_softmax(s2.reshape(shp[:-2] + (-1,)), mask.reshape(mask.shape[:-2] + (-1,))).reshape(shp)
    o_sel = jnp.einsum('bhgqks,bhqksd->bqhgd', p2, vg)
    kw_pos = pos0 - n_buf + jnp.arange(n_buf + t)
    sw = jnp.einsum('bqhgd,bkhd->bhgqk', q, win_all[:, :, 0]) * ATTN_SCALE
    diff = qpos[:, None] - kw_pos[None, :]
    pw = _masked_softmax(sw, (diff >= 0) & (diff < WINDOW) & (kw_pos[None, :] >= 0))
    o_win = jnp.einsum('bhgqk,bkhd->bqhgd', pw, win_all[:, :, 1])
    o = gates[..., 0:1] * o_cmp + gates[..., 1:2] * o_sel + gates[..., 2:3] * o_win
    return o.reshape(b, t, NSA_WIDTH), win_all[:, win_all.shape[1] - w_buf:]


def _to_chunks(a, c, pad):
    a = jnp.moveaxis(a, 1, 2)
    if pad:
        a = jnp.pad(a, [(0, 0), (0, 0), (0, pad)] + [(0, 0)] * (a.ndim - 3))
    b, h, tp = a.shape[:3]
    a = a.reshape((b, h, tp // c, c) + a.shape[3:])
    return jnp.moveaxis(a, 2, 0)


def _chunk_gated_delta_jnp(q, k, v, g, beta, s0):
    b, t, h, _ = q.shape
    c = min(GDN_CHUNK, t)
    pad = (-t) % c
    qc, kc, vc = _to_chunks(q, c, pad), _to_chunks(k, c, pad), _to_chunks(v, c, pad)
    gc = jnp.cumsum(_to_chunks(g, c, pad), axis=-1)
    bc = _to_chunks(beta, c, pad)
    ii = jnp.arange(c)
    incl = ii[:, None] >= ii[None, :]
    strict = ii[:, None] > ii[None, :]
    diff = gc[..., :, None] - gc[..., None, :]
    decay = jnp.where(incl, jnp.exp(jnp.where(incl, diff, 0.0)), 0.0)
    kb = kc * bc[..., None]
    a = jnp.where(strict, jnp.einsum('...id,...jd->...ij', kb, kc) * decay, 0.0)
    eye = jnp.eye(c, dtype=F32)
    tm = lax.linalg.triangular_solve(eye + a, jnp.broadcast_to(eye, a.shape), left_side=True, lower=True)
    u = jnp.einsum('...ij,...jd->...id', tm, vc * bc[..., None])
    w = jnp.einsum('...ij,...jd->...id', tm, kb * jnp.exp(gc)[..., None])

    def step(s, inp):
        qi, ki, ui, wi, gi, di = inp
        v_new = ui - jnp.einsum('bhck,bhkv->bhcv', wi, s)
        att = jnp.einsum('bhik,bhjk->bhij', qi, ki) * di
        o = jnp.einsum('bhck,bhkv->bhcv', qi * jnp.exp(gi)[..., None], s) + jnp.einsum('bhij,bhjv->bhiv', att, v_new)
        gl = gi[..., -1]
        s = s * jnp.exp(gl)[..., None, None] + jnp.einsum(
            'bhck,bhcv->bhkv', ki * jnp.exp(gl[..., None] - gi)[..., None], v_new)
        return s, o

    s, o = lax.scan(step, s0, (qc, kc, u, w, gc, decay))
    o = jnp.moveaxis(o, 0, 2).reshape(b, h, -1, o.shape[-1])[:, :, :t]
    return jnp.moveaxis(o, 1, 2), s


def _gdn_jnp(qkv, a, bb, z, conv_buf, state, conv_w, a_log, dt_bias, norm_w):
    b, t, _ = qkv.shape
    xp = jnp.concatenate([conv_buf, qkv], axis=1)
    c = xp[:, 0:t] * conv_w[0]
    for j in range(1, CONV_W):
        c = c + xp[:, j:j + t] * conv_w[j]
    c = jax.nn.silu(c)
    new_buf = xp[:, t:]
    qk_w = GDN_HEADS * GDN_DK
    l2 = lambda x: x * lax.rsqrt(jnp.sum(x * x, axis=-1, keepdims=True) + 1e-6)
    q = l2(c[..., :qk_w].reshape(b, t, GDN_HEADS, GDN_DK)) * GDN_DK ** -0.5
    k = l2(c[..., qk_w:2 * qk_w].reshape(b, t, GDN_HEADS, GDN_DK))
    v = c[..., 2 * qk_w:].reshape(b, t, GDN_HEADS, GDN_DV)
    g = -jnp.exp(a_log) * jax.nn.softplus(a + dt_bias)
    beta = jax.nn.sigmoid(bb)
    o, new_state = _chunk_gated_delta_jnp(q, k, v, g, beta, state)
    o = o * lax.rsqrt(jnp.mean(o * o, axis=-1, keepdims=True) + RMS_EPS) * norm_w
    o = o * jax.nn.silu(z).reshape(b, t, GDN_HEADS, GDN_DV)
    return o.reshape(b, t, GDN_WIDTH), new_buf, new_state


def _mem_attend_sample_jnp(x, gains, mem_kv, w_q, w_o):
    xf = x
    h = xf * lax.rsqrt(jnp.mean(xf * xf, axis=-1, keepdims=True) + RMS_EPS) * gains[4]
    q = (h @ w_q).reshape(-1, MEM_HEADS, MEM_HD)
    s = jnp.einsum('bhd,bmhd->bhm', q, mem_kv[:, :, 0]) * MEM_HD ** -0.5
    p = jax.nn.softmax(s, axis=-1)
    o = jnp.einsum('bhm,bmhd->bhd', p, mem_kv[:, :, 1]).reshape(-1, MEM_WIDTH)
    y = o @ w_o
    return x + y * lax.rsqrt(jnp.mean(y * y, axis=-1, keepdims=True) + RMS_EPS) * gains[5]


def _pack_w_in(w_in):
    nl, d, _ = w_in.shape
    o_q, o_kv, o_g = NSA_WIDTH, NSA_WIDTH + KV6_WIDTH, NSA_WIDTH + KV6_WIDTH + 3 * NSA_HEADS
    o_qkv = o_g
    o_a = o_qkv + GDN_QKV
    o_b = o_a + GDN_HEADS
    o_z = o_b + GDN_HEADS
    wq = w_in[:, :, :o_q].reshape(nl, d, NSA_HEADS, HEAD_DIM)
    zq = jnp.zeros_like(wq)
    first = jnp.concatenate([wq, zq], axis=-1)
    second = jnp.concatenate([zq, wq], axis=-1)
    kv_of_head = (jnp.arange(NSA_HEADS) // NSA_GROUP)[None, None, :, None]
    wq_pad = jnp.where(kv_of_head == 0, first, second).reshape(nl, d, QPAD_WIDTH)
    gate_grp = jnp.concatenate([w_in[:, :, o_kv:o_g], w_in[:, :, o_a:o_z],
                                jnp.zeros((nl, d, LANES - 3 * NSA_HEADS - 2 * GDN_HEADS), w_in.dtype)], axis=-1)
    packed = jnp.concatenate([wq_pad, w_in[:, :, o_q:o_kv], w_in[:, :, o_qkv:o_a], w_in[:, :, o_z:], gate_grp], axis=-1)
    return packed.astype(BF16)


def kernel(x_prompt, x_sample, cache_nsa_kv, cache_win_kv, state_gdn_S, state_gdn_conv, cache_mem_kv, page_table, mem_prompt, norm_gains, mem_norm, w_ffn_gu, w_ffn_down, w_in, w_out, cmp_pe, w_cmp1, w_cmp2, gdn_conv, gdn_A_log, gdn_dt_bias, gdn_norm, w_mem_q, w_mem_kv, w_mem_o):
    bp, t_len, d = x_prompt.shape
    bs = x_sample.shape[0]
    depth = w_in.shape[0]
    w_buf = cache_win_kv.shape[2]
    n_pages = page_table.shape[1]
    page = cache_nsa_kv.shape[2]
    past_len = n_pages * page
    n_mem = mem_prompt.shape[1]
    np_tok = bp * t_len

    gains = norm_gains.reshape(depth, 8, 1, d)
    wgu = w_ffn_gu.astype(BF16)
    wd = w_ffn_down.astype(BF16)
    w_in_p = _pack_w_in(w_in)
    w_out_b = w_out.astype(BF16)
    w_mq = w_mem_q.astype(BF16)
    w_mkv = w_mem_kv.astype(BF16)
    w_mo = w_mem_o.astype(BF16)
    mem_gain = mem_norm.reshape(depth, 1, d)

    pos_p = jnp.arange(t_len)
    tabs_p = _rope_tables(pos_p) + (_block_onehot(pos_p),)
    pos_s = jnp.full((bs,), past_len, jnp.int32)
    tabs_s = _rope_tables(pos_s) + (jnp.zeros((bs, LANES), F32),)

    half_k = CMP_STRIDE * HEAD_DIM
    pe8 = jnp.pad(cmp_pe.reshape(depth, 2, CMP_BLOCK // CMP_STRIDE, half_k), ((0, 0), (0, 0), (0, 6), (0, 0)))
    w1c = jnp.concatenate([w_cmp1[:, :, :half_k], w_cmp1[:, :, half_k:]], axis=-1).astype(BF16)
    z2 = jnp.zeros_like(w_cmp2)
    w2p = jnp.stack([jnp.concatenate([w_cmp2, z2], axis=-1), jnp.concatenate([z2, w_cmp2], axis=-1)], axis=2).astype(BF16)
    tabs_c = _rope_tables(jnp.arange(t_len // CMP_STRIDE) * CMP_STRIDE + (CMP_BLOCK - 1))
    lane_pad = lambda v: jnp.pad(v, ((0, 0), (GATE_A, LANES - GATE_A - GDN_HEADS))).reshape(depth, 1, LANES)
    alog_row = lane_pad(gdn_A_log)
    dtb_row = lane_pad(gdn_dt_bias)
    gnorm = gdn_norm.reshape(depth, 1, GDN_DV)

    n_cmp = (t_len - CMP_BLOCK) // CMP_STRIDE + 1
    n_cmp_pad = t_len // CMP_STRIDE
    n_sel = -(-t_len // SEL_BLOCK)
    c_start = jnp.arange(n_cmp) * CMP_STRIDE
    assert n_sel <= HEAD_DIM and t_len >= WINDOW + NSA_TQ and t_len % NSA_TK == 0
    covt = jnp.pad(_block_coverage(c_start, n_sel).T, ((0, HEAD_DIM - n_sel), (0, n_cmp_pad - n_cmp))).astype(BF16)
    topk = min(SEL_TOPK, n_sel)

    assert x_sample.shape[1] == 1 and past_len % SEL_BLOCK == 0 and bs % GDN_SB == 0
    t_kv = past_len + 1
    n_cmp_s = (t_kv - CMP_BLOCK) // CMP_STRIDE + 1
    assert (n_cmp_s - 1) * CMP_STRIDE + CMP_BLOCK <= past_len
    g_s = past_len // CMP_STRIDE
    n_sel_s = -(-t_kv // SEL_BLOCK)
    assert n_sel_s <= HEAD_DIM
    topk_s = min(SEL_TOPK, n_sel_s)
    cov_s = jnp.pad(_block_coverage(jnp.arange(n_cmp_s) * CMP_STRIDE, n_sel_s),
                    ((0, g_s - n_cmp_s), (0, HEAD_DIM - n_sel_s))).astype(BF16)
    eexp = (jnp.arange(past_len)[None, :] // SEL_BLOCK == jnp.arange(HEAD_DIM)[:, None]).astype(BF16)
    tabs_cs = _rope_tables(jnp.arange(g_s) * CMP_STRIDE + (CMP_BLOCK - 1))
    cache_cmp = cache_sel = cache_nsa_kv.reshape(depth, cache_nsa_kv.shape[1], page, 4 * LANES)
    win_cache = cache_win_kv.reshape(depth, bs, w_buf, 2 * LANES)
    conv_cache = state_gdn_conv.reshape(depth, bs, (CONV_W - 1) * GDN_QKV)
    mem_cache = cache_mem_kv.reshape(depth, bs, n_mem, 2 * MEM_WIDTH)

    tm_p = 512
    yp = x_prompt.reshape(np_tok, d)
    ys = x_sample.reshape(bs, d)
    mem_flat = mem_prompt.reshape(bp * n_mem, d)
    outs = [[] for _ in range(9)]
    for l in range(depth):
        mem_kv_p = _normmm(mem_flat, mem_gain, (l,), w_mkv, l, n_mem)
        yp = _ffn(yp, gains, wgu, wd, l, 0, tm_p)
        qpad, rows, win, kaug, vsel, kvwin, gates, qkv, z = _inproj(yp, gains, w_in_p, tabs_p, l, tm_p)
        rows5 = rows.reshape(bp, t_len, 4, NSA_KV_HEADS, HEAD_DIM)
        kc, vc = _compress_prompt(rows, pe8, w1c, w2p, tabs_c, l, bp, t_len)
        o_nsa = _nsa_prompt(qpad, gates, kaug, vsel, kvwin, kc, vc, covt, bp, t_len, n_cmp, n_sel, topk)
        qg, kd, w_g, u_g, att, egl = _gdn_prep(qkv, gates, gdn_conv, alog_row, dtb_row, l, bp, t_len)
        o_gdn, s_p = _gdn_scan(qg, kd, w_g, u_g, att, egl, z, gnorm, l, bp, t_len)
        conv_p = qkv.reshape(bp, t_len, GDN_QKV)[:, t_len - (CONV_W - 1):]
        yp = _outproj(yp, o_nsa, o_gdn, w_out_b, gains, l, tm_p)
        yp = _mem_block(yp, gains, w_mq, w_mo, mem_kv_p, l, tm_p, t_len // tm_p)
        yp = _ffn(yp, gains, wgu, wd, l, 1, tm_p)
        win5 = win.reshape(bp, t_len, 2, NSA_KV_HEADS, HEAD_DIM)
        win_p = win5[:, t_len - w_buf:] if t_len >= w_buf else jnp.pad(
            win5, ((0, 0), (w_buf - t_len, 0), (0, 0), (0, 0), (0, 0)))

        ys = _ffn(ys, gains, wgu, wd, l, 0, bs)
        qpad_s, rows_s, win_s, _, _, _, gates_s, qkv_s, z_s = _inproj(ys, gains, w_in_p, tabs_s, l, bs)
        o_nsa_s = _nsa_sample(
            page_table, cache_cmp, cache_sel, qpad_s.astype(F32).reshape(bs, NSA_HEADS, LANES), gates_s.reshape(bs, 1, LANES),
            rows_s.reshape(bs, 1, 4 * LANES), win_s.reshape(bs, 1, 2 * LANES), win_cache, pe8, w1c, w2p, tabs_cs,
            cov_s, eexp, l, n_cmp_s, n_sel_s, topk_s, 1).reshape(bs, NSA_WIDTH)
        o_gdn_s, s_s = _gdn_sample(qkv_s, conv_cache, gates_s, z_s, state_gdn_S, gdn_conv, alog_row, dtb_row, gnorm, l)
        ys = _outproj(ys, o_nsa_s, o_gdn_s, w_out_b, gains, l, bs)
        q_m = _normmm(ys, gains, (l, 4), w_mq, l, bs)
        o_m = _mem_sample(q_m.reshape(bs, 1, MEM_WIDTH), mem_cache, l).reshape(bs, MEM_WIDTH)
        ys = _outproj(ys, o_m[:, :MEM_WIDTH // 2], o_m[:, MEM_WIDTH // 2:], w_mo, gains, l, bs, gain_idx=5)
        ys = _ffn(ys, gains, wgu, wd, l, 1, bs)
        win_all = jnp.concatenate([cache_win_kv[l], win_s.reshape(bs, 1, 2, NSA_KV_HEADS, HEAD_DIM)], axis=1)
        wbuf_s = win_all[:, win_all.shape[1] - w_buf:]
        conv_s = jnp.concatenate([state_gdn_conv[l], qkv_s[:, None, :]], axis=1)[:, 1:]

        for lst, val in zip(outs, (rows5, win_p, s_p, conv_p, mem_kv_p.reshape(bp, n_mem, 2, MEM_HEADS, MEM_HD),
                                   rows_s.reshape(bs, 1, 4, NSA_KV_HEADS, HEAD_DIM), wbuf_s, s_s, conv_s)):
            lst.append(val)
    return (yp.reshape(bp, t_len, d), ys.reshape(bs, 1, d)) + tuple(jnp.stack(v) for v in outs)
```

```python
import functools
import math

import numpy as np
import jax
import jax.numpy as jnp
from jax import lax
from jax.experimental import pallas as pl
from jax.experimental.pallas import tpu as pltpu

F32 = jnp.float32
BF16 = jnp.bfloat16

HEAD_DIM = 64
NSA_HEADS = 8
NSA_KV_HEADS = 2
NSA_GROUP = NSA_HEADS // NSA_KV_HEADS
NSA_WIDTH = NSA_HEADS * HEAD_DIM
ROPE_DIM = HEAD_DIM // 4
ROPE_THETA = 500000.0
ATTN_SCALE = HEAD_DIM ** -0.5
CMP_BLOCK = 32
CMP_STRIDE = 16
CMP_HIDDEN = 2 * HEAD_DIM
SEL_BLOCK = 64
SEL_TOPK = 16
WINDOW = 512
FORCE_SCORE = 1.0e4
GDN_DK = 128
GDN_DV = 128
GDN_HEADS = 4
GDN_WIDTH = GDN_HEADS * GDN_DV
GDN_QKV = GDN_HEADS * (2 * GDN_DK + GDN_DV)
CONV_W = 4
GDN_CHUNK = 64
MEM_HEADS = 4
MEM_HD = 64
MEM_WIDTH = MEM_HEADS * MEM_HD
RMS_EPS = 1e-6
NEG_INF = -1.0e30
KV6_WIDTH = 6 * NSA_KV_HEADS * HEAD_DIM

LANES = 128
VMEM_LIMIT_BYTES = 56 * 1024 * 1024

QPAD_WIDTH = NSA_HEADS * LANES
COL_Q = 0
COL_KV = COL_Q + QPAD_WIDTH
COL_QKV = COL_KV + KV6_WIDTH
COL_Z = COL_QKV + GDN_QKV
COL_GATE = COL_Z + GDN_WIDTH
IN_PACKED = COL_GATE + LANES
GATE_A = 3 * NSA_HEADS
GATE_B = GATE_A + GDN_HEADS


def _cparams(*sem):
    return pltpu.CompilerParams(dimension_semantics=sem, vmem_limit_bytes=VMEM_LIMIT_BYTES)


def _rms(x, w):
    return x * lax.rsqrt(jnp.mean(x * x, axis=-1, keepdims=True) + RMS_EPS) * w


def _const_spec(shape):
    nd = len(shape)
    return pl.BlockSpec(shape, lambda *_: (0,) * nd)


def _layer_spec(shape, *lead):
    nlead = len(lead)
    nd = len(shape)
    return pl.BlockSpec((None,) * nlead + tuple(shape), lambda *_: tuple(lead) + (0,) * nd)


FFN_CHUNK = 256


def _ffn_body(x_ref, g0_ref, g1_ref, wgu_ref, wd_ref, o_ref, acc_ref, *, d_ff):
    x = x_ref[...]
    h = _rms(x, g0_ref[...]).astype(BF16)
    for f in range(d_ff // FFN_CHUNK):
        lo = f * FFN_CHUNK
        g = jnp.dot(h, wgu_ref[:, lo:lo + FFN_CHUNK], preferred_element_type=F32)
        u = jnp.dot(h, wgu_ref[:, d_ff + lo:d_ff + lo + FFN_CHUNK], preferred_element_type=F32)
        a = (g * jax.nn.sigmoid(g) * u).astype(BF16)
        d = jnp.dot(a, wd_ref[lo:lo + FFN_CHUNK, :], preferred_element_type=F32)
        if f == 0:
            acc_ref[...] = d
        else:
            acc_ref[...] += d
    o_ref[...] = x + 0.5 * _rms(acc_ref[...], g1_ref[...])


def _ffn(x, gains, wgu, wd, l, j, tm):
    n, d = x.shape
    d_ff = wd.shape[2]
    return pl.pallas_call(
        functools.partial(_ffn_body, d_ff=d_ff),
        grid=(n // tm,),
        in_specs=[
            pl.BlockSpec((tm, d), lambda i: (i, 0)),
            _layer_spec((1, d), l, 2 * j * 3),
            _layer_spec((1, d), l, 2 * j * 3 + 1),
            _layer_spec((d, 2 * d_ff), l, j),
            _layer_spec((d_ff, d), l, j),
        ],
        out_specs=pl.BlockSpec((tm, d), lambda i: (i, 0)),
        out_shape=jax.ShapeDtypeStruct((n, d), F32),
        scratch_shapes=[pltpu.VMEM((tm, d), F32)],
        compiler_params=_cparams("parallel"),
        name=f"ffn{j}",
    )(x, gains, gains, wgu, wd)


def _rope(v, c1, sm1, sp1):
    n = v.shape[1] // LANES
    c, sm, sp = (t if n == 1 else jnp.concatenate([t] * n, axis=1) for t in (c1, sm1, sp1))
    w = v.shape[1]
    return v * c + pltpu.roll(v, w - ROPE_DIM // 2, 1) * sm + pltpu.roll(v, ROPE_DIM // 2, 1) * sp


def _inproj_body(x_ref, g_ref, w_ref, c_ref, sm_ref, sp_ref, e_ref,
                 qpad_ref, rows_ref, win_ref, kaug_ref, vsel_ref, kvwin_ref, gates_ref, qkv_ref, z_ref):
    h = _rms(x_ref[...], g_ref[...]).astype(BF16)
    c1, sm1, sp1 = c_ref[...], sm_ref[...], sp_ref[...]

    def mm(lo, hi):
        return jnp.dot(h, w_ref[:, lo:hi], preferred_element_type=F32)

    q = _rope(mm(COL_Q, COL_KV), c1, sm1, sp1)
    qpad_ref[...] = (q * ATTN_SCALE).astype(BF16)
    kv = mm(COL_KV, COL_QKV)
    ksel = _rope(kv[:, 2 * LANES:3 * LANES], c1, sm1, sp1)
    vsel = kv[:, 3 * LANES:4 * LANES]
    kwin = _rope(kv[:, 4 * LANES:5 * LANES], c1, sm1, sp1)
    vwin = kv[:, 5 * LANES:6 * LANES]
    rows_ref[:, 0:2 * LANES] = kv[:, 0:2 * LANES]
    rows_ref[:, 2 * LANES:3 * LANES] = ksel
    rows_ref[:, 3 * LANES:4 * LANES] = vsel
    win_ref[:, 0:LANES] = kwin
    win_ref[:, LANES:2 * LANES] = vwin
    e2 = e_ref[...]
    lane = lax.broadcasted_iota(jnp.int32, ksel.shape, 1)
    first = lane < HEAD_DIM
    kaug_ref[:, 0:LANES] = jnp.where(first, ksel, e2).astype(BF16)
    kaug_ref[:, LANES:2 * LANES] = jnp.where(first, e2, ksel).astype(BF16)
    vsel_ref[...] = vsel.astype(BF16)
    kvwin_ref[:, 0:LANES] = kwin.astype(BF16)
    kvwin_ref[:, LANES:2 * LANES] = vwin.astype(BF16)
    qkv_ref[...] = mm(COL_QKV, COL_Z)
    z_ref[...] = mm(COL_Z, COL_GATE)
    gates_ref[...] = mm(COL_GATE, IN_PACKED)


def _inproj(x, gains, w_in_p, tabs, l, tm):
    n, d = x.shape
    c_t, sm_t, sp_t, e_t = tabs
    nt = c_t.shape[0] // tm
    tab = lambda w: pl.BlockSpec((tm, w), lambda i: (i % nt, 0))
    row = lambda w: pl.BlockSpec((tm, w), lambda i: (i, 0))
    widths = [(QPAD_WIDTH, BF16), (4 * LANES, F32), (2 * LANES, F32), (2 * LANES, BF16), (LANES, BF16),
              (2 * LANES, BF16), (LANES, F32), (GDN_QKV, F32), (GDN_WIDTH, F32)]
    return pl.pallas_call(
        _inproj_body,
        grid=(n // tm,),
        in_specs=[row(d), _layer_spec((1, d), l, 2), _layer_spec((d, IN_PACKED), l),
                  tab(LANES), tab(LANES), tab(LANES), tab(LANES)],
        out_specs=[row(w) for w, _ in widths],
        out_shape=[jax.ShapeDtypeStruct((n, w), dt) for w, dt in widths],
        compiler_params=_cparams("parallel"),
        name="inproj",
    )(x, gains, w_in_p, c_t, sm_t, sp_t, e_t)


def _normmm_body(x_ref, g_ref, w_ref, o_ref):
    h = _rms(x_ref[...], g_ref[...]).astype(BF16)
    o_ref[...] = jnp.dot(h, w_ref[...], preferred_element_type=F32)


def _normmm(x, gain, gain_lead, w, l, tm):
    n, d = x.shape
    nout = w.shape[-1]
    return pl.pallas_call(
        _normmm_body,
        grid=(n // tm,),
        in_specs=[pl.BlockSpec((tm, d), lambda i: (i, 0)), _layer_spec((1, d), *gain_lead), _layer_spec((d, nout), l)],
        out_specs=pl.BlockSpec((tm, nout), lambda i: (i, 0)),
        out_shape=jax.ShapeDtypeStruct((n, nout), F32),
        compiler_params=_cparams("parallel"),
        name="memkv",
    )(x, gain, w)


def _outproj_body(x_ref, a1_ref, a2_ref, w_ref, g_ref, o_ref):
    k1 = a1_ref.shape[1]
    acc = jnp.dot(a1_ref[...].astype(BF16), w_ref[0:k1, :], preferred_element_type=F32)
    acc = acc + jnp.dot(a2_ref[...].astype(BF16), w_ref[k1:, :], preferred_element_type=F32)
    o_ref[...] = x_ref[...] + _rms(acc, g_ref[...])


def _outproj(x, a1, a2, w_out, gains, l, tm, gain_idx=3):
    n, d = x.shape
    row = lambda w: pl.BlockSpec((tm, w), lambda i: (i, 0))
    return pl.pallas_call(
        _outproj_body,
        grid=(n // tm,),
        in_specs=[row(d), row(a1.shape[1]), row(a2.shape[1]),
                  _layer_spec((a1.shape[1] + a2.shape[1], d), l), _layer_spec((1, d), l, gain_idx)],
        out_specs=row(d),
        out_shape=jax.ShapeDtypeStruct((n, d), F32),
        compiler_params=_cparams("parallel"),
        name="outproj",
    )(x, a1, a2, w_out, gains)


def _mem_body(x_ref, g4_ref, g5_ref, wq_ref, wo_ref, kv_ref, o_ref):
    x = x_ref[...]
    h = _rms(x, g4_ref[...]).astype(BF16)
    q = jnp.dot(h, wq_ref[...], preferred_element_type=F32) * (MEM_HD ** -0.5)
    kv = kv_ref[...].astype(BF16)
    heads = range(MEM_HEADS)
    cols = [slice(hd * MEM_HD, (hd + 1) * MEM_HD) for hd in heads]
    ss = [lax.dot_general(q[:, cols[hd]].astype(BF16), kv[:, cols[hd]], (((1,), (1,)), ((), ())),
                          preferred_element_type=F32) for hd in heads]
    es = [jnp.exp(s - jnp.max(s, axis=-1, keepdims=True)) for s in ss]
    ps = [e / jnp.sum(e, axis=-1, keepdims=True) for e in es]
    outs = [jnp.dot(ps[hd].astype(BF16), kv[:, MEM_WIDTH + hd * MEM_HD:MEM_WIDTH + (hd + 1) * MEM_HD],
                    preferred_element_type=F32) for hd in heads]
    o = jnp.concatenate(outs, axis=1).astype(BF16)
    y = jnp.dot(o, wo_ref[...], preferred_element_type=F32)
    o_ref[...] = x + _rms(y, g5_ref[...])


def _mem_block(x, gains, w_q, w_o, mem_kv, l, tm, tiles_per_batch):
    n, d = x.shape
    m = mem_kv.shape[0] // (n // (tm * tiles_per_batch))
    return pl.pallas_call(
        _mem_body,
        grid=(n // tm,),
        in_specs=[pl.BlockSpec((tm, d), lambda i: (i, 0)), _layer_spec((1, d), l, 4), _layer_spec((1, d), l, 5),
                  _layer_spec((d, MEM_WIDTH), l), _layer_spec((MEM_WIDTH, d), l),
                  pl.BlockSpec((m, 2 * MEM_WIDTH), lambda i: (i // tiles_per_batch, 0))],
        out_specs=pl.BlockSpec((tm, d), lambda i: (i, 0)),
        out_shape=jax.ShapeDtypeStruct((n, d), F32),
        compiler_params=_cparams("parallel"),
        name="memattn",
    )(x, gains, gains, w_q, w_o, mem_kv)


NSA_TQ = 128
NSA_TK = 512


def _softmax_cols(s, valid):
    s = jnp.where(valid, s, NEG_INF)
    m = jnp.max(s, axis=0, keepdims=True)
    e = jnp.where(valid, jnp.exp(s - m), 0.0)
    return e / jnp.maximum(jnp.sum(e, axis=0, keepdims=True), 1e-30)


def _softmax_rows(s, valid):
    s = jnp.where(valid, s, NEG_INF)
    m = jnp.max(s, axis=1, keepdims=True)
    e = jnp.where(valid, jnp.exp(s - m), 0.0)
    return e / jnp.maximum(jnp.sum(e, axis=1, keepdims=True), 1e-30)


def _split3(x):
    hi = x.astype(BF16)
    r = x - hi.astype(F32)
    mid = r.astype(BF16)
    lo = (r - mid.astype(F32)).astype(BF16)
    return hi, mid, lo


def _select_topk(imp_t, blk, n_pick):
    n_blocks = imp_t.shape[0]
    sel = jnp.zeros(imp_t.shape, F32)
    work = imp_t
    for _ in range(n_pick):
        mx = jnp.max(work, axis=0, keepdims=True)
        first = jnp.min(jnp.where(work == mx, blk, float(n_blocks)), axis=0, keepdims=True)
        hit = blk == first
        sel = jnp.where(hit, 1.0, sel)
        work = jnp.where(hit, -3.0e38, work)
    return sel


def _transpose_01(x_t, eye):
    return lax.dot_general(eye, x_t.astype(BF16), (((1,), (1,)), ((), ())), preferred_element_type=F32)


def _nsa_body(q_ref, gt_ref, kaug_ref, vsel_ref, kvw_ref, kc_ref, vc_ref, covt_ref, o_ref, *, n_cmp, n_sel, topk):
    tq = q_ref.shape[0]
    n_blk = covt_ref.shape[0]
    n_cmp_pad = kc_ref.shape[0]
    g4 = NSA_GROUP
    i = pl.program_id(1)
    q0 = i * tq
    gates = jax.nn.sigmoid(gt_ref[...])

    qpos_row = q0 + lax.broadcasted_iota(jnp.int32, (1, tq), 1)
    qpos_row4 = jnp.concatenate([qpos_row] * g4, axis=1)
    qpos_col = q0 + lax.broadcasted_iota(jnp.int32, (tq, 1), 0)
    qpos_col4 = jnp.concatenate([qpos_col] * g4, axis=0)
    lane = lax.broadcasted_iota(jnp.int32, (tq, LANES), 1)
    blk = lax.broadcasted_iota(jnp.int32, (n_blk, 1), 0)
    blk_f = blk.astype(F32)
    qblk = qpos_row // SEL_BLOCK
    forced = (blk == 0) | (blk == qblk) | (blk == qblk - 1)
    future = blk > qblk
    exists = blk < n_sel
    n_eye = NSA_KV_HEADS * tq
    eye = (lax.broadcasted_iota(jnp.int32, (n_eye, n_eye), 0)
           == lax.broadcasted_iota(jnp.int32, (n_eye, n_eye), 1)).astype(BF16)
    c_idx = lax.broadcasted_iota(jnp.int32, (n_cmp_pad, 1), 0)
    c_end = c_idx * CMP_STRIDE + (CMP_BLOCK - 1)
    cmp_valid = (c_end <= qpos_row4) & (c_idx < n_cmp)

    w_start = pl.multiple_of(jnp.maximum(q0 - WINDOW, 0), tq)
    w_len = WINDOW + tq
    kpos_w = w_start + lax.broadcasted_iota(jnp.int32, (1, w_len), 1)
    dw = qpos_col4 - kpos_w
    win_valid = (dw >= 0) & (dw < WINDOW)

    n_kt = (q0 + tq + NSA_TK - 1) // NSA_TK
    kc = kc_ref[...]
    vc = vc_ref[...]
    covt = covt_ref[...]
    heads = range(NSA_KV_HEADS)
    nt_dims = (((1,), (1,)), ((), ()))
    owns = [lane < HEAD_DIM, lane >= HEAD_DIM]
    qpads = [[q_ref[:, (hk * g4 + g) * LANES:(hk * g4 + g + 1) * LANES] for g in range(g4)] for hk in heads]
    qpad4s = [jnp.concatenate(qpads[hk], axis=0) for hk in heads]

    s_ts = [lax.dot_general(kc, qpad4s[hk], nt_dims, preferred_element_type=F32) for hk in heads]
    p_ts = [_softmax_cols(s_t, cmp_valid) for s_t in s_ts]
    o_cmps = [jnp.dot(p_t.T.astype(BF16), vc, preferred_element_type=F32) for p_t in p_ts]
    p_sum = jnp.concatenate([sum(p_t[:, g * tq:(g + 1) * tq] for g in range(g4)) for p_t in p_ts], axis=1)
    imp_t = sum(jnp.dot(covt, part, preferred_element_type=F32) for part in _split3(p_sum))
    both = lambda m: jnp.concatenate([m] * NSA_KV_HEADS, axis=1)
    imp_t = jnp.where(both(forced), FORCE_SCORE, jnp.where(both(future), -1.0, imp_t))
    imp_t = jnp.where(exists, imp_t, -2.0)
    sel_t = _select_topk(imp_t, blk_f, topk)
    selneg = jnp.where(_transpose_01(sel_t, eye) > 0.5, 0.0, NEG_INF)
    lhs4s = []
    for hk in heads:
        sn = selneg[hk * tq:(hk + 1) * tq]
        selneg2 = jnp.concatenate([sn, sn], axis=1).astype(BF16)
        lhs4s.append(jnp.concatenate([jnp.where(owns[hk], qp, selneg2) for qp in qpads[hk]], axis=0))

    def kt_body(kt, carry, causal):
        ks = pl.multiple_of(kt * NSA_TK, NSA_TK)
        v = vsel_ref[pl.ds(ks, NSA_TK), :]
        heads = range(NSA_KV_HEADS)
        ss = [lax.dot_general(lhs4s[hk], kaug_ref[pl.ds(ks, NSA_TK), hk * LANES:(hk + 1) * LANES],
                              (((1,), (1,)), ((), ())), preferred_element_type=F32) for hk in heads]
        if causal:
            kpos = ks + lax.broadcasted_iota(jnp.int32, (1, NSA_TK), 1)
            ss = [jnp.where(kpos <= qpos_col4, s, NEG_INF) for s in ss]
        m_new = [jnp.maximum(carry[hk][0], jnp.max(ss[hk], axis=1, keepdims=True)) for hk in heads]
        alpha = [jnp.exp(carry[hk][0] - m_new[hk]) for hk in heads]
        ps = [jnp.exp(ss[hk] - m_new[hk]) for hk in heads]
        vs = [jnp.where(own_k[hk], v, 1.0) for hk in heads]
        pv = [jnp.dot(ps[hk].astype(BF16), vs[hk], preferred_element_type=F32) for hk in heads]
        return tuple((m_new[hk], alpha[hk] * carry[hk][1] + pv[hk]) for hk in heads)

    lane_k = lax.broadcasted_iota(jnp.int32, (NSA_TK, LANES), 1)
    own_k = [lane_k < HEAD_DIM, lane_k >= HEAD_DIM]
    init1 = (jnp.full((g4 * tq, 1), NEG_INF, F32), jnp.zeros((g4 * tq, LANES), F32))
    carry = lax.fori_loop(0, n_kt - 1, functools.partial(kt_body, causal=False), (init1,) * NSA_KV_HEADS)
    carry = kt_body(n_kt - 1, carry, causal=True)

    sum_lane = [HEAD_DIM, 0]
    o_sels = [carry[hk][1] / jnp.maximum(carry[hk][1][:, sum_lane[hk]:sum_lane[hk] + 1], 1e-30) for hk in heads]

    kw = kvw_ref[pl.ds(w_start, w_len), 0:LANES]
    vw = kvw_ref[pl.ds(w_start, w_len), LANES:2 * LANES]
    s_ws = [jnp.where(win_valid, lax.dot_general(qpad4s[hk], kw, nt_dims, preferred_element_type=F32), NEG_INF)
            for hk in heads]
    e_ws = [jnp.exp(s_w - jnp.max(s_w, axis=1, keepdims=True)) for s_w in s_ws]
    o_wins = [jnp.dot(e_w.astype(BF16), vw, preferred_element_type=F32)
              / jnp.maximum(jnp.sum(e_w, axis=1, keepdims=True), 1e-30) for e_w in e_ws]

    for hk in heads:
        def gate_col(j):
            cols = [gates[:, 3 * (hk * g4 + g) + j:3 * (hk * g4 + g) + j + 1] for g in range(g4)]
            return jnp.concatenate(cols, axis=0)

        o4 = gate_col(0) * o_cmps[hk] + gate_col(1) * o_sels[hk] + gate_col(2) * o_wins[hk]
        for pair in range(g4 // 2):
            a = o4[(2 * pair) * tq:(2 * pair + 1) * tq]
            b = o4[(2 * pair + 1) * tq:(2 * pair + 2) * tq]
            if hk == 0:
                both = jnp.where(lane < HEAD_DIM, a, pltpu.roll(b, HEAD_DIM, 1))
            else:
                both = jnp.where(lane < HEAD_DIM, pltpu.roll(a, HEAD_DIM, 1), b)
            col = (hk * (g4 // 2) + pair) * LANES
            o_ref[:, col:col + LANES] = both


def _nsa_prompt(qpad, gates, kaug, vsel, kvwin, kc, vc, covt, batch, t_len, n_cmp, n_sel, topk):
    n = qpad.shape[0]
    nq = t_len // NSA_TQ
    n_cmp_pad = kc.shape[0] // batch
    qrow = lambda w: pl.BlockSpec((NSA_TQ, w), lambda b, i: (b * nq + i, 0))
    per_b = lambda r, w: pl.BlockSpec((r, w), lambda b, i: (b, 0))
    return pl.pallas_call(
        functools.partial(_nsa_body, n_cmp=n_cmp, n_sel=n_sel, topk=topk),
        grid=(batch, nq),
        in_specs=[qrow(QPAD_WIDTH), qrow(LANES), per_b(t_len, 2 * LANES), per_b(t_len, LANES),
                  per_b(t_len, 2 * LANES), per_b(n_cmp_pad, LANES), per_b(n_cmp_pad, LANES),
                  _const_spec(covt.shape)],
        out_specs=qrow(NSA_WIDTH),
        out_shape=jax.ShapeDtypeStruct((n, NSA_WIDTH), F32),
        compiler_params=_cparams("parallel", "parallel"),
        name="nsa_prompt",
    )(qpad, gates, kaug, vsel, kvwin, kc, vc, covt)


def _group_features(load_l, lane):
    heads = ([], [])
    for m in range(CMP_STRIDE // 2):
        a = load_l(2 * m)
        b = load_l(2 * m + 1)
        heads[0].append(jnp.where(lane < HEAD_DIM, a, pltpu.roll(b, HEAD_DIM, 1)))
        heads[1].append(jnp.where(lane < HEAD_DIM, pltpu.roll(a, HEAD_DIM, 1), b))
    return [jnp.concatenate(h, axis=1) for h in heads]


def _compress_heads(x_heads, pe8, w1, w2_h0, w2_h1):
    g = x_heads[0].shape[0]
    x2 = jnp.concatenate(x_heads, axis=0).astype(BF16)
    z = jnp.dot(x2, w1, preferred_element_type=F32)
    zb = jnp.dot(pe8.astype(BF16), w1, preferred_element_type=F32)
    bias = zb[0:1, 0:CMP_HIDDEN] + zb[1:2, CMP_HIDDEN:]
    nxt = pltpu.roll(z[:, CMP_HIDDEN:], 2 * g - 1, 0)
    hid = jax.nn.gelu(z[:, 0:CMP_HIDDEN] + nxt + bias).astype(BF16)
    return (jnp.dot(hid[0:g], w2_h0, preferred_element_type=F32)
            + jnp.dot(hid[g:], w2_h1, preferred_element_type=F32))


def _compress_body(krows_ref, vrows_ref, pe_ref, w1_ref, w2_ref, c_ref, sm_ref, sp_ref, kc_ref, vc_ref):
    g = kc_ref.shape[0]
    lane = lax.broadcasted_iota(jnp.int32, (g, LANES), 1)
    outs = []
    for t, ref in enumerate((krows_ref, vrows_ref)):
        load = lambda l, ref=ref: ref[pl.ds(l, g, stride=CMP_STRIDE), :]
        outs.append(_compress_heads(_group_features(load, lane), pe_ref[t], w1_ref[t], w2_ref[t, 0], w2_ref[t, 1]))
    kc_ref[...] = _rope(outs[0], c_ref[...], sm_ref[...], sp_ref[...]).astype(BF16)
    vc_ref[...] = outs[1].astype(BF16)


def _compress_prompt(rows, pe8, w1c, w2p, tabs_c, l, batch, t_len):
    g = t_len // CMP_STRIDE
    out = jax.ShapeDtypeStruct((batch * g, LANES), BF16)
    return pl.pallas_call(
        _compress_body,
        grid=(batch,),
        in_specs=[pl.BlockSpec((t_len, LANES), lambda b: (b, 0)), pl.BlockSpec((t_len, LANES), lambda b: (b, 1)),
                  _layer_spec((2, 8, CMP_STRIDE * HEAD_DIM), l),
                  _layer_spec((2, CMP_STRIDE * HEAD_DIM, 2 * CMP_HIDDEN), l),
                  _layer_spec((2, 2, CMP_HIDDEN, LANES), l),
                  _const_spec((g, LANES)), _const_spec((g, LANES)), _const_spec((g, LANES))],
        out_specs=[pl.BlockSpec((g, LANES), lambda b: (b, 0))] * 2,
        out_shape=[out, out],
        compiler_params=_cparams("parallel"),
        name="compress",
    )(rows, rows, pe8, w1c, w2p, *tabs_c)


GDN_TT = 512
HALO = 8


def _dot3(a, x):
    return sum(jnp.dot(a, part, preferred_element_type=F32) for part in _split3(x))


def _dot3_nt(a, x):
    return sum(lax.dot_general(a, part, (((1,), (1,)), ((), ())), preferred_element_type=F32) for part in _split3(x))


def _l2n(x):
    return x * lax.rsqrt(jnp.sum(x * x, axis=-1, keepdims=True) + 1e-6)


def _softplus(x):
    return jnp.maximum(x, 0.0) + jnp.log(1.0 + jnp.exp(-jnp.abs(x)))


def _gdn_prep_body(x_ref, halo_ref, gt_ref, cw_ref, alog_ref, dtb_ref,
                   qg_ref, kd_ref, w_ref, u_ref, att_ref, egl_ref,
                   ext_ref, q_s, k_s, kb_s, rhs_s, gc_s):
    c_len = GDN_CHUNK
    tt = x_ref.shape[0]
    nch = tt // c_len
    qk_w = GDN_HEADS * GDN_DK
    first = pl.program_id(1) == 0
    ext_ref[0:HALO, :] = jnp.where(first, 0.0, halo_ref[...])
    ext_ref[HALO:HALO + tt, :] = x_ref[...]
    c = ext_ref[pl.ds(HALO - (CONV_W - 1), tt), :] * cw_ref[0:1, :]
    for j in range(1, CONV_W):
        c = c + ext_ref[pl.ds(HALO - (CONV_W - 1) + j, tt), :] * cw_ref[j:j + 1, :]
    c = c * jax.nn.sigmoid(c)

    gt = gt_ref[...]
    g_full = -jnp.exp(alog_ref[...]) * _softplus(gt + dtb_ref[...])
    beta_full = jax.nn.sigmoid(gt)
    r = lax.broadcasted_iota(jnp.int32, (tt, tt), 0)
    cc = lax.broadcasted_iota(jnp.int32, (tt, tt), 1)
    same = (r // c_len) == (cc // c_len)
    gcum = _dot3(jnp.where(same & (cc <= r), 1.0, 0.0).astype(BF16), g_full)
    gtot = _dot3(jnp.where(same, 1.0, 0.0).astype(BF16), g_full)
    gc_s[...] = gcum
    for h in range(GDN_HEADS):
        sl = slice(h * GDN_DK, (h + 1) * GDN_DK)
        gc_col = gcum[:, GATE_A + h:GATE_A + h + 1]
        gl_col = gtot[:, GATE_A + h:GATE_A + h + 1]
        beta_col = beta_full[:, GATE_B + h:GATE_B + h + 1]
        qh = _l2n(c[:, h * GDN_DK:(h + 1) * GDN_DK]) * (GDN_DK ** -0.5)
        kh = _l2n(c[:, qk_w + h * GDN_DK:qk_w + (h + 1) * GDN_DK])
        vh = c[:, 2 * qk_w + h * GDN_DV:2 * qk_w + (h + 1) * GDN_DV]
        eg = jnp.exp(gc_col)
        kb = kh * beta_col
        q_s[:, sl] = qh
        k_s[:, sl] = kh
        kb_s[:, sl] = kb
        rhs_s[:, 2 * h * GDN_DK:(2 * h + 1) * GDN_DK] = vh * beta_col
        rhs_s[:, (2 * h + 1) * GDN_DK:(2 * h + 2) * GDN_DK] = kb * eg
        qg_ref[:, sl] = (qh * eg).astype(BF16)
        kd_ref[:, sl] = (kh * jnp.exp(gl_col - gc_col)).astype(BF16)
        egl_b = jnp.broadcast_to(jnp.exp(gl_col), (tt, GDN_DK))
        for ch in range(nch):
            egl_ref[ch, :, sl] = egl_b[ch * c_len:ch * c_len + 1, :]

    nh = GDN_HEADS
    wide = nh * c_len
    wi = lax.broadcasted_iota(jnp.int32, (c_len, wide), 0)
    wj = lax.broadcasted_iota(jnp.int32, (c_len, wide), 1) & (c_len - 1)
    incl = wi >= wj
    strict = wi > wj
    eye = jnp.where(wi == wj, 1.0, 0.0)
    base_mask = (wi >> 1) == (wj >> 1)
    bi = lax.broadcasted_iota(jnp.int32, (wide, wide), 0)
    bj = lax.broadcasted_iota(jnp.int32, (wide, wide), 1)
    same_head = (bi // c_len) == (bj // c_len)
    bi_in = bi & (c_len - 1)
    bj_in = bj & (c_len - 1)
    level_masks = [same_head & ((bi_in >> (s + 1)) == (bj_in >> (s + 1))) & ((bi_in >> s) != (bj_in >> s))
                   for s in range(1, int(math.log2(c_len)))]
    head_of_row = lax.broadcasted_iota(jnp.int32, (wide, LANES), 0) // c_len
    lane_of = lax.broadcasted_iota(jnp.int32, (wide, LANES), 1)
    pick_rows = lane_of == GATE_A + head_of_row
    sel_k = lax.broadcasted_iota(jnp.int32, (LANES, wide), 0)
    sel_h = lax.broadcasted_iota(jnp.int32, (LANES, wide), 1) // c_len
    sel_cols = jnp.where(sel_k == GATE_A + sel_h, 1.0, 0.0).astype(BF16)
    ones_b = jnp.ones((c_len, LANES), BF16)
    key_head = (lax.broadcasted_iota(jnp.int32, (wide, GDN_WIDTH), 0) // c_len
                == lax.broadcasted_iota(jnp.int32, (wide, GDN_WIDTH), 1) // GDN_DK)
    rhs_head = (lax.broadcasted_iota(jnp.int32, (wide, 2 * GDN_WIDTH), 0) // c_len
                == lax.broadcasted_iota(jnp.int32, (wide, 2 * GDN_WIDTH), 1) // (2 * GDN_DK))
    nt = (((1,), (1,)), ((), ()))
    rep = lambda m: jnp.concatenate([m] * nh, axis=0)

    n_par = 4
    par = range(n_par)

    def chunk_group(cg, carry):
        rows = [pl.ds(pl.multiple_of((cg * n_par + c) * c_len, c_len), c_len) for c in par]
        gch = [gc_s[r, :] for r in rows]
        g_i = [sum(jnp.dot(part, sel_cols, preferred_element_type=F32) for part in _split3(g)) for g in gch]
        g_j = [_dot3_nt(ones_b, jnp.where(pick_rows, rep(g), 0.0)) for g in gch]
        decay = [jnp.where(incl, jnp.exp(jnp.where(incl, g_i[c] - g_j[c], 0.0)), 0.0) for c in par]
        k_bd = [jnp.where(key_head, rep(k_s[r, :]), 0.0).astype(BF16) for r in rows]
        qkb = [jnp.concatenate([kb_s[r, :], q_s[r, :]], axis=0).astype(BF16) for r in rows]
        prod = [lax.dot_general(qkb[c], k_bd[c], nt, preferred_element_type=F32) for c in par]
        a = [jnp.where(strict, prod[c][0:c_len] * decay[c], 0.0) for c in par]
        for c in par:
            att_ref[rows[c], :] = (prod[c][c_len:] * decay[c]).astype(BF16)
        a_rep = [rep(m) for m in a]
        x = [eye - jnp.where(base_mask, m, 0.0) for m in a]
        for lvl_mask in level_masks:
            off_bd = [jnp.where(lvl_mask, m, 0.0).astype(BF16) for m in a_rep]
            x_bd = [jnp.where(same_head, rep(m), 0.0).astype(BF16) for m in x]
            t = [jnp.dot(x[c].astype(BF16), off_bd[c], preferred_element_type=F32) for c in par]
            x = [x[c] - jnp.dot(t[c].astype(BF16), x_bd[c], preferred_element_type=F32) for c in par]
        rhs_bd = [jnp.where(rhs_head, rep(rhs_s[r, :]), 0.0).astype(BF16) for r in rows]
        uw = [jnp.dot(x[c].astype(BF16), rhs_bd[c], preferred_element_type=F32) for c in par]
        for c in par:
            for h in range(nh):
                sl = slice(h * GDN_DK, (h + 1) * GDN_DK)
                u_ref[rows[c], sl] = uw[c][:, 2 * h * GDN_DK:(2 * h + 1) * GDN_DK]
                w_ref[rows[c], sl] = uw[c][:, (2 * h + 1) * GDN_DK:(2 * h + 2) * GDN_DK].astype(BF16)
        return carry

    lax.fori_loop(0, nch // n_par, chunk_group, 0)


def _gdn_prep(qkv, gates, conv_w, alog_row, dtb_row, l, batch, t_len):
    n = qkv.shape[0]
    tt = GDN_TT
    nt = t_len // tt
    nch = tt // GDN_CHUNK
    row = lambda w: pl.BlockSpec((tt, w), lambda b, i: (b * nt + i, 0))
    outs = [(GDN_WIDTH, BF16), (GDN_WIDTH, BF16), (GDN_WIDTH, BF16), (GDN_WIDTH, F32)]
    return pl.pallas_call(
        _gdn_prep_body,
        grid=(batch, nt),
        in_specs=[row(GDN_QKV),
                  pl.BlockSpec((HALO, GDN_QKV), lambda b, i: (jnp.maximum((b * nt + i) * (tt // HALO) - 1, 0), 0)),
                  row(LANES), _layer_spec((CONV_W, GDN_QKV), l), _layer_spec((1, LANES), l), _layer_spec((1, LANES), l)],
        out_specs=[row(w) for w, _ in outs] + [
            pl.BlockSpec((tt, GDN_HEADS * GDN_CHUNK), lambda b, i: (b * nt + i, 0)),
            pl.BlockSpec((nch, 1, GDN_WIDTH), lambda b, i: (b * nt + i, 0, 0))],
        out_shape=[jax.ShapeDtypeStruct((n, w), dt) for w, dt in outs] + [
            jax.ShapeDtypeStruct((n, GDN_HEADS * GDN_CHUNK), BF16),
            jax.ShapeDtypeStruct((n // GDN_CHUNK, 1, GDN_WIDTH), F32)],
        scratch_shapes=[pltpu.VMEM((HALO + tt, GDN_QKV), F32), pltpu.VMEM((tt, GDN_WIDTH), F32),
                        pltpu.VMEM((tt, GDN_WIDTH), F32), pltpu.VMEM((tt, GDN_WIDTH), F32),
                        pltpu.VMEM((tt, 2 * GDN_WIDTH), F32), pltpu.VMEM((tt, LANES), F32)],
        compiler_params=_cparams("parallel", "parallel"),
        name="gdn_prep",
    )(qkv, qkv, gates, conv_w, alog_row, dtb_row)


def _gdn_scan_body(qg_ref, kd_ref, w_ref, u_ref, att_ref, egl_ref, z_ref, gn_ref, o_ref, sfin_ref, s_ref):
    c_len = GDN_CHUNK
    tt = qg_ref.shape[0]
    i = pl.program_id(1)

    @pl.when(i == 0)
    def _():
        s_ref[...] = jnp.zeros(s_ref.shape, F32)

    gn = gn_ref[...]
    tn = (((0,), (0,)), ((), ()))

    def chunk(ch, carry):
        r0 = pl.multiple_of(ch * c_len, c_len)
        heads = range(GDN_HEADS)
        sls = [slice(h * GDN_DK, (h + 1) * GDN_DK) for h in heads]
        rows = pl.ds(r0, c_len)
        ss = [s_ref[h] for h in heads]
        sbs = [s.astype(BF16) for s in ss]
        wq = [jnp.dot(jnp.concatenate([w_ref[rows, sls[h]], qg_ref[rows, sls[h]]], axis=0), sbs[h],
                      preferred_element_type=F32) for h in heads]
        vbs = [(u_ref[rows, sls[h]] - wq[h][0:c_len]).astype(BF16) for h in heads]
        av = [jnp.dot(att_ref[rows, h * c_len:(h + 1) * c_len], vbs[h], preferred_element_type=F32) for h in heads]
        kv = [lax.dot_general(kd_ref[rows, sls[h]], vbs[h], tn, preferred_element_type=F32) for h in heads]
        for h in heads:
            s_ref[h] = ss[h] * egl_ref[ch, :, sls[h]] + kv[h]
            o = wq[h][c_len:] + av[h]
            y = o * lax.rsqrt(jnp.mean(o * o, axis=-1, keepdims=True) + RMS_EPS) * gn
            zz = z_ref[rows, sls[h]]
            o_ref[rows, sls[h]] = y * (zz * jax.nn.sigmoid(zz))
        return carry

    lax.fori_loop(0, tt // c_len, chunk, 0)

    @pl.when(i == pl.num_programs(1) - 1)
    def _():
        sfin_ref[...] = s_ref[...]


def _gdn_scan(qg, kd, w, u, att, egl, z, gnorm, l, batch, t_len):
    n = qg.shape[0]
    tt = GDN_TT
    nt = t_len // tt
    nch = tt // GDN_CHUNK
    row = lambda: pl.BlockSpec((tt, GDN_WIDTH), lambda b, i: (b * nt + i, 0))
    return pl.pallas_call(
        _gdn_scan_body,
        grid=(batch, nt),
        in_specs=[row(), row(), row(), row(),
                  pl.BlockSpec((tt, GDN_HEADS * GDN_CHUNK), lambda b, i: (b * nt + i, 0)),
                  pl.BlockSpec((nch, 1, GDN_WIDTH), lambda b, i: (b * nt + i, 0, 0)),
                  row(), _layer_spec((1, GDN_DV), l)],
        out_specs=[row(), pl.BlockSpec((None, GDN_HEADS, GDN_DK, GDN_DV), lambda b, i: (b, 0, 0, 0))],
        out_shape=[jax.ShapeDtypeStruct((n, GDN_WIDTH), F32),
                   jax.ShapeDtypeStruct((batch, GDN_HEADS, GDN_DK, GDN_DV), F32)],
        scratch_shapes=[pltpu.VMEM((GDN_HEADS, GDN_DK, GDN_DV), F32)],
        compiler_params=_cparams("parallel", "arbitrary"),
        name="gdn_scan",
    )(qg, kd, w, u, att, egl, z, gnorm)


def _softmax_rows_extra(s, valid, s_new, valid_new):
    s = jnp.where(valid, s, NEG_INF)
    s_new = jnp.where(valid_new, s_new, NEG_INF)
    m = jnp.maximum(jnp.max(s, axis=1, keepdims=True), s_new)
    e = jnp.where(valid, jnp.exp(s - m), 0.0)
    e_new = jnp.where(valid_new, jnp.exp(s_new - m), 0.0)
    den = jnp.maximum(jnp.sum(e, axis=1, keepdims=True) + e_new, 1e-30)
    return e, e_new, 1.0 / den


NSA_SPS = 4


def _round_robin(gens):
    gens = list(gens)
    while gens:
        alive = []
        for gen in gens:
            try:
                next(gen)
                alive.append(gen)
            except StopIteration:
                pass
        gens = alive


def _nsa_sample_body(pt_ref, *refs, n_pages, **static):
    del pt_ref
    per = 3 * n_pages
    (q_ref, gt_ref, rows_ref, wnew_ref, wbuf_ref, pe_ref, w1_ref, w2_ref, c_ref, sm_ref, sp_ref,
     cov_ref, eexp_ref, o_ref) = refs[NSA_SPS * per:]
    gens = []
    for s in range(NSA_SPS):
        pages = refs[s * per:(s + 1) * per]
        gens.append(_nsa_sample_one(
            pages[0:n_pages], pages[n_pages:2 * n_pages], pages[2 * n_pages:],
            q_ref.at[s], gt_ref.at[s], rows_ref.at[s], wnew_ref.at[s], wbuf_ref.at[s],
            pe_ref, w1_ref, w2_ref, c_ref, sm_ref, sp_ref, cov_ref, eexp_ref, o_ref.at[s],
            n_pages=n_pages, **static))
    _round_robin(gens)


def _nsa_sample_one(kcmp_pages, vcmp_pages, sel_pages, q_ref, gt_ref, rows_ref, wnew_ref, wbuf_ref,
                    pe_ref, w1_ref, w2_ref, c_ref, sm_ref, sp_ref, cov_ref, eexp_ref, o_ref,
                    *, n_pages, page, n_buf, n_cmp, n_sel, topk):
    past = n_pages * page
    qpos = past
    gpp = page // CMP_STRIDE
    g = n_pages * gpp
    nt = (((1,), (1,)), ((), ()))
    nh = NSA_HEADS
    lane_g = lax.broadcasted_iota(jnp.int32, (g, LANES), 1)
    pr_i = lax.broadcasted_iota(jnp.int32, (page, page), 0)
    pt_i = lax.broadcasted_iota(jnp.int32, (page, page), 1)
    perm = jnp.where(pt_i == (pr_i % gpp) * CMP_STRIDE + pr_i // gpp, 1.0, 0.0).astype(BF16)
    qf = q_ref[...]
    qp = qf.astype(BF16)
    row8 = lax.broadcasted_iota(jnp.int32, (nh, 1), 0)

    outs = []
    for t, pages in enumerate((kcmp_pages, vcmp_pages)):
        xs = [lax.dot_general(perm, pr[...].astype(BF16), nt, preferred_element_type=F32) for pr in pages]
        load = lambda l, xs=xs: jnp.concatenate([x[l * gpp:(l + 1) * gpp, :] for x in xs], axis=0)
        outs.append(_compress_heads(_group_features(load, lane_g), pe_ref[t], w1_ref[t], w2_ref[t, 0], w2_ref[t, 1]))
        yield
    kc = _rope(outs[0], c_ref[...], sm_ref[...], sp_ref[...]).astype(BF16)
    vc = outs[1].astype(BF16)
    yield

    c_row = lax.broadcasted_iota(jnp.int32, (1, g), 1)
    cv_row = (c_row * CMP_STRIDE + (CMP_BLOCK - 1) <= qpos) & (c_row < n_cmp)
    s_c = lax.dot_general(qp, kc, nt, preferred_element_type=F32)
    yield
    p_c = _softmax_rows(s_c, cv_row)
    o_cmp = jnp.dot(p_c.astype(BF16), vc, preferred_element_type=F32)
    yield
    g0 = jnp.sum(p_c[0:NSA_GROUP], axis=0, keepdims=True)
    g1 = jnp.sum(p_c[NSA_GROUP:], axis=0, keepdims=True)
    psum8 = jnp.where(row8 == 0, g0, jnp.where(row8 == 1, g1, 0.0))
    imp8 = sum(jnp.dot(part, cov_ref[...], preferred_element_type=F32) for part in _split3(psum8))
    yield
    n_blk = imp8.shape[1]
    ib = lax.broadcasted_iota(jnp.int32, (n_blk, n_blk), 0)
    jb = lax.broadcasted_iota(jnp.int32, (n_blk, n_blk), 1)
    eye_b = jnp.where(ib == jb, 1.0, 0.0).astype(BF16)
    imp_t = _dot3_nt(eye_b, imp8)
    yield
    blk = lax.broadcasted_iota(jnp.int32, (n_blk, 1), 0)
    qblk = qpos // SEL_BLOCK
    forced = (blk == 0) | (blk == qblk) | (blk == qblk - 1)
    imp_t = jnp.where(forced, FORCE_SCORE, jnp.where(blk > qblk, -1.0, imp_t))
    imp_t = jnp.where(blk < n_sel, imp_t, -2.0)
    sel_t = _select_topk(imp_t, blk.astype(F32), topk)
    yield
    sel8 = lax.dot_general(sel_t.astype(BF16), eye_b, (((0,), (0,)), ((), ())), preferred_element_type=F32)
    yield
    head_sel = jnp.where(row8 < NSA_GROUP, sel8[0:1], sel8[1:2])

    key_sel = jnp.dot(head_sel.astype(BF16), eexp_ref[...], preferred_element_type=F32) > 0.5
    s_parts = [jnp.dot(qp, pr[0].astype(BF16), preferred_element_type=F32) for pr in sel_pages]
    s_sel = jnp.concatenate(s_parts, axis=1)
    yield
    kpos = lax.broadcasted_iota(jnp.int32, (1, past), 1)
    new_row = rows_ref[...]
    k_new = new_row[:, 2 * LANES:3 * LANES].astype(BF16).astype(F32)
    v_new = new_row[:, 3 * LANES:4 * LANES].astype(BF16).astype(F32)
    s_new = jnp.sum(qf * k_new, axis=1, keepdims=True)
    new_sel = head_sel[:, qblk:qblk + 1] > 0.5
    e, e_new, inv = _softmax_rows_extra(s_sel, key_sel & (kpos <= qpos), s_new, new_sel)
    acc = e_new * v_new
    for p_i, pr in enumerate(sel_pages):
        acc = acc + lax.dot_general(e[:, p_i * page:(p_i + 1) * page].astype(BF16), pr[1].astype(BF16), nt,
                                    preferred_element_type=F32)
    o_sel = acc * inv
    yield

    kw = wbuf_ref[0].astype(BF16)
    vw = wbuf_ref[1].astype(BF16)
    s_w = jnp.dot(qp, kw, preferred_element_type=F32)
    yield
    kpos_w = (past - n_buf) + lax.broadcasted_iota(jnp.int32, (1, n_buf), 1)
    dw = qpos - kpos_w
    w_new = wnew_ref[...]
    kw_new = w_new[:, 0:LANES].astype(BF16).astype(F32)
    vw_new = w_new[:, LANES:2 * LANES].astype(BF16).astype(F32)
    sw_new = jnp.sum(qf * kw_new, axis=1, keepdims=True)
    ew, ew_new, inv_w = _softmax_rows_extra(s_w, (dw >= 0) & (dw < WINDOW) & (kpos_w >= 0), sw_new, row8 >= 0)
    o_win = (lax.dot_general(ew.astype(BF16), vw, nt, preferred_element_type=F32) + ew_new * vw_new) * inv_w
    yield

    gates = jax.nn.sigmoid(gt_ref[...])
    lane8 = lax.broadcasted_iota(jnp.int32, (nh, LANES), 1)
    gcol = lambda j: jnp.sum(jnp.where(lane8 == 3 * row8 + j, gates, 0.0), axis=1, keepdims=True)
    o8 = gcol(0) * o_cmp + gcol(1) * o_sel + gcol(2) * o_win
    lane1 = lax.broadcasted_iota(jnp.int32, (1, LANES), 1)
    for pair in range(nh // 2):
        a = o8[2 * pair:2 * pair + 1]
        b = o8[2 * pair + 1:2 * pair + 2]
        if 2 * pair < NSA_GROUP:
            both = jnp.where(lane1 < HEAD_DIM, a, pltpu.roll(b, HEAD_DIM, 1))
        else:
            both = jnp.where(lane1 < HEAD_DIM, pltpu.roll(a, HEAD_DIM, 1), b)
        o_ref[:, pair * LANES:(pair + 1) * LANES] = both


def _nsa_sample(page_table, cache_t, q3, gates3, rows3, wnew3, win_cache, pe8, w1c, w2p, tabs_c, cov_s,
                eexp, l, n_cmp, n_sel, topk):
    bs, n_pages = page_table.shape
    page = cache_t.shape[4]
    n_buf = win_cache.shape[4]
    g = n_pages * page // CMP_STRIDE
    sps = NSA_SPS
    page_spec = lambda s, p, nty, ty: pl.BlockSpec(
        (None, None, nty, LANES, page), lambda b, pt, s=s, p=p: (l, pt[b * sps + s, p], ty, 0, 0))
    per_b = lambda *shape: pl.BlockSpec((sps,) + shape, lambda b, pt: (b,) + (0,) * len(shape))
    cst = lambda shape: pl.BlockSpec(shape, lambda b, pt: (0,) * len(shape))
    lyr = lambda shape: pl.BlockSpec((None,) + shape, lambda b, pt: (l,) + (0,) * len(shape))
    in_specs, args = [], []
    for s in range(sps):
        in_specs += ([page_spec(s, p, None, 0) for p in range(n_pages)]
                     + [page_spec(s, p, None, 1) for p in range(n_pages)]
                     + [page_spec(s, p, 2, 1) for p in range(n_pages)])
        args += [cache_t] * (3 * n_pages)
    in_specs += [per_b(NSA_HEADS, LANES), per_b(1, LANES), per_b(1, 4 * LANES), per_b(1, 2 * LANES),
                 pl.BlockSpec((None, sps, 2, LANES, n_buf), lambda b, pt: (l, b, 0, 0, 0)),
                 lyr((2, 8, CMP_STRIDE * HEAD_DIM)), lyr((2, CMP_STRIDE * HEAD_DIM, 2 * CMP_HIDDEN)),
                 lyr((2, 2, CMP_HIDDEN, LANES)), cst((g, LANES)), cst((g, LANES)), cst((g, LANES)),
                 cst(cov_s.shape), cst(eexp.shape)]
    args += [q3, gates3, rows3, wnew3, win_cache, pe8, w1c, w2p, *tabs_c, cov_s, eexp]
    grid_spec = pltpu.PrefetchScalarGridSpec(
        num_scalar_prefetch=1, grid=(bs // sps,), in_specs=in_specs,
        out_specs=pl.BlockSpec((sps, 1, NSA_WIDTH), lambda b, pt: (b, 0, 0)))
    return pl.pallas_call(
        functools.partial(_nsa_sample_body, n_pages=n_pages, page=page, n_buf=n_buf, n_cmp=n_cmp, n_sel=n_sel,
                          topk=topk),
        grid_spec=grid_spec,
        out_shape=jax.ShapeDtypeStruct((bs, 1, NSA_WIDTH), F32),
        compiler_params=_cparams("parallel"),
        name="nsa_sample",
    )(page_table, *args)


GDN_SB = 8


def _transpose3(x, eye):
    return _dot3_nt(eye, x)


def _gdn_sample_body(x_ref, cs_ref, gt_ref, z_ref, s_ref, cw_ref, alog_ref, dtb_ref, gn_ref, o_ref, so_ref):
    sb = x_ref.shape[0]
    qk_w = GDN_HEADS * GDN_DK
    c = x_ref[...] * cw_ref[CONV_W - 1:CONV_W, :]
    for j in range(CONV_W - 1):
        c = c + cs_ref[:, j * GDN_QKV:(j + 1) * GDN_QKV] * cw_ref[j:j + 1, :]
    c = c * jax.nn.sigmoid(c)
    gt = gt_ref[...]
    g_full = -jnp.exp(alog_ref[...]) * _softplus(gt + dtb_ref[...])
    eg_full = jnp.exp(g_full)
    beta_full = jax.nn.sigmoid(gt)
    ii = lax.broadcasted_iota(jnp.int32, (LANES, LANES), 0)
    jj = lax.broadcasted_iota(jnp.int32, (LANES, LANES), 1)
    eye = jnp.where(ii == jj, 1.0, 0.0).astype(BF16)
    gn = gn_ref[...]
    for h in range(GDN_HEADS):
        sl = slice(h * GDN_DK, (h + 1) * GDN_DK)
        q = _l2n(c[:, h * GDN_DK:(h + 1) * GDN_DK]) * (GDN_DK ** -0.5)
        k = _l2n(c[:, qk_w + h * GDN_DK:qk_w + (h + 1) * GDN_DK])
        v = c[:, 2 * qk_w + h * GDN_DV:2 * qk_w + (h + 1) * GDN_DV]
        eg = jnp.broadcast_to(eg_full[:, GATE_A + h:GATE_A + h + 1], (sb, GDN_DK))
        beta = jnp.broadcast_to(beta_full[:, GATE_B + h:GATE_B + h + 1], (sb, GDN_DK))
        att = jnp.sum(q * k, axis=1, keepdims=True)
        k_t = _transpose3(k, eye)
        w_t = _transpose3(k * beta * eg, eye)
        qg_t = _transpose3(q * eg, eye)
        u = v * beta
        zz = z_ref[:, sl]
        for i in range(sb):
            s = s_ref[i, h]
            v_new = u[i:i + 1] - jnp.sum(w_t[:, i:i + 1] * s, axis=0, keepdims=True)
            o = jnp.sum(qg_t[:, i:i + 1] * s, axis=0, keepdims=True) + att[i:i + 1] * v_new
            so_ref[i, h] = s * eg[i:i + 1] + k_t[:, i:i + 1] * v_new
            y = o * lax.rsqrt(jnp.mean(o * o, axis=-1, keepdims=True) + RMS_EPS) * gn
            zi = zz[i:i + 1]
            o_ref[i:i + 1, sl] = y * (zi * jax.nn.sigmoid(zi))


def _gdn_sample(qkv, conv_state, gates, z, s_state, conv_w, alog_row, dtb_row, gnorm, l):
    bs = qkv.shape[0]
    sb = GDN_SB
    row = lambda w: pl.BlockSpec((sb, w), lambda i: (i, 0))
    st = pl.BlockSpec((None, sb, GDN_HEADS, GDN_DK, GDN_DV), lambda i: (l, i, 0, 0, 0))
    return pl.pallas_call(
        _gdn_sample_body,
        grid=(bs // sb,),
        in_specs=[row(GDN_QKV), pl.BlockSpec((None, sb, (CONV_W - 1) * GDN_QKV), lambda i: (l, i, 0)),
                  row(LANES), row(GDN_WIDTH), st,
                  _layer_spec((CONV_W, GDN_QKV), l), _layer_spec((1, LANES), l), _layer_spec((1, LANES), l),
                  _layer_spec((1, GDN_DV), l)],
        out_specs=[row(GDN_WIDTH), pl.BlockSpec((sb, GDN_HEADS, GDN_DK, GDN_DV), lambda i: (i, 0, 0, 0))],
        out_shape=[jax.ShapeDtypeStruct((bs, GDN_WIDTH), F32),
                   jax.ShapeDtypeStruct((bs, GDN_HEADS, GDN_DK, GDN_DV), F32)],
        compiler_params=_cparams("parallel"),
        name="gdn_sample",
    )(qkv, conv_state, gates, z, s_state, conv_w, alog_row, dtb_row, gnorm)


def _mem_sample_body(q_ref, kv_ref, o_ref):
    nt = (((1,), (1,)), ((), ()))
    q = q_ref[...] * (MEM_HD ** -0.5)
    row = lax.broadcasted_iota(jnp.int32, (8, MEM_WIDTH), 0)
    lane = lax.broadcasted_iota(jnp.int32, (8, MEM_WIDTH), 1)
    own = (lane // MEM_HD) == row
    qm = jnp.where(own, q, 0.0).astype(BF16)
    k_t = kv_ref[0].astype(BF16)
    v_t = kv_ref[1].astype(BF16)
    s = jnp.dot(qm, k_t, preferred_element_type=F32)
    e = jnp.exp(s - jnp.max(s, axis=-1, keepdims=True))
    p = e / jnp.sum(e, axis=-1, keepdims=True)
    o = lax.dot_general(p.astype(BF16), v_t, nt, preferred_element_type=F32)
    o_ref[...] = jnp.sum(jnp.where(own, o, 0.0), axis=0, keepdims=True)


def _mem_sample(q3, mem_cache, l):
    bs = q3.shape[0]
    m = mem_cache.shape[4]
    return pl.pallas_call(
        _mem_sample_body,
        grid=(bs,),
        in_specs=[pl.BlockSpec((None, 1, MEM_WIDTH), lambda b: (b, 0, 0)),
                  pl.BlockSpec((None, None, 2, MEM_WIDTH, m), lambda b: (l, b, 0, 0, 0))],
        out_specs=pl.BlockSpec((None, 1, MEM_WIDTH), lambda b: (b, 0, 0)),
        out_shape=jax.ShapeDtypeStruct((bs, 1, MEM_WIDTH), F32),
        compiler_params=_cparams("parallel"),
        name="mem_sample",
    )(q3, mem_cache)


def _rope_tables(pos):
    half = ROPE_DIM // 2
    inv = ROPE_THETA ** (-2.0 * jnp.arange(half, dtype=F32) / ROPE_DIM)
    ang = pos.astype(F32)[:, None] * inv[None, :]
    cos, sin = jnp.cos(ang), jnp.sin(ang)
    n = pos.shape[0]
    one = jnp.ones((n, HEAD_DIM - ROPE_DIM), F32)
    zero = jnp.zeros((n, HEAD_DIM - ROPE_DIM), F32)
    zh = jnp.zeros((n, half), F32)
    c = jnp.concatenate([cos, cos, one], axis=1)
    sm = jnp.concatenate([-sin, zh, zero], axis=1)
    sp = jnp.concatenate([zh, sin, zero], axis=1)
    return tuple(jnp.concatenate([t, t], axis=1) for t in (c, sm, sp))


def _block_onehot(pos):
    e = (pos[:, None] // SEL_BLOCK == jnp.arange(HEAD_DIM)[None, :]).astype(F32)
    return jnp.concatenate([e, e], axis=1)


def _rope_jnp(x, pos):
    half = ROPE_DIM // 2
    inv = ROPE_THETA ** (-2.0 * jnp.arange(half, dtype=F32) / ROPE_DIM)
    ang = pos.astype(F32)[:, None] * inv[None, :]
    cos = jnp.cos(ang)[None, :, None, :]
    sin = jnp.sin(ang)[None, :, None, :]
    x1 = x[..., :half]
    x2 = x[..., half:ROPE_DIM]
    return jnp.concatenate([x1 * cos - x2 * sin, x2 * cos + x1 * sin, x[..., ROPE_DIM:]], axis=-1)


def _masked_softmax(s, mask):
    s = jnp.where(mask, s, NEG_INF)
    m = jnp.max(s, axis=-1, keepdims=True)
    e = jnp.where(mask, jnp.exp(s - m), 0.0)
    return e / jnp.maximum(jnp.sum(e, axis=-1, keepdims=True), 1e-30)


def _compress_jnp(k, pe, w1, w2):
    b, t, hk, dh = k.shape
    n_cmp = (t - CMP_BLOCK) // CMP_STRIDE + 1
    start = jnp.arange(n_cmp) * CMP_STRIDE
    idx = start[:, None] + jnp.arange(CMP_BLOCK)[None, :]
    blk = k[:, idx] + pe[None, None, :, None, :]
    blk = jnp.swapaxes(blk, 2, 3).reshape(b, n_cmp, hk, CMP_BLOCK * dh)
    return jax.nn.gelu(blk @ w1) @ w2, start + CMP_BLOCK - 1


def _block_coverage(c_start, n_sel):
    b_start = jnp.arange(n_sel) * SEL_BLOCK
    lo = jnp.maximum(c_start[:, None], b_start[None, :])
    hi = jnp.minimum(c_start[:, None] + CMP_BLOCK, b_start[None, :] + SEL_BLOCK)
    return jnp.clip(hi - lo, 0, None).astype(F32) / CMP_BLOCK


def _sample_nsa_jnp(q, gate_logits, rows_new, win_new, nsa_past, win_buf, w_buf, cmp_pe, w_cmp1, w_cmp2, topk):
    b, t = q.shape[:2]
    pos0 = nsa_past.shape[1]
    qpos = pos0 + jnp.arange(t)
    q = q.reshape(b, t, NSA_KV_HEADS, NSA_GROUP, HEAD_DIM)
    rows = jnp.concatenate([nsa_past, rows_new], axis=1)
    t_kv = rows.shape[1]
    kc, c_end = _compress_jnp(rows[:, :, 0], cmp_pe[0], w_cmp1[0], w_cmp2[0])
    vc, _ = _compress_jnp(rows[:, :, 1], cmp_pe[1], w_cmp1[1], w_cmp2[1])
    kc = _rope_jnp(kc, c_end)
    n_sel = -(-t_kv // SEL_BLOCK)
    cov = _block_coverage(c_end - CMP_BLOCK + 1, n_sel)
    sel = jnp.pad(rows[:, :, 2:], ((0, 0), (0, n_sel * SEL_BLOCK - t_kv), (0, 0), (0, 0), (0, 0)))
    sel = sel.reshape(b, n_sel, SEL_BLOCK, 2, NSA_KV_HEADS, HEAD_DIM).transpose(3, 0, 4, 1, 2, 5)
    k_blk, v_blk = sel[0], sel[1]
    win_all = jnp.concatenate([win_buf, win_new], axis=1)
    n_buf = win_buf.shape[1]
    gates = jax.nn.sigmoid(gate_logits).reshape(b, t, NSA_KV_HEADS, NSA_GROUP, 3)
    s = jnp.einsum('bqhgd,bchd->bhgqc', q, kc) * ATTN_SCALE
    p = _masked_softmax(s, c_end[None, :] <= qpos[:, None])
    o_cmp = jnp.einsum('bhgqc,bchd->bqhgd', p, vc)
    imp = jnp.einsum('bhgqc,cn->bhqn', p, cov)
    blk = jnp.arange(n_sel)[None, :]
    q_blk = (qpos // SEL_BLOCK)[:, None]
    forced = (blk == 0) | (blk == q_blk) | (blk == q_blk - 1)
    imp = jnp.where(forced, FORCE_SCORE, jnp.where(blk > q_blk, -1.0, imp))
    _, idx = lax.top_k(imp, min(topk, n_sel))
    bi = jnp.arange(b)[:, None, None, None]
    hi = jnp.arange(NSA_KV_HEADS)[None, :, None, None]
    kg = k_blk[bi, hi, idx]
    vg = v_blk[bi, hi, idx]
    k_pos = idx[..., None] * SEL_BLOCK + jnp.arange(SEL_BLOCK)
    mask = (k_pos <= qpos[None, None, :, None, None])[:, :, None]
    s2 = jnp.einsum('bqhgd,bhqksd->bhgqks', q, kg) * ATTN_SCALE
    shp = s2.shape
    p2 = _masked_softmax(s2.reshape(shp[:-2] + (-1,)), mask.reshape(mask.shape[:-2] + (-1,))).reshape(shp)
    o_sel = jnp.einsum('bhgqks,bhqksd->bqhgd', p2, vg)
    kw_pos = pos0 - n_buf + jnp.arange(n_buf + t)
    sw = jnp.einsum('bqhgd,bkhd->bhgqk', q, win_all[:, :, 0]) * ATTN_SCALE
    diff = qpos[:, None] - kw_pos[None, :]
    pw = _masked_softmax(sw, (diff >= 0) & (diff < WINDOW) & (kw_pos[None, :] >= 0))
    o_win = jnp.einsum('bhgqk,bkhd->bqhgd', pw, win_all[:, :, 1])
    o = gates[..., 0:1] * o_cmp + gates[..., 1:2] * o_sel + gates[..., 2:3] * o_win
    return o.reshape(b, t, NSA_WIDTH), win_all[:, win_all.shape[1] - w_buf:]


def _to_chunks(a, c, pad):
    a = jnp.moveaxis(a, 1, 2)
    if pad:
        a = jnp.pad(a, [(0, 0), (0, 0), (0, pad)] + [(0, 0)] * (a.ndim - 3))
    b, h, tp = a.shape[:3]
    a = a.reshape((b, h, tp // c, c) + a.shape[3:])
    return jnp.moveaxis(a, 2, 0)


def _chunk_gated_delta_jnp(q, k, v, g, beta, s0):
    b, t, h, _ = q.shape
    c = min(GDN_CHUNK, t)
    pad = (-t) % c
    qc, kc, vc = _to_chunks(q, c, pad), _to_chunks(k, c, pad), _to_chunks(v, c, pad)
    gc = jnp.cumsum(_to_chunks(g, c, pad), axis=-1)
    bc = _to_chunks(beta, c, pad)
    ii = jnp.arange(c)
    incl = ii[:, None] >= ii[None, :]
    strict = ii[:, None] > ii[None, :]
    diff = gc[..., :, None] - gc[..., None, :]
    decay = jnp.where(incl, jnp.exp(jnp.where(incl, diff, 0.0)), 0.0)
    kb = kc * bc[..., None]
    a = jnp.where(strict, jnp.einsum('...id,...jd->...ij', kb, kc) * decay, 0.0)
    eye = jnp.eye(c, dtype=F32)
    tm = lax.linalg.triangular_solve(eye + a, jnp.broadcast_to(eye, a.shape), left_side=True, lower=True)
    u = jnp.einsum('...ij,...jd->...id', tm, vc * bc[..., None])
    w = jnp.einsum('...ij,...jd->...id', tm, kb * jnp.exp(gc)[..., None])

    def step(s, inp):
        qi, ki, ui, wi, gi, di = inp
        v_new = ui - jnp.einsum('bhck,bhkv->bhcv', wi, s)
        att = jnp.einsum('bhik,bhjk->bhij', qi, ki) * di
        o = jnp.einsum('bhck,bhkv->bhcv', qi * jnp.exp(gi)[..., None], s) + jnp.einsum('bhij,bhjv->bhiv', att, v_new)
        gl = gi[..., -1]
        s = s * jnp.exp(gl)[..., None, None] + jnp.einsum(
            'bhck,bhcv->bhkv', ki * jnp.exp(gl[..., None] - gi)[..., None], v_new)
        return s, o

    s, o = lax.scan(step, s0, (qc, kc, u, w, gc, decay))
    o = jnp.moveaxis(o, 0, 2).reshape(b, h, -1, o.shape[-1])[:, :, :t]
    return jnp.moveaxis(o, 1, 2), s


def _gdn_jnp(qkv, a, bb, z, conv_buf, state, conv_w, a_log, dt_bias, norm_w):
    b, t, _ = qkv.shape
    xp = jnp.concatenate([conv_buf, qkv], axis=1)
    c = xp[:, 0:t] * conv_w[0]
    for j in range(1, CONV_W):
        c = c + xp[:, j:j + t] * conv_w[j]
    c = jax.nn.silu(c)
    new_buf = xp[:, t:]
    qk_w = GDN_HEADS * GDN_DK
    l2 = lambda x: x * lax.rsqrt(jnp.sum(x * x, axis=-1, keepdims=True) + 1e-6)
    q = l2(c[..., :qk_w].reshape(b, t, GDN_HEADS, GDN_DK)) * GDN_DK ** -0.5
    k = l2(c[..., qk_w:2 * qk_w].reshape(b, t, GDN_HEADS, GDN_DK))
    v = c[..., 2 * qk_w:].reshape(b, t, GDN_HEADS, GDN_DV)
    g = -jnp.exp(a_log) * jax.nn.softplus(a + dt_bias)
    beta = jax.nn.sigmoid(bb)
    o, new_state = _chunk_gated_delta_jnp(q, k, v, g, beta, state)
    o = o * lax.rsqrt(jnp.mean(o * o, axis=-1, keepdims=True) + RMS_EPS) * norm_w
    o = o * jax.nn.silu(z).reshape(b, t, GDN_HEADS, GDN_DV)
    return o.reshape(b, t, GDN_WIDTH), new_buf, new_state


def _mem_attend_sample_jnp(x, gains, mem_kv, w_q, w_o):
    xf = x
    h = xf * lax.rsqrt(jnp.mean(xf * xf, axis=-1, keepdims=True) + RMS_EPS) * gains[4]
    q = (h @ w_q).reshape(-1, MEM_HEADS, MEM_HD)
    s = jnp.einsum('bhd,bmhd->bhm', q, mem_kv[:, :, 0]) * MEM_HD ** -0.5
    p = jax.nn.softmax(s, axis=-1)
    o = jnp.einsum('bhm,bmhd->bhd', p, mem_kv[:, :, 1]).reshape(-1, MEM_WIDTH)
    y = o @ w_o
    return x + y * lax.rsqrt(jnp.mean(y * y, axis=-1, keepdims=True) + RMS_EPS) * gains[5]


def _pack_w_in(w_in):
    nl, d, _ = w_in.shape
    o_q, o_kv, o_g = NSA_WIDTH, NSA_WIDTH + KV6_WIDTH, NSA_WIDTH + KV6_WIDTH + 3 * NSA_HEADS
    o_qkv = o_g
    o_a = o_qkv + GDN_QKV
    o_b = o_a + GDN_HEADS
    o_z = o_b + GDN_HEADS
    wq = w_in[:, :, :o_q].reshape(nl, d, NSA_HEADS, HEAD_DIM)
    zq = jnp.zeros_like(wq)
    first = jnp.concatenate([wq, zq], axis=-1)
    second = jnp.concatenate([zq, wq], axis=-1)
    kv_of_head = (jnp.arange(NSA_HEADS) // NSA_GROUP)[None, None, :, None]
    wq_pad = jnp.where(kv_of_head == 0, first, second).reshape(nl, d, QPAD_WIDTH)
    gate_grp = jnp.concatenate([w_in[:, :, o_kv:o_g], w_in[:, :, o_a:o_z],
                                jnp.zeros((nl, d, LANES - 3 * NSA_HEADS - 2 * GDN_HEADS), w_in.dtype)], axis=-1)
    packed = jnp.concatenate([wq_pad, w_in[:, :, o_q:o_kv], w_in[:, :, o_qkv:o_a], w_in[:, :, o_z:], gate_grp], axis=-1)
    return packed.astype(BF16)


def kernel(x_prompt, x_sample, cache_nsa_kv, cache_win_kv, state_gdn_S, state_gdn_conv, cache_mem_kv, page_table, mem_prompt, norm_gains, mem_norm, w_ffn_gu, w_ffn_down, w_in, w_out, cmp_pe, w_cmp1, w_cmp2, gdn_conv, gdn_A_log, gdn_dt_bias, gdn_norm, w_mem_q, w_mem_kv, w_mem_o):
    bp, t_len, d = x_prompt.shape
    bs = x_sample.shape[0]
    depth = w_in.shape[0]
    w_buf = cache_win_kv.shape[2]
    n_pages = page_table.shape[1]
    page = cache_nsa_kv.shape[2]
    past_len = n_pages * page
    n_mem = mem_prompt.shape[1]
    np_tok = bp * t_len

    gains = norm_gains.reshape(depth, 8, 1, d)
    wgu = w_ffn_gu.astype(BF16)
    wd = w_ffn_down.astype(BF16)
    w_in_p = _pack_w_in(w_in)
    w_out_b = w_out.astype(BF16)
    w_mq = w_mem_q.astype(BF16)
    w_mkv = w_mem_kv.astype(BF16)
    w_mo = w_mem_o.astype(BF16)
    mem_gain = mem_norm.reshape(depth, 1, d)

    pos_p = jnp.arange(t_len)
    tabs_p = _rope_tables(pos_p) + (_block_onehot(pos_p),)
    pos_s = jnp.full((bs,), past_len, jnp.int32)
    tabs_s = _rope_tables(pos_s) + (jnp.zeros((bs, LANES), F32),)

    half_k = CMP_STRIDE * HEAD_DIM
    pe8 = jnp.pad(cmp_pe.reshape(depth, 2, CMP_BLOCK // CMP_STRIDE, half_k), ((0, 0), (0, 0), (0, 6), (0, 0)))
    w1c = jnp.concatenate([w_cmp1[:, :, :half_k], w_cmp1[:, :, half_k:]], axis=-1).astype(BF16)
    z2 = jnp.zeros_like(w_cmp2)
    w2p = jnp.stack([jnp.concatenate([w_cmp2, z2], axis=-1), jnp.concatenate([z2, w_cmp2], axis=-1)], axis=2).astype(BF16)
    tabs_c = _rope_tables(jnp.arange(t_len // CMP_STRIDE) * CMP_STRIDE + (CMP_BLOCK - 1))
    lane_pad = lambda v: jnp.pad(v, ((0, 0), (GATE_A, LANES - GATE_A - GDN_HEADS))).reshape(depth, 1, LANES)
    alog_row = lane_pad(gdn_A_log)
    dtb_row = lane_pad(gdn_dt_bias)
    gnorm = gdn_norm.reshape(depth, 1, GDN_DV)

    n_cmp = (t_len - CMP_BLOCK) // CMP_STRIDE + 1
    n_cmp_pad = t_len // CMP_STRIDE
    n_sel = -(-t_len // SEL_BLOCK)
    c_start = jnp.arange(n_cmp) * CMP_STRIDE
    assert n_sel <= HEAD_DIM and t_len >= WINDOW + NSA_TQ and t_len % NSA_TK == 0
    covt = jnp.pad(_block_coverage(c_start, n_sel).T, ((0, HEAD_DIM - n_sel), (0, n_cmp_pad - n_cmp))).astype(BF16)
    topk = min(SEL_TOPK, n_sel)

    assert x_sample.shape[1] == 1 and past_len % SEL_BLOCK == 0 and bs % GDN_SB == 0
    t_kv = past_len + 1
    n_cmp_s = (t_kv - CMP_BLOCK) // CMP_STRIDE + 1
    assert (n_cmp_s - 1) * CMP_STRIDE + CMP_BLOCK <= past_len
    g_s = past_len // CMP_STRIDE
    n_sel_s = -(-t_kv // SEL_BLOCK)
    assert n_sel_s <= HEAD_DIM
    topk_s = min(SEL_TOPK, n_sel_s)
    cov_s = jnp.pad(_block_coverage(jnp.arange(n_cmp_s) * CMP_STRIDE, n_sel_s),
                    ((0, g_s - n_cmp_s), (0, HEAD_DIM - n_sel_s))).astype(BF16)
    eexp = (jnp.arange(past_len)[None, :] // SEL_BLOCK == jnp.arange(HEAD_DIM)[:, None]).astype(BF16)
    tabs_cs = _rope_tables(jnp.arange(g_s) * CMP_STRIDE + (CMP_BLOCK - 1))
    token_last = (0, 1, 3, 4, 5, 2)
    n_phys = cache_nsa_kv.shape[1]
    cache_t = jnp.transpose(cache_nsa_kv, token_last).reshape(depth, n_phys, 4, LANES, page)
    win_cache = jnp.transpose(cache_win_kv, token_last).reshape(depth, bs, 2, LANES, w_buf)
    conv_cache = state_gdn_conv.reshape(depth, bs, (CONV_W - 1) * GDN_QKV)
    mem_cache = jnp.transpose(cache_mem_kv, token_last).reshape(depth, bs, 2, MEM_WIDTH, n_mem)

    tm_p = 512
    yp = x_prompt.reshape(np_tok, d)
    ys = x_sample.reshape(bs, d)
    mem_flat = mem_prompt.reshape(bp * n_mem, d)
    outs = [[] for _ in range(9)]
    for l in range(depth):
        mem_kv_p = _normmm(mem_flat, mem_gain, (l,), w_mkv, l, n_mem)
        yp = _ffn(yp, gains, wgu, wd, l, 0, tm_p)
        qpad, rows, win, kaug, vsel, kvwin, gates, qkv, z = _inproj(yp, gains, w_in_p, tabs_p, l, tm_p)
        rows5 = rows.reshape(bp, t_len, 4, NSA_KV_HEADS, HEAD_DIM)
        kc, vc = _compress_prompt(rows, pe8, w1c, w2p, tabs_c, l, bp, t_len)
        o_nsa = _nsa_prompt(qpad, gates, kaug, vsel, kvwin, kc, vc, covt, bp, t_len, n_cmp, n_sel, topk)
        qg, kd, w_g, u_g, att, egl = _gdn_prep(qkv, gates, gdn_conv, alog_row, dtb_row, l, bp, t_len)
        o_gdn, s_p = _gdn_scan(qg, kd, w_g, u_g, att, egl, z, gnorm, l, bp, t_len)
        conv_p = qkv.reshape(bp, t_len, GDN_QKV)[:, t_len - (CONV_W - 1):]
        yp = _outproj(yp, o_nsa, o_gdn, w_out_b, gains, l, tm_p)
        yp = _mem_block(yp, gains, w_mq, w_mo, mem_kv_p, l, tm_p, t_len // tm_p)
        yp = _ffn(yp, gains, wgu, wd, l, 1, tm_p)
        win5 = win.reshape(bp, t_len, 2, NSA_KV_HEADS, HEAD_DIM)
        win_p = win5[:, t_len - w_buf:] if t_len >= w_buf else jnp.pad(
            win5, ((0, 0), (w_buf - t_len, 0), (0, 0), (0, 0), (0, 0)))

        ys = _ffn(ys, gains, wgu, wd, l, 0, bs)
        qpad_s, rows_s, win_s, _, _, _, gates_s, qkv_s, z_s = _inproj(ys, gains, w_in_p, tabs_s, l, bs)
        o_nsa_s = _nsa_sample(
            page_table, cache_t, qpad_s.astype(F32).reshape(bs, NSA_HEADS, LANES), gates_s.reshape(bs, 1, LANES),
            rows_s.reshape(bs, 1, 4 * LANES), win_s.reshape(bs, 1, 2 * LANES), win_cache, pe8, w1c, w2p, tabs_cs,
            cov_s, eexp, l, n_cmp_s, n_sel_s, topk_s).reshape(bs, NSA_WIDTH)
        o_gdn_s, s_s = _gdn_sample(qkv_s, conv_cache, gates_s, z_s, state_gdn_S, gdn_conv, alog_row, dtb_row, gnorm, l)
        ys = _outproj(ys, o_nsa_s, o_gdn_s, w_out_b, gains, l, bs)
        q_m = _normmm(ys, gains, (l, 4), w_mq, l, bs)
        o_m = _mem_sample(q_m.reshape(bs, 1, MEM_WIDTH), mem_cache, l).reshape(bs, MEM_WIDTH)
        ys = _outproj(ys, o_m[:, :MEM_WIDTH // 2], o_m[:, MEM_WIDTH // 2:], w_mo, gains, l, bs, gain_idx=5)
        ys = _ffn(ys, gains, wgu, wd, l, 1, bs)
        win_all = jnp.concatenate([cache_win_kv[l], win_s.reshape(bs, 1, 2, NSA_KV_HEADS, HEAD_DIM)], axis=1)
        wbuf_s = win_all[:, win_all.shape[1] - w_buf:]
        conv_s = jnp.concatenate([state_gdn_conv[l], qkv_s[:, None, :]], axis=1)[:, 1:]

        for lst, val in zip(outs, (rows5, win_p, s_p, conv_p, mem_kv_p.reshape(bp, n_mem, 2, MEM_HEADS, MEM_HD),
                                   rows_s.reshape(bs, 1, 4, NSA_KV_HEADS, HEAD_DIM), wbuf_s, s_s, conv_s)):
            lst.append(val)
    return (yp.reshape(bp, t_len, d), ys.reshape(bs, 1, d)) + tuple(jnp.stack(v) for v in outs)
```

```python
import functools
import math

import numpy as np
import jax
import jax.numpy as jnp
from jax import lax
from jax.experimental import pallas as pl
from jax.experimental.pallas import tpu as pltpu

F32 = jnp.float32
BF16 = jnp.bfloat16

HEAD_DIM = 64
NSA_HEADS = 8
NSA_KV_HEADS = 2
NSA_GROUP = NSA_HEADS // NSA_KV_HEADS
NSA_WIDTH = NSA_HEADS * HEAD_DIM
ROPE_DIM = HEAD_DIM // 4
ROPE_THETA = 500000.0
ATTN_SCALE = HEAD_DIM ** -0.5
CMP_BLOCK = 32
CMP_STRIDE = 16
CMP_HIDDEN = 2 * HEAD_DIM
SEL_BLOCK = 64
SEL_TOPK = 16
WINDOW = 512
FORCE_SCORE = 1.0e4
GDN_DK = 128
GDN_DV = 128
GDN_HEADS = 4
GDN_WIDTH = GDN_HEADS * GDN_DV
GDN_QKV = GDN_HEADS * (2 * GDN_DK + GDN_DV)
CONV_W = 4
GDN_CHUNK = 64
MEM_HEADS = 4
MEM_HD = 64
MEM_WIDTH = MEM_HEADS * MEM_HD
RMS_EPS = 1e-6
NEG_INF = -1.0e30
KV6_WIDTH = 6 * NSA_KV_HEADS * HEAD_DIM

LANES = 128
VMEM_LIMIT_BYTES = 56 * 1024 * 1024

QPAD_WIDTH = NSA_HEADS * LANES
COL_Q = 0
COL_KV = COL_Q + QPAD_WIDTH
COL_QKV = COL_KV + KV6_WIDTH
COL_Z = COL_QKV + GDN_QKV
COL_GATE = COL_Z + GDN_WIDTH
IN_PACKED = COL_GATE + LANES
GATE_A = 3 * NSA_HEADS
GATE_B = GATE_A + GDN_HEADS


def _cparams(*sem):
    return pltpu.CompilerParams(dimension_semantics=sem, vmem_limit_bytes=VMEM_LIMIT_BYTES)


def _rms(x, w):
    return x * lax.rsqrt(jnp.mean(x * x, axis=-1, keepdims=True) + RMS_EPS) * w


def _const_spec(shape):
    nd = len(shape)
    return pl.BlockSpec(shape, lambda *_: (0,) * nd)


def _layer_spec(shape, *lead):
    nlead = len(lead)
    nd = len(shape)
    return pl.BlockSpec((None,) * nlead + tuple(shape), lambda *_: tuple(lead) + (0,) * nd)


FFN_CHUNK = 256


def _ffn_body(x_ref, g0_ref, g1_ref, wgu_ref, wd_ref, o_ref, acc_ref, *, d_ff):
    x = x_ref[...]
    h = _rms(x, g0_ref[...]).astype(BF16)
    for f in range(d_ff // FFN_CHUNK):
        lo = f * FFN_CHUNK
        g = jnp.dot(h, wgu_ref[:, lo:lo + FFN_CHUNK], preferred_element_type=F32)
        u = jnp.dot(h, wgu_ref[:, d_ff + lo:d_ff + lo + FFN_CHUNK], preferred_element_type=F32)
        a = (g * jax.nn.sigmoid(g) * u).astype(BF16)
        d = jnp.dot(a, wd_ref[lo:lo + FFN_CHUNK, :], preferred_element_type=F32)
        if f == 0:
            acc_ref[...] = d
        else:
            acc_ref[...] += d
    o_ref[...] = x + 0.5 * _rms(acc_ref[...], g1_ref[...])


def _ffn(x, gains, wgu, wd, l, j, tm):
    n, d = x.shape
    d_ff = wd.shape[2]
    return pl.pallas_call(
        functools.partial(_ffn_body, d_ff=d_ff),
        grid=(n // tm,),
        in_specs=[
            pl.BlockSpec((tm, d), lambda i: (i, 0)),
            _layer_spec((1, d), l, 2 * j * 3),
            _layer_spec((1, d), l, 2 * j * 3 + 1),
            _layer_spec((d, 2 * d_ff), l, j),
            _layer_spec((d_ff, d), l, j),
        ],
        out_specs=pl.BlockSpec((tm, d), lambda i: (i, 0)),
        out_shape=jax.ShapeDtypeStruct((n, d), F32),
        scratch_shapes=[pltpu.VMEM((tm, d), F32)],
        compiler_params=_cparams("parallel"),
        name=f"ffn{j}",
    )(x, gains, gains, wgu, wd)


def _rope(v, c1, sm1, sp1):
    n = v.shape[1] // LANES
    c, sm, sp = (t if n == 1 else jnp.concatenate([t] * n, axis=1) for t in (c1, sm1, sp1))
    w = v.shape[1]
    return v * c + pltpu.roll(v, w - ROPE_DIM // 2, 1) * sm + pltpu.roll(v, ROPE_DIM // 2, 1) * sp


def _inproj_body(x_ref, g_ref, w_ref, c_ref, sm_ref, sp_ref, e_ref,
                 qpad_ref, rows_ref, win_ref, kaug_ref, vsel_ref, kvwin_ref, gates_ref, qkv_ref, z_ref):
    h = _rms(x_ref[...], g_ref[...]).astype(BF16)
    c1, sm1, sp1 = c_ref[...], sm_ref[...], sp_ref[...]

    def mm(lo, hi):
        return jnp.dot(h, w_ref[:, lo:hi], preferred_element_type=F32)

    q = _rope(mm(COL_Q, COL_KV), c1, sm1, sp1)
    qpad_ref[...] = (q * ATTN_SCALE).astype(BF16)
    kv = mm(COL_KV, COL_QKV)
    ksel = _rope(kv[:, 2 * LANES:3 * LANES], c1, sm1, sp1)
    vsel = kv[:, 3 * LANES:4 * LANES]
    kwin = _rope(kv[:, 4 * LANES:5 * LANES], c1, sm1, sp1)
    vwin = kv[:, 5 * LANES:6 * LANES]
    rows_ref[:, 0:2 * LANES] = kv[:, 0:2 * LANES]
    rows_ref[:, 2 * LANES:3 * LANES] = ksel
    rows_ref[:, 3 * LANES:4 * LANES] = vsel
    win_ref[:, 0:LANES] = kwin
    win_ref[:, LANES:2 * LANES] = vwin
    e2 = e_ref[...]
    lane = lax.broadcasted_iota(jnp.int32, ksel.shape, 1)
    first = lane < HEAD_DIM
    kaug_ref[:, 0:LANES] = jnp.where(first, ksel, e2).astype(BF16)
    kaug_ref[:, LANES:2 * LANES] = jnp.where(first, e2, ksel).astype(BF16)
    vsel_ref[...] = vsel.astype(BF16)
    kvwin_ref[:, 0:LANES] = kwin.astype(BF16)
    kvwin_ref[:, LANES:2 * LANES] = vwin.astype(BF16)
    qkv_ref[...] = mm(COL_QKV, COL_Z)
    z_ref[...] = mm(COL_Z, COL_GATE)
    gates_ref[...] = mm(COL_GATE, IN_PACKED)


def _inproj(x, gains, w_in_p, tabs, l, tm):
    n, d = x.shape
    c_t, sm_t, sp_t, e_t = tabs
    nt = c_t.shape[0] // tm
    tab = lambda w: pl.BlockSpec((tm, w), lambda i: (i % nt, 0))
    row = lambda w: pl.BlockSpec((tm, w), lambda i: (i, 0))
    widths = [(QPAD_WIDTH, BF16), (4 * LANES, F32), (2 * LANES, F32), (2 * LANES, BF16), (LANES, BF16),
              (2 * LANES, BF16), (LANES, F32), (GDN_QKV, F32), (GDN_WIDTH, F32)]
    return pl.pallas_call(
        _inproj_body,
        grid=(n // tm,),
        in_specs=[row(d), _layer_spec((1, d), l, 2), _layer_spec((d, IN_PACKED), l),
                  tab(LANES), tab(LANES), tab(LANES), tab(LANES)],
        out_specs=[row(w) for w, _ in widths],
        out_shape=[jax.ShapeDtypeStruct((n, w), dt) for w, dt in widths],
        compiler_params=_cparams("parallel"),
        name="inproj",
    )(x, gains, w_in_p, c_t, sm_t, sp_t, e_t)


def _normmm_body(x_ref, g_ref, w_ref, o_ref):
    h = _rms(x_ref[...], g_ref[...]).astype(BF16)
    o_ref[...] = jnp.dot(h, w_ref[...], preferred_element_type=F32)


def _normmm(x, gain, gain_lead, w, l, tm):
    n, d = x.shape
    nout = w.shape[-1]
    return pl.pallas_call(
        _normmm_body,
        grid=(n // tm,),
        in_specs=[pl.BlockSpec((tm, d), lambda i: (i, 0)), _layer_spec((1, d), *gain_lead), _layer_spec((d, nout), l)],
        out_specs=pl.BlockSpec((tm, nout), lambda i: (i, 0)),
        out_shape=jax.ShapeDtypeStruct((n, nout), F32),
        compiler_params=_cparams("parallel"),
        name="memkv",
    )(x, gain, w)


def _outproj_body(x_ref, a1_ref, a2_ref, w_ref, g_ref, o_ref):
    k1 = a1_ref.shape[1]
    acc = jnp.dot(a1_ref[...].astype(BF16), w_ref[0:k1, :], preferred_element_type=F32)
    acc = acc + jnp.dot(a2_ref[...].astype(BF16), w_ref[k1:, :], preferred_element_type=F32)
    o_ref[...] = x_ref[...] + _rms(acc, g_ref[...])


def _outproj(x, a1, a2, w_out, gains, l, tm, gain_idx=3):
    n, d = x.shape
    row = lambda w: pl.BlockSpec((tm, w), lambda i: (i, 0))
    return pl.pallas_call(
        _outproj_body,
        grid=(n // tm,),
        in_specs=[row(d), row(a1.shape[1]), row(a2.shape[1]),
                  _layer_spec((a1.shape[1] + a2.shape[1], d), l), _layer_spec((1, d), l, gain_idx)],
        out_specs=row(d),
        out_shape=jax.ShapeDtypeStruct((n, d), F32),
        compiler_params=_cparams("parallel"),
        name="outproj",
    )(x, a1, a2, w_out, gains)


def _mem_body(x_ref, g4_ref, g5_ref, wq_ref, wo_ref, kv_ref, o_ref):
    x = x_ref[...]
    h = _rms(x, g4_ref[...]).astype(BF16)
    q = jnp.dot(h, wq_ref[...], preferred_element_type=F32) * (MEM_HD ** -0.5)
    kv = kv_ref[...].astype(BF16)
    heads = range(MEM_HEADS)
    cols = [slice(hd * MEM_HD, (hd + 1) * MEM_HD) for hd in heads]
    ss = [lax.dot_general(q[:, cols[hd]].astype(BF16), kv[:, cols[hd]], (((1,), (1,)), ((), ())),
                          preferred_element_type=F32) for hd in heads]
    es = [jnp.exp(s - jnp.max(s, axis=-1, keepdims=True)) for s in ss]
    ps = [e / jnp.sum(e, axis=-1, keepdims=True) for e in es]
    outs = [jnp.dot(ps[hd].astype(BF16), kv[:, MEM_WIDTH + hd * MEM_HD:MEM_WIDTH + (hd + 1) * MEM_HD],
                    preferred_element_type=F32) for hd in heads]
    o = jnp.concatenate(outs, axis=1).astype(BF16)
    y = jnp.dot(o, wo_ref[...], preferred_element_type=F32)
    o_ref[...] = x + _rms(y, g5_ref[...])


def _mem_block(x, gains, w_q, w_o, mem_kv, l, tm, tiles_per_batch):
    n, d = x.shape
    m = mem_kv.shape[0] // (n // (tm * tiles_per_batch))
    return pl.pallas_call(
        _mem_body,
        grid=(n // tm,),
        in_specs=[pl.BlockSpec((tm, d), lambda i: (i, 0)), _layer_spec((1, d), l, 4), _layer_spec((1, d), l, 5),
                  _layer_spec((d, MEM_WIDTH), l), _layer_spec((MEM_WIDTH, d), l),
                  pl.BlockSpec((m, 2 * MEM_WIDTH), lambda i: (i // tiles_per_batch, 0))],
        out_specs=pl.BlockSpec((tm, d), lambda i: (i, 0)),
        out_shape=jax.ShapeDtypeStruct((n, d), F32),
        compiler_params=_cparams("parallel"),
        name="memattn",
    )(x, gains, gains, w_q, w_o, mem_kv)


NSA_TQ = 128
NSA_TK = 1024


def _softmax_cols(s, valid):
    s = jnp.where(valid, s, NEG_INF)
    m = jnp.max(s, axis=0, keepdims=True)
    e = jnp.where(valid, jnp.exp(s - m), 0.0)
    return e / jnp.maximum(jnp.sum(e, axis=0, keepdims=True), 1e-30)


def _softmax_rows(s, valid):
    s = jnp.where(valid, s, NEG_INF)
    m = jnp.max(s, axis=1, keepdims=True)
    e = jnp.where(valid, jnp.exp(s - m), 0.0)
    return e / jnp.maximum(jnp.sum(e, axis=1, keepdims=True), 1e-30)


def _split3(x):
    hi = x.astype(BF16)
    r = x - hi.astype(F32)
    mid = r.astype(BF16)
    lo = (r - mid.astype(F32)).astype(BF16)
    return hi, mid, lo


def _select_topk(imp_t, blk, n_pick):
    n_blocks = imp_t.shape[0]
    sel = jnp.zeros(imp_t.shape, F32)
    work = imp_t
    for _ in range(n_pick):
        mx = jnp.max(work, axis=0, keepdims=True)
        first = jnp.min(jnp.where(work == mx, blk, float(n_blocks)), axis=0, keepdims=True)
        hit = blk == first
        sel = jnp.where(hit, 1.0, sel)
        work = jnp.where(hit, -3.0e38, work)
    return sel


def _transpose_01(x_t, eye):
    return lax.dot_general(eye, x_t.astype(BF16), (((1,), (1,)), ((), ())), preferred_element_type=F32)


def _nsa_body(q_ref, gt_ref, kaug_ref, vsel_ref, kvw_ref, kc_ref, vc_ref, covt_ref, o_ref, *, n_cmp, n_sel, topk):
    tq = q_ref.shape[0]
    n_blk = covt_ref.shape[0]
    n_cmp_pad = kc_ref.shape[0]
    g4 = NSA_GROUP
    i = pl.program_id(1)
    q0 = i * tq
    gates = jax.nn.sigmoid(gt_ref[...])

    qpos_row = q0 + lax.broadcasted_iota(jnp.int32, (1, tq), 1)
    qpos_row4 = jnp.concatenate([qpos_row] * g4, axis=1)
    qpos_col = q0 + lax.broadcasted_iota(jnp.int32, (tq, 1), 0)
    qpos_col4 = jnp.concatenate([qpos_col] * g4, axis=0)
    lane = lax.broadcasted_iota(jnp.int32, (tq, LANES), 1)
    blk = lax.broadcasted_iota(jnp.int32, (n_blk, 1), 0)
    blk_f = blk.astype(F32)
    qblk = qpos_row // SEL_BLOCK
    forced = (blk == 0) | (blk == qblk) | (blk == qblk - 1)
    future = blk > qblk
    exists = blk < n_sel
    n_eye = NSA_KV_HEADS * tq
    eye = (lax.broadcasted_iota(jnp.int32, (n_eye, n_eye), 0)
           == lax.broadcasted_iota(jnp.int32, (n_eye, n_eye), 1)).astype(BF16)
    c_idx = lax.broadcasted_iota(jnp.int32, (n_cmp_pad, 1), 0)
    c_end = c_idx * CMP_STRIDE + (CMP_BLOCK - 1)
    cmp_valid = (c_end <= qpos_row4) & (c_idx < n_cmp)

    w_start = pl.multiple_of(jnp.maximum(q0 - WINDOW, 0), tq)
    w_len = WINDOW + tq
    kpos_w = w_start + lax.broadcasted_iota(jnp.int32, (1, w_len), 1)
    dw = qpos_col4 - kpos_w
    win_valid = (dw >= 0) & (dw < WINDOW)

    n_kt = (q0 + tq + NSA_TK - 1) // NSA_TK
    kc = kc_ref[...]
    vc = vc_ref[...]
    covt = covt_ref[...]
    heads = range(NSA_KV_HEADS)
    nt_dims = (((1,), (1,)), ((), ()))
    owns = [lane < HEAD_DIM, lane >= HEAD_DIM]
    qpads = [[q_ref[:, (hk * g4 + g) * LANES:(hk * g4 + g + 1) * LANES] for g in range(g4)] for hk in heads]
    qpad4s = [jnp.concatenate(qpads[hk], axis=0) for hk in heads]

    s_ts = [lax.dot_general(kc, qpad4s[hk], nt_dims, preferred_element_type=F32) for hk in heads]
    p_ts = [_softmax_cols(s_t, cmp_valid) for s_t in s_ts]
    o_cmps = [jnp.dot(p_t.T.astype(BF16), vc, preferred_element_type=F32) for p_t in p_ts]
    p_sum = jnp.concatenate([sum(p_t[:, g * tq:(g + 1) * tq] for g in range(g4)) for p_t in p_ts], axis=1)
    imp_t = sum(jnp.dot(covt, part, preferred_element_type=F32) for part in _split3(p_sum))
    both = lambda m: jnp.concatenate([m] * NSA_KV_HEADS, axis=1)
    imp_t = jnp.where(both(forced), FORCE_SCORE, jnp.where(both(future), -1.0, imp_t))
    imp_t = jnp.where(exists, imp_t, -2.0)
    sel_t = _select_topk(imp_t, blk_f, topk)
    selneg = jnp.where(_transpose_01(sel_t, eye) > 0.5, 0.0, NEG_INF)
    lhs4s = []
    for hk in heads:
        sn = selneg[hk * tq:(hk + 1) * tq]
        selneg2 = jnp.concatenate([sn, sn], axis=1).astype(BF16)
        lhs4s.append(jnp.concatenate([jnp.where(owns[hk], qp, selneg2) for qp in qpads[hk]], axis=0))

    def kt_body(kt, carry, causal):
        ks = pl.multiple_of(kt * NSA_TK, NSA_TK)
        v = vsel_ref[pl.ds(ks, NSA_TK), :]
        heads = range(NSA_KV_HEADS)
        ss = [lax.dot_general(lhs4s[hk], kaug_ref[pl.ds(ks, NSA_TK), hk * LANES:(hk + 1) * LANES],
                              (((1,), (1,)), ((), ())), preferred_element_type=F32) for hk in heads]
        if causal:
            kpos = ks + lax.broadcasted_iota(jnp.int32, (1, NSA_TK), 1)
            ss = [jnp.where(kpos <= qpos_col4, s, NEG_INF) for s in ss]
        m_new = [jnp.maximum(carry[hk][0], jnp.max(ss[hk], axis=1, keepdims=True)) for hk in heads]
        alpha = [jnp.exp(carry[hk][0] - m_new[hk]) for hk in heads]
        ps = [jnp.exp(ss[hk] - m_new[hk]) for hk in heads]
        vs = [jnp.where(own_k[hk], v, 1.0) for hk in heads]
        pv = [jnp.dot(ps[hk].astype(BF16), vs[hk], preferred_element_type=F32) for hk in heads]
        return tuple((m_new[hk], alpha[hk] * carry[hk][1] + pv[hk]) for hk in heads)

    lane_k = lax.broadcasted_iota(jnp.int32, (NSA_TK, LANES), 1)
    own_k = [lane_k < HEAD_DIM, lane_k >= HEAD_DIM]
    init1 = (jnp.full((g4 * tq, 1), NEG_INF, F32), jnp.zeros((g4 * tq, LANES), F32))
    carry = lax.fori_loop(0, n_kt - 1, functools.partial(kt_body, causal=False), (init1,) * NSA_KV_HEADS)
    carry = kt_body(n_kt - 1, carry, causal=True)

    sum_lane = [HEAD_DIM, 0]
    o_sels = [carry[hk][1] / jnp.maximum(carry[hk][1][:, sum_lane[hk]:sum_lane[hk] + 1], 1e-30) for hk in heads]

    kw = kvw_ref[pl.ds(w_start, w_len), 0:LANES]
    vw = kvw_ref[pl.ds(w_start, w_len), LANES:2 * LANES]
    s_ws = [jnp.where(win_valid, lax.dot_general(qpad4s[hk], kw, nt_dims, preferred_element_type=F32), NEG_INF)
            for hk in heads]
    e_ws = [jnp.exp(s_w - jnp.max(s_w, axis=1, keepdims=True)) for s_w in s_ws]
    o_wins = [jnp.dot(e_w.astype(BF16), vw, preferred_element_type=F32)
              / jnp.maximum(jnp.sum(e_w, axis=1, keepdims=True), 1e-30) for e_w in e_ws]

    for hk in heads:
        def gate_col(j):
            cols = [gates[:, 3 * (hk * g4 + g) + j:3 * (hk * g4 + g) + j + 1] for g in range(g4)]
            return jnp.concatenate(cols, axis=0)

        o4 = gate_col(0) * o_cmps[hk] + gate_col(1) * o_sels[hk] + gate_col(2) * o_wins[hk]
        for pair in range(g4 // 2):
            a = o4[(2 * pair) * tq:(2 * pair + 1) * tq]
            b = o4[(2 * pair + 1) * tq:(2 * pair + 2) * tq]
            if hk == 0:
                both = jnp.where(lane < HEAD_DIM, a, pltpu.roll(b, HEAD_DIM, 1))
            else:
                both = jnp.where(lane < HEAD_DIM, pltpu.roll(a, HEAD_DIM, 1), b)
            col = (hk * (g4 // 2) + pair) * LANES
            o_ref[:, col:col + LANES] = both


def _nsa_prompt(qpad, gates, kaug, vsel, kvwin, kc, vc, covt, batch, t_len, n_cmp, n_sel, topk):
    n = qpad.shape[0]
    nq = t_len // NSA_TQ
    n_cmp_pad = kc.shape[0] // batch
    qrow = lambda w: pl.BlockSpec((NSA_TQ, w), lambda b, i: (b * nq + i, 0))
    per_b = lambda r, w: pl.BlockSpec((r, w), lambda b, i: (b, 0))
    return pl.pallas_call(
        functools.partial(_nsa_body, n_cmp=n_cmp, n_sel=n_sel, topk=topk),
        grid=(batch, nq),
        in_specs=[qrow(QPAD_WIDTH), qrow(LANES), per_b(t_len, 2 * LANES), per_b(t_len, LANES),
                  per_b(t_len, 2 * LANES), per_b(n_cmp_pad, LANES), per_b(n_cmp_pad, LANES),
                  _const_spec(covt.shape)],
        out_specs=qrow(NSA_WIDTH),
        out_shape=jax.ShapeDtypeStruct((n, NSA_WIDTH), F32),
        compiler_params=_cparams("parallel", "parallel"),
        name="nsa_prompt",
    )(qpad, gates, kaug, vsel, kvwin, kc, vc, covt)


def _group_features(load_l, lane):
    heads = ([], [])
    for m in range(CMP_STRIDE // 2):
        a = load_l(2 * m)
        b = load_l(2 * m + 1)
        heads[0].append(jnp.where(lane < HEAD_DIM, a, pltpu.roll(b, HEAD_DIM, 1)))
        heads[1].append(jnp.where(lane < HEAD_DIM, pltpu.roll(a, HEAD_DIM, 1), b))
    return [jnp.concatenate(h, axis=1) for h in heads]


def _compress_heads(x_heads, pe8, w1, w2_h0, w2_h1):
    g = x_heads[0].shape[0]
    x2 = jnp.concatenate(x_heads, axis=0).astype(BF16)
    z = jnp.dot(x2, w1, preferred_element_type=F32)
    zb = jnp.dot(pe8.astype(BF16), w1, preferred_element_type=F32)
    bias = zb[0:1, 0:CMP_HIDDEN] + zb[1:2, CMP_HIDDEN:]
    nxt = pltpu.roll(z[:, CMP_HIDDEN:], 2 * g - 1, 0)
    hid = jax.nn.gelu(z[:, 0:CMP_HIDDEN] + nxt + bias).astype(BF16)
    return (jnp.dot(hid[0:g], w2_h0, preferred_element_type=F32)
            + jnp.dot(hid[g:], w2_h1, preferred_element_type=F32))


def _compress_body(krows_ref, vrows_ref, pe_ref, w1_ref, w2_ref, c_ref, sm_ref, sp_ref, kc_ref, vc_ref):
    g = kc_ref.shape[0]
    lane = lax.broadcasted_iota(jnp.int32, (g, LANES), 1)
    outs = []
    for t, ref in enumerate((krows_ref, vrows_ref)):
        load = lambda l, ref=ref: ref[pl.ds(l, g, stride=CMP_STRIDE), :]
        outs.append(_compress_heads(_group_features(load, lane), pe_ref[t], w1_ref[t], w2_ref[t, 0], w2_ref[t, 1]))
    kc_ref[...] = _rope(outs[0], c_ref[...], sm_ref[...], sp_ref[...]).astype(BF16)
    vc_ref[...] = outs[1].astype(BF16)


def _compress_prompt(rows, pe8, w1c, w2p, tabs_c, l, batch, t_len):
    g = t_len // CMP_STRIDE
    out = jax.ShapeDtypeStruct((batch * g, LANES), BF16)
    return pl.pallas_call(
        _compress_body,
        grid=(batch,),
        in_specs=[pl.BlockSpec((t_len, LANES), lambda b: (b, 0)), pl.BlockSpec((t_len, LANES), lambda b: (b, 1)),
                  _layer_spec((2, 8, CMP_STRIDE * HEAD_DIM), l),
                  _layer_spec((2, CMP_STRIDE * HEAD_DIM, 2 * CMP_HIDDEN), l),
                  _layer_spec((2, 2, CMP_HIDDEN, LANES), l),
                  _const_spec((g, LANES)), _const_spec((g, LANES)), _const_spec((g, LANES))],
        out_specs=[pl.BlockSpec((g, LANES), lambda b: (b, 0))] * 2,
        out_shape=[out, out],
        compiler_params=_cparams("parallel"),
        name="compress",
    )(rows, rows, pe8, w1c, w2p, *tabs_c)


GDN_TT = 512
HALO = 8


def _dot3(a, x):
    return sum(jnp.dot(a, part, preferred_element_type=F32) for part in _split3(x))


def _dot3_nt(a, x):
    return sum(lax.dot_general(a, part, (((1,), (1,)), ((), ())), preferred_element_type=F32) for part in _split3(x))


def _l2n(x):
    return x * lax.rsqrt(jnp.sum(x * x, axis=-1, keepdims=True) + 1e-6)


def _softplus(x):
    return jnp.maximum(x, 0.0) + jnp.log(1.0 + jnp.exp(-jnp.abs(x)))


def _gdn_prep_body(x_ref, halo_ref, gt_ref, cw_ref, alog_ref, dtb_ref,
                   qg_ref, kd_ref, w_ref, u_ref, att_ref, egl_ref,
                   ext_ref, q_s, k_s, kb_s, rhs_s, gc_s):
    c_len = GDN_CHUNK
    tt = x_ref.shape[0]
    nch = tt // c_len
    qk_w = GDN_HEADS * GDN_DK
    first = pl.program_id(1) == 0
    ext_ref[0:HALO, :] = jnp.where(first, 0.0, halo_ref[...])
    ext_ref[HALO:HALO + tt, :] = x_ref[...]
    c = ext_ref[pl.ds(HALO - (CONV_W - 1), tt), :] * cw_ref[0:1, :]
    for j in range(1, CONV_W):
        c = c + ext_ref[pl.ds(HALO - (CONV_W - 1) + j, tt), :] * cw_ref[j:j + 1, :]
    c = c * jax.nn.sigmoid(c)

    gt = gt_ref[...]
    g_full = -jnp.exp(alog_ref[...]) * _softplus(gt + dtb_ref[...])
    beta_full = jax.nn.sigmoid(gt)
    r = lax.broadcasted_iota(jnp.int32, (tt, tt), 0)
    cc = lax.broadcasted_iota(jnp.int32, (tt, tt), 1)
    same = (r // c_len) == (cc // c_len)
    gcum = _dot3(jnp.where(same & (cc <= r), 1.0, 0.0).astype(BF16), g_full)
    gtot = _dot3(jnp.where(same, 1.0, 0.0).astype(BF16), g_full)
    gc_s[...] = gcum
    for h in range(GDN_HEADS):
        sl = slice(h * GDN_DK, (h + 1) * GDN_DK)
        gc_col = gcum[:, GATE_A + h:GATE_A + h + 1]
        gl_col = gtot[:, GATE_A + h:GATE_A + h + 1]
        beta_col = beta_full[:, GATE_B + h:GATE_B + h + 1]
        qh = _l2n(c[:, h * GDN_DK:(h + 1) * GDN_DK]) * (GDN_DK ** -0.5)
        kh = _l2n(c[:, qk_w + h * GDN_DK:qk_w + (h + 1) * GDN_DK])
        vh = c[:, 2 * qk_w + h * GDN_DV:2 * qk_w + (h + 1) * GDN_DV]
        eg = jnp.exp(gc_col)
        kb = kh * beta_col
        q_s[:, sl] = qh
        k_s[:, sl] = kh
        kb_s[:, sl] = kb
        rhs_s[:, 2 * h * GDN_DK:(2 * h + 1) * GDN_DK] = vh * beta_col
        rhs_s[:, (2 * h + 1) * GDN_DK:(2 * h + 2) * GDN_DK] = kb * eg
        qg_ref[:, sl] = (qh * eg).astype(BF16)
        kd_ref[:, sl] = (kh * jnp.exp(gl_col - gc_col)).astype(BF16)
        egl_b = jnp.broadcast_to(jnp.exp(gl_col), (tt, GDN_DK))
        for ch in range(nch):
            egl_ref[ch, :, sl] = egl_b[ch * c_len:ch * c_len + 1, :]

    nh = GDN_HEADS
    wide = nh * c_len
    wi = lax.broadcasted_iota(jnp.int32, (c_len, wide), 0)
    wj = lax.broadcasted_iota(jnp.int32, (c_len, wide), 1) & (c_len - 1)
    incl = wi >= wj
    strict = wi > wj
    eye = jnp.where(wi == wj, 1.0, 0.0)
    base_mask = (wi >> 1) == (wj >> 1)
    bi = lax.broadcasted_iota(jnp.int32, (wide, wide), 0)
    bj = lax.broadcasted_iota(jnp.int32, (wide, wide), 1)
    same_head = (bi // c_len) == (bj // c_len)
    bi_in = bi & (c_len - 1)
    bj_in = bj & (c_len - 1)
    level_masks = [same_head & ((bi_in >> (s + 1)) == (bj_in >> (s + 1))) & ((bi_in >> s) != (bj_in >> s))
                   for s in range(1, int(math.log2(c_len)))]
    head_of_row = lax.broadcasted_iota(jnp.int32, (wide, LANES), 0) // c_len
    lane_of = lax.broadcasted_iota(jnp.int32, (wide, LANES), 1)
    pick_rows = lane_of == GATE_A + head_of_row
    sel_k = lax.broadcasted_iota(jnp.int32, (LANES, wide), 0)
    sel_h = lax.broadcasted_iota(jnp.int32, (LANES, wide), 1) // c_len
    sel_cols = jnp.where(sel_k == GATE_A + sel_h, 1.0, 0.0).astype(BF16)
    ones_b = jnp.ones((c_len, LANES), BF16)
    key_head = (lax.broadcasted_iota(jnp.int32, (wide, GDN_WIDTH), 0) // c_len
                == lax.broadcasted_iota(jnp.int32, (wide, GDN_WIDTH), 1) // GDN_DK)
    rhs_head = (lax.broadcasted_iota(jnp.int32, (wide, 2 * GDN_WIDTH), 0) // c_len
                == lax.broadcasted_iota(jnp.int32, (wide, 2 * GDN_WIDTH), 1) // (2 * GDN_DK))
    nt = (((1,), (1,)), ((), ()))
    rep = lambda m: jnp.concatenate([m] * nh, axis=0)

    n_par = 4
    par = range(n_par)

    def chunk_group(cg, carry):
        rows = [pl.ds(pl.multiple_of((cg * n_par + c) * c_len, c_len), c_len) for c in par]
        gch = [gc_s[r, :] for r in rows]
        g_i = [sum(jnp.dot(part, sel_cols, preferred_element_type=F32) for part in _split3(g)) for g in gch]
        g_j = [_dot3_nt(ones_b, jnp.where(pick_rows, rep(g), 0.0)) for g in gch]
        decay = [jnp.where(incl, jnp.exp(jnp.where(incl, g_i[c] - g_j[c], 0.0)), 0.0) for c in par]
        k_bd = [jnp.where(key_head, rep(k_s[r, :]), 0.0).astype(BF16) for r in rows]
        qkb = [jnp.concatenate([kb_s[r, :], q_s[r, :]], axis=0).astype(BF16) for r in rows]
        prod = [lax.dot_general(qkb[c], k_bd[c], nt, preferred_element_type=F32) for c in par]
        a = [jnp.where(strict, prod[c][0:c_len] * decay[c], 0.0) for c in par]
        for c in par:
            att_ref[rows[c], :] = (prod[c][c_len:] * decay[c]).astype(BF16)
        a_rep = [rep(m) for m in a]
        x = [eye - jnp.where(base_mask, m, 0.0) for m in a]
        for lvl_mask in level_masks:
            off_bd = [jnp.where(lvl_mask, m, 0.0).astype(BF16) for m in a_rep]
            x_bd = [jnp.where(same_head, rep(m), 0.0).astype(BF16) for m in x]
            t = [jnp.dot(x[c].astype(BF16), off_bd[c], preferred_element_type=F32) for c in par]
            x = [x[c] - jnp.dot(t[c].astype(BF16), x_bd[c], preferred_element_type=F32) for c in par]
        rhs_bd = [jnp.where(rhs_head, rep(rhs_s[r, :]), 0.0).astype(BF16) for r in rows]
        uw = [jnp.dot(x[c].astype(BF16), rhs_bd[c], preferred_element_type=F32) for c in par]
        for c in par:
            for h in range(nh):
                sl = slice(h * GDN_DK, (h + 1) * GDN_DK)
                u_ref[rows[c], sl] = uw[c][:, 2 * h * GDN_DK:(2 * h + 1) * GDN_DK]
                w_ref[rows[c], sl] = uw[c][:, (2 * h + 1) * GDN_DK:(2 * h + 2) * GDN_DK].astype(BF16)
        return carry

    lax.fori_loop(0, nch // n_par, chunk_group, 0)


def _gdn_prep(qkv, gates, conv_w, alog_row, dtb_row, l, batch, t_len):
    n = qkv.shape[0]
    tt = GDN_TT
    nt = t_len // tt
    nch = tt // GDN_CHUNK
    row = lambda w: pl.BlockSpec((tt, w), lambda b, i: (b * nt + i, 0))
    outs = [(GDN_WIDTH, BF16), (GDN_WIDTH, BF16), (GDN_WIDTH, BF16), (GDN_WIDTH, F32)]
    return pl.pallas_call(
        _gdn_prep_body,
        grid=(batch, nt),
        in_specs=[row(GDN_QKV),
                  pl.BlockSpec((HALO, GDN_QKV), lambda b, i: (jnp.maximum((b * nt + i) * (tt // HALO) - 1, 0), 0)),
                  row(LANES), _layer_spec((CONV_W, GDN_QKV), l), _layer_spec((1, LANES), l), _layer_spec((1, LANES), l)],
        out_specs=[row(w) for w, _ in outs] + [
            pl.BlockSpec((tt, GDN_HEADS * GDN_CHUNK), lambda b, i: (b * nt + i, 0)),
            pl.BlockSpec((nch, 1, GDN_WIDTH), lambda b, i: (b * nt + i, 0, 0))],
        out_shape=[jax.ShapeDtypeStruct((n, w), dt) for w, dt in outs] + [
            jax.ShapeDtypeStruct((n, GDN_HEADS * GDN_CHUNK), BF16),
            jax.ShapeDtypeStruct((n // GDN_CHUNK, 1, GDN_WIDTH), F32)],
        scratch_shapes=[pltpu.VMEM((HALO + tt, GDN_QKV), F32), pltpu.VMEM((tt, GDN_WIDTH), F32),
                        pltpu.VMEM((tt, GDN_WIDTH), F32), pltpu.VMEM((tt, GDN_WIDTH), F32),
                        pltpu.VMEM((tt, 2 * GDN_WIDTH), F32), pltpu.VMEM((tt, LANES), F32)],
        compiler_params=_cparams("parallel", "parallel"),
        name="gdn_prep",
    )(qkv, qkv, gates, conv_w, alog_row, dtb_row)


def _gdn_scan_body(qg_ref, kd_ref, w_ref, u_ref, att_ref, egl_ref, z_ref, gn_ref, o_ref, sfin_ref, s_ref):
    c_len = GDN_CHUNK
    tt = qg_ref.shape[0]
    i = pl.program_id(1)

    @pl.when(i == 0)
    def _():
        s_ref[...] = jnp.zeros(s_ref.shape, F32)

    gn = gn_ref[...]
    tn = (((0,), (0,)), ((), ()))

    def chunk(ch, carry):
        r0 = pl.multiple_of(ch * c_len, c_len)
        heads = range(GDN_HEADS)
        sls = [slice(h * GDN_DK, (h + 1) * GDN_DK) for h in heads]
        rows = pl.ds(r0, c_len)
        ss = [s_ref[h] for h in heads]
        sbs = [s.astype(BF16) for s in ss]
        wq = [jnp.dot(jnp.concatenate([w_ref[rows, sls[h]], qg_ref[rows, sls[h]]], axis=0), sbs[h],
                      preferred_element_type=F32) for h in heads]
        vbs = [(u_ref[rows, sls[h]] - wq[h][0:c_len]).astype(BF16) for h in heads]
        av = [jnp.dot(att_ref[rows, h * c_len:(h + 1) * c_len], vbs[h], preferred_element_type=F32) for h in heads]
        kv = [lax.dot_general(kd_ref[rows, sls[h]], vbs[h], tn, preferred_element_type=F32) for h in heads]
        for h in heads:
            s_ref[h] = ss[h] * egl_ref[ch, :, sls[h]] + kv[h]
            o = wq[h][c_len:] + av[h]
            y = o * lax.rsqrt(jnp.mean(o * o, axis=-1, keepdims=True) + RMS_EPS) * gn
            zz = z_ref[rows, sls[h]]
            o_ref[rows, sls[h]] = y * (zz * jax.nn.sigmoid(zz))
        return carry

    lax.fori_loop(0, tt // c_len, chunk, 0)

    @pl.when(i == pl.num_programs(1) - 1)
    def _():
        sfin_ref[...] = s_ref[...]


def _gdn_scan(qg, kd, w, u, att, egl, z, gnorm, l, batch, t_len):
    n = qg.shape[0]
    tt = GDN_TT
    nt = t_len // tt
    nch = tt // GDN_CHUNK
    row = lambda: pl.BlockSpec((tt, GDN_WIDTH), lambda b, i: (b * nt + i, 0))
    return pl.pallas_call(
        _gdn_scan_body,
        grid=(batch, nt),
        in_specs=[row(), row(), row(), row(),
                  pl.BlockSpec((tt, GDN_HEADS * GDN_CHUNK), lambda b, i: (b * nt + i, 0)),
                  pl.BlockSpec((nch, 1, GDN_WIDTH), lambda b, i: (b * nt + i, 0, 0)),
                  row(), _layer_spec((1, GDN_DV), l)],
        out_specs=[row(), pl.BlockSpec((None, GDN_HEADS, GDN_DK, GDN_DV), lambda b, i: (b, 0, 0, 0))],
        out_shape=[jax.ShapeDtypeStruct((n, GDN_WIDTH), F32),
                   jax.ShapeDtypeStruct((batch, GDN_HEADS, GDN_DK, GDN_DV), F32)],
        scratch_shapes=[pltpu.VMEM((GDN_HEADS, GDN_DK, GDN_DV), F32)],
        compiler_params=_cparams("parallel", "arbitrary"),
        name="gdn_scan",
    )(qg, kd, w, u, att, egl, z, gnorm)


def _softmax_rows_extra(s, valid, s_new, valid_new):
    s = jnp.where(valid, s, NEG_INF)
    s_new = jnp.where(valid_new, s_new, NEG_INF)
    m = jnp.maximum(jnp.max(s, axis=1, keepdims=True), s_new)
    e = jnp.where(valid, jnp.exp(s - m), 0.0)
    e_new = jnp.where(valid_new, jnp.exp(s_new - m), 0.0)
    den = jnp.maximum(jnp.sum(e, axis=1, keepdims=True) + e_new, 1e-30)
    return e, e_new, 1.0 / den


NSA_SPS = 4


def _round_robin(gens):
    gens = list(gens)
    while gens:
        alive = []
        for gen in gens:
            try:
                next(gen)
                alive.append(gen)
            except StopIteration:
                pass
        gens = alive


def _nsa_sample_body(pt_ref, *refs, n_pages, **static):
    del pt_ref
    per = 3 * n_pages
    (q_ref, gt_ref, rows_ref, wnew_ref, wbuf_ref, pe_ref, w1_ref, w2_ref, c_ref, sm_ref, sp_ref,
     cov_ref, eexp_ref, o_ref) = refs[NSA_SPS * per:]
    gens = []
    for s in range(NSA_SPS):
        pages = refs[s * per:(s + 1) * per]
        gens.append(_nsa_sample_one(
            pages[0:n_pages], pages[n_pages:2 * n_pages], pages[2 * n_pages:],
            q_ref.at[s], gt_ref.at[s], rows_ref.at[s], wnew_ref.at[s], wbuf_ref.at[s],
            pe_ref, w1_ref, w2_ref, c_ref, sm_ref, sp_ref, cov_ref, eexp_ref, o_ref.at[s],
            n_pages=n_pages, **static))
    _round_robin(gens)


def _nsa_sample_one(kcmp_pages, vcmp_pages, sel_pages, q_ref, gt_ref, rows_ref, wnew_ref, wbuf_ref,
                    pe_ref, w1_ref, w2_ref, c_ref, sm_ref, sp_ref, cov_ref, eexp_ref, o_ref,
                    *, n_pages, page, n_buf, n_cmp, n_sel, topk):
    past = n_pages * page
    qpos = past
    gpp = page // CMP_STRIDE
    g = n_pages * gpp
    nt = (((1,), (1,)), ((), ()))
    nh = NSA_HEADS
    lane_g = lax.broadcasted_iota(jnp.int32, (g, LANES), 1)
    pr_i = lax.broadcasted_iota(jnp.int32, (page, page), 0)
    pt_i = lax.broadcasted_iota(jnp.int32, (page, page), 1)
    perm = jnp.where(pt_i == (pr_i % gpp) * CMP_STRIDE + pr_i // gpp, 1.0, 0.0).astype(BF16)
    qf = q_ref[...]
    qp = qf.astype(BF16)
    row8 = lax.broadcasted_iota(jnp.int32, (nh, 1), 0)

    outs = []
    for t, pages in enumerate((kcmp_pages, vcmp_pages)):
        xs = [lax.dot_general(perm, pr[...].astype(BF16), nt, preferred_element_type=F32) for pr in pages]
        load = lambda l, xs=xs: jnp.concatenate([x[l * gpp:(l + 1) * gpp, :] for x in xs], axis=0)
        outs.append(_compress_heads(_group_features(load, lane_g), pe_ref[t], w1_ref[t], w2_ref[t, 0], w2_ref[t, 1]))
        yield
    kc = _rope(outs[0], c_ref[...], sm_ref[...], sp_ref[...]).astype(BF16)
    vc = outs[1].astype(BF16)
    yield

    c_row = lax.broadcasted_iota(jnp.int32, (1, g), 1)
    cv_row = (c_row * CMP_STRIDE + (CMP_BLOCK - 1) <= qpos) & (c_row < n_cmp)
    s_c = lax.dot_general(qp, kc, nt, preferred_element_type=F32)
    yield
    p_c = _softmax_rows(s_c, cv_row)
    o_cmp = jnp.dot(p_c.astype(BF16), vc, preferred_element_type=F32)
    yield
    g0 = jnp.sum(p_c[0:NSA_GROUP], axis=0, keepdims=True)
    g1 = jnp.sum(p_c[NSA_GROUP:], axis=0, keepdims=True)
    psum8 = jnp.where(row8 == 0, g0, jnp.where(row8 == 1, g1, 0.0))
    imp8 = sum(jnp.dot(part, cov_ref[...], preferred_element_type=F32) for part in _split3(psum8))
    yield
    n_blk = imp8.shape[1]
    ib = lax.broadcasted_iota(jnp.int32, (n_blk, n_blk), 0)
    jb = lax.broadcasted_iota(jnp.int32, (n_blk, n_blk), 1)
    eye_b = jnp.where(ib == jb, 1.0, 0.0).astype(BF16)
    imp_t = _dot3_nt(eye_b, imp8)
    yield
    blk = lax.broadcasted_iota(jnp.int32, (n_blk, 1), 0)
    qblk = qpos // SEL_BLOCK
    forced = (blk == 0) | (blk == qblk) | (blk == qblk - 1)
    imp_t = jnp.where(forced, FORCE_SCORE, jnp.where(blk > qblk, -1.0, imp_t))
    imp_t = jnp.where(blk < n_sel, imp_t, -2.0)
    sel_t = _select_topk(imp_t, blk.astype(F32), topk)
    yield
    sel8 = lax.dot_general(sel_t.astype(BF16), eye_b, (((0,), (0,)), ((), ())), preferred_element_type=F32)
    yield
    head_sel = jnp.where(row8 < NSA_GROUP, sel8[0:1], sel8[1:2])

    key_sel = jnp.dot(head_sel.astype(BF16), eexp_ref[...], preferred_element_type=F32) > 0.5
    s_parts = [jnp.dot(qp, pr[0].astype(BF16), preferred_element_type=F32) for pr in sel_pages]
    s_sel = jnp.concatenate(s_parts, axis=1)
    yield
    kpos = lax.broadcasted_iota(jnp.int32, (1, past), 1)
    new_row = rows_ref[...]
    k_new = new_row[:, 2 * LANES:3 * LANES].astype(BF16).astype(F32)
    v_new = new_row[:, 3 * LANES:4 * LANES].astype(BF16).astype(F32)
    s_new = jnp.sum(qf * k_new, axis=1, keepdims=True)
    new_sel = head_sel[:, qblk:qblk + 1] > 0.5
    e, e_new, inv = _softmax_rows_extra(s_sel, key_sel & (kpos <= qpos), s_new, new_sel)
    acc = e_new * v_new
    for p_i, pr in enumerate(sel_pages):
        acc = acc + lax.dot_general(e[:, p_i * page:(p_i + 1) * page].astype(BF16), pr[1].astype(BF16), nt,
                                    preferred_element_type=F32)
    o_sel = acc * inv
    yield

    kw = wbuf_ref[0].astype(BF16)
    vw = wbuf_ref[1].astype(BF16)
    s_w = jnp.dot(qp, kw, preferred_element_type=F32)
    yield
    kpos_w = (past - n_buf) + lax.broadcasted_iota(jnp.int32, (1, n_buf), 1)
    dw = qpos - kpos_w
    w_new = wnew_ref[...]
    kw_new = w_new[:, 0:LANES].astype(BF16).astype(F32)
    vw_new = w_new[:, LANES:2 * LANES].astype(BF16).astype(F32)
    sw_new = jnp.sum(qf * kw_new, axis=1, keepdims=True)
    ew, ew_new, inv_w = _softmax_rows_extra(s_w, (dw >= 0) & (dw < WINDOW) & (kpos_w >= 0), sw_new, row8 >= 0)
    o_win = (lax.dot_general(ew.astype(BF16), vw, nt, preferred_element_type=F32) + ew_new * vw_new) * inv_w
    yield

    gates = jax.nn.sigmoid(gt_ref[...])
    lane8 = lax.broadcasted_iota(jnp.int32, (nh, LANES), 1)
    gcol = lambda j: jnp.sum(jnp.where(lane8 == 3 * row8 + j, gates, 0.0), axis=1, keepdims=True)
    o8 = gcol(0) * o_cmp + gcol(1) * o_sel + gcol(2) * o_win
    lane1 = lax.broadcasted_iota(jnp.int32, (1, LANES), 1)
    for pair in range(nh // 2):
        a = o8[2 * pair:2 * pair + 1]
        b = o8[2 * pair + 1:2 * pair + 2]
        if 2 * pair < NSA_GROUP:
            both = jnp.where(lane1 < HEAD_DIM, a, pltpu.roll(b, HEAD_DIM, 1))
        else:
            both = jnp.where(lane1 < HEAD_DIM, pltpu.roll(a, HEAD_DIM, 1), b)
        o_ref[:, pair * LANES:(pair + 1) * LANES] = both


def _nsa_sample(page_table, cache_t, q3, gates3, rows3, wnew3, win_cache, pe8, w1c, w2p, tabs_c, cov_s,
                eexp, l, n_cmp, n_sel, topk):
    bs, n_pages = page_table.shape
    page = cache_t.shape[4]
    n_buf = win_cache.shape[4]
    g = n_pages * page // CMP_STRIDE
    sps = NSA_SPS
    page_spec = lambda s, p, nty, ty: pl.BlockSpec(
        (None, None, nty, LANES, page), lambda b, pt, s=s, p=p: (l, pt[b * sps + s, p], ty, 0, 0))
    per_b = lambda *shape: pl.BlockSpec((sps,) + shape, lambda b, pt: (b,) + (0,) * len(shape))
    cst = lambda shape: pl.BlockSpec(shape, lambda b, pt: (0,) * len(shape))
    lyr = lambda shape: pl.BlockSpec((None,) + shape, lambda b, pt: (l,) + (0,) * len(shape))
    in_specs, args = [], []
    for s in range(sps):
        in_specs += ([page_spec(s, p, None, 0) for p in range(n_pages)]
                     + [page_spec(s, p, None, 1) for p in range(n_pages)]
                     + [page_spec(s, p, 2, 1) for p in range(n_pages)])
        args += [cache_t] * (3 * n_pages)
    in_specs += [per_b(NSA_HEADS, LANES), per_b(1, LANES), per_b(1, 4 * LANES), per_b(1, 2 * LANES),
                 pl.BlockSpec((None, sps, 2, LANES, n_buf), lambda b, pt: (l, b, 0, 0, 0)),
                 lyr((2, 8, CMP_STRIDE * HEAD_DIM)), lyr((2, CMP_STRIDE * HEAD_DIM, 2 * CMP_HIDDEN)),
                 lyr((2, 2, CMP_HIDDEN, LANES)), cst((g, LANES)), cst((g, LANES)), cst((g, LANES)),
                 cst(cov_s.shape), cst(eexp.shape)]
    args += [q3, gates3, rows3, wnew3, win_cache, pe8, w1c, w2p, *tabs_c, cov_s, eexp]
    grid_spec = pltpu.PrefetchScalarGridSpec(
        num_scalar_prefetch=1, grid=(bs // sps,), in_specs=in_specs,
        out_specs=pl.BlockSpec((sps, 1, NSA_WIDTH), lambda b, pt: (b, 0, 0)))
    return pl.pallas_call(
        functools.partial(_nsa_sample_body, n_pages=n_pages, page=page, n_buf=n_buf, n_cmp=n_cmp, n_sel=n_sel,
                          topk=topk),
        grid_spec=grid_spec,
        out_shape=jax.ShapeDtypeStruct((bs, 1, NSA_WIDTH), F32),
        compiler_params=_cparams("parallel"),
        name="nsa_sample",
    )(page_table, *args)


GDN_SB = 8


def _transpose3(x, eye):
    return _dot3_nt(eye, x)


def _gdn_sample_body(x_ref, cs_ref, gt_ref, z_ref, s_ref, cw_ref, alog_ref, dtb_ref, gn_ref, o_ref, so_ref):
    sb = x_ref.shape[0]
    qk_w = GDN_HEADS * GDN_DK
    c = x_ref[...] * cw_ref[CONV_W - 1:CONV_W, :]
    for j in range(CONV_W - 1):
        c = c + cs_ref[:, j * GDN_QKV:(j + 1) * GDN_QKV] * cw_ref[j:j + 1, :]
    c = c * jax.nn.sigmoid(c)
    gt = gt_ref[...]
    g_full = -jnp.exp(alog_ref[...]) * _softplus(gt + dtb_ref[...])
    eg_full = jnp.exp(g_full)
    beta_full = jax.nn.sigmoid(gt)
    ii = lax.broadcasted_iota(jnp.int32, (LANES, LANES), 0)
    jj = lax.broadcasted_iota(jnp.int32, (LANES, LANES), 1)
    eye = jnp.where(ii == jj, 1.0, 0.0).astype(BF16)
    gn = gn_ref[...]
    for h in range(GDN_HEADS):
        sl = slice(h * GDN_DK, (h + 1) * GDN_DK)
        q = _l2n(c[:, h * GDN_DK:(h + 1) * GDN_DK]) * (GDN_DK ** -0.5)
        k = _l2n(c[:, qk_w + h * GDN_DK:qk_w + (h + 1) * GDN_DK])
        v = c[:, 2 * qk_w + h * GDN_DV:2 * qk_w + (h + 1) * GDN_DV]
        eg = jnp.broadcast_to(eg_full[:, GATE_A + h:GATE_A + h + 1], (sb, GDN_DK))
        beta = jnp.broadcast_to(beta_full[:, GATE_B + h:GATE_B + h + 1], (sb, GDN_DK))
        att = jnp.sum(q * k, axis=1, keepdims=True)
        k_t = _transpose3(k, eye)
        w_t = _transpose3(k * beta * eg, eye)
        qg_t = _transpose3(q * eg, eye)
        u = v * beta
        zz = z_ref[:, sl]
        for i in range(sb):
            s = s_ref[i, h]
            v_new = u[i:i + 1] - jnp.sum(w_t[:, i:i + 1] * s, axis=0, keepdims=True)
            o = jnp.sum(qg_t[:, i:i + 1] * s, axis=0, keepdims=True) + att[i:i + 1] * v_new
            so_ref[i, h] = s * eg[i:i + 1] + k_t[:, i:i + 1] * v_new
            y = o * lax.rsqrt(jnp.mean(o * o, axis=-1, keepdims=True) + RMS_EPS) * gn
            zi = zz[i:i + 1]
            o_ref[i:i + 1, sl] = y * (zi * jax.nn.sigmoid(zi))


def _gdn_sample(qkv, conv_state, gates, z, s_state, conv_w, alog_row, dtb_row, gnorm, l):
    bs = qkv.shape[0]
    sb = GDN_SB
    row = lambda w: pl.BlockSpec((sb, w), lambda i: (i, 0))
    st = pl.BlockSpec((None, sb, GDN_HEADS, GDN_DK, GDN_DV), lambda i: (l, i, 0, 0, 0))
    return pl.pallas_call(
        _gdn_sample_body,
        grid=(bs // sb,),
        in_specs=[row(GDN_QKV), pl.BlockSpec((None, sb, (CONV_W - 1) * GDN_QKV), lambda i: (l, i, 0)),
                  row(LANES), row(GDN_WIDTH), st,
                  _layer_spec((CONV_W, GDN_QKV), l), _layer_spec((1, LANES), l), _layer_spec((1, LANES), l),
                  _layer_spec((1, GDN_DV), l)],
        out_specs=[row(GDN_WIDTH), pl.BlockSpec((sb, GDN_HEADS, GDN_DK, GDN_DV), lambda i: (i, 0, 0, 0))],
        out_shape=[jax.ShapeDtypeStruct((bs, GDN_WIDTH), F32),
                   jax.ShapeDtypeStruct((bs, GDN_HEADS, GDN_DK, GDN_DV), F32)],
        compiler_params=_cparams("parallel"),
        name="gdn_sample",
    )(qkv, conv_state, gates, z, s_state, conv_w, alog_row, dtb_row, gnorm)


def _mem_sample_body(q_ref, kv_ref, o_ref):
    nt = (((1,), (1,)), ((), ()))
    q = q_ref[...] * (MEM_HD ** -0.5)
    row = lax.broadcasted_iota(jnp.int32, (8, MEM_WIDTH), 0)
    lane = lax.broadcasted_iota(jnp.int32, (8, MEM_WIDTH), 1)
    own = (lane // MEM_HD) == row
    qm = jnp.where(own, q, 0.0).astype(BF16)
    k_t = kv_ref[0].astype(BF16)
    v_t = kv_ref[1].astype(BF16)
    s = jnp.dot(qm, k_t, preferred_element_type=F32)
    e = jnp.exp(s - jnp.max(s, axis=-1, keepdims=True))
    p = e / jnp.sum(e, axis=-1, keepdims=True)
    o = lax.dot_general(p.astype(BF16), v_t, nt, preferred_element_type=F32)
    o_ref[...] = jnp.sum(jnp.where(own, o, 0.0), axis=0, keepdims=True)


def _mem_sample(q3, mem_cache, l):
    bs = q3.shape[0]
    m = mem_cache.shape[4]
    return pl.pallas_call(
        _mem_sample_body,
        grid=(bs,),
        in_specs=[pl.BlockSpec((None, 1, MEM_WIDTH), lambda b: (b, 0, 0)),
                  pl.BlockSpec((None, None, 2, MEM_WIDTH, m), lambda b: (l, b, 0, 0, 0))],
        out_specs=pl.BlockSpec((None, 1, MEM_WIDTH), lambda b: (b, 0, 0)),
        out_shape=jax.ShapeDtypeStruct((bs, 1, MEM_WIDTH), F32),
        compiler_params=_cparams("parallel"),
        name="mem_sample",
    )(q3, mem_cache)


def _rope_tables(pos):
    half = ROPE_DIM // 2
    inv = ROPE_THETA ** (-2.0 * jnp.arange(half, dtype=F32) / ROPE_DIM)
    ang = pos.astype(F32)[:, None] * inv[None, :]
    cos, sin = jnp.cos(ang), jnp.sin(ang)
    n = pos.shape[0]
    one = jnp.ones((n, HEAD_DIM - ROPE_DIM), F32)
    zero = jnp.zeros((n, HEAD_DIM - ROPE_DIM), F32)
    zh = jnp.zeros((n, half), F32)
    c = jnp.concatenate([cos, cos, one], axis=1)
    sm = jnp.concatenate([-sin, zh, zero], axis=1)
    sp = jnp.concatenate([zh, sin, zero], axis=1)
    return tuple(jnp.concatenate([t, t], axis=1) for t in (c, sm, sp))


def _block_onehot(pos):
    e = (pos[:, None] // SEL_BLOCK == jnp.arange(HEAD_DIM)[None, :]).astype(F32)
    return jnp.concatenate([e, e], axis=1)


def _rope_jnp(x, pos):
    half = ROPE_DIM // 2
    inv = ROPE_THETA ** (-2.0 * jnp.arange(half, dtype=F32) / ROPE_DIM)
    ang = pos.astype(F32)[:, None] * inv[None, :]
    cos = jnp.cos(ang)[None, :, None, :]
    sin = jnp.sin(ang)[None, :, None, :]
    x1 = x[..., :half]
    x2 = x[..., half:ROPE_DIM]
    return jnp.concatenate([x1 * cos - x2 * sin, x2 * cos + x1 * sin, x[..., ROPE_DIM:]], axis=-1)


def _masked_softmax(s, mask):
    s = jnp.where(mask, s, NEG_INF)
    m = jnp.max(s, axis=-1, keepdims=True)
    e = jnp.where(mask, jnp.exp(s - m), 0.0)
    return e / jnp.maximum(jnp.sum(e, axis=-1, keepdims=True), 1e-30)


def _compress_jnp(k, pe, w1, w2):
    b, t, hk, dh = k.shape
    n_cmp = (t - CMP_BLOCK) // CMP_STRIDE + 1
    start = jnp.arange(n_cmp) * CMP_STRIDE
    idx = start[:, None] + jnp.arange(CMP_BLOCK)[None, :]
    blk = k[:, idx] + pe[None, None, :, None, :]
    blk = jnp.swapaxes(blk, 2, 3).reshape(b, n_cmp, hk, CMP_BLOCK * dh)
    return jax.nn.gelu(blk @ w1) @ w2, start + CMP_BLOCK - 1


def _block_coverage(c_start, n_sel):
    b_start = jnp.arange(n_sel) * SEL_BLOCK
    lo = jnp.maximum(c_start[:, None], b_start[None, :])
    hi = jnp.minimum(c_start[:, None] + CMP_BLOCK, b_start[None, :] + SEL_BLOCK)
    return jnp.clip(hi - lo, 0, None).astype(F32) / CMP_BLOCK


def _sample_nsa_jnp(q, gate_logits, rows_new, win_new, nsa_past, win_buf, w_buf, cmp_pe, w_cmp1, w_cmp2, topk):
    b, t = q.shape[:2]
    pos0 = nsa_past.shape[1]
    qpos = pos0 + jnp.arange(t)
    q = q.reshape(b, t, NSA_KV_HEADS, NSA_GROUP, HEAD_DIM)
    rows = jnp.concatenate([nsa_past, rows_new], axis=1)
    t_kv = rows.shape[1]
    kc, c_end = _compress_jnp(rows[:, :, 0], cmp_pe[0], w_cmp1[0], w_cmp2[0])
    vc, _ = _compress_jnp(rows[:, :, 1], cmp_pe[1], w_cmp1[1], w_cmp2[1])
    kc = _rope_jnp(kc, c_end)
    n_sel = -(-t_kv // SEL_BLOCK)
    cov = _block_coverage(c_end - CMP_BLOCK + 1, n_sel)
    sel = jnp.pad(rows[:, :, 2:], ((0, 0), (0, n_sel * SEL_BLOCK - t_kv), (0, 0), (0, 0), (0, 0)))
    sel = sel.reshape(b, n_sel, SEL_BLOCK, 2, NSA_KV_HEADS, HEAD_DIM).transpose(3, 0, 4, 1, 2, 5)
    k_blk, v_blk = sel[0], sel[1]
    win_all = jnp.concatenate([win_buf, win_new], axis=1)
    n_buf = win_buf.shape[1]
    gates = jax.nn.sigmoid(gate_logits).reshape(b, t, NSA_KV_HEADS, NSA_GROUP, 3)
    s = jnp.einsum('bqhgd,bchd->bhgqc', q, kc) * ATTN_SCALE
    p = _masked_softmax(s, c_end[None, :] <= qpos[:, None])
    o_cmp = jnp.einsum('bhgqc,bchd->bqhgd', p, vc)
    imp = jnp.einsum('bhgqc,cn->bhqn', p, cov)
    blk = jnp.arange(n_sel)[None, :]
    q_blk = (qpos // SEL_BLOCK)[:, None]
    forced = (blk == 0) | (blk == q_blk) | (blk == q_blk - 1)
    imp = jnp.where(forced, FORCE_SCORE, jnp.where(blk > q_blk, -1.0, imp))
    _, idx = lax.top_k(imp, min(topk, n_sel))
    bi = jnp.arange(b)[:, None, None, None]
    hi = jnp.arange(NSA_KV_HEADS)[None, :, None, None]
    kg = k_blk[bi, hi, idx]
    vg = v_blk[bi, hi, idx]
    k_pos = idx[..., None] * SEL_BLOCK + jnp.arange(SEL_BLOCK)
    mask = (k_pos <= qpos[None, None, :, None, None])[:, :, None]
    s2 = jnp.einsum('bqhgd,bhqksd->bhgqks', q, kg) * ATTN_SCALE
    shp = s2.shape
    p2 = _masked_softmax(s2.reshape(shp[:-2] + (-1,)), mask.reshape(mask.shape[:-2] + (-1,))).reshape(shp)
    o_sel = jnp.einsum('bhgqks,bhqksd->bqhgd', p2, vg)
    kw_pos = pos0 - n_buf + jnp.arange(n_buf + t)
    sw = jnp.einsum('bqhgd,bkhd->bhgqk', q, win_all[:, :, 0]) * ATTN_SCALE
    diff = qpos[:, None] - kw_pos[None, :]
    pw = _masked_softmax(sw, (diff >= 0) & (diff < WINDOW) & (kw_pos[None, :] >= 0))
    o_win = jnp.einsum('bhgqk,bkhd->bqhgd', pw, win_all[:, :, 1])
    o = gates[..., 0:1] * o_cmp + gates[..., 1:2] * o_sel + gates[..., 2:3] * o_win
    return o.reshape(b, t, NSA_WIDTH), win_all[:, win_all.shape[1] - w_buf:]


def _to_chunks(a, c, pad):
    a = jnp.moveaxis(a, 1, 2)
    if pad:
        a = jnp.pad(a, [(0, 0), (0, 0), (0, pad)] + [(0, 0)] * (a.ndim - 3))
    b, h, tp = a.shape[:3]
    a = a.reshape((b, h, tp // c, c) + a.shape[3:])
    return jnp.moveaxis(a, 2, 0)


def _chunk_gated_delta_jnp(q, k, v, g, beta, s0):
    b, t, h, _ = q.shape
    c = min(GDN_CHUNK, t)
    pad = (-t) % c
    qc, kc, vc = _to_chunks(q, c, pad), _to_chunks(k, c, pad), _to_chunks(v, c, pad)
    gc = jnp.cumsum(_to_chunks(g, c, pad), axis=-1)
    bc = _to_chunks(beta, c, pad)
    ii = jnp.arange(c)
    incl = ii[:, None] >= ii[None, :]
    strict = ii[:, None] > ii[None, :]
    diff = gc[..., :, None] - gc[..., None, :]
    decay = jnp.where(incl, jnp.exp(jnp.where(incl, diff, 0.0)), 0.0)
    kb = kc * bc[..., None]
    a = jnp.where(strict, jnp.einsum('...id,...jd->...ij', kb, kc) * decay, 0.0)
    eye = jnp.eye(c, dtype=F32)
    tm = lax.linalg.triangular_solve(eye + a, jnp.broadcast_to(eye, a.shape), left_side=True, lower=True)
    u = jnp.einsum('...ij,...jd->...id', tm, vc * bc[..., None])
    w = jnp.einsum('...ij,...jd->...id', tm, kb * jnp.exp(gc)[..., None])

    def step(s, inp):
        qi, ki, ui, wi, gi, di = inp
        v_new = ui - jnp.einsum('bhck,bhkv->bhcv', wi, s)
        att = jnp.einsum('bhik,bhjk->bhij', qi, ki) * di
        o = jnp.einsum('bhck,bhkv->bhcv', qi * jnp.exp(gi)[..., None], s) + jnp.einsum('bhij,bhjv->bhiv', att, v_new)
        gl = gi[..., -1]
        s = s * jnp.exp(gl)[..., None, None] + jnp.einsum(
            'bhck,bhcv->bhkv', ki * jnp.exp(gl[..., None] - gi)[..., None], v_new)
        return s, o

    s, o = lax.scan(step, s0, (qc, kc, u, w, gc, decay))
    o = jnp.moveaxis(o, 0, 2).reshape(b, h, -1, o.shape[-1])[:, :, :t]
    return jnp.moveaxis(o, 1, 2), s


def _gdn_jnp(qkv, a, bb, z, conv_buf, state, conv_w, a_log, dt_bias, norm_w):
    b, t, _ = qkv.shape
    xp = jnp.concatenate([conv_buf, qkv], axis=1)
    c = xp[:, 0:t] * conv_w[0]
    for j in range(1, CONV_W):
        c = c + xp[:, j:j + t] * conv_w[j]
    c = jax.nn.silu(c)
    new_buf = xp[:, t:]
    qk_w = GDN_HEADS * GDN_DK
    l2 = lambda x: x * lax.rsqrt(jnp.sum(x * x, axis=-1, keepdims=True) + 1e-6)
    q = l2(c[..., :qk_w].reshape(b, t, GDN_HEADS, GDN_DK)) * GDN_DK ** -0.5
    k = l2(c[..., qk_w:2 * qk_w].reshape(b, t, GDN_HEADS, GDN_DK))
    v = c[..., 2 * qk_w:].reshape(b, t, GDN_HEADS, GDN_DV)
    g = -jnp.exp(a_log) * jax.nn.softplus(a + dt_bias)
    beta = jax.nn.sigmoid(bb)
    o, new_state = _chunk_gated_delta_jnp(q, k, v, g, beta, state)
    o = o * lax.rsqrt(jnp.mean(o * o, axis=-1, keepdims=True) + RMS_EPS) * norm_w
    o = o * jax.nn.silu(z).reshape(b, t, GDN_HEADS, GDN_DV)
    return o.reshape(b, t, GDN_WIDTH), new_buf, new_state


def _mem_attend_sample_jnp(x, gains, mem_kv, w_q, w_o):
    xf = x
    h = xf * lax.rsqrt(jnp.mean(xf * xf, axis=-1, keepdims=True) + RMS_EPS) * gains[4]
    q = (h @ w_q).reshape(-1, MEM_HEADS, MEM_HD)
    s = jnp.einsum('bhd,bmhd->bhm', q, mem_kv[:, :, 0]) * MEM_HD ** -0.5
    p = jax.nn.softmax(s, axis=-1)
    o = jnp.einsum('bhm,bmhd->bhd', p, mem_kv[:, :, 1]).reshape(-1, MEM_WIDTH)
    y = o @ w_o
    return x + y * lax.rsqrt(jnp.mean(y * y, axis=-1, keepdims=True) + RMS_EPS) * gains[5]


def _pack_w_in(w_in):
    nl, d, _ = w_in.shape
    o_q, o_kv, o_g = NSA_WIDTH, NSA_WIDTH + KV6_WIDTH, NSA_WIDTH + KV6_WIDTH + 3 * NSA_HEADS
    o_qkv = o_g
    o_a = o_qkv + GDN_QKV
    o_b = o_a + GDN_HEADS
    o_z = o_b + GDN_HEADS
    wq = w_in[:, :, :o_q].reshape(nl, d, NSA_HEADS, HEAD_DIM)
    zq = jnp.zeros_like(wq)
    first = jnp.concatenate([wq, zq], axis=-1)
    second = jnp.concatenate([zq, wq], axis=-1)
    kv_of_head = (jnp.arange(NSA_HEADS) // NSA_GROUP)[None, None, :, None]
    wq_pad = jnp.where(kv_of_head == 0, first, second).reshape(nl, d, QPAD_WIDTH)
    gate_grp = jnp.concatenate([w_in[:, :, o_kv:o_g], w_in[:, :, o_a:o_z],
                                jnp.zeros((nl, d, LANES - 3 * NSA_HEADS - 2 * GDN_HEADS), w_in.dtype)], axis=-1)
    packed = jnp.concatenate([wq_pad, w_in[:, :, o_q:o_kv], w_in[:, :, o_qkv:o_a], w_in[:, :, o_z:], gate_grp], axis=-1)
    return packed.astype(BF16)


def kernel(x_prompt, x_sample, cache_nsa_kv, cache_win_kv, state_gdn_S, state_gdn_conv, cache_mem_kv, page_table, mem_prompt, norm_gains, mem_norm, w_ffn_gu, w_ffn_down, w_in, w_out, cmp_pe, w_cmp1, w_cmp2, gdn_conv, gdn_A_log, gdn_dt_bias, gdn_norm, w_mem_q, w_mem_kv, w_mem_o):
    bp, t_len, d = x_prompt.shape
    bs = x_sample.shape[0]
    depth = w_in.shape[0]
    w_buf = cache_win_kv.shape[2]
    n_pages = page_table.shape[1]
    page = cache_nsa_kv.shape[2]
    past_len = n_pages * page
    n_mem = mem_prompt.shape[1]
    np_tok = bp * t_len

    gains = norm_gains.reshape(depth, 8, 1, d)
    wgu = w_ffn_gu.astype(BF16)
    wd = w_ffn_down.astype(BF16)
    w_in_p = _pack_w_in(w_in)
    w_out_b = w_out.astype(BF16)
    w_mq = w_mem_q.astype(BF16)
    w_mkv = w_mem_kv.astype(BF16)
    w_mo = w_mem_o.astype(BF16)
    mem_gain = mem_norm.reshape(depth, 1, d)

    pos_p = jnp.arange(t_len)
    tabs_p = _rope_tables(pos_p) + (_block_onehot(pos_p),)
    pos_s = jnp.full((bs,), past_len, jnp.int32)
    tabs_s = _rope_tables(pos_s) + (jnp.zeros((bs, LANES), F32),)

    half_k = CMP_STRIDE * HEAD_DIM
    pe8 = jnp.pad(cmp_pe.reshape(depth, 2, CMP_BLOCK // CMP_STRIDE, half_k), ((0, 0), (0, 0), (0, 6), (0, 0)))
    w1c = jnp.concatenate([w_cmp1[:, :, :half_k], w_cmp1[:, :, half_k:]], axis=-1).astype(BF16)
    z2 = jnp.zeros_like(w_cmp2)
    w2p = jnp.stack([jnp.concatenate([w_cmp2, z2], axis=-1), jnp.concatenate([z2, w_cmp2], axis=-1)], axis=2).astype(BF16)
    tabs_c = _rope_tables(jnp.arange(t_len // CMP_STRIDE) * CMP_STRIDE + (CMP_BLOCK - 1))
    lane_pad = lambda v: jnp.pad(v, ((0, 0), (GATE_A, LANES - GATE_A - GDN_HEADS))).reshape(depth, 1, LANES)
    alog_row = lane_pad(gdn_A_log)
    dtb_row = lane_pad(gdn_dt_bias)
    gnorm = gdn_norm.reshape(depth, 1, GDN_DV)

    n_cmp = (t_len - CMP_BLOCK) // CMP_STRIDE + 1
    n_cmp_pad = t_len // CMP_STRIDE
    n_sel = -(-t_len // SEL_BLOCK)
    c_start = jnp.arange(n_cmp) * CMP_STRIDE
    assert n_sel <= HEAD_DIM and t_len >= WINDOW + NSA_TQ and t_len % NSA_TK == 0
    covt = jnp.pad(_block_coverage(c_start, n_sel).T, ((0, HEAD_DIM - n_sel), (0, n_cmp_pad - n_cmp))).astype(BF16)
    topk = min(SEL_TOPK, n_sel)

    assert x_sample.shape[1] == 1 and past_len % SEL_BLOCK == 0 and bs % GDN_SB == 0
    t_kv = past_len + 1
    n_cmp_s = (t_kv - CMP_BLOCK) // CMP_STRIDE + 1
    assert (n_cmp_s - 1) * CMP_STRIDE + CMP_BLOCK <= past_len
    g_s = past_len // CMP_STRIDE
    n_sel_s = -(-t_kv // SEL_BLOCK)
    assert n_sel_s <= HEAD_DIM
    topk_s = min(SEL_TOPK, n_sel_s)
    cov_s = jnp.pad(_block_coverage(jnp.arange(n_cmp_s) * CMP_STRIDE, n_sel_s),
                    ((0, g_s - n_cmp_s), (0, HEAD_DIM - n_sel_s))).astype(BF16)
    eexp = (jnp.arange(past_len)[None, :] // SEL_BLOCK == jnp.arange(HEAD_DIM)[:, None]).astype(BF16)
    tabs_cs = _rope_tables(jnp.arange(g_s) * CMP_STRIDE + (CMP_BLOCK - 1))
    token_last = (0, 1, 3, 4, 5, 2)
    n_phys = cache_nsa_kv.shape[1]
    cache_t = jnp.transpose(cache_nsa_kv, token_last).reshape(depth, n_phys, 4, LANES, page)
    win_cache = jnp.transpose(cache_win_kv, token_last).reshape(depth, bs, 2, LANES, w_buf)
    conv_cache = state_gdn_conv.reshape(depth, bs, (CONV_W - 1) * GDN_QKV)
    mem_cache = jnp.transpose(cache_mem_kv, token_last).reshape(depth, bs, 2, MEM_WIDTH, n_mem)

    tm_p = 512
    yp = x_prompt.reshape(np_tok, d)
    ys = x_sample.reshape(bs, d)
    mem_flat = mem_prompt.reshape(bp * n_mem, d)
    outs = [[] for _ in range(9)]
    for l in range(depth):
        mem_kv_p = _normmm(mem_flat, mem_gain, (l,), w_mkv, l, n_mem)
        yp = _ffn(yp, gains, wgu, wd, l, 0, tm_p)
        qpad, rows, win, kaug, vsel, kvwin, gates, qkv, z = _inproj(yp, gains, w_in_p, tabs_p, l, tm_p)
        rows5 = rows.reshape(bp, t_len, 4, NSA_KV_HEADS, HEAD_DIM)
        kc, vc = _compress_prompt(rows, pe8, w1c, w2p, tabs_c, l, bp, t_len)
        o_nsa = _nsa_prompt(qpad, gates, kaug, vsel, kvwin, kc, vc, covt, bp, t_len, n_cmp, n_sel, topk)
        qg, kd, w_g, u_g, att, egl = _gdn_prep(qkv, gates, gdn_conv, alog_row, dtb_row, l, bp, t_len)
        o_gdn, s_p = _gdn_scan(qg, kd, w_g, u_g, att, egl, z, gnorm, l, bp, t_len)
        conv_p = qkv.reshape(bp, t_len, GDN_QKV)[:, t_len - (CONV_W - 1):]
        yp = _outproj(yp, o_nsa, o_gdn, w_out_b, gains, l, tm_p)
        yp = _mem_block(yp, gains, w_mq, w_mo, mem_kv_p, l, tm_p, t_len // tm_p)
        yp = _ffn(yp, gains, wgu, wd, l, 1, tm_p)
        win5 = win.reshape(bp, t_len, 2, NSA_KV_HEADS, HEAD_DIM)
        win_p = win5[:, t_len - w_buf:] if t_len >= w_buf else jnp.pad(
            win5, ((0, 0), (w_buf - t_len, 0), (0, 0), (0, 0), (0, 0)))

        ys = _ffn(ys, gains, wgu, wd, l, 0, bs)
        qpad_s, rows_s, win_s, _, _, _, gates_s, qkv_s, z_s = _inproj(ys, gains, w_in_p, tabs_s, l, bs)
        o_nsa_s = _nsa_sample(
            page_table, cache_t, qpad_s.astype(F32).reshape(bs, NSA_HEADS, LANES), gates_s.reshape(bs, 1, LANES),
            rows_s.reshape(bs, 1, 4 * LANES), win_s.reshape(bs, 1, 2 * LANES), win_cache, pe8, w1c, w2p, tabs_cs,
            cov_s, eexp, l, n_cmp_s, n_sel_s, topk_s).reshape(bs, NSA_WIDTH)
        o_gdn_s, s_s = _gdn_sample(qkv_s, conv_cache, gates_s, z_s, state_gdn_S, gdn_conv, alog_row, dtb_row, gnorm, l)
        ys = _outproj(ys, o_nsa_s, o_gdn_s, w_out_b, gains, l, bs)
        q_m = _normmm(ys, gains, (l, 4), w_mq, l, bs)
        o_m = _mem_sample(q_m.reshape(bs, 1, MEM_WIDTH), mem_cache, l).reshape(bs, MEM_WIDTH)
        ys = _outproj(ys, o_m[:, :MEM_WIDTH // 2], o_m[:, MEM_WIDTH // 2:], w_mo, gains, l, bs, gain_idx=5)
        ys = _ffn(ys, gains, wgu, wd, l, 1, bs)
        win_all = jnp.concatenate([cache_win_kv[l], win_s.reshape(bs, 1, 2, NSA_KV_HEADS, HEAD_DIM)], axis=1)
        wbuf_s = win_all[:, win_all.shape[1] - w_buf:]
        conv_s = jnp.concatenate([state_gdn_conv[l], qkv_s[:, None, :]], axis=1)[:, 1:]

        for lst, val in zip(outs, (rows5, win_p, s_p, conv_p, mem_kv_p.reshape(bp, n_mem, 2, MEM_HEADS, MEM_HD),
                                   rows_s.reshape(bs, 1, 4, NSA_KV_HEADS, HEAD_DIM), wbuf_s, s_s, conv_s)):
            lst.append(val)
    return (yp.reshape(bp, t_len, d), ys.reshape(bs, 1, d)) + tuple(jnp.stack(v) for v in outs)
```
